```python
import jax
import jax.numpy as jnp
from jax import lax
import numpy as np

D_MODEL = 2048
BATCH = 1
SEQ = 8192
DEPTH = 1
DEC_BATCH = 32
DEC_SEQ = 32
PAST_LEN = 4096

CHUNK = 64
N_META = 16
EPS = 1e-6
D_FF = 4 * D_MODEL
A_HEADS = 8
A_DK = 128
A_DV = D_MODEL // A_HEADS
A_QK = A_HEADS * A_DK
A_VW = A_HEADS * A_DV
A_CONV = 4
B_HEAD = 64
B_HEADS = D_MODEL // B_HEAD
B_W = B_HEADS * B_HEAD
LORA_DECAY = max(32, int(round(1.8 * D_MODEL ** 0.5 / 32)) * 32)
LORA_ICLR = max(32, int(round(1.8 * D_MODEL ** 0.5 / 32)) * 32)
LORA_GATE = max(32, int(round(0.6 * D_MODEL ** 0.8 / 32)) * 32)
GN_EPS = 64e-5
CONV_COLS = 2 * A_QK
A_COLS = 2 * A_VW + 2 * A_HEADS
SHIFT_COLS = 3 * B_W + LORA_DECAY + LORA_ICLR + LORA_GATE
GATE_COLS = 2 * D_MODEL
P_TOTAL = CONV_COLS + A_COLS + SHIFT_COLS + GATE_COLS

kernel_name = "mlstm_rwkv7_parallel_stream_step"


def split_cols(u, sizes):
    idx = [int(s) for s in np.cumsum(sizes)[:-1]]
    return jnp.split(u, idx, axis=-1)


def rmsnorm(x, g):
    return x * lax.rsqrt(jnp.mean(x * x, axis=-1, keepdims=True) + EPS) * g


def causal_conv(u, buf, w, b):
    L = u.shape[1]
    ext = jnp.concatenate([buf, u], axis=1)
    out = b + sum(ext[:, j:j + L] * w[j] for j in range(A_CONV))
    return out, ext[:, L:]


def token_shift(u, buf, mu):
    L = u.shape[1]
    ext = jnp.concatenate([buf, u], axis=1)
    return u + mu * (ext[:, :L] - u), ext[:, L:]


def to_blocks(t, block):
    Bsz, L = t.shape[:2]
    t = t.reshape((Bsz, L // block, block) + t.shape[2:])
    return jnp.swapaxes(jnp.moveaxis(t, 1, 0), 2, 3)


def mlstm_chunkwise(q, k, v, ig, lf, C, n, m, block):
    Bsz, L, H, DV = v.shape
    tri = jnp.tril(jnp.ones((block, block), dtype=bool))

    def step(carry, blk):
        C, n, m = carry
        qb, kb, vb, ib, fb = blk
        b = jnp.cumsum(fb, axis=-1)
        d = jnp.where(tri, b[..., :, None] - b[..., None, :] + ib[..., None, :], -jnp.inf)
        inter = b + m[..., None]
        m_t = jnp.maximum(inter, jnp.max(d, axis=-1))
        s = jnp.einsum('bhtd,bhsd->bhts', qb, kb) * jnp.exp(d - m_t[..., None])
        e = jnp.exp(inter - m_t)
        num = jnp.einsum('bhts,bhsv->bhtv', s, vb) + e[..., None] * jnp.einsum('bhtd,bhdv->bhtv', qb, C)
        den = jnp.sum(s, axis=-1) + e * jnp.einsum('bhtd,bhd->bht', qb, n)
        h = num / jnp.maximum(jnp.abs(den), jnp.exp(-m_t))[..., None]
        b_end = b[..., -1]
        g = b_end[..., None] - b + ib
        m_new = jnp.maximum(b_end + m, jnp.max(g, axis=-1))
        carry_decay = jnp.exp(b_end + m - m_new)
        wk = kb * jnp.exp(g - m_new[..., None])[..., None]
        C = carry_decay[..., None, None] * C + jnp.einsum('bhsd,bhsv->bhdv', wk, vb)
        n = carry_decay[..., None] * n + jnp.sum(wk, axis=2)
        return (C, n, m_new), h

    blocks = tuple(to_blocks(t, block) for t in (q, k, v, ig, lf))
    (C, n, m), h = lax.scan(step, (C, n, m), blocks)
    h = jnp.moveaxis(jnp.swapaxes(h, 2, 3), 0, 1).reshape(Bsz, L, H, DV)
    return h, C, n, m


def mlstm_segments(q, k, v, ig, lf, C, n, m, segments):
    outs = []
    start = 0
    for length, block in segments:
        sl = slice(start, start + length)
        h, C, n, m = mlstm_chunkwise(q[:, sl], k[:, sl], v[:, sl], ig[:, sl], lf[:, sl], C, n, m, block)
        outs.append(h)
        start += length
    return jnp.concatenate(outs, axis=1), C, n, m


def rwkv_scan(r, decay, k, v, a, b, S):
    def step(S, inp):
        rt, wt, kt, vt, at, bt = inp
        Sa = jnp.einsum('bhij,bhj->bhi', S, at)
        S = S * wt[:, :, None, :] + Sa[..., None] * bt[:, :, None, :] + vt[..., None] * kt[:, :, None, :]
        return S, jnp.einsum('bhij,bhj->bhi', S, rt)

    xs = tuple(jnp.moveaxis(t, 1, 0) for t in (r, decay, k, v, a, b))
    S, out = lax.scan(step, S, xs)
    return jnp.moveaxis(out, 0, 1), S


def trunk_layer(x, C, n, m, conv_buf, S, shift_buf, p, segments):
    (g_mix_pre, g_mix_post, g_mlp_pre, g_mlp_post, w_in, conv_w, conv_b, b_i, b_f, a_norm_w,
     mu, w0, w2, a0, a2, g2, k_k, k_a, r_k, ln_w, ln_b, w_out, w_up, w_down) = p
    Bsz, L, _ = x.shape
    hn = rmsnorm(x, g_mix_pre)
    proj = hn @ w_in
    u_conv, u_a, u_shift, u_gate = split_cols(proj, (CONV_COLS, A_COLS, SHIFT_COLS, GATE_COLS))
    qk, conv_new = causal_conv(u_conv, conv_buf, conv_w, conv_b)
    q, k = split_cols(jax.nn.silu(qk), (A_QK, A_QK))
    q = q.reshape(Bsz, L, A_HEADS, A_DK)
    k = k.reshape(Bsz, L, A_HEADS, A_DK) * (A_DK ** -0.5)
    v_a, o_a, i_pre, f_pre = split_cols(u_a, (A_VW, A_VW, A_HEADS, A_HEADS))
    v_a = v_a.reshape(Bsz, L, A_HEADS, A_DV)
    ig = i_pre + b_i
    lf = jax.nn.log_sigmoid(f_pre + b_f)
    h_a, C, n, m = mlstm_segments(q, k, v_a, ig, lf, C, n, m, segments)
    h_a = h_a * lax.rsqrt(jnp.mean(h_a * h_a, axis=-1, keepdims=True) + EPS)
    h_a = h_a.reshape(Bsz, L, A_VW) * a_norm_w * jax.nn.sigmoid(o_a)
    xs, shift_new = token_shift(u_shift, shift_buf, mu)
    r, kb, vb, wd, ad, gd = split_cols(xs, (B_W, B_W, B_W, LORA_DECAY, LORA_ICLR, LORA_GATE))
    w_log = -jax.nn.softplus(-(w0 + jnp.tanh(wd) @ w2)) - 0.5
    decay = jnp.exp(-jnp.exp(w_log))
    a = jax.nn.sigmoid(a0 + ad @ a2)
    g = jax.nn.sigmoid(gd) @ g2
    heads = lambda t: t.reshape(Bsz, L, B_HEADS, B_HEAD)
    kk = heads(kb * k_k)
    kk = kk / jnp.maximum(jnp.sqrt(jnp.sum(kk * kk, axis=-1, keepdims=True)), 1e-12)
    kb = kb * (1.0 + (a - 1.0) * k_a)
    rh, kh, vh, ah = heads(r), heads(kb), heads(vb), heads(a)
    o, S = rwkv_scan(rh, heads(decay), kh, vh, -kk, kk * ah, S)
    o_mean = jnp.mean(o, axis=-1, keepdims=True)
    o_c = o - o_mean
    o = o_c * lax.rsqrt(jnp.mean(o_c * o_c, axis=-1, keepdims=True) + GN_EPS)
    o = o.reshape(Bsz, L, B_W) * ln_w + ln_b
    bonus = jnp.sum(rh * kh * r_k, axis=-1, keepdims=True) * vh
    h_b = (o + bonus.reshape(Bsz, L, B_W)) * g
    g_a, g_b = split_cols(u_gate, (D_MODEL, D_MODEL))
    mixed = jax.nn.sigmoid(g_a) * h_a + jax.nn.sigmoid(g_b) * h_b
    x = x + rmsnorm(mixed @ w_out, g_mix_post)
    hn = rmsnorm(x, g_mlp_pre)
    ff = jnp.square(jax.nn.relu(hn @ w_up)) @ w_down
    x = x + rmsnorm(ff, g_mlp_post)
    return x, (C, n, m, conv_new, S, shift_new)


def setup_inputs(seed: int = 0) -> dict:
    key = jax.random.key(seed)
    ks = jax.random.split(key, 40)
    nrm = lambda i, shape, s: s * jax.random.normal(ks[i], shape, jnp.float32)
    L = DEPTH
    return {
        "x_prompt": nrm(0, (BATCH, SEQ, D_MODEL), 1.0),
        "x_sample": nrm(1, (DEC_BATCH, DEC_SEQ, D_MODEL), 1.0),
        "state_mlstm_C": nrm(2, (L, DEC_BATCH, A_HEADS, A_DK, A_DV), 0.1),
        "state_mlstm_n": nrm(3, (L, DEC_BATCH, A_HEADS, A_DK), 0.1),
        "state_mlstm_m": nrm(4, (L, DEC_BATCH, A_HEADS), 0.5),
        "state_mlstm_conv": nrm(5, (L, DEC_BATCH, A_CONV - 1, CONV_COLS), 1.0),
        "state_rwkv_S": nrm(6, (L, DEC_BATCH, B_HEADS, B_HEAD, B_HEAD), 0.1),
        "state_rwkv_shift": nrm(7, (L, DEC_BATCH, 1, SHIFT_COLS), 1.0),
        "meta_tokens": nrm(8, (N_META, D_MODEL), 1.0),
        "norm_mix_pre": 1.0 + nrm(9, (L, D_MODEL), 0.05),
        "norm_mix_post": 1.0 + nrm(10, (L, D_MODEL), 0.05),
        "norm_mlp_pre": 1.0 + nrm(11, (L, D_MODEL), 0.05),
        "norm_mlp_post": 1.0 + nrm(12, (L, D_MODEL), 0.05),
        "w_in": nrm(13, (L, D_MODEL, P_TOTAL), D_MODEL ** -0.5),
        "mlstm_conv_w": nrm(14, (L, A_CONV, CONV_COLS), A_CONV ** -0.5),
        "mlstm_conv_b": nrm(15, (L, CONV_COLS), 0.02),
        "mlstm_b_i": nrm(16, (L, A_HEADS), 0.5),
        "mlstm_b_f": 3.0 + nrm(17, (L, A_HEADS), 0.5),
        "mlstm_norm_w": 1.0 + nrm(18, (L, A_VW), 0.05),
        "rwkv_mu": jax.random.uniform(ks[19], (L, SHIFT_COLS), jnp.float32),
        "rwkv_w0": jax.random.uniform(ks[20], (L, B_W), jnp.float32, -6.0, 0.0),
        "rwkv_w2": nrm(21, (L, LORA_DECAY, B_W), 0.1 * LORA_DECAY ** -0.5),
        "rwkv_a0": nrm(22, (L, B_W), 0.1),
        "rwkv_a2": nrm(23, (L, LORA_ICLR, B_W), 0.1 * LORA_ICLR ** -0.5),
        "rwkv_g2": nrm(24, (L, LORA_GATE, B_W), LORA_GATE ** -0.5),
        "rwkv_k_k": 0.85 + nrm(25, (L, B_W), 0.05),
        "rwkv_k_a": 1.0 + nrm(26, (L, B_W), 0.05),
        "rwkv_r_k": nrm(27, (L, B_HEADS, B_HEAD), 0.1),
        "rwkv_ln_w": 1.0 + nrm(28, (L, B_W), 0.05),
        "rwkv_ln_b": nrm(29, (L, B_W), 0.02),
        "w_out": nrm(30, (L, D_MODEL, D_MODEL), D_MODEL ** -0.5),
        "w_up": nrm(31, (L, D_MODEL, D_FF), D_MODEL ** -0.5),
        "w_down": nrm(32, (L, D_FF, D_MODEL), D_FF ** -0.5),
    }


def reference(x_prompt, x_sample, state_mlstm_C, state_mlstm_n, state_mlstm_m, state_mlstm_conv,
              state_rwkv_S, state_rwkv_shift, meta_tokens, norm_mix_pre, norm_mix_post, norm_mlp_pre,
              norm_mlp_post, w_in, mlstm_conv_w, mlstm_conv_b, mlstm_b_i, mlstm_b_f, mlstm_norm_w,
              rwkv_mu, rwkv_w0, rwkv_w2, rwkv_a0, rwkv_a2, rwkv_g2, rwkv_k_k, rwkv_k_a, rwkv_r_k,
              rwkv_ln_w, rwkv_ln_b, w_out, w_up, w_down):
    f32 = jnp.float32
    Bp, seq, _ = x_prompt.shape
    Bs, dseq, _ = x_sample.shape
    meta = jnp.broadcast_to(meta_tokens.astype(f32)[None], (Bp, N_META, D_MODEL))
    hp = jnp.concatenate([meta, x_prompt.astype(f32)], axis=1)
    hs = x_sample.astype(f32)
    seg_p = ((N_META, N_META), (seq, CHUNK))
    seg_s = ((dseq, dseq),)
    p_states = []
    s_states = []
    for l in range(DEPTH):
        p = tuple(t[l].astype(f32) for t in (
            norm_mix_pre, norm_mix_post, norm_mlp_pre, norm_mlp_post, w_in, mlstm_conv_w, mlstm_conv_b,
            mlstm_b_i, mlstm_b_f, mlstm_norm_w, rwkv_mu, rwkv_w0, rwkv_w2, rwkv_a0, rwkv_a2, rwkv_g2,
            rwkv_k_k, rwkv_k_a, rwkv_r_k, rwkv_ln_w, rwkv_ln_b, w_out, w_up, w_down))
        hp, sp = trunk_layer(
            hp,
            jnp.zeros((Bp, A_HEADS, A_DK, A_DV), f32),
            jnp.zeros((Bp, A_HEADS, A_DK), f32),
            jnp.zeros((Bp, A_HEADS), f32),
            jnp.zeros((Bp, A_CONV - 1, CONV_COLS), f32),
            jnp.zeros((Bp, B_HEADS, B_HEAD, B_HEAD), f32),
            jnp.zeros((Bp, 1, SHIFT_COLS), f32),
            p, seg_p)
        hs, ss = trunk_layer(
            hs,
            state_mlstm_C[l].astype(f32),
            state_mlstm_n[l].astype(f32),
            state_mlstm_m[l].astype(f32),
            state_mlstm_conv[l].astype(f32),
            state_rwkv_S[l].astype(f32),
            state_rwkv_shift[l].astype(f32),
            p, seg_s)
        p_states.append(sp)
        s_states.append(ss)
    y_prompt = hp[:, N_META:].astype(x_prompt.dtype)
    y_sample = hs.astype(x_sample.dtype)
    pC = jnp.stack([s[0] for s in p_states]).astype(state_mlstm_C.dtype)
    pn = jnp.stack([s[1] for s in p_states]).astype(state_mlstm_n.dtype)
    pm = jnp.stack([s[2] for s in p_states]).astype(state_mlstm_m.dtype)
    pconv = jnp.stack([s[3] for s in p_states]).astype(state_mlstm_conv.dtype)
    pS = jnp.stack([s[4] for s in p_states]).astype(state_rwkv_S.dtype)
    pshift = jnp.stack([s[5] for s in p_states]).astype(state_rwkv_shift.dtype)
    sC = jnp.stack([s[0] for s in s_states]).astype(state_mlstm_C.dtype)
    sn = jnp.stack([s[1] for s in s_states]).astype(state_mlstm_n.dtype)
    sm = jnp.stack([s[2] for s in s_states]).astype(state_mlstm_m.dtype)
    sconv = jnp.stack([s[3] for s in s_states]).astype(state_mlstm_conv.dtype)
    sS = jnp.stack([s[4] for s in s_states]).astype(state_rwkv_S.dtype)
    sshift = jnp.stack([s[5] for s in s_states]).astype(state_rwkv_shift.dtype)
    return (y_prompt, y_sample, pC, pn, pm, pconv, pS, pshift, sC, sn, sm, sconv, sS, sshift)
```

```python
import functools

import jax
import jax.numpy as jnp
from jax import lax
from jax.experimental import pallas as pl
from jax.experimental.pallas import tpu as pltpu

F32 = jnp.float32
BF16 = jnp.bfloat16

D_MODEL = 2048
N_META = 16
EPS = 1e-6
D_FF = 4 * D_MODEL
A_HEADS = 8
A_DK = 128
A_DV = D_MODEL // A_HEADS
A_QK = A_HEADS * A_DK
A_VW = A_HEADS * A_DV
A_CONV = 4
MLSTM_CHUNK = 64
B_HEAD = 64
B_HEADS = D_MODEL // B_HEAD
B_W = B_HEADS * B_HEAD
LORA_DECAY = max(32, int(round(1.8 * D_MODEL ** 0.5 / 32)) * 32)
LORA_ICLR = max(32, int(round(1.8 * D_MODEL ** 0.5 / 32)) * 32)
LORA_GATE = max(32, int(round(0.6 * D_MODEL ** 0.8 / 32)) * 32)
GN_EPS = 64e-5
CONV_COLS = 2 * A_QK
SHIFT_COLS = 3 * B_W + LORA_DECAY + LORA_ICLR + LORA_GATE

LANES = 128
SUBLANES = 8

MAIN_COLS = 8 * D_MODEL
TAIL_WD, TAIL_AD, TAIL_GD, TAIL_IF = 0, LANES, 2 * LANES, 2 * LANES + LORA_GATE
TAIL_COLS = TAIL_IF + LANES
RWKV_CHUNK = 64
RWKV_PAIRS_PER_STEP = 4
N_PAIRS = B_HEADS // 2


def _cparams(semantics, vmem_mib):
    return pltpu.CompilerParams(dimension_semantics=semantics, vmem_limit_bytes=vmem_mib << 20)


def _softplus(x):
    return jnp.maximum(x, 0.0) + jnp.log1p(jnp.exp(-jnp.abs(x)))


def _dot(a, b):
    return jnp.dot(a.astype(BF16), b.astype(BF16), preferred_element_type=F32)


def _dot_nt(a, b):
    return lax.dot_general(a.astype(BF16), b.astype(BF16), (((1,), (1,)), ((), ())),
                           preferred_element_type=F32)


def _dot_tn(a, b):
    return lax.dot_general(a.astype(BF16), b.astype(BF16), (((0,), (0,)), ((), ())),
                           preferred_element_type=F32)


def _split_bf16(x, pieces):
    out = []
    for _ in range(pieces - 1):
        p = x.astype(BF16)
        out.append(p)
        x = x - p.astype(F32)
    out.append(x.astype(BF16))
    return out


def _dot_exact_lhs(m, x, pieces=3):
    mb = m.astype(BF16)
    acc = None
    for p in _split_bf16(x, pieces):
        t = jnp.dot(mb, p, preferred_element_type=F32)
        acc = t if acc is None else acc + t
    return acc


def _dot_exact_rhs(x, m, pieces=3):
    mb = m.astype(BF16)
    acc = None
    for p in _split_bf16(x, pieces):
        t = jnp.dot(p, mb, preferred_element_type=F32)
        acc = t if acc is None else acc + t
    return acc


def _norm_matmul_body(x_ref, g_ref, w_ref, o_ref, hn_ref, *, row_chunk):
    @pl.when(pl.program_id(1) == 0)
    def _():
        def body(c, carry):
            rows = pl.ds(pl.multiple_of(c * row_chunk, row_chunk), row_chunk)
            x = x_ref[rows, :]
            ms = jnp.mean(x * x, axis=-1, keepdims=True)
            hn_ref[rows, :] = (x * lax.rsqrt(ms + EPS) * g_ref[...]).astype(BF16)
            return carry
        lax.fori_loop(0, x_ref.shape[0] // row_chunk, body, 0)

    o_ref[...] = jnp.dot(hn_ref[...], w_ref[...], preferred_element_type=F32)


def _norm_matmul(x, g, w, tm, tn, name):
    n, d = x.shape
    p = w.shape[1]
    row_chunk = min(tm, 128)
    return pl.pallas_call(
        functools.partial(_norm_matmul_body, row_chunk=row_chunk),
        grid=(n // tm, p // tn),
        in_specs=[pl.BlockSpec((tm, d), lambda i, j: (i, 0)),
                  pl.BlockSpec((1, d), lambda i, j: (0, 0)),
                  pl.BlockSpec((d, tn), lambda i, j: (0, j))],
        out_specs=pl.BlockSpec((tm, tn), lambda i, j: (i, j)),
        out_shape=jax.ShapeDtypeStruct((n, p), F32),
        scratch_shapes=[pltpu.VMEM((tm, d), BF16)],
        compiler_params=_cparams(("parallel", "arbitrary"), 48),
        name=name,
    )(x, g, w)


def _shift_rows(u, e, sh):
    ext = jnp.concatenate([e, u], axis=0)
    return pltpu.roll(ext, sh, 0)[SUBLANES:, :]


def _conv_body(u_ref, e_ref, w_ref, b_ref, q_ref, k_ref, *, seg):
    w = w_ref[...]
    for s in range(u_ref.shape[0] // seg):
        rows = slice(s * seg, (s + 1) * seg)
        u = u_ref[rows, :]
        e = e_ref[s]
        acc = b_ref[...] + u * w[A_CONV - 1:A_CONV, :]
        for sh in range(1, A_CONV):
            acc = acc + _shift_rows(u, e, sh) * w[A_CONV - 1 - sh:A_CONV - sh, :]
        y = acc * jax.nn.sigmoid(acc)
        q_ref[rows, :] = y[:, :A_QK]
        k_ref[rows, :] = y[:, A_QK:] * (A_DK ** -0.5)


def _conv_silu(proj_main, e_conv, conv_w, conv_b, tm, seg):
    n = proj_main.shape[0]
    return pl.pallas_call(
        functools.partial(_conv_body, seg=seg),
        grid=(n // tm,),
        in_specs=[pl.BlockSpec((tm, CONV_COLS), lambda i: (i, 0)),
                  pl.BlockSpec((tm // seg, SUBLANES, CONV_COLS), lambda i: (i, 0, 0)),
                  pl.BlockSpec((A_CONV, CONV_COLS), lambda i: (0, 0)),
                  pl.BlockSpec((1, CONV_COLS), lambda i: (0, 0))],
        out_specs=[pl.BlockSpec((tm, A_QK), lambda i: (i, 0)),
                   pl.BlockSpec((tm, A_QK), lambda i: (i, 0))],
        out_shape=[jax.ShapeDtypeStruct((n, A_QK), F32)] * 2,
        compiler_params=_cparams(("parallel",), 32),
        name="conv_silu",
    )(proj_main, e_conv, conv_w, conv_b)


def _head_block_mask(scale):
    a = lax.broadcasted_iota(jnp.int32, (LANES, LANES), 0) // B_HEAD
    b = lax.broadcasted_iota(jnp.int32, (LANES, LANES), 1) // B_HEAD
    return jnp.where(a == b, scale, 0.0).astype(F32)


def _rwkv_prep_body(um_ref, ul_ref, em_ref, el_ref, mum_ref, mul_ref, w0_ref, w2_ref, a0_ref,
                    a2_ref, g2_ref, kk_ref, ka_ref,
                    r_out, ld_out, k_out, v_out, kn_out, bi_out, g_out,
                    kb_s, xl_s, *, seg):
    for s in range(um_ref.shape[0] // seg):
        rows = slice(s * seg, (s + 1) * seg)
        u = um_ref[rows, :]
        xs = u + mum_ref[...] * (_shift_rows(u, em_ref[s], 1) - u)
        r_out[rows, :] = xs[:, :B_W]
        kb_s[rows, :] = xs[:, B_W:2 * B_W]
        v_out[rows, :] = xs[:, 2 * B_W:]
        ul = ul_ref[rows, :]
        xl_s[rows, :] = ul + mul_ref[...] * (_shift_rows(ul, el_ref[s], 1) - ul)

    xl = xl_s[...]
    lw = _dot(jnp.tanh(xl[:, TAIL_WD:TAIL_WD + LANES]), w2_ref[...])
    w_log = -_softplus(-(w0_ref[...] + lw)) - 0.5
    ld_out[...] = -jnp.exp(w_log)
    a = jax.nn.sigmoid(a0_ref[...] + _dot(xl[:, TAIL_AD:TAIL_AD + LANES], a2_ref[...]))
    g_out[...] = _dot(jax.nn.sigmoid(xl[:, TAIL_GD:TAIL_GD + LORA_GATE]), g2_ref[...])
    kb = kb_s[...]
    k_out[...] = kb * (1.0 + (a - 1.0) * ka_ref[...])
    kk = kb * kk_ref[...]
    ones_blk = _head_block_mask(1.0)
    for p in range(N_PAIRS):
        cols = slice(p * LANES, (p + 1) * LANES)
        kp = kk[:, cols]
        ssq = _dot_exact_rhs(kp * kp, ones_blk)
        kn = kp / jnp.maximum(jnp.sqrt(ssq), 1e-12)
        kn_out[:, cols] = kn
        bi_out[:, cols] = kn * a[:, cols]


def _rwkv_prep(proj_main, proj_tail, e_main, e_tail, mu_main, mu_tail, w0, w2p, a0, a2p, g2, k_k, k_a,
               tm, seg):
    n = proj_main.shape[0]
    row = lambda i: (i, 0)
    fixed = lambda i: (0, 0)
    wide = pl.BlockSpec((tm, B_W), row)
    return pl.pallas_call(
        functools.partial(_rwkv_prep_body, seg=seg),
        grid=(n // tm,),
        in_specs=[pl.BlockSpec((tm, 3 * B_W), lambda i: (i, 1)),
                  pl.BlockSpec((tm, TAIL_IF), row),
                  pl.BlockSpec((tm // seg, SUBLANES, 3 * B_W), lambda i: (i, 0, 0)),
                  pl.BlockSpec((tm // seg, SUBLANES, TAIL_IF), lambda i: (i, 0, 0)),
                  pl.BlockSpec((1, 3 * B_W), fixed),
                  pl.BlockSpec((1, TAIL_IF), fixed),
                  pl.BlockSpec((1, B_W), fixed),
                  pl.BlockSpec((LANES, B_W), fixed),
                  pl.BlockSpec((1, B_W), fixed),
                  pl.BlockSpec((LANES, B_W), fixed),
                  pl.BlockSpec((LORA_GATE, B_W), fixed),
                  pl.BlockSpec((1, B_W), fixed),
                  pl.BlockSpec((1, B_W), fixed)],
        out_specs=[wide] * 7,
        out_shape=[jax.ShapeDtypeStruct((n, B_W), F32)] * 7,
        scratch_shapes=[pltpu.VMEM((tm, B_W), F32), pltpu.VMEM((tm, TAIL_IF), F32)],
        compiler_params=_cparams(("parallel",), 48),
        name="rwkv_prep",
    )(proj_main, proj_tail, e_main, e_tail, mu_main, mu_tail, w0, w2p, a0, a2p, g2, k_k, k_a)


def _mlstm_body(q_ref, k_ref, v_ref, gc_ref, gr_ref, brow_ref, bcol_ref, nw_ref, c0_ref, n0_ref, m0_ref,
                h_ref, cn_ref, nn_ref, mn_ref, c_s, n_s, m_s, *, T):
    c = pl.program_id(1)

    @pl.when(c == 0)
    def _():
        c_s[...] = c0_ref[0]
        n_s[...] = n0_ref[0]
        m_s[...] = m0_ref[0]

    gcol = gc_ref[...] + brow_ref[...]
    grow = gr_ref[0] + bcol_ref[...]
    lf_col = -_softplus(-gcol)
    ig_row = grow[:A_HEADS, :]
    lf_row = -_softplus(-grow[A_HEADS:, :])
    ri = lax.broadcasted_iota(jnp.int32, (T, T), 0)
    ci = lax.broadcasted_iota(jnp.int32, (T, T), 1)
    tri = ci <= ri
    b_col = _dot_exact_lhs(jnp.where(tri, 1.0, 0.0), lf_col)
    b_row = _dot_exact_rhs(lf_row, jnp.where(ri <= ci, 1.0, 0.0))
    m_prev = m_s[...]
    lane = lax.broadcasted_iota(jnp.int32, (1, LANES), 1)
    m_next = m_prev
    for h in range(A_HEADS):
        fl = A_HEADS + h
        bc = b_col[:, fl:fl + 1]
        mh = m_prev[:, fl:fl + 1]
        br = b_row[h:h + 1, :]
        igr = ig_row[h:h + 1, :]
        igc = gcol[:, h:h + 1]
        d = jnp.where(tri, bc - br + igr, -jnp.inf)
        inter = bc + mh
        m_t = jnp.maximum(inter, jnp.max(d, axis=-1, keepdims=True))
        qh = q_ref[:, h * A_DK:(h + 1) * A_DK]
        kh = k_ref[:, h * A_DK:(h + 1) * A_DK]
        vh = v_ref[:, h * A_DV:(h + 1) * A_DV]
        s = _dot_nt(qh, kh) * jnp.exp(d - m_t)
        e = jnp.exp(inter - m_t)
        ch = c_s[h]
        nh = n_s[h:h + 1, :]
        num = _dot(s, vh) + e * _dot(qh, ch)
        den = jnp.sum(s, axis=-1, keepdims=True) + e * jnp.sum(qh * nh, axis=-1, keepdims=True)
        hh = num / jnp.maximum(jnp.abs(den), jnp.exp(-m_t))
        hh = hh * lax.rsqrt(jnp.mean(hh * hh, axis=-1, keepdims=True) + EPS)
        h_ref[:, h * A_DV:(h + 1) * A_DV] = hh * nw_ref[:, h * A_DV:(h + 1) * A_DV]
        b_end = bc[T - 1:T, :]
        g_row = b_end - br + igr
        m_new = jnp.maximum(b_end + mh, jnp.max(g_row, axis=-1, keepdims=True))
        carry = jnp.exp(b_end + mh - m_new)
        wk = kh * jnp.exp(b_end - bc + igc - m_new)
        c_s[h] = carry * ch + _dot_tn(wk, vh)
        n_s[h:h + 1, :] = carry * nh + jnp.sum(wk, axis=0, keepdims=True)
        m_next = jnp.where(lane == fl, m_new, m_next)
    m_s[...] = m_next

    @pl.when(c == pl.num_programs(1) - 1)
    def _():
        cn_ref[0] = c_s[...]
        nn_ref[0] = n_s[...]
        mn_ref[0] = m_s[...]


def _mlstm(q, k, proj_main, proj_tail, gates_row, bias_row, bias_col, norm_w, c0, n0, m0, *,
           T, n_seq, n_chunk, row_off):
    n = q.shape[0]
    blk0 = row_off // T
    row = lambda s, c: (blk0 + s * n_chunk + c, 0)
    fixed = lambda s, c: (0, 0)
    return pl.pallas_call(
        functools.partial(_mlstm_body, T=T),
        grid=(n_seq, n_chunk),
        in_specs=[pl.BlockSpec((T, A_QK), row),
                  pl.BlockSpec((T, A_QK), row),
                  pl.BlockSpec((T, A_VW), lambda s, c: (blk0 + s * n_chunk + c, 1)),
                  pl.BlockSpec((T, LANES), lambda s, c: (blk0 + s * n_chunk + c, TAIL_IF // LANES)),
                  pl.BlockSpec((1, 2 * A_HEADS, T), lambda s, c: (s * n_chunk + c, 0, 0)),
                  pl.BlockSpec((1, LANES), fixed),
                  pl.BlockSpec((2 * A_HEADS, 1), fixed),
                  pl.BlockSpec((1, A_VW), fixed),
                  pl.BlockSpec((1, A_HEADS, A_DK, A_DV), lambda s, c: (s, 0, 0, 0)),
                  pl.BlockSpec((1, A_HEADS, A_DK), lambda s, c: (s, 0, 0)),
                  pl.BlockSpec((1, 1, LANES), lambda s, c: (s, 0, 0))],
        out_specs=[pl.BlockSpec((T, A_VW), lambda s, c: (s * n_chunk + c, 0)),
                   pl.BlockSpec((1, A_HEADS, A_DK, A_DV), lambda s, c: (s, 0, 0, 0)),
                   pl.BlockSpec((1, A_HEADS, A_DK), lambda s, c: (s, 0, 0)),
                   pl.BlockSpec((1, 1, LANES), lambda s, c: (s, 0, 0))],
        out_shape=[jax.ShapeDtypeStruct((n_seq * n_chunk * T, A_VW), F32),
                   jax.ShapeDtypeStruct((n_seq, A_HEADS, A_DK, A_DV), F32),
                   jax.ShapeDtypeStruct((n_seq, A_HEADS, A_DK), F32),
                   jax.ShapeDtypeStruct((n_seq, 1, LANES), F32)],
        scratch_shapes=[pltpu.VMEM((A_HEADS, A_DK, A_DV), F32),
                        pltpu.VMEM((A_HEADS, A_DK), F32),
                        pltpu.VMEM((1, LANES), F32)],
        compiler_params=_cparams(("arbitrary", "arbitrary"), 32),
        name="mlstm_chunk",
    )(q, k, proj_main, proj_tail, gates_row, bias_row, bias_col, norm_w, c0, n0, m0)


def _rwkv_pair(r, ld, k, v, kk, bi, g, rk, lnw, lnb, S, T):
    lane = lax.broadcasted_iota(jnp.int32, (1, LANES), 1)
    lo = lane < B_HEAD
    ri = lax.broadcasted_iota(jnp.int32, (T, T), 0)
    ci = lax.broadcasted_iota(jnp.int32, (T, T), 1)
    cum = _dot_exact_lhs(jnp.where(ci <= ri, 1.0, 0.0), ld)
    ewt = jnp.exp(cum[T - 1:T, :])
    Rt = r * jnp.exp(cum)
    At = -kk * jnp.exp(cum - ld)
    einv = jnp.exp(-cum)
    Bt = bi * einv
    Kt = k * einv
    Bh = Bt * ewt
    Kh = Kt * ewt
    zero = jnp.zeros((T, LANES), F32)
    At1 = jnp.where(lo, At, 0.0)
    At2 = jnp.where(lo, 0.0, At)
    Rt1 = jnp.where(lo, Rt, 0.0)
    Rt2 = jnp.where(lo, 0.0, Rt)
    V1 = jnp.where(lo, v, 0.0)
    V2 = jnp.where(lo, 0.0, v)
    BK = jnp.concatenate([Bt, Kt], axis=0)
    KB = jnp.concatenate([Kt, Bt], axis=0)
    GA1 = _dot_nt(At1, BK)
    GA2 = _dot_nt(At2, KB)
    GR1 = _dot_nt(Rt1, BK)
    GR2 = _dot_nt(Rt2, BK)
    r2 = lax.broadcasted_iota(jnp.int32, (T, 2 * T), 0)
    c2 = lax.broadcasted_iota(jnp.int32, (T, 2 * T), 1)
    left = c2 < T
    cj = jnp.where(left, c2, c2 - T)
    strict2 = cj < r2
    incl2 = cj <= r2
    LL1 = jnp.where(strict2, GA1, 0.0)
    LL2 = jnp.where(strict2, GA2, 0.0)
    P1 = jnp.where(incl2, GR1, 0.0)
    P2 = jnp.where(incl2, GR2, 0.0)
    Lbd = jnp.concatenate([jnp.where(left, LL1, 0.0), jnp.where(left, 0.0, LL2)], axis=0)
    e0 = lax.broadcasted_iota(jnp.int32, (2 * T, 2 * T), 0)
    e1 = lax.broadcasted_iota(jnp.int32, (2 * T, 2 * T), 1)
    inv = jnp.where(e0 == e1, 1.0, 0.0) + Lbd
    pw = Lbd
    span = 2
    while span < T:
        pw = _dot(pw, pw)
        inv = inv + _dot(inv, pw)
        span *= 2
    IV = inv[:T, :] + inv[T:, :]
    X0 = _dot(LL1, jnp.concatenate([zero, V1], axis=0)) + _dot(LL2, jnp.concatenate([V2, zero], axis=0))
    Ah = _dot(IV, jnp.concatenate([At1, At2], axis=0))
    U0 = _dot(IV, jnp.concatenate([jnp.where(lo, X0, 0.0), jnp.where(lo, 0.0, X0)], axis=0))
    Ah1 = jnp.where(lo, Ah, 0.0)
    Ah2 = jnp.where(lo, 0.0, Ah)
    U01 = jnp.where(lo, U0, 0.0)
    U02 = jnp.where(lo, 0.0, U0)
    Rh = Rt + _dot(P1, jnp.concatenate([Ah1, zero], axis=0)) + _dot(P2, jnp.concatenate([Ah2, zero], axis=0))
    O0 = _dot(P1, jnp.concatenate([U01, V1], axis=0)) + _dot(P2, jnp.concatenate([U02, V2], axis=0))
    same_head = _head_block_mask(1.0) > 0.5
    Mab = jnp.where(same_head, _dot_tn(Ah, Bh), 0.0)
    N0 = jnp.where(same_head, _dot_tn(U0, Bh) + _dot_tn(v, Kh), 0.0)
    o = _dot_nt(Rh, S) + O0
    S_new = S * ewt + _dot(S, Mab) + N0
    mean_blk = _head_block_mask(1.0 / B_HEAD)
    oc = o - _dot_exact_rhs(o, mean_blk, 2)
    var = _dot_exact_rhs(oc * oc, mean_blk, 2)
    on = oc * lax.rsqrt(var + GN_EPS) * lnw + lnb
    bonus = _dot_exact_rhs(r * k * rk, _head_block_mask(1.0), 2) * v
    return (on + bonus) * g, S_new


def _rwkv_body(r_ref, ld_ref, k_ref, v_ref, kk_ref, bi_ref, g_ref, rk_ref, lnw_ref, lnb_ref, s0_ref,
               hb_ref, sn_ref, s_s, *, T):
    c = pl.program_id(2)

    @pl.when(c == 0)
    def _():
        s_s[...] = s0_ref[0]

    for p in range(RWKV_PAIRS_PER_STEP):
        cols = slice(p * LANES, (p + 1) * LANES)
        hb, s_new = _rwkv_pair(r_ref[:, cols], ld_ref[:, cols], k_ref[:, cols], v_ref[:, cols],
                               kk_ref[:, cols], bi_ref[:, cols], g_ref[:, cols], rk_ref[:, cols],
                               lnw_ref[:, cols], lnb_ref[:, cols], s_s[p], T)
        hb_ref[:, cols] = hb
        s_s[p] = s_new

    @pl.when(c == pl.num_programs(2) - 1)
    def _():
        sn_ref[0] = s_s[...]


def _rwkv(r, ld, k, v, kk, bi, g, rk, lnw, lnb, s0, *, T, n_seq, n_chunk, row_off):
    assert T <= RWKV_CHUNK
    n = r.shape[0]
    blk0 = row_off // T
    gp = RWKV_PAIRS_PER_STEP
    wide = pl.BlockSpec((T, gp * LANES), lambda s, pg, c: (blk0 + s * n_chunk + c, pg))
    wide_out = pl.BlockSpec((T, gp * LANES), lambda s, pg, c: (s * n_chunk + c, pg))
    vec = pl.BlockSpec((1, gp * LANES), lambda s, pg, c: (0, pg))
    st = pl.BlockSpec((1, gp, LANES, LANES), lambda s, pg, c: (s, pg, 0, 0))
    return pl.pallas_call(
        functools.partial(_rwkv_body, T=T),
        grid=(n_seq, N_PAIRS // gp, n_chunk),
        in_specs=[wide] * 7 + [vec] * 3 + [st],
        out_specs=[wide_out, st],
        out_shape=[jax.ShapeDtypeStruct((n_seq * n_chunk * T, B_W), F32),
                   jax.ShapeDtypeStruct((n_seq, N_PAIRS, LANES, LANES), F32)],
        scratch_shapes=[pltpu.VMEM((gp, LANES, LANES), F32)],
        compiler_params=_cparams(("arbitrary", "arbitrary", "arbitrary"), 32),
        name="rwkv_chunk",
    )(r, ld, k, v, kk, bi, g, rk, lnw, lnb, s0)


def _merge_out_body(ha_ref, hb_ref, o_ref, ga_ref, gb_ref, x_ref, w_ref, g_ref, y_ref):
    mixed = (jax.nn.sigmoid(ga_ref[...]) * (ha_ref[...] * jax.nn.sigmoid(o_ref[...]))
             + jax.nn.sigmoid(gb_ref[...]) * hb_ref[...])
    y = jnp.dot(mixed.astype(BF16), w_ref[...], preferred_element_type=F32)
    ms = jnp.mean(y * y, axis=-1, keepdims=True)
    y_ref[...] = x_ref[...] + y * lax.rsqrt(ms + EPS) * g_ref[...]


def _merge_out(h_a, h_b, proj_main, x, w_out, g_post, tm):
    n = x.shape[0]
    row = lambda i: (i, 0)
    wide = pl.BlockSpec((tm, D_MODEL), row)
    return pl.pallas_call(
        _merge_out_body,
        grid=(n // tm,),
        in_specs=[wide, wide,
                  pl.BlockSpec((tm, D_MODEL), lambda i: (i, 2)),
                  pl.BlockSpec((tm, D_MODEL), lambda i: (i, 6)),
                  pl.BlockSpec((tm, D_MODEL), lambda i: (i, 7)),
                  wide,
                  pl.BlockSpec((D_MODEL, D_MODEL), lambda i: (0, 0)),
                  pl.BlockSpec((1, D_MODEL), lambda i: (0, 0))],
        out_specs=wide,
        out_shape=jax.ShapeDtypeStruct((n, D_MODEL), F32),
        compiler_params=_cparams(("parallel",), 56),
        name="merge_out_proj",
    )(h_a, h_b, proj_main, proj_main, proj_main, x, w_out, g_post)


def _mlp_body(x_ref, g1_ref, wu_ref, wd_ref, g2_ref, y_ref, hn_ref, acc_ref, *, row_chunk):
    j = pl.program_id(1)

    @pl.when(j == 0)
    def _():
        def body(c, carry):
            rows = pl.ds(pl.multiple_of(c * row_chunk, row_chunk), row_chunk)
            x = x_ref[rows, :]
            ms = jnp.mean(x * x, axis=-1, keepdims=True)
            hn_ref[rows, :] = (x * lax.rsqrt(ms + EPS) * g1_ref[...]).astype(BF16)
            return carry
        lax.fori_loop(0, x_ref.shape[0] // row_chunk, body, 0)
        acc_ref[...] = jnp.zeros_like(acc_ref)

    u = jnp.dot(hn_ref[...], wu_ref[...], preferred_element_type=F32)
    a = jnp.square(jnp.maximum(u, 0.0))
    acc_ref[...] += jnp.dot(a.astype(BF16), wd_ref[...], preferred_element_type=F32)

    @pl.when(j == pl.num_programs(1) - 1)
    def _():
        def body(c, carry):
            rows = pl.ds(pl.multiple_of(c * row_chunk, row_chunk), row_chunk)
            ff = acc_ref[rows, :]
            ms = jnp.mean(ff * ff, axis=-1, keepdims=True)
            y_ref[rows, :] = x_ref[rows, :] + ff * lax.rsqrt(ms + EPS) * g2_ref[...]
            return carry
        lax.fori_loop(0, x_ref.shape[0] // row_chunk, body, 0)


def _mlp(x, g_pre, w_up, w_down, g_post, tm, tf):
    n, d = x.shape
    f = w_up.shape[1]
    return pl.pallas_call(
        functools.partial(_mlp_body, row_chunk=min(tm, 128)),
        grid=(n // tm, f // tf),
        in_specs=[pl.BlockSpec((tm, d), lambda i, j: (i, 0)),
                  pl.BlockSpec((1, d), lambda i, j: (0, 0)),
                  pl.BlockSpec((d, tf), lambda i, j: (0, j)),
                  pl.BlockSpec((tf, d), lambda i, j: (j, 0)),
                  pl.BlockSpec((1, d), lambda i, j: (0, 0))],
        out_specs=pl.BlockSpec((tm, d), lambda i, j: (i, 0)),
        out_shape=jax.ShapeDtypeStruct((n, d), F32),
        scratch_shapes=[pltpu.VMEM((tm, d), BF16), pltpu.VMEM((tm, d), F32)],
        compiler_params=_cparams(("parallel", "arbitrary"), 56),
        name="mlp",
    )(x, g_pre, w_up, w_down, g_post)


def _pad_cols(a, width):
    return jnp.pad(a, ((0, 0), (0, width - a.shape[1])))


def _tail_layout(wd, ad, gd, gates=None):
    parts = [_pad_cols(wd, LANES), _pad_cols(ad, LANES), gd]
    if gates is not None:
        parts.append(_pad_cols(gates, LANES))
    return jnp.concatenate(parts, axis=1)


def _boundary_blocks(prev_rows):
    k = prev_rows.shape[1]
    return jnp.pad(prev_rows, ((0, 0), (SUBLANES - k, 0), (0, 0)))


def _pack_state(s):
    n = s.shape[0]
    s = s.reshape(n, N_PAIRS, 2, B_HEAD, B_HEAD)
    z = jnp.zeros((n, N_PAIRS, B_HEAD, B_HEAD), s.dtype)
    top = jnp.concatenate([s[:, :, 0], z], axis=-1)
    bot = jnp.concatenate([z, s[:, :, 1]], axis=-1)
    return jnp.concatenate([top, bot], axis=-2)


def _unpack_state(sp):
    n = sp.shape[0]
    a = sp[:, :, :B_HEAD, :B_HEAD]
    b = sp[:, :, B_HEAD:, B_HEAD:]
    return jnp.stack([a, b], axis=2).reshape(n, B_HEADS, B_HEAD, B_HEAD)


def _gates_row_form(proj_tail, T):
    g = proj_tail[:, TAIL_IF:TAIL_IF + 2 * A_HEADS]
    return jnp.swapaxes(g.reshape(-1, T, 2 * A_HEADS), 1, 2)


def kernel(x_prompt, x_sample, state_mlstm_C, state_mlstm_n, state_mlstm_m, state_mlstm_conv, state_rwkv_S, state_rwkv_shift, meta_tokens, norm_mix_pre, norm_mix_post, norm_mlp_pre, norm_mlp_post, w_in, mlstm_conv_w, mlstm_conv_b, mlstm_b_i, mlstm_b_f, mlstm_norm_w, rwkv_mu, rwkv_w0, rwkv_w2, rwkv_a0, rwkv_a2, rwkv_g2, rwkv_k_k, rwkv_k_a, rwkv_r_k, rwkv_ln_w, rwkv_ln_b, w_out, w_up, w_down):
    n_s_seq, s_len, _ = x_sample.shape
    _, p_len, _ = x_prompt.shape
    n_s = n_s_seq * s_len
    n_all = n_s + p_len
    seg = s_len

    w = w_in[0]
    c_if = CONV_COLS + 2 * A_VW
    c_sh = c_if + 2 * A_HEADS
    c_lora = c_sh + 3 * B_W
    c_gate = c_sh + SHIFT_COLS
    w_main = jnp.concatenate([w[:, :c_if], w[:, c_sh:c_lora], w[:, c_gate:]], axis=1).astype(BF16)
    w_tail = _tail_layout(w[:, c_lora:c_lora + LORA_DECAY],
                          w[:, c_lora + LORA_DECAY:c_lora + LORA_DECAY + LORA_ICLR],
                          w[:, c_lora + LORA_DECAY + LORA_ICLR:c_gate],
                          w[:, c_if:c_sh]).astype(BF16)
    mu = rwkv_mu[0][None, :]
    mu_main = mu[:, :3 * B_W]
    mu_tail = _tail_layout(mu[:, 3 * B_W:3 * B_W + LORA_DECAY],
                           mu[:, 3 * B_W + LORA_DECAY:3 * B_W + LORA_DECAY + LORA_ICLR],
                           mu[:, 3 * B_W + LORA_DECAY + LORA_ICLR:])
    w2p = jnp.pad(rwkv_w2[0], ((0, LANES - LORA_DECAY), (0, 0))).astype(BF16)
    a2p = jnp.pad(rwkv_a2[0], ((0, LANES - LORA_ICLR), (0, 0))).astype(BF16)
    g2 = rwkv_g2[0].astype(BF16)
    g_pre = norm_mix_pre[0][None, :]
    bias_row = _pad_cols(jnp.concatenate([mlstm_b_i[0], mlstm_b_f[0]])[None, :], LANES)
    bias_col = jnp.concatenate([mlstm_b_i[0], mlstm_b_f[0]])[:, None]
    norm_w = mlstm_norm_w[0][None, :]
    conv_w = mlstm_conv_w[0]
    conv_b = mlstm_conv_b[0][None, :]
    w0 = rwkv_w0[0][None, :]
    a0 = rwkv_a0[0][None, :]
    k_k = rwkv_k_k[0][None, :]
    k_a = rwkv_k_a[0][None, :]
    r_k = rwkv_r_k[0].reshape(1, B_W)
    ln_w = rwkv_ln_w[0][None, :]
    ln_b = rwkv_ln_b[0][None, :]

    def project(x_rows, tm):
        pm = _norm_matmul(x_rows, g_pre, w_main, tm, 1024, "in_proj_main")
        pt = _norm_matmul(x_rows, g_pre, w_tail, tm, TAIL_COLS, "in_proj_tail")
        return pm, pt

    xm = meta_tokens.astype(F32)
    pm_m, pt_m = project(xm, N_META)
    zeros_e = lambda width: jnp.zeros((1, SUBLANES, width), F32)
    q_m, k_m = _conv_silu(pm_m, zeros_e(CONV_COLS), conv_w, conv_b, N_META, N_META)
    rw_m = _rwkv_prep(pm_m, pt_m, zeros_e(3 * B_W), zeros_e(TAIL_IF), mu_main, mu_tail, w0, w2p, a0, a2p,
                      g2, k_k, k_a, N_META, N_META)
    _, c_m, n_m, m_m = _mlstm(q_m, k_m, pm_m, pt_m, _gates_row_form(pt_m, N_META), bias_row, bias_col, norm_w,
                              jnp.zeros((1, A_HEADS, A_DK, A_DV), F32), jnp.zeros((1, A_HEADS, A_DK), F32),
                              jnp.zeros((1, 1, LANES), F32), T=N_META, n_seq=1, n_chunk=1, row_off=0)
    _, s_m = _rwkv(*rw_m, r_k, ln_w, ln_b, jnp.zeros((1, N_PAIRS, LANES, LANES), F32),
                   T=N_META, n_seq=1, n_chunk=1, row_off=0)

    x_all = jnp.concatenate([x_sample.reshape(n_s, D_MODEL), x_prompt[0]], axis=0).astype(F32)
    pm, pt = project(x_all, 1024)

    n_pseg = p_len // seg

    def prev_rows(cols_main, lo, hi, first, k):
        body = cols_main[n_s:].reshape(n_pseg, seg, hi - lo)[:-1, seg - k:, :]
        return jnp.concatenate([first[None], body], axis=0)

    conv_prev_s = state_mlstm_conv[0].astype(F32)
    conv_prev_p = prev_rows(pm[:, :CONV_COLS], 0, CONV_COLS, pm_m[N_META - (A_CONV - 1):, :CONV_COLS], A_CONV - 1)
    e_conv = _boundary_blocks(jnp.concatenate([conv_prev_s, conv_prev_p], axis=0))
    sh = state_rwkv_shift[0].astype(F32)
    sh_main_s = sh[:, :, :3 * B_W]
    sh_tail_s = _tail_layout(sh[:, 0, 3 * B_W:3 * B_W + LORA_DECAY],
                             sh[:, 0, 3 * B_W + LORA_DECAY:3 * B_W + LORA_DECAY + LORA_ICLR],
                             sh[:, 0, 3 * B_W + LORA_DECAY + LORA_ICLR:])[:, None, :]
    sh_main_p = prev_rows(pm[:, 3 * B_W:6 * B_W], 0, 3 * B_W, pm_m[N_META - 1:, 3 * B_W:6 * B_W], 1)
    sh_tail_p = prev_rows(pt[:, :TAIL_IF], 0, TAIL_IF, pt_m[N_META - 1:, :TAIL_IF], 1)
    e_main = _boundary_blocks(jnp.concatenate([sh_main_s, sh_main_p], axis=0))
    e_tail = _boundary_blocks(jnp.concatenate([sh_tail_s, sh_tail_p], axis=0))

    q, k = _conv_silu(pm, e_conv, conv_w, conv_b, 256, seg)
    rw = _rwkv_prep(pm, pt, e_main, e_tail, mu_main, mu_tail, w0, w2p, a0, a2p, g2, k_k, k_a, 128, seg)

    m0_s = jnp.pad(state_mlstm_m[0].astype(F32), ((0, 0), (A_HEADS, LANES - 2 * A_HEADS)))[:, None, :]
    gr_s = _gates_row_form(pt[:n_s], s_len)
    gr_p = _gates_row_form(pt[n_s:], MLSTM_CHUNK)
    ha_s, c_s, nn_s, mm_s = _mlstm(q, k, pm, pt, gr_s, bias_row, bias_col, norm_w,
                                   state_mlstm_C[0].astype(F32), state_mlstm_n[0].astype(F32), m0_s,
                                   T=s_len, n_seq=n_s_seq, n_chunk=1, row_off=0)
    ha_p, c_p, nn_p, mm_p = _mlstm(q, k, pm, pt, gr_p, bias_row, bias_col, norm_w, c_m, n_m, m_m,
                                   T=MLSTM_CHUNK, n_seq=1, n_chunk=p_len // MLSTM_CHUNK, row_off=n_s)
    h_a = jnp.concatenate([ha_s, ha_p], axis=0)

    hb_s, s_s = _rwkv(*rw, r_k, ln_w, ln_b, _pack_state(state_rwkv_S[0].astype(F32)),
                      T=s_len, n_seq=n_s_seq, n_chunk=1, row_off=0)
    hb_p, s_p = _rwkv(*rw, r_k, ln_w, ln_b, s_m,
                      T=RWKV_CHUNK, n_seq=1, n_chunk=p_len // RWKV_CHUNK, row_off=n_s)
    h_b = jnp.concatenate([hb_s, hb_p], axis=0)

    x1 = _merge_out(h_a, h_b, pm, x_all, w_out[0].astype(BF16), norm_mix_post[0][None, :], 256)
    x2 = _mlp(x1, norm_mlp_pre[0][None, :], w_up[0].astype(BF16), w_down[0].astype(BF16),
              norm_mlp_post[0][None, :], 512, 1024)

    y_sample = x2[:n_s].reshape(x_sample.shape).astype(x_sample.dtype)
    y_prompt = x2[n_s:][None].astype(x_prompt.dtype)

    def shift_state(rows_main, rows_tail):
        return jnp.concatenate([rows_main,
                                rows_tail[..., TAIL_WD:TAIL_WD + LORA_DECAY],
                                rows_tail[..., TAIL_AD:TAIL_AD + LORA_ICLR],
                                rows_tail[..., TAIL_GD:TAIL_GD + LORA_GATE]], axis=-1)

    k3 = A_CONV - 1
    conv_s = pm[:n_s, :CONV_COLS].reshape(n_s_seq, s_len, CONV_COLS)[:, s_len - k3:, :]
    conv_p = pm[n_all - k3:, :CONV_COLS][None]
    shift_s = shift_state(pm[:n_s, 3 * B_W:6 * B_W].reshape(n_s_seq, s_len, 3 * B_W)[:, s_len - 1:, :],
                          pt[:n_s, :TAIL_IF].reshape(n_s_seq, s_len, TAIL_IF)[:, s_len - 1:, :])
    shift_p = shift_state(pm[n_all - 1:, 3 * B_W:6 * B_W], pt[n_all - 1:, :TAIL_IF])[None]
    dt_c, dt_n, dt_m = state_mlstm_C.dtype, state_mlstm_n.dtype, state_mlstm_m.dtype
    dt_cv, dt_s, dt_sh = state_mlstm_conv.dtype, state_rwkv_S.dtype, state_rwkv_shift.dtype
    lanes_m = slice(A_HEADS, 2 * A_HEADS)
    return (y_prompt, y_sample,
            c_p[None].astype(dt_c), nn_p[None].astype(dt_n), mm_p[:, 0, lanes_m][None].astype(dt_m),
            conv_p[None].astype(dt_cv), _unpack_state(s_p)[None].astype(dt_s), shift_p[None].astype(dt_sh),
            c_s[None].astype(dt_c), nn_s[None].astype(dt_n), mm_s[:, 0, lanes_m][None].astype(dt_m),
            conv_s[None].astype(dt_cv), _unpack_state(s_s)[None].astype(dt_s), shift_s[None].astype(dt_sh))
```

```python
import functools

import jax
import jax.numpy as jnp
from jax import lax
from jax.experimental import pallas as pl
from jax.experimental.pallas import tpu as pltpu

F32 = jnp.float32
BF16 = jnp.bfloat16

D_MODEL = 2048
N_META = 16
EPS = 1e-6
D_FF = 4 * D_MODEL
A_HEADS = 8
A_DK = 128
A_DV = D_MODEL // A_HEADS
A_QK = A_HEADS * A_DK
A_VW = A_HEADS * A_DV
A_CONV = 4
MLSTM_CHUNK = 64
B_HEAD = 64
B_HEADS = D_MODEL // B_HEAD
B_W = B_HEADS * B_HEAD
LORA_DECAY = max(32, int(round(1.8 * D_MODEL ** 0.5 / 32)) * 32)
LORA_ICLR = max(32, int(round(1.8 * D_MODEL ** 0.5 / 32)) * 32)
LORA_GATE = max(32, int(round(0.6 * D_MODEL ** 0.8 / 32)) * 32)
GN_EPS = 64e-5
CONV_COLS = 2 * A_QK
SHIFT_COLS = 3 * B_W + LORA_DECAY + LORA_ICLR + LORA_GATE

LANES = 128
SUBLANES = 8

MAIN_COLS = 8 * D_MODEL
TAIL_WD, TAIL_AD, TAIL_GD, TAIL_IF = 0, LANES, 2 * LANES, 2 * LANES + LORA_GATE
TAIL_COLS = TAIL_IF + LANES
RWKV_CHUNK = 64
N_PAIRS = B_HEADS // 2


def _cparams(semantics, vmem_mib):
    return pltpu.CompilerParams(dimension_semantics=semantics, vmem_limit_bytes=vmem_mib << 20)


def _softplus(x):
    return jnp.maximum(x, 0.0) + jnp.log1p(jnp.exp(-jnp.abs(x)))


def _dot(a, b):
    return jnp.dot(a.astype(BF16), b.astype(BF16), preferred_element_type=F32)


def _dot_nt(a, b):
    return lax.dot_general(a.astype(BF16), b.astype(BF16), (((1,), (1,)), ((), ())),
                           preferred_element_type=F32)


def _dot_tn(a, b):
    return lax.dot_general(a.astype(BF16), b.astype(BF16), (((0,), (0,)), ((), ())),
                           preferred_element_type=F32)


def _each(f, *lists):
    return [f(*args) for args in zip(*lists)]


def _rows(*parts):
    return jnp.concatenate(parts, axis=0)


def _split_bf16(x, pieces):
    out = []
    for _ in range(pieces - 1):
        p = x.astype(BF16)
        out.append(p)
        x = x - p.astype(F32)
    out.append(x.astype(BF16))
    return out


def _dot_exact_lhs(m, x, pieces=3):
    mb = m.astype(BF16)
    acc = None
    for p in _split_bf16(x, pieces):
        t = jnp.dot(mb, p, preferred_element_type=F32)
        acc = t if acc is None else acc + t
    return acc


def _dot_exact_rhs(x, m, pieces=3):
    mb = m.astype(BF16)
    acc = None
    for p in _split_bf16(x, pieces):
        t = jnp.dot(p, mb, preferred_element_type=F32)
        acc = t if acc is None else acc + t
    return acc


def _norm_matmul_body(x_ref, g_ref, w_ref, o_ref, hn_ref, *, row_chunk):
    @pl.when(pl.program_id(1) == 0)
    def _():
        def body(c, carry):
            rows = pl.ds(pl.multiple_of(c * row_chunk, row_chunk), row_chunk)
            x = x_ref[rows, :]
            ms = jnp.mean(x * x, axis=-1, keepdims=True)
            hn_ref[rows, :] = (x * lax.rsqrt(ms + EPS) * g_ref[...]).astype(BF16)
            return carry
        lax.fori_loop(0, x_ref.shape[0] // row_chunk, body, 0)

    o_ref[...] = jnp.dot(hn_ref[...], w_ref[...], preferred_element_type=F32)


def _norm_matmul(x, g, w, tm, tn, name):
    n, d = x.shape
    p = w.shape[1]
    row_chunk = min(tm, 128)
    return pl.pallas_call(
        functools.partial(_norm_matmul_body, row_chunk=row_chunk),
        grid=(n // tm, p // tn),
        in_specs=[pl.BlockSpec((tm, d), lambda i, j: (i, 0)),
                  pl.BlockSpec((1, d), lambda i, j: (0, 0)),
                  pl.BlockSpec((d, tn), lambda i, j: (0, j))],
        out_specs=pl.BlockSpec((tm, tn), lambda i, j: (i, j)),
        out_shape=jax.ShapeDtypeStruct((n, p), F32),
        scratch_shapes=[pltpu.VMEM((tm, d), BF16)],
        compiler_params=_cparams(("parallel", "arbitrary"), 48),
        name=name,
    )(x, g, w)


def _shift_rows(u, e, sh):
    ext = jnp.concatenate([e, u], axis=0)
    return pltpu.roll(ext, sh, 0)[SUBLANES:, :]


def _conv_body(u_ref, e_ref, w_ref, b_ref, q_ref, k_ref, *, seg):
    w = w_ref[...]
    for s in range(u_ref.shape[0] // seg):
        rows = slice(s * seg, (s + 1) * seg)
        u = u_ref[rows, :]
        e = e_ref[s]
        acc = b_ref[...] + u * w[A_CONV - 1:A_CONV, :]
        for sh in range(1, A_CONV):
            acc = acc + _shift_rows(u, e, sh) * w[A_CONV - 1 - sh:A_CONV - sh, :]
        y = acc * jax.nn.sigmoid(acc)
        q_ref[rows, :] = y[:, :A_QK]
        k_ref[rows, :] = y[:, A_QK:] * (A_DK ** -0.5)


def _conv_silu(proj_main, e_conv, conv_w, conv_b, tm, seg):
    n = proj_main.shape[0]
    return pl.pallas_call(
        functools.partial(_conv_body, seg=seg),
        grid=(n // tm,),
        in_specs=[pl.BlockSpec((tm, CONV_COLS), lambda i: (i, 0)),
                  pl.BlockSpec((tm // seg, SUBLANES, CONV_COLS), lambda i: (i, 0, 0)),
                  pl.BlockSpec((A_CONV, CONV_COLS), lambda i: (0, 0)),
                  pl.BlockSpec((1, CONV_COLS), lambda i: (0, 0))],
        out_specs=[pl.BlockSpec((tm, A_QK), lambda i: (i, 0)),
                   pl.BlockSpec((tm, A_QK), lambda i: (i, 0))],
        out_shape=[jax.ShapeDtypeStruct((n, A_QK), F32)] * 2,
        compiler_params=_cparams(("parallel",), 32),
        name="conv_silu",
    )(proj_main, e_conv, conv_w, conv_b)


def _head_block_mask(scale):
    a = lax.broadcasted_iota(jnp.int32, (LANES, LANES), 0) // B_HEAD
    b = lax.broadcasted_iota(jnp.int32, (LANES, LANES), 1) // B_HEAD
    return jnp.where(a == b, scale, 0.0).astype(F32)


def _rwkv_prep_body(um_ref, ul_ref, em_ref, el_ref, mum_ref, mul_ref, w0_ref, w2_ref, a0_ref,
                    a2_ref, g2_ref, kk_ref, ka_ref,
                    r_out, ld_out, k_out, v_out, kn_out, bi_out, g_out,
                    kb_s, xl_s, *, seg):
    for s in range(um_ref.shape[0] // seg):
        rows = slice(s * seg, (s + 1) * seg)
        u = um_ref[rows, :]
        xs = u + mum_ref[...] * (_shift_rows(u, em_ref[s], 1) - u)
        r_out[rows, :] = xs[:, :B_W]
        kb_s[rows, :] = xs[:, B_W:2 * B_W]
        v_out[rows, :] = xs[:, 2 * B_W:]
        ul = ul_ref[rows, :]
        xl_s[rows, :] = ul + mul_ref[...] * (_shift_rows(ul, el_ref[s], 1) - ul)

    xl = xl_s[...]
    lw = _dot(jnp.tanh(xl[:, TAIL_WD:TAIL_WD + LANES]), w2_ref[...])
    w_log = -_softplus(-(w0_ref[...] + lw)) - 0.5
    ld_out[...] = -jnp.exp(w_log)
    a = jax.nn.sigmoid(a0_ref[...] + _dot(xl[:, TAIL_AD:TAIL_AD + LANES], a2_ref[...]))
    g_out[...] = _dot(jax.nn.sigmoid(xl[:, TAIL_GD:TAIL_GD + LORA_GATE]), g2_ref[...])
    kb = kb_s[...]
    k_out[...] = kb * (1.0 + (a - 1.0) * ka_ref[...])
    kk = kb * kk_ref[...]
    ones_blk = _head_block_mask(1.0)
    for p in range(N_PAIRS):
        cols = slice(p * LANES, (p + 1) * LANES)
        kp = kk[:, cols]
        ssq = _dot_exact_rhs(kp * kp, ones_blk)
        kn = kp / jnp.maximum(jnp.sqrt(ssq), 1e-12)
        kn_out[:, cols] = kn
        bi_out[:, cols] = kn * a[:, cols]


def _rwkv_prep(proj_main, proj_tail, e_main, e_tail, mu_main, mu_tail, w0, w2p, a0, a2p, g2, k_k, k_a,
               tm, seg):
    n = proj_main.shape[0]
    row = lambda i: (i, 0)
    fixed = lambda i: (0, 0)
    wide = pl.BlockSpec((tm, B_W), row)
    return pl.pallas_call(
        functools.partial(_rwkv_prep_body, seg=seg),
        grid=(n // tm,),
        in_specs=[pl.BlockSpec((tm, 3 * B_W), lambda i: (i, 1)),
                  pl.BlockSpec((tm, TAIL_IF), row),
                  pl.BlockSpec((tm // seg, SUBLANES, 3 * B_W), lambda i: (i, 0, 0)),
                  pl.BlockSpec((tm // seg, SUBLANES, TAIL_IF), lambda i: (i, 0, 0)),
                  pl.BlockSpec((1, 3 * B_W), fixed),
                  pl.BlockSpec((1, TAIL_IF), fixed),
                  pl.BlockSpec((1, B_W), fixed),
                  pl.BlockSpec((LANES, B_W), fixed),
                  pl.BlockSpec((1, B_W), fixed),
                  pl.BlockSpec((LANES, B_W), fixed),
                  pl.BlockSpec((LORA_GATE, B_W), fixed),
                  pl.BlockSpec((1, B_W), fixed),
                  pl.BlockSpec((1, B_W), fixed)],
        out_specs=[wide] * 7,
        out_shape=[jax.ShapeDtypeStruct((n, B_W), F32)] * 7,
        scratch_shapes=[pltpu.VMEM((tm, B_W), F32), pltpu.VMEM((tm, TAIL_IF), F32)],
        compiler_params=_cparams(("parallel",), 48),
        name="rwkv_prep",
    )(proj_main, proj_tail, e_main, e_tail, mu_main, mu_tail, w0, w2p, a0, a2p, g2, k_k, k_a)


def _mlstm_body(q_ref, k_ref, v_ref, gc_ref, gr_ref, brow_ref, bcol_ref, nw_ref, c0_ref, n0_ref, m0_ref,
                h_ref, cn_ref, nn_ref, mn_ref, c_s, n_s, m_s, *, T):
    c = pl.program_id(1)

    @pl.when(c == 0)
    def _():
        c_s[...] = c0_ref[0]
        n_s[...] = n0_ref[0]
        m_s[...] = m0_ref[0]

    gcol = gc_ref[...] + brow_ref[...]
    grow = gr_ref[0] + bcol_ref[...]
    lf_col = -_softplus(-gcol)
    ig_row = grow[:A_HEADS, :]
    lf_row = -_softplus(-grow[A_HEADS:, :])
    ri = lax.broadcasted_iota(jnp.int32, (T, T), 0)
    ci = lax.broadcasted_iota(jnp.int32, (T, T), 1)
    tri = ci <= ri
    b_col = _dot_exact_lhs(jnp.where(tri, 1.0, 0.0), lf_col)
    b_row = _dot_exact_rhs(lf_row, jnp.where(ri <= ci, 1.0, 0.0))
    m_prev = m_s[...]
    lane = lax.broadcasted_iota(jnp.int32, (1, LANES), 1)
    heads = list(range(A_HEADS))
    bc = [b_col[:, A_HEADS + h:A_HEADS + h + 1] for h in heads]
    mh = [m_prev[:, A_HEADS + h:A_HEADS + h + 1] for h in heads]
    br = [b_row[h:h + 1, :] for h in heads]
    igr = [ig_row[h:h + 1, :] for h in heads]
    igc = [gcol[:, h:h + 1] for h in heads]
    qh = [q_ref[:, h * A_DK:(h + 1) * A_DK] for h in heads]
    kh = [k_ref[:, h * A_DK:(h + 1) * A_DK] for h in heads]
    vh = [v_ref[:, h * A_DV:(h + 1) * A_DV] for h in heads]
    ch = [c_s[h] for h in heads]
    nh = [n_s[h:h + 1, :] for h in heads]
    qk = _each(_dot_nt, qh, kh)
    qc = _each(_dot, qh, ch)
    d = _each(lambda a, b, i: jnp.where(tri, a - b + i, -jnp.inf), bc, br, igr)
    inter = _each(jnp.add, bc, mh)
    m_t = _each(lambda i, x: jnp.maximum(i, jnp.max(x, axis=-1, keepdims=True)), inter, d)
    s = _each(lambda a, x, m: a * jnp.exp(x - m), qk, d, m_t)
    e = _each(lambda i, m: jnp.exp(i - m), inter, m_t)
    sv = _each(_dot, s, vh)
    num = _each(lambda a, x, b: a + x * b, sv, e, qc)
    den = _each(lambda a, x, qq, nn: jnp.sum(a, axis=-1, keepdims=True)
                + x * jnp.sum(qq * nn, axis=-1, keepdims=True), s, e, qh, nh)
    hh = _each(lambda a, b, m: a / jnp.maximum(jnp.abs(b), jnp.exp(-m)), num, den, m_t)
    hh = _each(lambda a: a * lax.rsqrt(jnp.mean(a * a, axis=-1, keepdims=True) + EPS), hh)
    b_end = _each(lambda a: a[T - 1:T, :], bc)
    m_new = _each(lambda be, m, b, i: jnp.maximum(be + m, jnp.max(be - b + i, axis=-1, keepdims=True)),
                  b_end, mh, br, igr)
    carry = _each(lambda be, m, mn: jnp.exp(be + m - mn), b_end, mh, m_new)
    wk = _each(lambda kk_, be, a, i, mn: kk_ * jnp.exp(be - a + i - mn), kh, b_end, bc, igc, m_new)
    kv = _each(_dot_tn, wk, vh)
    m_next = m_prev
    for h in heads:
        h_ref[:, h * A_DV:(h + 1) * A_DV] = hh[h] * nw_ref[:, h * A_DV:(h + 1) * A_DV]
        c_s[h] = carry[h] * ch[h] + kv[h]
        n_s[h:h + 1, :] = carry[h] * nh[h] + jnp.sum(wk[h], axis=0, keepdims=True)
        m_next = jnp.where(lane == A_HEADS + h, m_new[h], m_next)
    m_s[...] = m_next

    @pl.when(c == pl.num_programs(1) - 1)
    def _():
        cn_ref[0] = c_s[...]
        nn_ref[0] = n_s[...]
        mn_ref[0] = m_s[...]


def _mlstm(q, k, proj_main, proj_tail, gates_row, bias_row, bias_col, norm_w, c0, n0, m0, *,
           T, n_seq, n_chunk, row_off):
    n = q.shape[0]
    blk0 = row_off // T
    row = lambda s, c: (blk0 + s * n_chunk + c, 0)
    fixed = lambda s, c: (0, 0)
    return pl.pallas_call(
        functools.partial(_mlstm_body, T=T),
        grid=(n_seq, n_chunk),
        in_specs=[pl.BlockSpec((T, A_QK), row),
                  pl.BlockSpec((T, A_QK), row),
                  pl.BlockSpec((T, A_VW), lambda s, c: (blk0 + s * n_chunk + c, 1)),
                  pl.BlockSpec((T, LANES), lambda s, c: (blk0 + s * n_chunk + c, TAIL_IF // LANES)),
                  pl.BlockSpec((1, 2 * A_HEADS, T), lambda s, c: (s * n_chunk + c, 0, 0)),
                  pl.BlockSpec((1, LANES), fixed),
                  pl.BlockSpec((2 * A_HEADS, 1), fixed),
                  pl.BlockSpec((1, A_VW), fixed),
                  pl.BlockSpec((1, A_HEADS, A_DK, A_DV), lambda s, c: (s, 0, 0, 0)),
                  pl.BlockSpec((1, A_HEADS, A_DK), lambda s, c: (s, 0, 0)),
                  pl.BlockSpec((1, 1, LANES), lambda s, c: (s, 0, 0))],
        out_specs=[pl.BlockSpec((T, A_VW), lambda s, c: (s * n_chunk + c, 0)),
                   pl.BlockSpec((1, A_HEADS, A_DK, A_DV), lambda s, c: (s, 0, 0, 0)),
                   pl.BlockSpec((1, A_HEADS, A_DK), lambda s, c: (s, 0, 0)),
                   pl.BlockSpec((1, 1, LANES), lambda s, c: (s, 0, 0))],
        out_shape=[jax.ShapeDtypeStruct((n_seq * n_chunk * T, A_VW), F32),
                   jax.ShapeDtypeStruct((n_seq, A_HEADS, A_DK, A_DV), F32),
                   jax.ShapeDtypeStruct((n_seq, A_HEADS, A_DK), F32),
                   jax.ShapeDtypeStruct((n_seq, 1, LANES), F32)],
        scratch_shapes=[pltpu.VMEM((A_HEADS, A_DK, A_DV), F32),
                        pltpu.VMEM((A_HEADS, A_DK), F32),
                        pltpu.VMEM((1, LANES), F32)],
        compiler_params=_cparams(("arbitrary", "arbitrary"), 32),
        name="mlstm_chunk",
    )(q, k, proj_main, proj_tail, gates_row, bias_row, bias_col, norm_w, c0, n0, m0)


def _kcat_dot(a1, b1, a2, b2):
    if a1.shape[1] % LANES == 0:
        return _dot(jnp.concatenate([a1, a2], axis=1), _rows(b1, b2))
    return _dot(a1, b1) + _dot(a2, b2)


def _rwkv_pairs(r, ld, k, v, kk, bi, g, rk, lnw, lnb, S, T):
    lo = lax.broadcasted_iota(jnp.int32, (1, LANES), 1) < B_HEAD
    head1 = lambda x: jnp.where(lo, x, 0.0)
    head2 = lambda x: jnp.where(lo, 0.0, x)
    ri = lax.broadcasted_iota(jnp.int32, (T, T), 0)
    ci = lax.broadcasted_iota(jnp.int32, (T, T), 1)
    tri = jnp.where(ci <= ri, 1.0, 0.0).astype(BF16)
    r2 = lax.broadcasted_iota(jnp.int32, (T, 2 * T), 0)
    c2 = lax.broadcasted_iota(jnp.int32, (T, 2 * T), 1)
    left = c2 < T
    cj = jnp.where(left, c2, c2 - T)
    strict2 = cj < r2
    incl2 = cj <= r2
    e0 = lax.broadcasted_iota(jnp.int32, (2 * T, 2 * T), 0)
    e1 = lax.broadcasted_iota(jnp.int32, (2 * T, 2 * T), 1)
    eye2 = jnp.where(e0 == e1, 1.0, 0.0)
    ones_blk = _head_block_mask(1.0).astype(BF16)
    same_head = _head_block_mask(1.0) > 0.5
    zero = jnp.zeros((T, LANES), F32)

    def cumsum(x):
        pieces = jnp.dot(tri, jnp.concatenate(_split_bf16(x, 3), axis=1), preferred_element_type=F32)
        return pieces[:, :LANES] + pieces[:, LANES:2 * LANES] + pieces[:, 2 * LANES:]

    cum = _each(cumsum, ld)
    ewt = _each(lambda c: jnp.exp(c[T - 1:T, :]), cum)
    Rt = _each(lambda x, c: x * jnp.exp(c), r, cum)
    At = _each(lambda x, c, l: -x * jnp.exp(c - l), kk, cum, ld)
    einv = _each(lambda c: jnp.exp(-c), cum)
    Bt = _each(jnp.multiply, bi, einv)
    Kt = _each(jnp.multiply, k, einv)
    Bh = _each(jnp.multiply, Bt, ewt)
    Kh = _each(jnp.multiply, Kt, ewt)
    V1 = _each(head1, v)
    V2 = _each(head2, v)
    G1 = _each(lambda a, x, b, kt: _dot_nt(_rows(head1(a), head1(x)), _rows(b, kt)), At, Rt, Bt, Kt)
    G2 = _each(lambda a, x, b, kt: _dot_nt(_rows(head2(a), head2(x)), _rows(kt, b)), At, Rt, Bt, Kt)
    LL1 = _each(lambda gm: jnp.where(strict2, gm[:T], 0.0), G1)
    LL2 = _each(lambda gm: jnp.where(strict2, gm[:T], 0.0), G2)
    P1 = _each(lambda gm: jnp.where(incl2, gm[T:], 0.0), G1)
    P2 = _each(lambda gm: jnp.where(incl2, gm[T:], 0.0), G2)
    pw = _each(lambda a, b: _rows(jnp.where(left, a, 0.0), jnp.where(left, 0.0, b)), LL1, LL2)
    inv = _each(lambda m: eye2 + m, pw)
    span = 2
    while span < T:
        pw = _each(lambda m: _dot(m, m), pw)
        inv = _each(lambda i, m: i + _dot(i, m), inv, pw)
        span *= 2
    IV = _each(lambda i: i[:T, :] + i[T:, :], inv)
    X0 = _each(lambda a, b, x1, x2: _kcat_dot(a, _rows(zero, x1), b, _rows(x2, zero)), LL1, LL2, V1, V2)
    AU = _each(lambda iv, a, x: _dot(iv, jnp.concatenate([_rows(head1(a), head2(a)),
                                                          _rows(head1(x), head2(x))], axis=1)), IV, At, X0)
    Ah = _each(lambda m: m[:, :LANES], AU)
    U0 = _each(lambda m: m[:, LANES:], AU)
    RO = _each(lambda p1, p2, a, u, x1, x2: _kcat_dot(
        p1, jnp.concatenate([_rows(head1(a), zero), _rows(head1(u), x1)], axis=1),
        p2, jnp.concatenate([_rows(zero, head2(a)), _rows(x2, head2(u))], axis=1)), P1, P2, Ah, U0, V1, V2)
    Rh = _each(lambda x, m: x + m[:, :LANES], Rt, RO)
    O0 = _each(lambda m: m[:, LANES:], RO)
    MN = _each(_dot_tn, AU, Bh)
    VK = _each(_dot_tn, v, Kh)
    Mab = _each(lambda m: jnp.where(same_head, m[:LANES], 0.0), MN)
    N0 = _each(lambda m, x: jnp.where(same_head, m[LANES:] + x, 0.0), MN, VK)
    o = _each(lambda x, s, o0: _dot_nt(x, s) + o0, Rh, S, O0)
    S_new = _each(lambda s, w, m, n0: s * w + _dot(s, m) + n0, S, ewt, Mab, N0)

    def head_sums(*xs):
        parts = []
        for x in xs:
            parts += _split_bf16(x, 2)
        sums = jnp.dot(_rows(*parts), ones_blk, preferred_element_type=F32)
        return [sums[2 * i * T:(2 * i + 1) * T] + sums[(2 * i + 1) * T:(2 * i + 2) * T] for i in range(len(xs))]

    sums = _each(lambda x, rr, kx, w: head_sums(x, rr * kx * w), o, r, k, rk)
    oc = _each(lambda x, sm: x - sm[0] * (1.0 / B_HEAD), o, sums)
    var = _each(lambda x: head_sums(x * x)[0] * (1.0 / B_HEAD), oc)
    hb = _each(lambda x, vr, w, b, sm, vv, gg: (x * lax.rsqrt(vr + GN_EPS) * w + b + sm[1] * vv) * gg,
               oc, var, lnw, lnb, sums, v, g)
    return hb, S_new


def _rwkv_body(r_ref, ld_ref, k_ref, v_ref, kk_ref, bi_ref, g_ref, rk_ref, lnw_ref, lnb_ref, s0_ref,
               hb_ref, sn_ref, s_s, *, T, n_pairs):
    c = pl.program_id(2)

    @pl.when(c == 0)
    def _():
        s_s[...] = s0_ref[0]

    cols = [slice(p * LANES, (p + 1) * LANES) for p in range(n_pairs)]
    pick = lambda ref: [ref[:, cs] for cs in cols]
    hb, s_new = _rwkv_pairs(pick(r_ref), pick(ld_ref), pick(k_ref), pick(v_ref), pick(kk_ref), pick(bi_ref),
                            pick(g_ref), pick(rk_ref), pick(lnw_ref), pick(lnb_ref),
                            [s_s[p] for p in range(n_pairs)], T)
    for p in range(n_pairs):
        hb_ref[:, cols[p]] = hb[p]
        s_s[p] = s_new[p]

    @pl.when(c == pl.num_programs(2) - 1)
    def _():
        sn_ref[0] = s_s[...]


def _rwkv(r, ld, k, v, kk, bi, g, rk, lnw, lnb, s0, *, T, n_seq, n_chunk, row_off, gp):
    assert T <= RWKV_CHUNK
    blk0 = row_off // T
    wide = pl.BlockSpec((T, gp * LANES), lambda s, pg, c: (blk0 + s * n_chunk + c, pg))
    wide_out = pl.BlockSpec((T, gp * LANES), lambda s, pg, c: (s * n_chunk + c, pg))
    vec = pl.BlockSpec((1, gp * LANES), lambda s, pg, c: (0, pg))
    st = pl.BlockSpec((1, gp, LANES, LANES), lambda s, pg, c: (s, pg, 0, 0))
    return pl.pallas_call(
        functools.partial(_rwkv_body, T=T, n_pairs=gp),
        grid=(n_seq, N_PAIRS // gp, n_chunk),
        in_specs=[wide] * 7 + [vec] * 3 + [st],
        out_specs=[wide_out, st],
        out_shape=[jax.ShapeDtypeStruct((n_seq * n_chunk * T, B_W), F32),
                   jax.ShapeDtypeStruct((n_seq, N_PAIRS, LANES, LANES), F32)],
        scratch_shapes=[pltpu.VMEM((gp, LANES, LANES), F32)],
        compiler_params=_cparams(("arbitrary", "arbitrary", "arbitrary"), 32),
        name="rwkv_chunk",
    )(r, ld, k, v, kk, bi, g, rk, lnw, lnb, s0)


def _merge_out_body(ha_ref, hb_ref, o_ref, ga_ref, gb_ref, x_ref, w_ref, g_ref, y_ref):
    mixed = (jax.nn.sigmoid(ga_ref[...]) * (ha_ref[...] * jax.nn.sigmoid(o_ref[...]))
             + jax.nn.sigmoid(gb_ref[...]) * hb_ref[...])
    y = jnp.dot(mixed.astype(BF16), w_ref[...], preferred_element_type=F32)
    ms = jnp.mean(y * y, axis=-1, keepdims=True)
    y_ref[...] = x_ref[...] + y * lax.rsqrt(ms + EPS) * g_ref[...]


def _merge_out(h_a, h_b, proj_main, x, w_out, g_post, tm):
    n = x.shape[0]
    row = lambda i: (i, 0)
    wide = pl.BlockSpec((tm, D_MODEL), row)
    return pl.pallas_call(
        _merge_out_body,
        grid=(n // tm,),
        in_specs=[wide, wide,
                  pl.BlockSpec((tm, D_MODEL), lambda i: (i, 2)),
                  pl.BlockSpec((tm, D_MODEL), lambda i: (i, 6)),
                  pl.BlockSpec((tm, D_MODEL), lambda i: (i, 7)),
                  wide,
                  pl.BlockSpec((D_MODEL, D_MODEL), lambda i: (0, 0)),
                  pl.BlockSpec((1, D_MODEL), lambda i: (0, 0))],
        out_specs=wide,
        out_shape=jax.ShapeDtypeStruct((n, D_MODEL), F32),
        compiler_params=_cparams(("parallel",), 56),
        name="merge_out_proj",
    )(h_a, h_b, proj_main, proj_main, proj_main, x, w_out, g_post)


def _mlp_body(x_ref, g1_ref, wu_ref, wd_ref, g2_ref, y_ref, hn_ref, acc_ref, *, row_chunk):
    j = pl.program_id(1)

    @pl.when(j == 0)
    def _():
        def body(c, carry):
            rows = pl.ds(pl.multiple_of(c * row_chunk, row_chunk), row_chunk)
            x = x_ref[rows, :]
            ms = jnp.mean(x * x, axis=-1, keepdims=True)
            hn_ref[rows, :] = (x * lax.rsqrt(ms + EPS) * g1_ref[...]).astype(BF16)
            return carry
        lax.fori_loop(0, x_ref.shape[0] // row_chunk, body, 0)
        acc_ref[...] = jnp.zeros_like(acc_ref)

    u = jnp.dot(hn_ref[...], wu_ref[...], preferred_element_type=F32)
    a = jnp.square(jnp.maximum(u, 0.0))
    acc_ref[...] += jnp.dot(a.astype(BF16), wd_ref[...], preferred_element_type=F32)

    @pl.when(j == pl.num_programs(1) - 1)
    def _():
        def body(c, carry):
            rows = pl.ds(pl.multiple_of(c * row_chunk, row_chunk), row_chunk)
            ff = acc_ref[rows, :]
            ms = jnp.mean(ff * ff, axis=-1, keepdims=True)
            y_ref[rows, :] = x_ref[rows, :] + ff * lax.rsqrt(ms + EPS) * g2_ref[...]
            return carry
        lax.fori_loop(0, x_ref.shape[0] // row_chunk, body, 0)


def _mlp(x, g_pre, w_up, w_down, g_post, tm, tf):
    n, d = x.shape
    f = w_up.shape[1]
    return pl.pallas_call(
        functools.partial(_mlp_body, row_chunk=min(tm, 128)),
        grid=(n // tm, f // tf),
        in_specs=[pl.BlockSpec((tm, d), lambda i, j: (i, 0)),
                  pl.BlockSpec((1, d), lambda i, j: (0, 0)),
                  pl.BlockSpec((d, tf), lambda i, j: (0, j)),
                  pl.BlockSpec((tf, d), lambda i, j: (j, 0)),
                  pl.BlockSpec((1, d), lambda i, j: (0, 0))],
        out_specs=pl.BlockSpec((tm, d), lambda i, j: (i, 0)),
        out_shape=jax.ShapeDtypeStruct((n, d), F32),
        scratch_shapes=[pltpu.VMEM((tm, d), BF16), pltpu.VMEM((tm, d), F32)],
        compiler_params=_cparams(("parallel", "arbitrary"), 56),
        name="mlp",
    )(x, g_pre, w_up, w_down, g_post)


def _pad_cols(a, width):
    return jnp.pad(a, ((0, 0), (0, width - a.shape[1])))


def _tail_layout(wd, ad, gd, gates=None):
    parts = [_pad_cols(wd, LANES), _pad_cols(ad, LANES), gd]
    if gates is not None:
        parts.append(_pad_cols(gates, LANES))
    return jnp.concatenate(parts, axis=1)


def _boundary_blocks(prev_rows):
    k = prev_rows.shape[1]
    return jnp.pad(prev_rows, ((0, 0), (SUBLANES - k, 0), (0, 0)))


def _pack_state(s):
    n = s.shape[0]
    s = s.reshape(n, N_PAIRS, 2, B_HEAD, B_HEAD)
    z = jnp.zeros((n, N_PAIRS, B_HEAD, B_HEAD), s.dtype)
    top = jnp.concatenate([s[:, :, 0], z], axis=-1)
    bot = jnp.concatenate([z, s[:, :, 1]], axis=-1)
    return jnp.concatenate([top, bot], axis=-2)


def _unpack_state(sp):
    n = sp.shape[0]
    a = sp[:, :, :B_HEAD, :B_HEAD]
    b = sp[:, :, B_HEAD:, B_HEAD:]
    return jnp.stack([a, b], axis=2).reshape(n, B_HEADS, B_HEAD, B_HEAD)


def _gates_row_form(proj_tail, T):
    g = proj_tail[:, TAIL_IF:TAIL_IF + 2 * A_HEADS]
    return jnp.swapaxes(g.reshape(-1, T, 2 * A_HEADS), 1, 2)


def kernel(x_prompt, x_sample, state_mlstm_C, state_mlstm_n, state_mlstm_m, state_mlstm_conv, state_rwkv_S, state_rwkv_shift, meta_tokens, norm_mix_pre, norm_mix_post, norm_mlp_pre, norm_mlp_post, w_in, mlstm_conv_w, mlstm_conv_b, mlstm_b_i, mlstm_b_f, mlstm_norm_w, rwkv_mu, rwkv_w0, rwkv_w2, rwkv_a0, rwkv_a2, rwkv_g2, rwkv_k_k, rwkv_k_a, rwkv_r_k, rwkv_ln_w, rwkv_ln_b, w_out, w_up, w_down):
    n_s_seq, s_len, _ = x_sample.shape
    _, p_len, _ = x_prompt.shape
    n_s = n_s_seq * s_len
    n_all = n_s + p_len
    seg = s_len

    w = w_in[0]
    c_if = CONV_COLS + 2 * A_VW
    c_sh = c_if + 2 * A_HEADS
    c_lora = c_sh + 3 * B_W
    c_gate = c_sh + SHIFT_COLS
    w_main = jnp.concatenate([w[:, :c_if], w[:, c_sh:c_lora], w[:, c_gate:]], axis=1).astype(BF16)
    w_tail = _tail_layout(w[:, c_lora:c_lora + LORA_DECAY],
                          w[:, c_lora + LORA_DECAY:c_lora + LORA_DECAY + LORA_ICLR],
                          w[:, c_lora + LORA_DECAY + LORA_ICLR:c_gate],
                          w[:, c_if:c_sh]).astype(BF16)
    mu = rwkv_mu[0][None, :]
    mu_main = mu[:, :3 * B_W]
    mu_tail = _tail_layout(mu[:, 3 * B_W:3 * B_W + LORA_DECAY],
                           mu[:, 3 * B_W + LORA_DECAY:3 * B_W + LORA_DECAY + LORA_ICLR],
                           mu[:, 3 * B_W + LORA_DECAY + LORA_ICLR:])
    w2p = jnp.pad(rwkv_w2[0], ((0, LANES - LORA_DECAY), (0, 0))).astype(BF16)
    a2p = jnp.pad(rwkv_a2[0], ((0, LANES - LORA_ICLR), (0, 0))).astype(BF16)
    g2 = rwkv_g2[0].astype(BF16)
    g_pre = norm_mix_pre[0][None, :]
    bias_row = _pad_cols(jnp.concatenate([mlstm_b_i[0], mlstm_b_f[0]])[None, :], LANES)
    bias_col = jnp.concatenate([mlstm_b_i[0], mlstm_b_f[0]])[:, None]
    norm_w = mlstm_norm_w[0][None, :]
    conv_w = mlstm_conv_w[0]
    conv_b = mlstm_conv_b[0][None, :]
    w0 = rwkv_w0[0][None, :]
    a0 = rwkv_a0[0][None, :]
    k_k = rwkv_k_k[0][None, :]
    k_a = rwkv_k_a[0][None, :]
    r_k = rwkv_r_k[0].reshape(1, B_W)
    ln_w = rwkv_ln_w[0][None, :]
    ln_b = rwkv_ln_b[0][None, :]

    def project(x_rows, tm):
        pm = _norm_matmul(x_rows, g_pre, w_main, tm, 1024, "in_proj_main")
        pt = _norm_matmul(x_rows, g_pre, w_tail, tm, TAIL_COLS, "in_proj_tail")
        return pm, pt

    xm = meta_tokens.astype(F32)
    pm_m, pt_m = project(xm, N_META)
    zeros_e = lambda width: jnp.zeros((1, SUBLANES, width), F32)
    q_m, k_m = _conv_silu(pm_m, zeros_e(CONV_COLS), conv_w, conv_b, N_META, N_META)
    rw_m = _rwkv_prep(pm_m, pt_m, zeros_e(3 * B_W), zeros_e(TAIL_IF), mu_main, mu_tail, w0, w2p, a0, a2p,
                      g2, k_k, k_a, N_META, N_META)
    _, c_m, n_m, m_m = _mlstm(q_m, k_m, pm_m, pt_m, _gates_row_form(pt_m, N_META), bias_row, bias_col, norm_w,
                              jnp.zeros((1, A_HEADS, A_DK, A_DV), F32), jnp.zeros((1, A_HEADS, A_DK), F32),
                              jnp.zeros((1, 1, LANES), F32), T=N_META, n_seq=1, n_chunk=1, row_off=0)
    _, s_m = _rwkv(*rw_m, r_k, ln_w, ln_b, jnp.zeros((1, N_PAIRS, LANES, LANES), F32),
                   T=N_META, n_seq=1, n_chunk=1, row_off=0, gp=8)

    x_all = jnp.concatenate([x_sample.reshape(n_s, D_MODEL), x_prompt[0]], axis=0).astype(F32)
    pm, pt = project(x_all, 1024)

    n_pseg = p_len // seg

    def prev_rows(cols_main, lo, hi, first, k):
        body = cols_main[n_s:].reshape(n_pseg, seg, hi - lo)[:-1, seg - k:, :]
        return jnp.concatenate([first[None], body], axis=0)

    conv_prev_s = state_mlstm_conv[0].astype(F32)
    conv_prev_p = prev_rows(pm[:, :CONV_COLS], 0, CONV_COLS, pm_m[N_META - (A_CONV - 1):, :CONV_COLS], A_CONV - 1)
    e_conv = _boundary_blocks(jnp.concatenate([conv_prev_s, conv_prev_p], axis=0))
    sh = state_rwkv_shift[0].astype(F32)
    sh_main_s = sh[:, :, :3 * B_W]
    sh_tail_s = _tail_layout(sh[:, 0, 3 * B_W:3 * B_W + LORA_DECAY],
                             sh[:, 0, 3 * B_W + LORA_DECAY:3 * B_W + LORA_DECAY + LORA_ICLR],
                             sh[:, 0, 3 * B_W + LORA_DECAY + LORA_ICLR:])[:, None, :]
    sh_main_p = prev_rows(pm[:, 3 * B_W:6 * B_W], 0, 3 * B_W, pm_m[N_META - 1:, 3 * B_W:6 * B_W], 1)
    sh_tail_p = prev_rows(pt[:, :TAIL_IF], 0, TAIL_IF, pt_m[N_META - 1:, :TAIL_IF], 1)
    e_main = _boundary_blocks(jnp.concatenate([sh_main_s, sh_main_p], axis=0))
    e_tail = _boundary_blocks(jnp.concatenate([sh_tail_s, sh_tail_p], axis=0))

    q, k = _conv_silu(pm, e_conv, conv_w, conv_b, 256, seg)
    rw = _rwkv_prep(pm, pt, e_main, e_tail, mu_main, mu_tail, w0, w2p, a0, a2p, g2, k_k, k_a, 128, seg)

    m0_s = jnp.pad(state_mlstm_m[0].astype(F32), ((0, 0), (A_HEADS, LANES - 2 * A_HEADS)))[:, None, :]
    gr_s = _gates_row_form(pt[:n_s], s_len)
    gr_p = _gates_row_form(pt[n_s:], MLSTM_CHUNK)
    ha_s, c_s, nn_s, mm_s = _mlstm(q, k, pm, pt, gr_s, bias_row, bias_col, norm_w,
                                   state_mlstm_C[0].astype(F32), state_mlstm_n[0].astype(F32), m0_s,
                                   T=s_len, n_seq=n_s_seq, n_chunk=1, row_off=0)
    ha_p, c_p, nn_p, mm_p = _mlstm(q, k, pm, pt, gr_p, bias_row, bias_col, norm_w, c_m, n_m, m_m,
                                   T=MLSTM_CHUNK, n_seq=1, n_chunk=p_len // MLSTM_CHUNK, row_off=n_s)
    h_a = jnp.concatenate([ha_s, ha_p], axis=0)

    hb_s, s_s = _rwkv(*rw, r_k, ln_w, ln_b, _pack_state(state_rwkv_S[0].astype(F32)),
                      T=s_len, n_seq=n_s_seq, n_chunk=1, row_off=0, gp=8)
    hb_p, s_p = _rwkv(*rw, r_k, ln_w, ln_b, s_m,
                      T=RWKV_CHUNK, n_seq=1, n_chunk=p_len // RWKV_CHUNK, row_off=n_s, gp=8)
    h_b = jnp.concatenate([hb_s, hb_p], axis=0)

    x1 = _merge_out(h_a, h_b, pm, x_all, w_out[0].astype(BF16), norm_mix_post[0][None, :], 256)
    x2 = _mlp(x1, norm_mlp_pre[0][None, :], w_up[0].astype(BF16), w_down[0].astype(BF16),
              norm_mlp_post[0][None, :], 512, 1024)

    y_sample = x2[:n_s].reshape(x_sample.shape).astype(x_sample.dtype)
    y_prompt = x2[n_s:][None].astype(x_prompt.dtype)

    def shift_state(rows_main, rows_tail):
        return jnp.concatenate([rows_main,
                                rows_tail[..., TAIL_WD:TAIL_WD + LORA_DECAY],
                                rows_tail[..., TAIL_AD:TAIL_AD + LORA_ICLR],
                                rows_tail[..., TAIL_GD:TAIL_GD + LORA_GATE]], axis=-1)

    k3 = A_CONV - 1
    conv_s = pm[:n_s, :CONV_COLS].reshape(n_s_seq, s_len, CONV_COLS)[:, s_len - k3:, :]
    conv_p = pm[n_all - k3:, :CONV_COLS][None]
    shift_s = shift_state(pm[:n_s, 3 * B_W:6 * B_W].reshape(n_s_seq, s_len, 3 * B_W)[:, s_len - 1:, :],
                          pt[:n_s, :TAIL_IF].reshape(n_s_seq, s_len, TAIL_IF)[:, s_len - 1:, :])
    shift_p = shift_state(pm[n_all - 1:, 3 * B_W:6 * B_W], pt[n_all - 1:, :TAIL_IF])[None]
    dt_c, dt_n, dt_m = state_mlstm_C.dtype, state_mlstm_n.dtype, state_mlstm_m.dtype
    dt_cv, dt_s, dt_sh = state_mlstm_conv.dtype, state_rwkv_S.dtype, state_rwkv_shift.dtype
    lanes_m = slice(A_HEADS, 2 * A_HEADS)
    return (y_prompt, y_sample,
            c_p[None].astype(dt_c), nn_p[None].astype(dt_n), mm_p[:, 0, lanes_m][None].astype(dt_m),
            conv_p[None].astype(dt_cv), _unpack_state(s_p)[None].astype(dt_s), shift_p[None].astype(dt_sh),
            c_s[None].astype(dt_c), nn_s[None].astype(dt_n), mm_s[:, 0, lanes_m][None].astype(dt_m),
            conv_s[None].astype(dt_cv), _unpack_state(s_s)[None].astype(dt_s), shift_s[None].astype(dt_sh))
```

```python
import functools

import jax
import jax.numpy as jnp
from jax import lax
from jax.experimental import pallas as pl
from jax.experimental.pallas import tpu as pltpu

F32 = jnp.float32
BF16 = jnp.bfloat16

D_MODEL = 2048
N_META = 16
EPS = 1e-6
D_FF = 4 * D_MODEL
A_HEADS = 8
A_DK = 128
A_DV = D_MODEL // A_HEADS
A_QK = A_HEADS * A_DK
A_VW = A_HEADS * A_DV
A_CONV = 4
MLSTM_CHUNK = 64
B_HEAD = 64
B_HEADS = D_MODEL // B_HEAD
B_W = B_HEADS * B_HEAD
LORA_DECAY = max(32, int(round(1.8 * D_MODEL ** 0.5 / 32)) * 32)
LORA_ICLR = max(32, int(round(1.8 * D_MODEL ** 0.5 / 32)) * 32)
LORA_GATE = max(32, int(round(0.6 * D_MODEL ** 0.8 / 32)) * 32)
GN_EPS = 64e-5
CONV_COLS = 2 * A_QK
SHIFT_COLS = 3 * B_W + LORA_DECAY + LORA_ICLR + LORA_GATE

LANES = 128
SUBLANES = 8

COL_CONV, COL_V, COL_O, COL_R, COL_K, COL_VB, COL_GA, COL_GB = (i * D_MODEL for i in range(8))
COL_TAIL = 8 * D_MODEL
TAIL_WD, TAIL_AD, TAIL_GD, TAIL_IF = 0, LANES, 2 * LANES, 2 * LANES + LORA_GATE
TAIL_MIX = TAIL_IF
PROJ_TN = 1024
PROJ_COLS = -(-(COL_TAIL + TAIL_IF + LANES) // PROJ_TN) * PROJ_TN
PROJ_TM = 1024
MERGE_TM = 256
MLP_TM, MLP_TF = 512, 1024
RWKV_CHUNK = 64
N_PAIRS = B_HEADS // 2
GROUP_PAIRS = 8
GROUP_W = GROUP_PAIRS * LANES


def _cparams(semantics, vmem_mib):
    return pltpu.CompilerParams(dimension_semantics=semantics, vmem_limit_bytes=vmem_mib << 20)


def _softplus(x):
    return jnp.maximum(x, 0.0) + jnp.log1p(jnp.exp(-jnp.abs(x)))


def _dot(a, b):
    return jnp.dot(a.astype(BF16), b.astype(BF16), preferred_element_type=F32)


def _dot_nt(a, b):
    return lax.dot_general(a.astype(BF16), b.astype(BF16), (((1,), (1,)), ((), ())),
                           preferred_element_type=F32)


def _dot_tn(a, b):
    return lax.dot_general(a.astype(BF16), b.astype(BF16), (((0,), (0,)), ((), ())),
                           preferred_element_type=F32)


def _each(f, *lists):
    return [f(*args) for args in zip(*lists)]


def _rows(*parts):
    return jnp.concatenate(parts, axis=0)


def _split_bf16(x, pieces):
    out = []
    for _ in range(pieces - 1):
        p = x.astype(BF16)
        out.append(p)
        x = x - p.astype(F32)
    out.append(x.astype(BF16))
    return out


def _dot_exact_lhs(m, x, pieces=3):
    mb = m.astype(BF16)
    acc = None
    for p in _split_bf16(x, pieces):
        t = jnp.dot(mb, p, preferred_element_type=F32)
        acc = t if acc is None else acc + t
    return acc


def _dot_exact_rhs(x, m, pieces=3):
    mb = m.astype(BF16)
    acc = None
    for p in _split_bf16(x, pieces):
        t = jnp.dot(p, mb, preferred_element_type=F32)
        acc = t if acc is None else acc + t
    return acc


def _shift_rows(u, e, sh):
    ext = jnp.concatenate([e, u], axis=0)
    return pltpu.roll(ext, sh, 0)[SUBLANES:, :]


def _head_block_mask(scale):
    a = lax.broadcasted_iota(jnp.int32, (LANES, LANES), 0) // B_HEAD
    b = lax.broadcasted_iota(jnp.int32, (LANES, LANES), 1) // B_HEAD
    return jnp.where(a == b, scale, 0.0).astype(F32)


def _norm_rows(x_ref, g_ref, out_ref, row_chunk):
    def body(c, carry):
        rows = pl.ds(pl.multiple_of(c * row_chunk, row_chunk), row_chunk)
        x = x_ref[rows, :]
        ms = jnp.mean(x * x, axis=-1, keepdims=True)
        out_ref[rows, :] = (x * lax.rsqrt(ms + EPS) * g_ref[...]).astype(out_ref.dtype)
        return carry
    lax.fori_loop(0, x_ref.shape[0] // row_chunk, body, 0)


def _norm_matmul_body(x_ref, g_ref, w_ref, o_ref, hn_ref, *, row_chunk):
    @pl.when(pl.program_id(1) == 0)
    def _():
        _norm_rows(x_ref, g_ref, hn_ref, row_chunk)

    o_ref[...] = jnp.dot(hn_ref[...], w_ref[...], preferred_element_type=F32)


def _norm_matmul(x, g, w, tm):
    n, d = x.shape
    p = w.shape[1]
    return pl.pallas_call(
        functools.partial(_norm_matmul_body, row_chunk=min(tm, LANES)),
        grid=(n // tm, p // PROJ_TN),
        in_specs=[pl.BlockSpec((tm, d), lambda i, j: (i, 0)),
                  pl.BlockSpec((1, d), lambda i, j: (0, 0)),
                  pl.BlockSpec((d, PROJ_TN), lambda i, j: (0, j))],
        out_specs=pl.BlockSpec((tm, PROJ_TN), lambda i, j: (i, j)),
        out_shape=jax.ShapeDtypeStruct((n, p), F32),
        scratch_shapes=[pltpu.VMEM((tm, d), BF16)],
        compiler_params=_cparams(("parallel", "arbitrary"), 48),
        name="in_proj",
    )(x, g, w)


def _mlstm_body(u_ref, v_ref, o_ref, ga_ref, gc_ref, gr_ref, e_ref, cw_ref, cb_ref, brow_ref, bcol_ref,
                nw_ref, c0_ref, n0_ref, m0_ref,
                h_ref, cn_ref, nn_ref, mn_ref, c_s, n_s, m_s, e_s, *, T):
    c = pl.program_id(1)

    @pl.when(c == 0)
    def _():
        c_s[...] = c0_ref[0]
        n_s[...] = n0_ref[0]
        m_s[...] = m0_ref[0]
        e_s[...] = e_ref[0]

    u = u_ref[...]
    e = e_s[...]
    w = cw_ref[...]
    acc = cb_ref[...] + u * w[A_CONV - 1:A_CONV, :]
    for sh in range(1, A_CONV):
        acc = acc + _shift_rows(u, e, sh) * w[A_CONV - 1 - sh:A_CONV - sh, :]
    e_s[...] = u[T - SUBLANES:, :]
    qk_act = acc * jax.nn.sigmoid(acc)

    gcol = gc_ref[...] + brow_ref[...]
    grow = gr_ref[0] + bcol_ref[...]
    lf_col = -_softplus(-gcol)
    ig_row = grow[:A_HEADS, :]
    lf_row = -_softplus(-grow[A_HEADS:, :])
    ri = lax.broadcasted_iota(jnp.int32, (T, T), 0)
    ci = lax.broadcasted_iota(jnp.int32, (T, T), 1)
    tri = ci <= ri
    b_col = _dot_exact_lhs(jnp.where(tri, 1.0, 0.0), lf_col)
    b_row = _dot_exact_rhs(lf_row, jnp.where(ri <= ci, 1.0, 0.0))
    m_prev = m_s[...]
    lane = lax.broadcasted_iota(jnp.int32, (1, LANES), 1)
    heads = list(range(A_HEADS))
    bc = [b_col[:, A_HEADS + h:A_HEADS + h + 1] for h in heads]
    mh = [m_prev[:, A_HEADS + h:A_HEADS + h + 1] for h in heads]
    br = [b_row[h:h + 1, :] for h in heads]
    igr = [ig_row[h:h + 1, :] for h in heads]
    igc = [gcol[:, h:h + 1] for h in heads]
    qh = [qk_act[:, h * A_DK:(h + 1) * A_DK] for h in heads]
    kh = [qk_act[:, A_QK + h * A_DK:A_QK + (h + 1) * A_DK] * (A_DK ** -0.5) for h in heads]
    vh = [v_ref[:, h * A_DV:(h + 1) * A_DV] for h in heads]
    ch = [c_s[h] for h in heads]
    nh = [n_s[h:h + 1, :] for h in heads]
    qk = _each(_dot_nt, qh, kh)
    qc = _each(_dot, qh, ch)
    d = _each(lambda a, b, i: jnp.where(tri, a - b + i, -jnp.inf), bc, br, igr)
    inter = _each(jnp.add, bc, mh)
    m_t = _each(lambda i, x: jnp.maximum(i, jnp.max(x, axis=-1, keepdims=True)), inter, d)
    s = _each(lambda a, x, m: a * jnp.exp(x - m), qk, d, m_t)
    e_in = _each(lambda i, m: jnp.exp(i - m), inter, m_t)
    sv = _each(_dot, s, vh)
    num = _each(lambda a, x, b: a + x * b, sv, e_in, qc)
    den = _each(lambda a, x, qq, nn: jnp.sum(a, axis=-1, keepdims=True)
                + x * jnp.sum(qq * nn, axis=-1, keepdims=True), s, e_in, qh, nh)
    hh = _each(lambda a, b, m: a / jnp.maximum(jnp.abs(b), jnp.exp(-m)), num, den, m_t)
    hh = _each(lambda a: a * lax.rsqrt(jnp.mean(a * a, axis=-1, keepdims=True) + EPS), hh)
    b_end = _each(lambda a: a[T - 1:T, :], bc)
    m_new = _each(lambda be, m, b, i: jnp.maximum(be + m, jnp.max(be - b + i, axis=-1, keepdims=True)),
                  b_end, mh, br, igr)
    carry = _each(lambda be, m, mn: jnp.exp(be + m - mn), b_end, mh, m_new)
    wk = _each(lambda kk_, be, a, i, mn: kk_ * jnp.exp(be - a + i - mn), kh, b_end, bc, igc, m_new)
    kv = _each(_dot_tn, wk, vh)
    m_next = m_prev
    for h in heads:
        cols = slice(h * A_DV, (h + 1) * A_DV)
        gate = jax.nn.sigmoid(o_ref[:, cols]) * jax.nn.sigmoid(ga_ref[:, cols])
        h_ref[:, cols] = (hh[h] * nw_ref[:, cols] * gate).astype(h_ref.dtype)
        c_s[h] = carry[h] * ch[h] + kv[h]
        n_s[h:h + 1, :] = carry[h] * nh[h] + jnp.sum(wk[h], axis=0, keepdims=True)
        m_next = jnp.where(lane == A_HEADS + h, m_new[h], m_next)
    m_s[...] = m_next

    @pl.when(c == pl.num_programs(1) - 1)
    def _():
        cn_ref[0] = c_s[...]
        nn_ref[0] = n_s[...]
        mn_ref[0] = m_s[...]


def _mlstm(proj, gates_row, e_conv, conv_w, conv_b, bias_row, bias_col, norm_w, c0, n0, m0, *, T, n_seq, n_chunk):
    row = lambda s, c: (s * n_chunk + c, 0)
    fixed = lambda s, c: (0, 0)
    wide = lambda col: pl.BlockSpec((T, D_MODEL), lambda s, c: (s * n_chunk + c, col // D_MODEL))
    return pl.pallas_call(
        functools.partial(_mlstm_body, T=T),
        grid=(n_seq, n_chunk),
        in_specs=[wide(COL_CONV), wide(COL_V), wide(COL_O), wide(COL_GA),
                  pl.BlockSpec((T, LANES), lambda s, c: (s * n_chunk + c, (COL_TAIL + TAIL_IF) // LANES)),
                  pl.BlockSpec((1, 2 * A_HEADS, T), lambda s, c: (s * n_chunk + c, 0, 0)),
                  pl.BlockSpec((1, SUBLANES, CONV_COLS), lambda s, c: (s, 0, 0)),
                  pl.BlockSpec((A_CONV, CONV_COLS), fixed),
                  pl.BlockSpec((1, CONV_COLS), fixed),
                  pl.BlockSpec((1, LANES), fixed),
                  pl.BlockSpec((2 * A_HEADS, 1), fixed),
                  pl.BlockSpec((1, A_VW), fixed),
                  pl.BlockSpec((1, A_HEADS, A_DK, A_DV), lambda s, c: (s, 0, 0, 0)),
                  pl.BlockSpec((1, A_HEADS, A_DK), lambda s, c: (s, 0, 0)),
                  pl.BlockSpec((1, 1, LANES), lambda s, c: (s, 0, 0))],
        out_specs=[pl.BlockSpec((T, A_VW), row),
                   pl.BlockSpec((1, A_HEADS, A_DK, A_DV), lambda s, c: (s, 0, 0, 0)),
                   pl.BlockSpec((1, A_HEADS, A_DK), lambda s, c: (s, 0, 0)),
                   pl.BlockSpec((1, 1, LANES), lambda s, c: (s, 0, 0))],
        out_shape=[jax.ShapeDtypeStruct((n_seq * n_chunk * T, A_VW), BF16),
                   jax.ShapeDtypeStruct((n_seq, A_HEADS, A_DK, A_DV), F32),
                   jax.ShapeDtypeStruct((n_seq, A_HEADS, A_DK), F32),
                   jax.ShapeDtypeStruct((n_seq, 1, LANES), F32)],
        scratch_shapes=[pltpu.VMEM((A_HEADS, A_DK, A_DV), F32),
                        pltpu.VMEM((A_HEADS, A_DK), F32),
                        pltpu.VMEM((1, LANES), F32),
                        pltpu.VMEM((SUBLANES, CONV_COLS), F32)],
        compiler_params=_cparams(("arbitrary", "arbitrary"), 32),
        name="mlstm_chunk",
    )(proj, proj, proj, proj, proj, gates_row, e_conv, conv_w, conv_b, bias_row, bias_col, norm_w, c0, n0, m0)


def _kcat_dot(a1, b1, a2, b2):
    if a1.shape[1] % LANES == 0:
        return _dot(jnp.concatenate([a1, a2], axis=1), _rows(b1, b2))
    return _dot(a1, b1) + _dot(a2, b2)


def _head_sums(ones_blk, T, *xs):
    parts = []
    for x in xs:
        parts += _split_bf16(x, 2)
    sums = jnp.dot(_rows(*parts), ones_blk, preferred_element_type=F32)
    return [sums[2 * i * T:(2 * i + 1) * T] + sums[(2 * i + 1) * T:(2 * i + 2) * T] for i in range(len(xs))]


def _rwkv_pairs(r, ld, k, v, kk, bi, g, rk, lnw, lnb, S, T):
    lo = lax.broadcasted_iota(jnp.int32, (1, LANES), 1) < B_HEAD
    head1 = lambda x: jnp.where(lo, x, 0.0)
    head2 = lambda x: jnp.where(lo, 0.0, x)
    ri = lax.broadcasted_iota(jnp.int32, (T, T), 0)
    ci = lax.broadcasted_iota(jnp.int32, (T, T), 1)
    tri = jnp.where(ci <= ri, 1.0, 0.0).astype(BF16)
    r2 = lax.broadcasted_iota(jnp.int32, (T, 2 * T), 0)
    c2 = lax.broadcasted_iota(jnp.int32, (T, 2 * T), 1)
    left = c2 < T
    cj = jnp.where(left, c2, c2 - T)
    strict2 = cj < r2
    incl2 = cj <= r2
    e0 = lax.broadcasted_iota(jnp.int32, (2 * T, 2 * T), 0)
    e1 = lax.broadcasted_iota(jnp.int32, (2 * T, 2 * T), 1)
    eye2 = jnp.where(e0 == e1, 1.0, 0.0)
    ones_blk = _head_block_mask(1.0).astype(BF16)
    same_head = _head_block_mask(1.0) > 0.5
    zero = jnp.zeros((T, LANES), F32)

    def cumsum(x):
        pieces = jnp.dot(tri, jnp.concatenate(_split_bf16(x, 3), axis=1), preferred_element_type=F32)
        return pieces[:, :LANES] + pieces[:, LANES:2 * LANES] + pieces[:, 2 * LANES:]

    cum = _each(cumsum, ld)
    ewt = _each(lambda c: jnp.exp(c[T - 1:T, :]), cum)
    Rt = _each(lambda x, c: x * jnp.exp(c), r, cum)
    At = _each(lambda x, c, l: -x * jnp.exp(c - l), kk, cum, ld)
    einv = _each(lambda c: jnp.exp(-c), cum)
    Bt = _each(jnp.multiply, bi, einv)
    Kt = _each(jnp.multiply, k, einv)
    Bh = _each(jnp.multiply, Bt, ewt)
    Kh = _each(jnp.multiply, Kt, ewt)
    V1 = _each(head1, v)
    V2 = _each(head2, v)
    G1 = _each(lambda a, x, b, kt: _dot_nt(_rows(head1(a), head1(x)), _rows(b, kt)), At, Rt, Bt, Kt)
    G2 = _each(lambda a, x, b, kt: _dot_nt(_rows(head2(a), head2(x)), _rows(kt, b)), At, Rt, Bt, Kt)
    LL1 = _each(lambda gm: jnp.where(strict2, gm[:T], 0.0), G1)
    LL2 = _each(lambda gm: jnp.where(strict2, gm[:T], 0.0), G2)
    P1 = _each(lambda gm: jnp.where(incl2, gm[T:], 0.0), G1)
    P2 = _each(lambda gm: jnp.where(incl2, gm[T:], 0.0), G2)
    pw = _each(lambda a, b: _rows(jnp.where(left, a, 0.0), jnp.where(left, 0.0, b)), LL1, LL2)
    inv = _each(lambda m: eye2 + m, pw)
    span = 2
    while span < T:
        pw = _each(lambda m: _dot(m, m), pw)
        inv = _each(lambda i, m: i + _dot(i, m), inv, pw)
        span *= 2
    IV = _each(lambda i: i[:T, :] + i[T:, :], inv)
    X0 = _each(lambda a, b, x1, x2: _kcat_dot(a, _rows(zero, x1), b, _rows(x2, zero)), LL1, LL2, V1, V2)
    AU = _each(lambda iv, a, x: _dot(iv, jnp.concatenate([_rows(head1(a), head2(a)),
                                                          _rows(head1(x), head2(x))], axis=1)), IV, At, X0)
    Ah = _each(lambda m: m[:, :LANES], AU)
    U0 = _each(lambda m: m[:, LANES:], AU)
    RO = _each(lambda p1, p2, a, u, x1, x2: _kcat_dot(
        p1, jnp.concatenate([_rows(head1(a), zero), _rows(head1(u), x1)], axis=1),
        p2, jnp.concatenate([_rows(zero, head2(a)), _rows(x2, head2(u))], axis=1)), P1, P2, Ah, U0, V1, V2)
    Rh = _each(lambda x, m: x + m[:, :LANES], Rt, RO)
    O0 = _each(lambda m: m[:, LANES:], RO)
    MN = _each(_dot_tn, AU, Bh)
    VK = _each(_dot_tn, v, Kh)
    Mab = _each(lambda m: jnp.where(same_head, m[:LANES], 0.0), MN)
    N0 = _each(lambda m, x: jnp.where(same_head, m[LANES:] + x, 0.0), MN, VK)
    o = _each(lambda x, s, o0: _dot_nt(x, s) + o0, Rh, S, O0)
    S_new = _each(lambda s, w, m, n0: s * w + _dot(s, m) + n0, S, ewt, Mab, N0)
    sums = _each(lambda x, rr, kx, w: _head_sums(ones_blk, T, x, rr * kx * w), o, r, k, rk)
    oc = _each(lambda x, sm: x - sm[0] * (1.0 / B_HEAD), o, sums)
    var = _each(lambda x: _head_sums(ones_blk, T, x * x)[0] * (1.0 / B_HEAD), oc)
    hb = _each(lambda x, vr, w, b, sm, vv, gg: (x * lax.rsqrt(vr + GN_EPS) * w + b + sm[1] * vv) * gg,
               oc, var, lnw, lnb, sums, v, g)
    return hb, S_new


def _rwkv_body(ur_ref, uk_ref, uv_ref, ul_ref, gb_ref, er_ref, ek_ref, ev_ref, el_ref,
               mur_ref, muk_ref, muv_ref, mul_ref, w0_ref, w2_ref, a0_ref, a2_ref, g2_ref, kkw_ref, ka_ref,
               rk_ref, lnw_ref, lnb_ref, s0_ref,
               hb_ref, sn_ref, s_s, cr_s, ck_s, cv_s, cl_s, *, T, n_pairs):
    c = pl.program_id(2)

    @pl.when(c == 0)
    def _():
        s_s[...] = s0_ref[0]
        cr_s[...] = er_ref[0]
        ck_s[...] = ek_ref[0]
        cv_s[...] = ev_ref[0]
        cl_s[...] = el_ref[0]

    def shift_mix(u_ref, carry, mu_ref):
        u = u_ref[...]
        xs = u + mu_ref[...] * (_shift_rows(u, carry[...], 1) - u)
        carry[...] = u[T - SUBLANES:, :]
        return xs

    xr = shift_mix(ur_ref, cr_s, mur_ref)
    xk = shift_mix(uk_ref, ck_s, muk_ref)
    xv = shift_mix(uv_ref, cv_s, muv_ref)
    xl = shift_mix(ul_ref, cl_s, mul_ref)
    lw = _dot(jnp.tanh(xl[:, TAIL_WD:TAIL_WD + LANES]), w2_ref[...])
    ld = -jnp.exp(-_softplus(-(w0_ref[...] + lw)) - 0.5)
    a = jax.nn.sigmoid(a0_ref[...] + _dot(xl[:, TAIL_AD:TAIL_AD + LANES], a2_ref[...]))
    gate = _dot(jax.nn.sigmoid(xl[:, TAIL_GD:TAIL_GD + LORA_GATE]), g2_ref[...]) * jax.nn.sigmoid(gb_ref[...])
    kmod = xk * (1.0 + (a - 1.0) * ka_ref[...])
    kkw = xk * kkw_ref[...]

    cols = [slice(p * LANES, (p + 1) * LANES) for p in range(n_pairs)]
    pick = lambda x: [x[:, cs] for cs in cols]
    ones_blk = _head_block_mask(1.0).astype(BF16)
    kk_p = pick(kkw)
    ssq = _each(lambda x: _head_sums(ones_blk, T, x * x)[0], kk_p)
    kn = _each(lambda x, s: x / jnp.maximum(jnp.sqrt(s), 1e-12), kk_p, ssq)
    bi = _each(jnp.multiply, kn, pick(a))
    hb, s_new = _rwkv_pairs(pick(xr), pick(ld), pick(kmod), pick(xv), kn, bi, pick(gate),
                            pick(rk_ref[...]), pick(lnw_ref[...]), pick(lnb_ref[...]),
                            [s_s[p] for p in range(n_pairs)], T)
    for p in range(n_pairs):
        hb_ref[:, cols[p]] = hb[p].astype(hb_ref.dtype)
        s_s[p] = s_new[p]

    @pl.when(c == pl.num_programs(2) - 1)
    def _():
        sn_ref[0] = s_s[...]


def _rwkv(proj, e_main, e_tail, mu_main, mu_tail, w0, w2p, a0, a2p, g2, k_k, k_a, rk, lnw, lnb, s0, *,
          T, n_seq, n_chunk):
    assert T <= RWKV_CHUNK
    gw = GROUP_W
    wide = lambda col: pl.BlockSpec((T, gw), lambda s, pg, c: (s * n_chunk + c, col // gw + pg))
    edge = lambda col: pl.BlockSpec((1, SUBLANES, gw), lambda s, pg, c: (s, 0, col // gw + pg))
    vec = lambda col: pl.BlockSpec((1, gw), lambda s, pg, c: (0, col // gw + pg))
    mat = lambda rows: pl.BlockSpec((rows, gw), lambda s, pg, c: (0, pg))
    st = pl.BlockSpec((1, GROUP_PAIRS, LANES, LANES), lambda s, pg, c: (s, pg, 0, 0))
    return pl.pallas_call(
        functools.partial(_rwkv_body, T=T, n_pairs=GROUP_PAIRS),
        grid=(n_seq, N_PAIRS // GROUP_PAIRS, n_chunk),
        in_specs=[wide(COL_R), wide(COL_K), wide(COL_VB),
                  pl.BlockSpec((T, TAIL_MIX), lambda s, pg, c: (s * n_chunk + c, COL_TAIL // TAIL_MIX)),
                  wide(COL_GB),
                  edge(0), edge(B_W), edge(2 * B_W),
                  pl.BlockSpec((1, SUBLANES, TAIL_MIX), lambda s, pg, c: (s, 0, 0)),
                  vec(0), vec(B_W), vec(2 * B_W),
                  pl.BlockSpec((1, TAIL_MIX), lambda s, pg, c: (0, 0)),
                  vec(0), mat(LANES), vec(0), mat(LANES), mat(LORA_GATE), vec(0), vec(0),
                  vec(0), vec(0), vec(0), st],
        out_specs=[pl.BlockSpec((T, gw), lambda s, pg, c: (s * n_chunk + c, pg)), st],
        out_shape=[jax.ShapeDtypeStruct((n_seq * n_chunk * T, B_W), BF16),
                   jax.ShapeDtypeStruct((n_seq, N_PAIRS, LANES, LANES), F32)],
        scratch_shapes=[pltpu.VMEM((GROUP_PAIRS, LANES, LANES), F32),
                        pltpu.VMEM((SUBLANES, gw), F32), pltpu.VMEM((SUBLANES, gw), F32),
                        pltpu.VMEM((SUBLANES, gw), F32), pltpu.VMEM((SUBLANES, TAIL_MIX), F32)],
        compiler_params=_cparams(("arbitrary", "arbitrary", "arbitrary"), 32),
        name="rwkv_chunk",
    )(proj, proj, proj, proj, proj, e_main, e_main, e_main, e_tail, mu_main, mu_main, mu_main, mu_tail,
      w0, w2p, a0, a2p, g2, k_k, k_a, rk, lnw, lnb, s0)


def _merge_out_body(ha_ref, hb_ref, x_ref, w_ref, g_ref, y_ref):
    mixed = ha_ref[...].astype(F32) + hb_ref[...].astype(F32)
    y = jnp.dot(mixed.astype(BF16), w_ref[...], preferred_element_type=F32)
    ms = jnp.mean(y * y, axis=-1, keepdims=True)
    y_ref[...] = x_ref[...] + y * lax.rsqrt(ms + EPS) * g_ref[...]


def _merge_out(h_a, h_b, x, w_out, g_post):
    n = x.shape[0]
    wide = pl.BlockSpec((MERGE_TM, D_MODEL), lambda i: (i, 0))
    return pl.pallas_call(
        _merge_out_body,
        grid=(n // MERGE_TM,),
        in_specs=[wide, wide, wide,
                  pl.BlockSpec((D_MODEL, D_MODEL), lambda i: (0, 0)),
                  pl.BlockSpec((1, D_MODEL), lambda i: (0, 0))],
        out_specs=wide,
        out_shape=jax.ShapeDtypeStruct((n, D_MODEL), F32),
        compiler_params=_cparams(("parallel",), 48),
        name="merge_out_proj",
    )(h_a, h_b, x, w_out, g_post)


def _mlp_body(x_ref, g1_ref, wu_ref, wd_ref, g2_ref, y_ref, hn_ref, acc_ref, *, row_chunk):
    j = pl.program_id(1)

    @pl.when(j == 0)
    def _():
        _norm_rows(x_ref, g1_ref, hn_ref, row_chunk)
        acc_ref[...] = jnp.zeros_like(acc_ref)

    u = jnp.dot(hn_ref[...], wu_ref[...], preferred_element_type=F32)
    a = jnp.square(jnp.maximum(u, 0.0))
    acc_ref[...] += jnp.dot(a.astype(BF16), wd_ref[...], preferred_element_type=F32)

    @pl.when(j == pl.num_programs(1) - 1)
    def _():
        def body(c, carry):
            rows = pl.ds(pl.multiple_of(c * row_chunk, row_chunk), row_chunk)
            ff = acc_ref[rows, :]
            ms = jnp.mean(ff * ff, axis=-1, keepdims=True)
            y_ref[rows, :] = x_ref[rows, :] + ff * lax.rsqrt(ms + EPS) * g2_ref[...]
            return carry
        lax.fori_loop(0, x_ref.shape[0] // row_chunk, body, 0)


def _mlp(x, g_pre, w_up, w_down, g_post):
    n, d = x.shape
    f = w_up.shape[1]
    return pl.pallas_call(
        functools.partial(_mlp_body, row_chunk=LANES),
        grid=(n // MLP_TM, f // MLP_TF),
        in_specs=[pl.BlockSpec((MLP_TM, d), lambda i, j: (i, 0)),
                  pl.BlockSpec((1, d), lambda i, j: (0, 0)),
                  pl.BlockSpec((d, MLP_TF), lambda i, j: (0, j)),
                  pl.BlockSpec((MLP_TF, d), lambda i, j: (j, 0)),
                  pl.BlockSpec((1, d), lambda i, j: (0, 0))],
        out_specs=pl.BlockSpec((MLP_TM, d), lambda i, j: (i, 0)),
        out_shape=jax.ShapeDtypeStruct((n, d), F32),
        scratch_shapes=[pltpu.VMEM((MLP_TM, d), BF16), pltpu.VMEM((MLP_TM, d), F32)],
        compiler_params=_cparams(("parallel", "arbitrary"), 56),
        name="mlp",
    )(x, g_pre, w_up, w_down, g_post)


def _pad_cols(a, width):
    return jnp.pad(a, [(0, 0)] * (a.ndim - 1) + [(0, width - a.shape[-1])])


def _tail_layout(x):
    wd = x[..., :LORA_DECAY]
    ad = x[..., LORA_DECAY:LORA_DECAY + LORA_ICLR]
    gd = x[..., LORA_DECAY + LORA_ICLR:]
    return jnp.concatenate([_pad_cols(wd, LANES), _pad_cols(ad, LANES), gd], axis=-1)


def _tail_unlayout(x):
    return jnp.concatenate([x[..., TAIL_WD:TAIL_WD + LORA_DECAY], x[..., TAIL_AD:TAIL_AD + LORA_ICLR],
                            x[..., TAIL_GD:TAIL_GD + LORA_GATE]], axis=-1)


def _edge_blocks(prev_rows):
    return jnp.pad(prev_rows, ((0, 0), (SUBLANES - prev_rows.shape[1], 0), (0, 0)))


def _pack_state(s):
    n = s.shape[0]
    s = s.reshape(n, N_PAIRS, 2, B_HEAD, B_HEAD)
    z = jnp.zeros((n, N_PAIRS, B_HEAD, B_HEAD), s.dtype)
    top = jnp.concatenate([s[:, :, 0], z], axis=-1)
    bot = jnp.concatenate([z, s[:, :, 1]], axis=-1)
    return jnp.concatenate([top, bot], axis=-2)


def _unpack_state(sp):
    n = sp.shape[0]
    a = sp[:, :, :B_HEAD, :B_HEAD]
    b = sp[:, :, B_HEAD:, B_HEAD:]
    return jnp.stack([a, b], axis=2).reshape(n, B_HEADS, B_HEAD, B_HEAD)


def _gates_row_form(proj, T):
    g = proj[:, COL_TAIL + TAIL_IF:COL_TAIL + TAIL_IF + 2 * A_HEADS]
    return jnp.swapaxes(g.reshape(-1, T, 2 * A_HEADS), 1, 2)


def kernel(x_prompt, x_sample, state_mlstm_C, state_mlstm_n, state_mlstm_m, state_mlstm_conv, state_rwkv_S, state_rwkv_shift, meta_tokens, norm_mix_pre, norm_mix_post, norm_mlp_pre, norm_mlp_post, w_in, mlstm_conv_w, mlstm_conv_b, mlstm_b_i, mlstm_b_f, mlstm_norm_w, rwkv_mu, rwkv_w0, rwkv_w2, rwkv_a0, rwkv_a2, rwkv_g2, rwkv_k_k, rwkv_k_a, rwkv_r_k, rwkv_ln_w, rwkv_ln_b, w_out, w_up, w_down):
    n_s_seq, s_len, _ = x_sample.shape
    _, p_len, _ = x_prompt.shape
    k3 = A_CONV - 1

    w = w_in[0]
    c_if = CONV_COLS + 2 * A_VW
    c_sh = c_if + 2 * A_HEADS
    c_lora = c_sh + 3 * B_W
    c_gate = c_sh + SHIFT_COLS
    bf = lambda t: t.astype(BF16)
    z = lambda width: jnp.zeros((D_MODEL, width), BF16)
    w_all = jnp.concatenate([
        bf(w[:, :c_if]), bf(w[:, c_sh:c_lora]), bf(w[:, c_gate:]),
        bf(w[:, c_lora:c_lora + LORA_DECAY]), z(LANES - LORA_DECAY),
        bf(w[:, c_lora + LORA_DECAY:c_lora + LORA_DECAY + LORA_ICLR]), z(LANES - LORA_ICLR),
        bf(w[:, c_lora + LORA_DECAY + LORA_ICLR:c_gate]),
        bf(w[:, c_if:c_sh]), z(PROJ_COLS - COL_TAIL - TAIL_IF - 2 * A_HEADS)], axis=1)
    mu = rwkv_mu[0][None, :]
    mu_main = mu[:, :3 * B_W]
    mu_tail = _tail_layout(mu[:, 3 * B_W:])
    w2p = jnp.pad(rwkv_w2[0], ((0, LANES - LORA_DECAY), (0, 0))).astype(BF16)
    a2p = jnp.pad(rwkv_a2[0], ((0, LANES - LORA_ICLR), (0, 0))).astype(BF16)
    g2 = rwkv_g2[0].astype(BF16)
    g_pre = norm_mix_pre[0][None, :]
    bias = jnp.concatenate([mlstm_b_i[0], mlstm_b_f[0]])
    bias_row = _pad_cols(bias[None, :], LANES)
    bias_col = bias[:, None]
    norm_w = mlstm_norm_w[0][None, :]
    conv_w = mlstm_conv_w[0]
    conv_b = mlstm_conv_b[0][None, :]
    rwkv_vecs = (rwkv_w0[0][None, :], w2p, rwkv_a0[0][None, :], a2p, g2, rwkv_k_k[0][None, :],
                 rwkv_k_a[0][None, :], rwkv_r_k[0].reshape(1, B_W), rwkv_ln_w[0][None, :], rwkv_ln_b[0][None, :])
    w_out_b = w_out[0].astype(BF16)
    w_up_b = w_up[0].astype(BF16)
    w_down_b = w_down[0].astype(BF16)

    def branches(x_rows, tm, T, n_seq, n_chunk, e_conv, e_main, e_tail, c0, n0, m0, s0):
        proj = _norm_matmul(x_rows, g_pre, w_all, tm)
        h_a, c_n, n_n, m_n = _mlstm(proj, _gates_row_form(proj, T), e_conv, conv_w, conv_b, bias_row, bias_col,
                                    norm_w, c0, n0, m0, T=T, n_seq=n_seq, n_chunk=n_chunk)
        h_b, s_n = _rwkv(proj, e_main, e_tail, mu_main, mu_tail, *rwkv_vecs, s0,
                         T=T, n_seq=n_seq, n_chunk=n_chunk)
        return proj, h_a, h_b, (c_n, n_n, m_n, s_n)

    def finish(h_a, h_b, x_rows):
        x1 = _merge_out(h_a, h_b, x_rows, w_out_b, norm_mix_post[0][None, :])
        return _mlp(x1, norm_mlp_pre[0][None, :], w_up_b, w_down_b, norm_mlp_post[0][None, :])

    zeros = lambda *shape: jnp.zeros(shape, F32)
    proj_m, _, _, (c_m, n_m, m_m, s_m) = branches(
        meta_tokens.astype(F32), N_META, N_META, 1, 1,
        zeros(1, SUBLANES, CONV_COLS), zeros(1, SUBLANES, 3 * B_W), zeros(1, SUBLANES, TAIL_MIX),
        zeros(1, A_HEADS, A_DK, A_DV), zeros(1, A_HEADS, A_DK), zeros(1, 1, LANES),
        zeros(1, N_PAIRS, LANES, LANES))

    xp = x_prompt[0].astype(F32)
    proj_p, ha_p, hb_p, (c_p, n_p, m_p, s_p) = branches(
        xp, PROJ_TM, MLSTM_CHUNK, 1, p_len // MLSTM_CHUNK,
        _edge_blocks(proj_m[None, N_META - k3:, COL_CONV:COL_CONV + CONV_COLS]),
        _edge_blocks(proj_m[None, N_META - 1:, COL_R:COL_R + 3 * B_W]),
        _edge_blocks(proj_m[None, N_META - 1:, COL_TAIL:COL_TAIL + TAIL_MIX]),
        c_m, n_m, m_m, s_m)
    y_prompt = finish(ha_p, hb_p, xp)[None].astype(x_prompt.dtype)

    xs = x_sample.reshape(n_s_seq * s_len, D_MODEL).astype(F32)
    sh = state_rwkv_shift[0].astype(F32)
    m0_s = jnp.pad(state_mlstm_m[0].astype(F32), ((0, 0), (A_HEADS, LANES - 2 * A_HEADS)))[:, None, :]
    proj_s, ha_s, hb_s, (c_s, n_s, m_s, s_s) = branches(
        xs, PROJ_TM, s_len, n_s_seq, 1,
        _edge_blocks(state_mlstm_conv[0].astype(F32)),
        _edge_blocks(sh[:, :, :3 * B_W]), _edge_blocks(_tail_layout(sh[:, :, 3 * B_W:])),
        state_mlstm_C[0].astype(F32), state_mlstm_n[0].astype(F32), m0_s,
        _pack_state(state_rwkv_S[0].astype(F32)))
    y_sample = finish(ha_s, hb_s, xs).reshape(x_sample.shape).astype(x_sample.dtype)

    def shift_state(rows):
        return jnp.concatenate([rows[..., COL_R:COL_R + 3 * B_W],
                                _tail_unlayout(rows[..., COL_TAIL:COL_TAIL + TAIL_MIX])], axis=-1)

    last_s = proj_s.reshape(n_s_seq, s_len, PROJ_COLS)[:, s_len - k3:, :]
    conv_s = last_s[..., COL_CONV:COL_CONV + CONV_COLS]
    shift_s = shift_state(last_s[:, k3 - 1:, :])
    last_p = proj_p[p_len - k3:, :]
    conv_p = last_p[None, :, COL_CONV:COL_CONV + CONV_COLS]
    shift_p = shift_state(last_p[None, k3 - 1:, :])
    dt_c, dt_n, dt_m = state_mlstm_C.dtype, state_mlstm_n.dtype, state_mlstm_m.dtype
    dt_cv, dt_s, dt_sh = state_mlstm_conv.dtype, state_rwkv_S.dtype, state_rwkv_shift.dtype
    lanes_m = slice(A_HEADS, 2 * A_HEADS)
    return (y_prompt, y_sample,
            c_p[None].astype(dt_c), n_p[None].astype(dt_n), m_p[:, 0, lanes_m][None].astype(dt_m),
            conv_p[None].astype(dt_cv), _unpack_state(s_p)[None].astype(dt_s), shift_p[None].astype(dt_sh),
            c_s[None].astype(dt_c), n_s[None].astype(dt_n), m_s[:, 0, lanes_m][None].astype(dt_m),
            conv_s[None].astype(dt_cv), _unpack_state(s_s)[None].astype(dt_s), shift_s[None].astype(dt_sh))
```

```python
import functools

import jax
import jax.numpy as jnp
from jax import lax
from jax.experimental import pallas as pl
from jax.experimental.pallas import tpu as pltpu

F32 = jnp.float32
BF16 = jnp.bfloat16

D_MODEL = 2048
N_META = 16
EPS = 1e-6
D_FF = 4 * D_MODEL
A_HEADS = 8
A_DK = 128
A_DV = D_MODEL // A_HEADS
A_QK = A_HEADS * A_DK
A_VW = A_HEADS * A_DV
A_CONV = 4
MLSTM_CHUNK = 64
B_HEAD = 64
B_HEADS = D_MODEL // B_HEAD
B_W = B_HEADS * B_HEAD
LORA_DECAY = max(32, int(round(1.8 * D_MODEL ** 0.5 / 32)) * 32)
LORA_ICLR = max(32, int(round(1.8 * D_MODEL ** 0.5 / 32)) * 32)
LORA_GATE = max(32, int(round(0.6 * D_MODEL ** 0.8 / 32)) * 32)
GN_EPS = 64e-5
CONV_COLS = 2 * A_QK
SHIFT_COLS = 3 * B_W + LORA_DECAY + LORA_ICLR + LORA_GATE

LANES = 128
SUBLANES = 8

COL_CONV, COL_V, COL_O, COL_R, COL_K, COL_VB, COL_GA, COL_GB = (i * D_MODEL for i in range(8))
COL_TAIL = 8 * D_MODEL
LORA_COLS = LORA_DECAY + LORA_ICLR + LORA_GATE
TAIL_MIX = -(-LORA_COLS // LANES) * LANES
TAIL_IF = TAIL_MIX
TAIL_GD_WIN = (LORA_DECAY + LORA_ICLR) // LANES * LANES
PROJ_TN = 1024
PROJ_COLS = -(-(COL_TAIL + TAIL_IF + LANES) // PROJ_TN) * PROJ_TN
PROJ_TM = 1024
MERGE_TM = 256
MLP_TM, MLP_TF = 512, 1024
RWKV_CHUNK = 64
N_PAIRS = B_HEADS // 2
GROUP_PAIRS = 8
GROUP_W = GROUP_PAIRS * LANES


def _cparams(semantics, vmem_mib):
    return pltpu.CompilerParams(dimension_semantics=semantics, vmem_limit_bytes=vmem_mib << 20)


def _softplus(x):
    return jnp.maximum(x, 0.0) + jnp.log1p(jnp.exp(-jnp.abs(x)))


def _dot(a, b):
    return jnp.dot(a.astype(BF16), b.astype(BF16), preferred_element_type=F32)


def _dot_nt(a, b):
    return lax.dot_general(a.astype(BF16), b.astype(BF16), (((1,), (1,)), ((), ())),
                           preferred_element_type=F32)


def _dot_tn(a, b):
    return lax.dot_general(a.astype(BF16), b.astype(BF16), (((0,), (0,)), ((), ())),
                           preferred_element_type=F32)


def _each(f, *lists):
    return [f(*args) for args in zip(*lists)]


def _rows(*parts):
    return jnp.concatenate(parts, axis=0)


def _split_bf16(x, pieces):
    out = []
    for _ in range(pieces - 1):
        p = x.astype(BF16)
        out.append(p)
        x = x - p.astype(F32)
    out.append(x.astype(BF16))
    return out


def _dot_exact_rhs(x, m, pieces=3):
    mb = m.astype(BF16)
    acc = None
    for p in _split_bf16(x, pieces):
        t = jnp.dot(p, mb, preferred_element_type=F32)
        acc = t if acc is None else acc + t
    return acc


def _sigmoid(x):
    return 0.5 * jnp.tanh(0.5 * x) + 0.5


def _stage_rows(ext_ref, u):
    ext_ref[SUBLANES:, :] = u


def _rows_before(ext_ref, sh, T):
    return ext_ref[pl.ds(SUBLANES - sh, T), :]


def _keep_last_rows(ext_ref, u, T):
    ext_ref[:SUBLANES, :] = u[T - SUBLANES:, :]


def _cumsum_rows(x):
    n = x.shape[0]
    row = lax.broadcasted_iota(jnp.int32, x.shape, 0)
    step = 1
    while step < n:
        x = x + jnp.where(row >= step, pltpu.roll(x, step, 0), 0.0)
        step *= 2
    return x


def _head_block_mask(scale):
    a = lax.broadcasted_iota(jnp.int32, (LANES, LANES), 0) // B_HEAD
    b = lax.broadcasted_iota(jnp.int32, (LANES, LANES), 1) // B_HEAD
    return jnp.where(a == b, scale, 0.0).astype(F32)


SRC_IF = CONV_COLS + 2 * A_VW
SRC_SHIFT = SRC_IF + 2 * A_HEADS
SRC_GATE = SRC_SHIFT + SHIFT_COLS
SRC_COLS = SRC_GATE + 2 * D_MODEL
RELAYOUT_ROWS = 128


def _rotate_left(x, off):
    return pltpu.roll(x, x.shape[1] - off, 1)


def _relayout_body(w_ref, wlast_ref, o_ref):
    cast = lambda x: x.astype(BF16)
    o_ref[:, :SRC_IF] = cast(w_ref[:, :SRC_IF])
    o_ref[:, COL_TAIL + TAIL_IF:COL_TAIL + TAIL_IF + LANES] = cast(w_ref[:, SRC_IF:SRC_IF + LANES])
    o_ref[:, COL_TAIL + TAIL_IF + LANES:] = jnp.zeros((o_ref.shape[0], PROJ_COLS - COL_TAIL - TAIL_IF - LANES), BF16)
    a0 = SRC_SHIFT // LANES * LANES
    run = 3 * B_W + TAIL_MIX
    x = _rotate_left(w_ref[:, a0:a0 + run + LANES], SRC_SHIFT - a0)
    o_ref[:, COL_R:COL_R + 3 * B_W] = cast(x[:, :3 * B_W])
    o_ref[:, COL_TAIL:COL_TAIL + TAIL_MIX] = cast(x[:, 3 * B_W:run])
    g0 = SRC_GATE // LANES * LANES
    whole = (SRC_COLS - g0) // LANES * LANES
    y = _rotate_left(jnp.concatenate([w_ref[:, g0:g0 + whole], wlast_ref[...]], axis=1), SRC_GATE - g0)
    o_ref[:, COL_GA:COL_GA + 2 * D_MODEL] = cast(y[:, :2 * D_MODEL])


def _relayout_weight(w):
    d = w.shape[0]
    assert w.shape[1] == SRC_COLS and SRC_SHIFT // LANES * LANES + 3 * B_W + TAIL_MIX + LANES <= SRC_COLS
    return pl.pallas_call(
        _relayout_body,
        grid=(d // RELAYOUT_ROWS,),
        in_specs=[pl.BlockSpec((RELAYOUT_ROWS, SRC_COLS), lambda i: (i, 0)),
                  pl.BlockSpec((RELAYOUT_ROWS, LANES), lambda i: (i, SRC_COLS // LANES))],
        out_specs=pl.BlockSpec((RELAYOUT_ROWS, PROJ_COLS), lambda i: (i, 0)),
        out_shape=jax.ShapeDtypeStruct((d, PROJ_COLS), BF16),
        compiler_params=_cparams(("parallel",), 48),
        name="relayout_w_in",
    )(w, w)


def _norm_rows(x_ref, g_ref, out_ref, row_chunk):
    def body(c, carry):
        rows = pl.ds(pl.multiple_of(c * row_chunk, row_chunk), row_chunk)
        x = x_ref[rows, :]
        ms = jnp.mean(x * x, axis=-1, keepdims=True)
        out_ref[rows, :] = (x * lax.rsqrt(ms + EPS) * g_ref[...]).astype(out_ref.dtype)
        return carry
    lax.fori_loop(0, x_ref.shape[0] // row_chunk, body, 0)


def _norm_matmul_body(x_ref, g_ref, w_ref, o_ref, hn_ref, *, row_chunk):
    @pl.when(pl.program_id(1) == 0)
    def _():
        _norm_rows(x_ref, g_ref, hn_ref, row_chunk)

    o_ref[...] = jnp.dot(hn_ref[...], w_ref[...], preferred_element_type=F32)


def _norm_matmul(x, g, w, tm):
    n, d = x.shape
    p = w.shape[1]
    return pl.pallas_call(
        functools.partial(_norm_matmul_body, row_chunk=min(tm, LANES)),
        grid=(n // tm, p // PROJ_TN),
        in_specs=[pl.BlockSpec((tm, d), lambda i, j: (i, 0)),
                  pl.BlockSpec((1, d), lambda i, j: (0, 0)),
                  pl.BlockSpec((d, PROJ_TN), lambda i, j: (0, j))],
        out_specs=pl.BlockSpec((tm, PROJ_TN), lambda i, j: (i, j)),
        out_shape=jax.ShapeDtypeStruct((n, p), F32),
        scratch_shapes=[pltpu.VMEM((tm, d), BF16)],
        compiler_params=_cparams(("parallel", "arbitrary"), 48),
        name="in_proj",
    )(x, g, w)


def _mlstm_body(u_ref, v_ref, o_ref, ga_ref, gc_ref, gr_ref, e_ref, cw_ref, cb_ref, brow_ref, bcol_ref,
                nw_ref, c0_ref, n0_ref, m0_ref,
                h_ref, cn_ref, nn_ref, mn_ref, c_s, n_s, m_s, e_s, *, T):
    c = pl.program_id(1)

    @pl.when(c == 0)
    def _():
        c_s[...] = c0_ref[0]
        n_s[...] = n0_ref[0]
        m_s[...] = m0_ref[0]
        e_s[:SUBLANES, :] = e_ref[0]

    u = u_ref[...]
    _stage_rows(e_s, u)
    w = cw_ref[...]
    acc = cb_ref[...] + u * w[A_CONV - 1:A_CONV, :]
    for sh in range(1, A_CONV):
        acc = acc + _rows_before(e_s, sh, T) * w[A_CONV - 1 - sh:A_CONV - sh, :]
    _keep_last_rows(e_s, u, T)
    qk_act = acc * _sigmoid(acc)

    gcol = gc_ref[...] + brow_ref[...]
    grow = gr_ref[0] + bcol_ref[...]
    lf_col = -_softplus(-gcol)
    ig_row = grow[:A_HEADS, :]
    lf_row = -_softplus(-grow[A_HEADS:, :])
    ri = lax.broadcasted_iota(jnp.int32, (T, T), 0)
    ci = lax.broadcasted_iota(jnp.int32, (T, T), 1)
    tri = ci <= ri
    b_col = _cumsum_rows(lf_col)
    b_row = _dot_exact_rhs(lf_row, jnp.where(ri <= ci, 1.0, 0.0))
    m_prev = m_s[...]
    lane = lax.broadcasted_iota(jnp.int32, (1, LANES), 1)
    heads = list(range(A_HEADS))
    bc = [b_col[:, A_HEADS + h:A_HEADS + h + 1] for h in heads]
    mh = [m_prev[:, A_HEADS + h:A_HEADS + h + 1] for h in heads]
    br = [b_row[h:h + 1, :] for h in heads]
    igr = [ig_row[h:h + 1, :] for h in heads]
    igc = [gcol[:, h:h + 1] for h in heads]
    qh = [qk_act[:, h * A_DK:(h + 1) * A_DK] for h in heads]
    kh = [qk_act[:, A_QK + h * A_DK:A_QK + (h + 1) * A_DK] * (A_DK ** -0.5) for h in heads]
    vh = [v_ref[:, h * A_DV:(h + 1) * A_DV] for h in heads]
    ch = [c_s[h] for h in heads]
    nh = [n_s[h:h + 1, :] for h in heads]
    qk = _each(_dot_nt, qh, kh)
    qc = _each(_dot, qh, ch)
    d = _each(lambda a, b, i: jnp.where(tri, a - b + i, -jnp.inf), bc, br, igr)
    inter = _each(jnp.add, bc, mh)
    m_t = _each(lambda i, x: jnp.maximum(i, jnp.max(x, axis=-1, keepdims=True)), inter, d)
    s = _each(lambda a, x, m: a * jnp.exp(x - m), qk, d, m_t)
    e_in = _each(lambda i, m: jnp.exp(i - m), inter, m_t)
    sv = _each(_dot, s, vh)
    num = _each(lambda a, x, b: a + x * b, sv, e_in, qc)
    den = _each(lambda a, x, qq, nn: jnp.sum(a, axis=-1, keepdims=True)
                + x * jnp.sum(qq * nn, axis=-1, keepdims=True), s, e_in, qh, nh)
    hh = _each(lambda a, b, m: a / jnp.maximum(jnp.abs(b), jnp.exp(-m)), num, den, m_t)
    hh = _each(lambda a: a * lax.rsqrt(jnp.mean(a * a, axis=-1, keepdims=True) + EPS), hh)
    b_end = _each(lambda a: a[T - 1:T, :], bc)
    m_new = _each(lambda be, m, b, i: jnp.maximum(be + m, jnp.max(be - b + i, axis=-1, keepdims=True)),
                  b_end, mh, br, igr)
    carry = _each(lambda be, m, mn: jnp.exp(be + m - mn), b_end, mh, m_new)
    wk = _each(lambda kk_, be, a, i, mn: kk_ * jnp.exp(be - a + i - mn), kh, b_end, bc, igc, m_new)
    kv = _each(_dot_tn, wk, vh)
    m_next = m_prev
    for h in heads:
        cols = slice(h * A_DV, (h + 1) * A_DV)
        gate = _sigmoid(o_ref[:, cols]) * _sigmoid(ga_ref[:, cols])
        h_ref[:, cols] = (hh[h] * nw_ref[:, cols] * gate).astype(h_ref.dtype)
        c_s[h] = carry[h] * ch[h] + kv[h]
        n_s[h:h + 1, :] = carry[h] * nh[h] + jnp.sum(wk[h], axis=0, keepdims=True)
        m_next = jnp.where(lane == A_HEADS + h, m_new[h], m_next)
    m_s[...] = m_next

    @pl.when(c == pl.num_programs(1) - 1)
    def _():
        cn_ref[0] = c_s[...]
        nn_ref[0] = n_s[...]
        mn_ref[0] = m_s[...]


def _mlstm(proj, gates_row, e_conv, conv_w, conv_b, bias_row, bias_col, norm_w, c0, n0, m0, *, T, n_seq, n_chunk):
    row = lambda s, c: (s * n_chunk + c, 0)
    fixed = lambda s, c: (0, 0)
    wide = lambda col: pl.BlockSpec((T, D_MODEL), lambda s, c: (s * n_chunk + c, col // D_MODEL))
    return pl.pallas_call(
        functools.partial(_mlstm_body, T=T),
        grid=(n_seq, n_chunk),
        in_specs=[wide(COL_CONV), wide(COL_V), wide(COL_O), wide(COL_GA),
                  pl.BlockSpec((T, LANES), lambda s, c: (s * n_chunk + c, (COL_TAIL + TAIL_IF) // LANES)),
                  pl.BlockSpec((1, 2 * A_HEADS, T), lambda s, c: (s * n_chunk + c, 0, 0)),
                  pl.BlockSpec((1, SUBLANES, CONV_COLS), lambda s, c: (s, 0, 0)),
                  pl.BlockSpec((A_CONV, CONV_COLS), fixed),
                  pl.BlockSpec((1, CONV_COLS), fixed),
                  pl.BlockSpec((1, LANES), fixed),
                  pl.BlockSpec((2 * A_HEADS, 1), fixed),
                  pl.BlockSpec((1, A_VW), fixed),
                  pl.BlockSpec((1, A_HEADS, A_DK, A_DV), lambda s, c: (s, 0, 0, 0)),
                  pl.BlockSpec((1, A_HEADS, A_DK), lambda s, c: (s, 0, 0)),
                  pl.BlockSpec((1, 1, LANES), lambda s, c: (s, 0, 0))],
        out_specs=[pl.BlockSpec((T, A_VW), row),
                   pl.BlockSpec((1, A_HEADS, A_DK, A_DV), lambda s, c: (s, 0, 0, 0)),
                   pl.BlockSpec((1, A_HEADS, A_DK), lambda s, c: (s, 0, 0)),
                   pl.BlockSpec((1, 1, LANES), lambda s, c: (s, 0, 0))],
        out_shape=[jax.ShapeDtypeStruct((n_seq * n_chunk * T, A_VW), BF16),
                   jax.ShapeDtypeStruct((n_seq, A_HEADS, A_DK, A_DV), F32),
                   jax.ShapeDtypeStruct((n_seq, A_HEADS, A_DK), F32),
                   jax.ShapeDtypeStruct((n_seq, 1, LANES), F32)],
        scratch_shapes=[pltpu.VMEM((A_HEADS, A_DK, A_DV), F32),
                        pltpu.VMEM((A_HEADS, A_DK), F32),
                        pltpu.VMEM((1, LANES), F32),
                        pltpu.VMEM((SUBLANES + T, CONV_COLS), F32)],
        compiler_params=_cparams(("arbitrary", "arbitrary"), 32),
        name="mlstm_chunk",
    )(proj, proj, proj, proj, proj, gates_row, e_conv, conv_w, conv_b, bias_row, bias_col, norm_w, c0, n0, m0)


def _kcat_dot(a1, b1, a2, b2):
    if a1.shape[1] % LANES == 0:
        return _dot(jnp.concatenate([a1, a2], axis=1), _rows(b1, b2))
    return _dot(a1, b1) + _dot(a2, b2)


def _head_sums(ones_blk, T, *xs):
    sums = jnp.dot(_rows(*xs).astype(BF16), ones_blk, preferred_element_type=F32)
    return [sums[i * T:(i + 1) * T] for i in range(len(xs))]


def _rwkv_pairs(r, ld, k, v, kk, bi, g, rk, lnw, lnb, S, T):
    lo = lax.broadcasted_iota(jnp.int32, (1, LANES), 1) < B_HEAD
    head1 = lambda x: jnp.where(lo, x, 0.0)
    head2 = lambda x: jnp.where(lo, 0.0, x)
    r2 = lax.broadcasted_iota(jnp.int32, (T, 2 * T), 0)
    c2 = lax.broadcasted_iota(jnp.int32, (T, 2 * T), 1)
    left = c2 < T
    cj = jnp.where(left, c2, c2 - T)
    strict2 = cj < r2
    incl2 = cj <= r2
    e0 = lax.broadcasted_iota(jnp.int32, (2 * T, 2 * T), 0)
    e1 = lax.broadcasted_iota(jnp.int32, (2 * T, 2 * T), 1)
    eye2 = jnp.where(e0 == e1, 1.0, 0.0)
    ones_blk = _head_block_mask(1.0).astype(BF16)
    same_head = _head_block_mask(1.0) > 0.5
    zero = jnp.zeros((T, LANES), F32)

    cum = _each(_cumsum_rows, ld)
    ewt = _each(lambda c: jnp.exp(c[T - 1:T, :]), cum)
    Rt = _each(lambda x, c: x * jnp.exp(c), r, cum)
    At = _each(lambda x, c, l: -x * jnp.exp(c - l), kk, cum, ld)
    einv = _each(lambda c: jnp.exp(-c), cum)
    Bt = _each(jnp.multiply, bi, einv)
    Kt = _each(jnp.multiply, k, einv)
    Bh = _each(jnp.multiply, Bt, ewt)
    Kh = _each(jnp.multiply, Kt, ewt)
    V1 = _each(head1, v)
    V2 = _each(head2, v)
    G1 = _each(lambda a, x, b, kt: _dot_nt(_rows(head1(a), head1(x)), _rows(b, kt)), At, Rt, Bt, Kt)
    G2 = _each(lambda a, x, b, kt: _dot_nt(_rows(head2(a), head2(x)), _rows(kt, b)), At, Rt, Bt, Kt)
    LL1 = _each(lambda gm: jnp.where(strict2, gm[:T], 0.0), G1)
    LL2 = _each(lambda gm: jnp.where(strict2, gm[:T], 0.0), G2)
    P1 = _each(lambda gm: jnp.where(incl2, gm[T:], 0.0), G1)
    P2 = _each(lambda gm: jnp.where(incl2, gm[T:], 0.0), G2)
    pw = _each(lambda a, b: _rows(jnp.where(left, a, 0.0), jnp.where(left, 0.0, b)), LL1, LL2)
    inv = _each(lambda m: eye2 + m, pw)
    span = 2
    while span < T:
        pw = _each(lambda m: _dot(m, m), pw)
        inv = _each(lambda i, m: i + _dot(i, m), inv, pw)
        span *= 2
    IV = _each(lambda i: i[:T, :] + i[T:, :], inv)
    X0 = _each(lambda a, b, x1, x2: _kcat_dot(a, _rows(zero, x1), b, _rows(x2, zero)), LL1, LL2, V1, V2)
    AU = _each(lambda iv, a, x: _dot(iv, jnp.concatenate([_rows(head1(a), head2(a)),
                                                          _rows(head1(x), head2(x))], axis=1)), IV, At, X0)
    Ah = _each(lambda m: m[:, :LANES], AU)
    U0 = _each(lambda m: m[:, LANES:], AU)
    RO = _each(lambda p1, p2, a, u, x1, x2: _kcat_dot(
        p1, jnp.concatenate([_rows(head1(a), zero), _rows(head1(u), x1)], axis=1),
        p2, jnp.concatenate([_rows(zero, head2(a)), _rows(x2, head2(u))], axis=1)), P1, P2, Ah, U0, V1, V2)
    Rh = _each(lambda x, m: x + m[:, :LANES], Rt, RO)
    O0 = _each(lambda m: m[:, LANES:], RO)
    MN = _each(_dot_tn, AU, Bh)
    VK = _each(_dot_tn, v, Kh)
    Mab = _each(lambda m: jnp.where(same_head, m[:LANES], 0.0), MN)
    N0 = _each(lambda m, x: jnp.where(same_head, m[LANES:] + x, 0.0), MN, VK)
    o = _each(lambda x, s, o0: _dot_nt(x, s) + o0, Rh, S, O0)
    S_new = _each(lambda s, w, m, n0: s * w + _dot(s, m) + n0, S, ewt, Mab, N0)
    sums = _each(lambda x, rr, kx, w: _head_sums(ones_blk, T, x, rr * kx * w), o, r, k, rk)
    oc = _each(lambda x, sm: x - sm[0] * (1.0 / B_HEAD), o, sums)
    var = _each(lambda x: _head_sums(ones_blk, T, x * x)[0] * (1.0 / B_HEAD), oc)
    hb = _each(lambda x, vr, w, b, sm, vv, gg: (x * lax.rsqrt(vr + GN_EPS) * w + b + sm[1] * vv) * gg,
               oc, var, lnw, lnb, sums, v, g)
    return hb, S_new


def _rwkv_body(ur_ref, uk_ref, uv_ref, ul_ref, gb_ref, er_ref, ek_ref, ev_ref, el_ref,
               mur_ref, muk_ref, muv_ref, mul_ref, w0_ref, w2_ref, a0_ref, a2_ref, g2_ref, kkw_ref, ka_ref,
               rk_ref, lnw_ref, lnb_ref, s0_ref,
               hb_ref, sn_ref, s_s, cr_s, ck_s, cv_s, cl_s, *, T, n_pairs):
    c = pl.program_id(2)

    @pl.when(c == 0)
    def _():
        s_s[...] = s0_ref[0]
        cr_s[:SUBLANES, :] = er_ref[0]
        ck_s[:SUBLANES, :] = ek_ref[0]
        cv_s[:SUBLANES, :] = ev_ref[0]
        cl_s[:SUBLANES, :] = el_ref[0]

    def shift_mix(u_ref, ext_ref, mu_ref):
        u = u_ref[...]
        _stage_rows(ext_ref, u)
        xs = u + mu_ref[...] * (_rows_before(ext_ref, 1, T) - u)
        _keep_last_rows(ext_ref, u, T)
        return xs

    xr = shift_mix(ur_ref, cr_s, mur_ref)
    xk = shift_mix(uk_ref, ck_s, muk_ref)
    xv = shift_mix(uv_ref, cv_s, muv_ref)
    xl = shift_mix(ul_ref, cl_s, mul_ref)
    lw = _dot(jnp.tanh(xl[:, :LANES]), w2_ref[...])
    ld = -jnp.exp(-_softplus(-(w0_ref[...] + lw)) - 0.5)
    a = _sigmoid(a0_ref[...] + _dot(xl[:, :2 * LANES], a2_ref[...]))
    gate = _dot(_sigmoid(xl[:, TAIL_GD_WIN:]), g2_ref[...]) * _sigmoid(gb_ref[...])
    kmod = xk * (1.0 + (a - 1.0) * ka_ref[...])
    kkw = xk * kkw_ref[...]

    cols = [slice(p * LANES, (p + 1) * LANES) for p in range(n_pairs)]
    pick = lambda x: [x[:, cs] for cs in cols]
    ones_blk = _head_block_mask(1.0).astype(BF16)
    kk_p = pick(kkw)
    ssq = _each(lambda x: _head_sums(ones_blk, T, x * x)[0], kk_p)
    kn = _each(lambda x, s: x / jnp.maximum(jnp.sqrt(s), 1e-12), kk_p, ssq)
    bi = _each(jnp.multiply, kn, pick(a))
    hb, s_new = _rwkv_pairs(pick(xr), pick(ld), pick(kmod), pick(xv), kn, bi, pick(gate),
                            pick(rk_ref[...]), pick(lnw_ref[...]), pick(lnb_ref[...]),
                            [s_s[p] for p in range(n_pairs)], T)
    for p in range(n_pairs):
        hb_ref[:, cols[p]] = hb[p].astype(hb_ref.dtype)
        s_s[p] = s_new[p]

    @pl.when(c == pl.num_programs(2) - 1)
    def _():
        sn_ref[0] = s_s[...]


def _rwkv(proj, e_main, e_tail, mu_main, mu_tail, w0, w2p, a0, a2p, g2, k_k, k_a, rk, lnw, lnb, s0, *,
          T, n_seq, n_chunk):
    assert T <= RWKV_CHUNK
    gw = GROUP_W
    wide = lambda col: pl.BlockSpec((T, gw), lambda s, pg, c: (s * n_chunk + c, col // gw + pg))
    edge = lambda col: pl.BlockSpec((1, SUBLANES, gw), lambda s, pg, c: (s, 0, col // gw + pg))
    vec = lambda col: pl.BlockSpec((1, gw), lambda s, pg, c: (0, col // gw + pg))
    mat = lambda rows: pl.BlockSpec((rows, gw), lambda s, pg, c: (0, pg))
    st = pl.BlockSpec((1, GROUP_PAIRS, LANES, LANES), lambda s, pg, c: (s, pg, 0, 0))
    return pl.pallas_call(
        functools.partial(_rwkv_body, T=T, n_pairs=GROUP_PAIRS),
        grid=(n_seq, N_PAIRS // GROUP_PAIRS, n_chunk),
        in_specs=[wide(COL_R), wide(COL_K), wide(COL_VB),
                  pl.BlockSpec((T, TAIL_MIX), lambda s, pg, c: (s * n_chunk + c, COL_TAIL // TAIL_MIX)),
                  wide(COL_GB),
                  edge(0), edge(B_W), edge(2 * B_W),
                  pl.BlockSpec((1, SUBLANES, TAIL_MIX), lambda s, pg, c: (s, 0, 0)),
                  vec(0), vec(B_W), vec(2 * B_W),
                  pl.BlockSpec((1, TAIL_MIX), lambda s, pg, c: (0, 0)),
                  vec(0), mat(LANES), vec(0), mat(2 * LANES), mat(TAIL_MIX - TAIL_GD_WIN), vec(0), vec(0),
                  vec(0), vec(0), vec(0), st],
        out_specs=[pl.BlockSpec((T, gw), lambda s, pg, c: (s * n_chunk + c, pg)), st],
        out_shape=[jax.ShapeDtypeStruct((n_seq * n_chunk * T, B_W), BF16),
                   jax.ShapeDtypeStruct((n_seq, N_PAIRS, LANES, LANES), F32)],
        scratch_shapes=[pltpu.VMEM((GROUP_PAIRS, LANES, LANES), F32),
                        pltpu.VMEM((SUBLANES + T, gw), F32), pltpu.VMEM((SUBLANES + T, gw), F32),
                        pltpu.VMEM((SUBLANES + T, gw), F32), pltpu.VMEM((SUBLANES + T, TAIL_MIX), F32)],
        compiler_params=_cparams(("arbitrary", "arbitrary", "arbitrary"), 32),
        name="rwkv_chunk",
    )(proj, proj, proj, proj, proj, e_main, e_main, e_main, e_tail, mu_main, mu_main, mu_main, mu_tail,
      w0, w2p, a0, a2p, g2, k_k, k_a, rk, lnw, lnb, s0)


def _merge_out_body(ha_ref, hb_ref, x_ref, w_ref, g_ref, y_ref):
    mixed = ha_ref[...].astype(F32) + hb_ref[...].astype(F32)
    y = jnp.dot(mixed.astype(BF16), w_ref[...], preferred_element_type=F32)
    ms = jnp.mean(y * y, axis=-1, keepdims=True)
    y_ref[...] = x_ref[...] + y * lax.rsqrt(ms + EPS) * g_ref[...]


def _merge_out(h_a, h_b, x, w_out, g_post):
    n = x.shape[0]
    wide = pl.BlockSpec((MERGE_TM, D_MODEL), lambda i: (i, 0))
    return pl.pallas_call(
        _merge_out_body,
        grid=(n // MERGE_TM,),
        in_specs=[wide, wide, wide,
                  pl.BlockSpec((D_MODEL, D_MODEL), lambda i: (0, 0)),
                  pl.BlockSpec((1, D_MODEL), lambda i: (0, 0))],
        out_specs=wide,
        out_shape=jax.ShapeDtypeStruct((n, D_MODEL), F32),
        compiler_params=_cparams(("parallel",), 48),
        name="merge_out_proj",
    )(h_a, h_b, x, w_out, g_post)


def _mlp_body(x_ref, g1_ref, wu_ref, wd_ref, g2_ref, y_ref, hn_ref, acc_ref, *, row_chunk):
    j = pl.program_id(1)

    @pl.when(j == 0)
    def _():
        _norm_rows(x_ref, g1_ref, hn_ref, row_chunk)
        acc_ref[...] = jnp.zeros_like(acc_ref)

    u = jnp.dot(hn_ref[...], wu_ref[...], preferred_element_type=F32)
    a = jnp.square(jnp.maximum(u, 0.0))
    acc_ref[...] += jnp.dot(a.astype(BF16), wd_ref[...], preferred_element_type=F32)

    @pl.when(j == pl.num_programs(1) - 1)
    def _():
        def body(c, carry):
            rows = pl.ds(pl.multiple_of(c * row_chunk, row_chunk), row_chunk)
            ff = acc_ref[rows, :]
            ms = jnp.mean(ff * ff, axis=-1, keepdims=True)
            y_ref[rows, :] = x_ref[rows, :] + ff * lax.rsqrt(ms + EPS) * g2_ref[...]
            return carry
        lax.fori_loop(0, x_ref.shape[0] // row_chunk, body, 0)


def _mlp(x, g_pre, w_up, w_down, g_post):
    n, d = x.shape
    f = w_up.shape[1]
    return pl.pallas_call(
        functools.partial(_mlp_body, row_chunk=LANES),
        grid=(n // MLP_TM, f // MLP_TF),
        in_specs=[pl.BlockSpec((MLP_TM, d), lambda i, j: (i, 0)),
                  pl.BlockSpec((1, d), lambda i, j: (0, 0)),
                  pl.BlockSpec((d, MLP_TF), lambda i, j: (0, j)),
                  pl.BlockSpec((MLP_TF, d), lambda i, j: (j, 0)),
                  pl.BlockSpec((1, d), lambda i, j: (0, 0))],
        out_specs=pl.BlockSpec((MLP_TM, d), lambda i, j: (i, 0)),
        out_shape=jax.ShapeDtypeStruct((n, d), F32),
        scratch_shapes=[pltpu.VMEM((MLP_TM, d), BF16), pltpu.VMEM((MLP_TM, d), F32)],
        compiler_params=_cparams(("parallel", "arbitrary"), 56),
        name="mlp",
    )(x, g_pre, w_up, w_down, g_post)


def _pad_cols(a, width):
    return jnp.pad(a, [(0, 0)] * (a.ndim - 1) + [(0, width - a.shape[-1])])


def _pad_rows(a, before, total):
    return jnp.pad(a, ((before, total - before - a.shape[0]), (0, 0)))


def _edge_blocks(prev_rows):
    return jnp.pad(prev_rows, ((0, 0), (SUBLANES - prev_rows.shape[1], 0), (0, 0)))


def _pack_state(s):
    n = s.shape[0]
    s = s.reshape(n, N_PAIRS, 2, B_HEAD, B_HEAD)
    z = jnp.zeros((n, N_PAIRS, B_HEAD, B_HEAD), s.dtype)
    top = jnp.concatenate([s[:, :, 0], z], axis=-1)
    bot = jnp.concatenate([z, s[:, :, 1]], axis=-1)
    return jnp.concatenate([top, bot], axis=-2)


def _unpack_state(sp):
    n = sp.shape[0]
    a = sp[:, :, :B_HEAD, :B_HEAD]
    b = sp[:, :, B_HEAD:, B_HEAD:]
    return jnp.stack([a, b], axis=2).reshape(n, B_HEADS, B_HEAD, B_HEAD)


def _gates_row_form(proj, T):
    g = proj[:, COL_TAIL + TAIL_IF:COL_TAIL + TAIL_IF + 2 * A_HEADS]
    return jnp.swapaxes(g.reshape(-1, T, 2 * A_HEADS), 1, 2)


def kernel(x_prompt, x_sample, state_mlstm_C, state_mlstm_n, state_mlstm_m, state_mlstm_conv, state_rwkv_S, state_rwkv_shift, meta_tokens, norm_mix_pre, norm_mix_post, norm_mlp_pre, norm_mlp_post, w_in, mlstm_conv_w, mlstm_conv_b, mlstm_b_i, mlstm_b_f, mlstm_norm_w, rwkv_mu, rwkv_w0, rwkv_w2, rwkv_a0, rwkv_a2, rwkv_g2, rwkv_k_k, rwkv_k_a, rwkv_r_k, rwkv_ln_w, rwkv_ln_b, w_out, w_up, w_down):
    n_s_seq, s_len, _ = x_sample.shape
    _, p_len, _ = x_prompt.shape
    k3 = A_CONV - 1

    w_all = _relayout_weight(w_in[0])
    mu = rwkv_mu[0][None, :]
    mu_main = mu[:, :3 * B_W]
    mu_tail = _pad_cols(mu[:, 3 * B_W:], TAIL_MIX)
    w2p = _pad_rows(rwkv_w2[0], 0, LANES).astype(BF16)
    a2p = _pad_rows(rwkv_a2[0], LORA_DECAY, 2 * LANES).astype(BF16)
    g2 = _pad_rows(rwkv_g2[0], LORA_DECAY + LORA_ICLR - TAIL_GD_WIN, TAIL_MIX - TAIL_GD_WIN).astype(BF16)
    g_pre = norm_mix_pre[0][None, :]
    bias = jnp.concatenate([mlstm_b_i[0], mlstm_b_f[0]])
    bias_row = _pad_cols(bias[None, :], LANES)
    bias_col = bias[:, None]
    norm_w = mlstm_norm_w[0][None, :]
    conv_w = mlstm_conv_w[0]
    conv_b = mlstm_conv_b[0][None, :]
    rwkv_vecs = (rwkv_w0[0][None, :], w2p, rwkv_a0[0][None, :], a2p, g2, rwkv_k_k[0][None, :],
                 rwkv_k_a[0][None, :], rwkv_r_k[0].reshape(1, B_W), rwkv_ln_w[0][None, :], rwkv_ln_b[0][None, :])
    w_out_b = w_out[0].astype(BF16)
    w_up_b = w_up[0].astype(BF16)
    w_down_b = w_down[0].astype(BF16)

    def branches(x_rows, tm, T, n_seq, n_chunk, e_conv, e_main, e_tail, c0, n0, m0, s0):
        proj = _norm_matmul(x_rows, g_pre, w_all, tm)
        h_a, c_n, n_n, m_n = _mlstm(proj, _gates_row_form(proj, T), e_conv, conv_w, conv_b, bias_row, bias_col,
                                    norm_w, c0, n0, m0, T=T, n_seq=n_seq, n_chunk=n_chunk)
        h_b, s_n = _rwkv(proj, e_main, e_tail, mu_main, mu_tail, *rwkv_vecs, s0,
                         T=T, n_seq=n_seq, n_chunk=n_chunk)
        return proj, h_a, h_b, (c_n, n_n, m_n, s_n)

    def finish(h_a, h_b, x_rows):
        x1 = _merge_out(h_a, h_b, x_rows, w_out_b, norm_mix_post[0][None, :])
        return _mlp(x1, norm_mlp_pre[0][None, :], w_up_b, w_down_b, norm_mlp_post[0][None, :])

    zeros = lambda *shape: jnp.zeros(shape, F32)
    proj_m, _, _, (c_m, n_m, m_m, s_m) = branches(
        meta_tokens.astype(F32), N_META, N_META, 1, 1,
        zeros(1, SUBLANES, CONV_COLS), zeros(1, SUBLANES, 3 * B_W), zeros(1, SUBLANES, TAIL_MIX),
        zeros(1, A_HEADS, A_DK, A_DV), zeros(1, A_HEADS, A_DK), zeros(1, 1, LANES),
        zeros(1, N_PAIRS, LANES, LANES))

    xp = x_prompt[0].astype(F32)
    proj_p, ha_p, hb_p, (c_p, n_p, m_p, s_p) = branches(
        xp, PROJ_TM, MLSTM_CHUNK, 1, p_len // MLSTM_CHUNK,
        _edge_blocks(proj_m[None, N_META - k3:, COL_CONV:COL_CONV + CONV_COLS]),
        _edge_blocks(proj_m[None, N_META - 1:, COL_R:COL_R + 3 * B_W]),
        _edge_blocks(proj_m[None, N_META - 1:, COL_TAIL:COL_TAIL + TAIL_MIX]),
        c_m, n_m, m_m, s_m)
    y_prompt = finish(ha_p, hb_p, xp)[None].astype(x_prompt.dtype)

    xs = x_sample.reshape(n_s_seq * s_len, D_MODEL).astype(F32)
    sh = state_rwkv_shift[0].astype(F32)
    m0_s = jnp.pad(state_mlstm_m[0].astype(F32), ((0, 0), (A_HEADS, LANES - 2 * A_HEADS)))[:, None, :]
    proj_s, ha_s, hb_s, (c_s, n_s, m_s, s_s) = branches(
        xs, PROJ_TM, s_len, n_s_seq, 1,
        _edge_blocks(state_mlstm_conv[0].astype(F32)),
        _edge_blocks(sh[:, :, :3 * B_W]), _edge_blocks(_pad_cols(sh[:, :, 3 * B_W:], TAIL_MIX)),
        state_mlstm_C[0].astype(F32), state_mlstm_n[0].astype(F32), m0_s,
        _pack_state(state_rwkv_S[0].astype(F32)))
    y_sample = finish(ha_s, hb_s, xs).reshape(x_sample.shape).astype(x_sample.dtype)

    def shift_state(rows):
        return jnp.concatenate([rows[..., COL_R:COL_R + 3 * B_W],
                                rows[..., COL_TAIL:COL_TAIL + LORA_COLS]], axis=-1)

    last_s = proj_s.reshape(n_s_seq, s_len, PROJ_COLS)[:, s_len - k3:, :]
    conv_s = last_s[..., COL_CONV:COL_CONV + CONV_COLS]
    shift_s = shift_state(last_s[:, k3 - 1:, :])
    last_p = proj_p[p_len - k3:, :]
    conv_p = last_p[None, :, COL_CONV:COL_CONV + CONV_COLS]
    shift_p = shift_state(last_p[None, k3 - 1:, :])
    dt_c, dt_n, dt_m = state_mlstm_C.dtype, state_mlstm_n.dtype, state_mlstm_m.dtype
    dt_cv, dt_s, dt_sh = state_mlstm_conv.dtype, state_rwkv_S.dtype, state_rwkv_shift.dtype
    lanes_m = slice(A_HEADS, 2 * A_HEADS)
    return (y_prompt, y_sample,
            c_p[None].astype(dt_c), n_p[None].astype(dt_n), m_p[:, 0, lanes_m][None].astype(dt_m),
            conv_p[None].astype(dt_cv), _unpack_state(s_p)[None].astype(dt_s), shift_p[None].astype(dt_sh),
            c_s[None].astype(dt_c), n_s[None].astype(dt_n), m_s[:, 0, lanes_m][None].astype(dt_m),
            conv_s[None].astype(dt_cv), _unpack_state(s_s)[None].astype(dt_s), shift_s[None].astype(dt_sh))
```

```python
import functools

import jax
import jax.numpy as jnp
from jax import lax
from jax.experimental import pallas as pl
from jax.experimental.pallas import tpu as pltpu

F32 = jnp.float32
BF16 = jnp.bfloat16

D_MODEL = 2048
N_META = 16
EPS = 1e-6
D_FF = 4 * D_MODEL
A_HEADS = 8
A_DK = 128
A_DV = D_MODEL // A_HEADS
A_QK = A_HEADS * A_DK
A_VW = A_HEADS * A_DV
A_CONV = 4
MLSTM_CHUNK = 64
B_HEAD = 64
B_HEADS = D_MODEL // B_HEAD
B_W = B_HEADS * B_HEAD
LORA_DECAY = max(32, int(round(1.8 * D_MODEL ** 0.5 / 32)) * 32)
LORA_ICLR = max(32, int(round(1.8 * D_MODEL ** 0.5 / 32)) * 32)
LORA_GATE = max(32, int(round(0.6 * D_MODEL ** 0.8 / 32)) * 32)
GN_EPS = 64e-5
CONV_COLS = 2 * A_QK
SHIFT_COLS = 3 * B_W + LORA_DECAY + LORA_ICLR + LORA_GATE

LANES = 128
SUBLANES = 8

COL_CONV, COL_V, COL_O, COL_R, COL_K, COL_VB, COL_GA, COL_GB = (i * D_MODEL for i in range(8))
COL_TAIL = 8 * D_MODEL
LORA_COLS = LORA_DECAY + LORA_ICLR + LORA_GATE
TAIL_MIX = -(-LORA_COLS // LANES) * LANES
TAIL_IF = TAIL_MIX
TAIL_GD_WIN = (LORA_DECAY + LORA_ICLR) // LANES * LANES
PROJ_TN = 1024
PROJ_COLS = -(-(COL_TAIL + TAIL_IF + LANES) // PROJ_TN) * PROJ_TN
PROJ_TM = 1024
MERGE_TM = 256
MLP_TM, MLP_TF = 512, 1024
DECAY_SCALE = -0.6065306597126334
RWKV_CHUNK = 64
N_PAIRS = B_HEADS // 2
GROUP_PAIRS = 16
RWKV_WAVE, RWKV_LAG = 8, 2
GROUP_W = GROUP_PAIRS * LANES


def _cparams(semantics, vmem_mib):
    return pltpu.CompilerParams(dimension_semantics=semantics, vmem_limit_bytes=vmem_mib << 20)


def _softplus(x):
    return jnp.maximum(x, 0.0) + jnp.log1p(jnp.exp(-jnp.abs(x)))


def _dot(a, b):
    return jnp.dot(a.astype(BF16), b.astype(BF16), preferred_element_type=F32)


def _dot_nt(a, b):
    return lax.dot_general(a.astype(BF16), b.astype(BF16), (((1,), (1,)), ((), ())),
                           preferred_element_type=F32)


def _dot_tn(a, b):
    return lax.dot_general(a.astype(BF16), b.astype(BF16), (((0,), (0,)), ((), ())),
                           preferred_element_type=F32)


def _each(f, *lists):
    return [f(*args) for args in zip(*lists)]


def _rows(*parts):
    return jnp.concatenate(parts, axis=0)


def _split_bf16(x, pieces):
    out = []
    for _ in range(pieces - 1):
        p = x.astype(BF16)
        out.append(p)
        x = x - p.astype(F32)
    out.append(x.astype(BF16))
    return out


def _dot_exact_rhs(x, m, pieces=3):
    mb = m.astype(BF16)
    acc = None
    for p in _split_bf16(x, pieces):
        t = jnp.dot(p, mb, preferred_element_type=F32)
        acc = t if acc is None else acc + t
    return acc


def _sigmoid(x):
    return 0.5 * jnp.tanh(0.5 * x) + 0.5


def _stage_rows(ext_ref, u):
    ext_ref[SUBLANES:, :] = u


def _rows_before(ext_ref, sh, T):
    return ext_ref[pl.ds(SUBLANES - sh, T), :]


def _keep_last_rows(ext_ref, u, T):
    ext_ref[:SUBLANES, :] = u[T - SUBLANES:, :]


def _cumsum_rows(x):
    n = x.shape[0]
    row = lax.broadcasted_iota(jnp.int32, x.shape, 0)
    step = 1
    while step < n:
        x = x + jnp.where(row >= step, pltpu.roll(x, step, 0), 0.0)
        step *= 2
    return x


def _head_block_mask(scale):
    a = lax.broadcasted_iota(jnp.int32, (LANES, LANES), 0) // B_HEAD
    b = lax.broadcasted_iota(jnp.int32, (LANES, LANES), 1) // B_HEAD
    return jnp.where(a == b, scale, 0.0).astype(F32)


SRC_IF = CONV_COLS + 2 * A_VW
SRC_SHIFT = SRC_IF + 2 * A_HEADS
SRC_GATE = SRC_SHIFT + SHIFT_COLS
SRC_COLS = SRC_GATE + 2 * D_MODEL
RELAYOUT_ROWS = TAIL_MIX


def _relayout_src_row(t):
    unit = 2 * SUBLANES
    dst = t * (RELAYOUT_ROWS // unit)
    src = jnp.where(dst < COL_R // unit, dst,
                    jnp.where(dst < COL_GA // unit, dst + (SRC_SHIFT - COL_R) // unit,
                              jnp.where(dst < COL_TAIL // unit, dst + (SRC_GATE - COL_GA) // unit,
                                        jnp.where(dst < (COL_TAIL + TAIL_MIX) // unit,
                                                  (SRC_SHIFT + 3 * B_W) // unit, SRC_IF // unit))))
    return src * unit


def _relayout_body(w_ref, o_ref):
    o_ref[...] = w_ref[...].astype(BF16)


def _relayout_weight(wt):
    d = wt.shape[1]
    assert wt.shape[0] == SRC_COLS and PROJ_COLS == COL_TAIL + 2 * RELAYOUT_ROWS and COL_R == SRC_IF
    assert SRC_SHIFT + 3 * B_W + RELAYOUT_ROWS <= SRC_COLS and SRC_SHIFT % (2 * SUBLANES) == 0
    return pl.pallas_call(
        _relayout_body,
        grid=(PROJ_COLS // RELAYOUT_ROWS,),
        in_specs=[pl.BlockSpec((pl.Element(RELAYOUT_ROWS), pl.Element(d)), lambda t: (_relayout_src_row(t), 0))],
        out_specs=pl.BlockSpec((RELAYOUT_ROWS, d), lambda t: (t, 0)),
        out_shape=jax.ShapeDtypeStruct((PROJ_COLS, d), BF16),
        compiler_params=_cparams(("parallel",), 32),
        name="relayout_w_in",
    )(wt)


def _norm_rows(x_ref, g_ref, out_ref, row_chunk):
    def body(c, carry):
        rows = pl.ds(pl.multiple_of(c * row_chunk, row_chunk), row_chunk)
        x = x_ref[rows, :]
        ms = jnp.mean(x * x, axis=-1, keepdims=True)
        out_ref[rows, :] = (x * lax.rsqrt(ms + EPS) * g_ref[...]).astype(out_ref.dtype)
        return carry
    lax.fori_loop(0, x_ref.shape[0] // row_chunk, body, 0)


def _norm_matmul_body(x_ref, g_ref, w_ref, o_ref, hn_ref, *, row_chunk):
    @pl.when(pl.program_id(1) == 0)
    def _():
        _norm_rows(x_ref, g_ref, hn_ref, row_chunk)

    o_ref[...] = lax.dot_general(hn_ref[...], w_ref[...], (((1,), (1,)), ((), ())), preferred_element_type=F32)


def _norm_matmul(x, g, w, tm):
    n, d = x.shape
    p = w.shape[0]
    return pl.pallas_call(
        functools.partial(_norm_matmul_body, row_chunk=min(tm, LANES)),
        grid=(n // tm, p // PROJ_TN),
        in_specs=[pl.BlockSpec((tm, d), lambda i, j: (i, 0)),
                  pl.BlockSpec((1, d), lambda i, j: (0, 0)),
                  pl.BlockSpec((PROJ_TN, d), lambda i, j: (j, 0))],
        out_specs=pl.BlockSpec((tm, PROJ_TN), lambda i, j: (i, j)),
        out_shape=jax.ShapeDtypeStruct((n, p), F32),
        scratch_shapes=[pltpu.VMEM((tm, d), BF16)],
        compiler_params=_cparams(("parallel", "arbitrary"), 48),
        name="in_proj",
    )(x, g, w)


def _mlstm_body(u_ref, v_ref, o_ref, ga_ref, gc_ref, gr_ref, e_ref, cw_ref, cb_ref, brow_ref, bcol_ref,
                nw_ref, c0_ref, n0_ref, m0_ref,
                h_ref, cn_ref, nn_ref, mn_ref, c_s, n_s, m_s, e_s, *, T):
    c = pl.program_id(1)

    @pl.when(c == 0)
    def _():
        c_s[...] = c0_ref[0]
        n_s[...] = n0_ref[0]
        m_s[...] = m0_ref[0]
        e_s[:SUBLANES, :] = e_ref[0]

    u = u_ref[...]
    _stage_rows(e_s, u)
    w = cw_ref[...]
    acc = cb_ref[...] + u * w[A_CONV - 1:A_CONV, :]
    for sh in range(1, A_CONV):
        acc = acc + _rows_before(e_s, sh, T) * w[A_CONV - 1 - sh:A_CONV - sh, :]
    _keep_last_rows(e_s, u, T)
    qk_act = acc * _sigmoid(acc)

    gcol = gc_ref[...] + brow_ref[...]
    grow = gr_ref[0] + bcol_ref[...]
    lf_col = -_softplus(-gcol)
    ig_row = grow[:A_HEADS, :]
    lf_row = -_softplus(-grow[A_HEADS:, :])
    ri = lax.broadcasted_iota(jnp.int32, (T, T), 0)
    ci = lax.broadcasted_iota(jnp.int32, (T, T), 1)
    tri = ci <= ri
    b_col = _cumsum_rows(lf_col)
    b_row = _dot_exact_rhs(lf_row, jnp.where(ri <= ci, 1.0, 0.0))
    m_prev = m_s[...]
    lane = lax.broadcasted_iota(jnp.int32, (1, LANES), 1)
    heads = list(range(A_HEADS))
    bc = [b_col[:, A_HEADS + h:A_HEADS + h + 1] for h in heads]
    mh = [m_prev[:, A_HEADS + h:A_HEADS + h + 1] for h in heads]
    br = [b_row[h:h + 1, :] for h in heads]
    igr = [ig_row[h:h + 1, :] for h in heads]
    igc = [gcol[:, h:h + 1] for h in heads]
    qh = [qk_act[:, h * A_DK:(h + 1) * A_DK] for h in heads]
    kh = [qk_act[:, A_QK + h * A_DK:A_QK + (h + 1) * A_DK] * (A_DK ** -0.5) for h in heads]
    vh = [v_ref[:, h * A_DV:(h + 1) * A_DV] for h in heads]
    ch = [c_s[h] for h in heads]
    nh = [n_s[h:h + 1, :] for h in heads]
    qk = _each(_dot_nt, qh, kh)
    qc = _each(_dot, qh, ch)
    d = _each(lambda a, b, i: jnp.where(tri, a - b + i, -jnp.inf), bc, br, igr)
    inter = _each(jnp.add, bc, mh)
    m_t = _each(lambda i, x: jnp.maximum(i, jnp.max(x, axis=-1, keepdims=True)), inter, d)
    s = _each(lambda a, x, m: a * jnp.exp(x - m), qk, d, m_t)
    e_in = _each(lambda i, m: jnp.exp(i - m), inter, m_t)
    sv = _each(_dot, s, vh)
    num = _each(lambda a, x, b: a + x * b, sv, e_in, qc)
    den = _each(lambda a, x, qq, nn: jnp.sum(a, axis=-1, keepdims=True)
                + x * jnp.sum(qq * nn, axis=-1, keepdims=True), s, e_in, qh, nh)
    hh = _each(lambda a, b, m: a / jnp.maximum(jnp.abs(b), jnp.exp(-m)), num, den, m_t)
    hh = _each(lambda a: a * lax.rsqrt(jnp.mean(a * a, axis=-1, keepdims=True) + EPS), hh)
    b_end = _each(lambda a: a[T - 1:T, :], bc)
    m_new = _each(lambda be, m, b, i: jnp.maximum(be + m, jnp.max(be - b + i, axis=-1, keepdims=True)),
                  b_end, mh, br, igr)
    carry = _each(lambda be, m, mn: jnp.exp(be + m - mn), b_end, mh, m_new)
    wk = _each(lambda kk_, be, a, i, mn: kk_ * jnp.exp(be - a + i - mn), kh, b_end, bc, igc, m_new)
    kv = _each(_dot_tn, wk, vh)
    m_next = m_prev
    for h in heads:
        cols = slice(h * A_DV, (h + 1) * A_DV)
        gate = _sigmoid(o_ref[:, cols]) * _sigmoid(ga_ref[:, cols])
        h_ref[:, cols] = (hh[h] * nw_ref[:, cols] * gate).astype(h_ref.dtype)
        c_s[h] = carry[h] * ch[h] + kv[h]
        n_s[h:h + 1, :] = carry[h] * nh[h] + jnp.sum(wk[h], axis=0, keepdims=True)
        m_next = jnp.where(lane == A_HEADS + h, m_new[h], m_next)
    m_s[...] = m_next

    @pl.when(c == pl.num_programs(1) - 1)
    def _():
        cn_ref[0] = c_s[...]
        nn_ref[0] = n_s[...]
        mn_ref[0] = m_s[...]


def _mlstm(proj, gates_row, e_conv, conv_w, conv_b, bias_row, bias_col, norm_w, c0, n0, m0, *, T, n_seq, n_chunk):
    row = lambda s, c: (s * n_chunk + c, 0)
    fixed = lambda s, c: (0, 0)
    wide = lambda col: pl.BlockSpec((T, D_MODEL), lambda s, c: (s * n_chunk + c, col // D_MODEL))
    return pl.pallas_call(
        functools.partial(_mlstm_body, T=T),
        grid=(n_seq, n_chunk),
        in_specs=[wide(COL_CONV), wide(COL_V), wide(COL_O), wide(COL_GA),
                  pl.BlockSpec((T, LANES), lambda s, c: (s * n_chunk + c, (COL_TAIL + TAIL_IF) // LANES)),
                  pl.BlockSpec((1, 2 * A_HEADS, T), lambda s, c: (s * n_chunk + c, 0, 0)),
                  pl.BlockSpec((1, SUBLANES, CONV_COLS), lambda s, c: (s, 0, 0)),
                  pl.BlockSpec((A_CONV, CONV_COLS), fixed),
                  pl.BlockSpec((1, CONV_COLS), fixed),
                  pl.BlockSpec((1, LANES), fixed),
                  pl.BlockSpec((2 * A_HEADS, 1), fixed),
                  pl.BlockSpec((1, A_VW), fixed),
                  pl.BlockSpec((1, A_HEADS, A_DK, A_DV), lambda s, c: (s, 0, 0, 0)),
                  pl.BlockSpec((1, A_HEADS, A_DK), lambda s, c: (s, 0, 0)),
                  pl.BlockSpec((1, 1, LANES), lambda s, c: (s, 0, 0))],
        out_specs=[pl.BlockSpec((T, A_VW), row),
                   pl.BlockSpec((1, A_HEADS, A_DK, A_DV), lambda s, c: (s, 0, 0, 0)),
                   pl.BlockSpec((1, A_HEADS, A_DK), lambda s, c: (s, 0, 0)),
                   pl.BlockSpec((1, 1, LANES), lambda s, c: (s, 0, 0))],
        out_shape=[jax.ShapeDtypeStruct((n_seq * n_chunk * T, A_VW), BF16),
                   jax.ShapeDtypeStruct((n_seq, A_HEADS, A_DK, A_DV), F32),
                   jax.ShapeDtypeStruct((n_seq, A_HEADS, A_DK), F32),
                   jax.ShapeDtypeStruct((n_seq, 1, LANES), F32)],
        scratch_shapes=[pltpu.VMEM((A_HEADS, A_DK, A_DV), F32),
                        pltpu.VMEM((A_HEADS, A_DK), F32),
                        pltpu.VMEM((1, LANES), F32),
                        pltpu.VMEM((SUBLANES + T, CONV_COLS), F32)],
        compiler_params=_cparams(("arbitrary", "arbitrary"), 32),
        name="mlstm_chunk",
    )(proj, proj, proj, proj, proj, gates_row, e_conv, conv_w, conv_b, bias_row, bias_col, norm_w, c0, n0, m0)


def _kcat_dot(a1, b1, a2, b2):
    if a1.shape[1] % LANES == 0:
        return _dot(jnp.concatenate([a1, a2], axis=1), _rows(b1, b2))
    return _dot(a1, b1) + _dot(a2, b2)


def _head_sums(ones_blk, T, *xs):
    sums = jnp.dot(_rows(*xs).astype(BF16), ones_blk, preferred_element_type=F32)
    return [sums[i * T:(i + 1) * T] for i in range(len(xs))]


def _run_staggered(programs, wave, lag):
    live = list(enumerate(programs))
    slot = 0
    while live:
        still = []
        for p, prog in live:
            if slot >= (p // wave) * lag:
                try:
                    next(prog)
                except StopIteration:
                    continue
            still.append((p, prog))
        live = still
        slot += 1


def _rwkv_body(ur_ref, uk_ref, uv_ref, ul_ref, gb_ref, er_ref, ek_ref, ev_ref, el_ref,
               mur_ref, muk_ref, muv_ref, mul_ref, w0_ref, w2_ref, a0_ref, a2_ref, g2_ref, kkw_ref, ka_ref,
               rk_ref, lnw_ref, lnb_ref, s0_ref,
               hb_ref, sn_ref, s_s, cr_s, ck_s, cv_s, cl_s, *, T, n_pairs):
    c = pl.program_id(2)

    @pl.when(c == 0)
    def _():
        s_s[...] = s0_ref[0]
        cr_s[:SUBLANES, :] = er_ref[0]
        ck_s[:SUBLANES, :] = ek_ref[0]
        cv_s[:SUBLANES, :] = ev_ref[0]
        cl_s[:SUBLANES, :] = el_ref[0]

    def shift_mix(u_ref, ext_ref, mu_ref, cs):
        u = u_ref[:, cs]
        ext_ref[SUBLANES:, cs] = u
        xs = u + mu_ref[:, cs] * (ext_ref[pl.ds(SUBLANES - 1, T), cs] - u)
        ext_ref[:SUBLANES, cs] = u[T - SUBLANES:, :]
        return xs

    xl = shift_mix(ul_ref, cl_s, mul_ref, slice(None))
    th_wd = jnp.tanh(xl[:, :LANES]).astype(BF16)
    x_ad = xl[:, :2 * LANES].astype(BF16)
    sg_gd = _sigmoid(xl[:, TAIL_GD_WIN:]).astype(BF16)

    lo = lax.broadcasted_iota(jnp.int32, (1, LANES), 1) < B_HEAD
    head1 = lambda x: jnp.where(lo, x, 0.0)
    head2 = lambda x: jnp.where(lo, 0.0, x)
    r2 = lax.broadcasted_iota(jnp.int32, (T, 2 * T), 0)
    c2 = lax.broadcasted_iota(jnp.int32, (T, 2 * T), 1)
    left = c2 < T
    cj = jnp.where(left, c2, c2 - T)
    strict2 = cj < r2
    incl2 = cj <= r2
    e0 = lax.broadcasted_iota(jnp.int32, (2 * T, 2 * T), 0)
    e1 = lax.broadcasted_iota(jnp.int32, (2 * T, 2 * T), 1)
    eye2 = jnp.where(e0 == e1, 1.0, 0.0)
    ones_blk = _head_block_mask(1.0).astype(BF16)
    same_head = _head_block_mask(1.0) > 0.5
    zero = jnp.zeros((T, LANES), F32)

    def pair_program(p):
        cs = slice(p * LANES, (p + 1) * LANES)
        r = shift_mix(ur_ref, cr_s, mur_ref, cs)
        xk = shift_mix(uk_ref, ck_s, muk_ref, cs)
        v = shift_mix(uv_ref, cv_s, muv_ref, cs)
        lw = jnp.dot(th_wd, w2_ref[:, cs], preferred_element_type=F32)
        la = jnp.dot(x_ad, a2_ref[:, cs], preferred_element_type=F32)
        lg = jnp.dot(sg_gd, g2_ref[:, cs], preferred_element_type=F32)
        kkw = xk * kkw_ref[:, cs]
        ssq, = _head_sums(ones_blk, T, kkw * kkw)
        yield
        ld = DECAY_SCALE * _sigmoid(w0_ref[:, cs] + lw)
        a = _sigmoid(a0_ref[:, cs] + la)
        g = lg * _sigmoid(gb_ref[:, cs])
        k = xk * (1.0 + (a - 1.0) * ka_ref[:, cs])
        kk = kkw / jnp.maximum(jnp.sqrt(ssq), 1e-12)
        bi = kk * a
        cum = _cumsum_rows(ld)
        ewt = jnp.exp(cum[T - 1:T, :])
        Rt = r * jnp.exp(cum)
        At = -kk * jnp.exp(cum - ld)
        einv = jnp.exp(-cum)
        Bt = bi * einv
        Kt = k * einv
        Bh = Bt * ewt
        Kh = Kt * ewt
        V1 = head1(v)
        V2 = head2(v)
        G1 = _dot_nt(_rows(head1(At), head1(Rt)), _rows(Bt, Kt))
        G2 = _dot_nt(_rows(head2(At), head2(Rt)), _rows(Kt, Bt))
        yield
        LL1 = jnp.where(strict2, G1[:T], 0.0)
        LL2 = jnp.where(strict2, G2[:T], 0.0)
        P1 = jnp.where(incl2, G1[T:], 0.0)
        P2 = jnp.where(incl2, G2[T:], 0.0)
        pw = _rows(jnp.where(left, LL1, 0.0), jnp.where(left, 0.0, LL2))
        inv = eye2 + pw
        X0 = _kcat_dot(LL1, _rows(zero, V1), LL2, _rows(V2, zero))
        pw = _dot(pw, pw)
        yield
        span = 4
        while span < T:
            inv, pw = inv + _dot(inv, pw), _dot(pw, pw)
            span *= 2
            yield
        inv = inv + _dot(inv, pw)
        yield
        IV = inv[:T, :] + inv[T:, :]
        AU = _dot(IV, jnp.concatenate([_rows(head1(At), head2(At)), _rows(head1(X0), head2(X0))], axis=1))
        yield
        Ah = AU[:, :LANES]
        U0 = AU[:, LANES:]
        RO = _kcat_dot(P1, jnp.concatenate([_rows(head1(Ah), zero), _rows(head1(U0), V1)], axis=1),
                       P2, jnp.concatenate([_rows(zero, head2(Ah)), _rows(V2, head2(U0))], axis=1))
        MN = _dot_tn(AU, Bh)
        VK = _dot_tn(v, Kh)
        yield
        Rh = Rt + RO[:, :LANES]
        O0 = RO[:, LANES:]
        Mab = jnp.where(same_head, MN[:LANES], 0.0)
        N0 = jnp.where(same_head, MN[LANES:] + VK, 0.0)
        S = s_s[p]
        o = _dot_nt(Rh, S) + O0
        s_s[p] = S * ewt + _dot(S, Mab) + N0
        yield
        o_sum, rk_sum = _head_sums(ones_blk, T, o, r * k * rk_ref[:, cs])
        yield
        oc = o - o_sum * (1.0 / B_HEAD)
        var, = _head_sums(ones_blk, T, oc * oc)
        yield
        on = oc * lax.rsqrt(var * (1.0 / B_HEAD) + GN_EPS) * lnw_ref[:, cs] + lnb_ref[:, cs]
        hb_ref[:, cs] = ((on + rk_sum * v) * g).astype(hb_ref.dtype)

    _run_staggered([pair_program(p) for p in range(n_pairs)], RWKV_WAVE, RWKV_LAG)

    @pl.when(c == pl.num_programs(2) - 1)
    def _():
        sn_ref[0] = s_s[...]


def _rwkv(proj, e_main, e_tail, mu_main, mu_tail, w0, w2p, a0, a2p, g2, k_k, k_a, rk, lnw, lnb, s0, *,
          T, n_seq, n_chunk):
    assert T <= RWKV_CHUNK
    gw = GROUP_W
    wide = lambda col: pl.BlockSpec((T, gw), lambda s, pg, c: (s * n_chunk + c, col // gw + pg))
    edge = lambda col: pl.BlockSpec((1, SUBLANES, gw), lambda s, pg, c: (s, 0, col // gw + pg))
    vec = lambda col: pl.BlockSpec((1, gw), lambda s, pg, c: (0, col // gw + pg))
    mat = lambda rows: pl.BlockSpec((rows, gw), lambda s, pg, c: (0, pg))
    st = pl.BlockSpec((1, GROUP_PAIRS, LANES, LANES), lambda s, pg, c: (s, pg, 0, 0))
    return pl.pallas_call(
        functools.partial(_rwkv_body, T=T, n_pairs=GROUP_PAIRS),
        grid=(n_seq, N_PAIRS // GROUP_PAIRS, n_chunk),
        in_specs=[wide(COL_R), wide(COL_K), wide(COL_VB),
                  pl.BlockSpec((T, TAIL_MIX), lambda s, pg, c: (s * n_chunk + c, COL_TAIL // TAIL_MIX)),
                  wide(COL_GB),
                  edge(0), edge(B_W), edge(2 * B_W),
                  pl.BlockSpec((1, SUBLANES, TAIL_MIX), lambda s, pg, c: (s, 0, 0)),
                  vec(0), vec(B_W), vec(2 * B_W),
                  pl.BlockSpec((1, TAIL_MIX), lambda s, pg, c: (0, 0)),
                  vec(0), mat(LANES), vec(0), mat(2 * LANES), mat(TAIL_MIX - TAIL_GD_WIN), vec(0), vec(0),
                  vec(0), vec(0), vec(0), st],
        out_specs=[pl.BlockSpec((T, gw), lambda s, pg, c: (s * n_chunk + c, pg)), st],
        out_shape=[jax.ShapeDtypeStruct((n_seq * n_chunk * T, B_W), BF16),
                   jax.ShapeDtypeStruct((n_seq, N_PAIRS, LANES, LANES), F32)],
        scratch_shapes=[pltpu.VMEM((GROUP_PAIRS, LANES, LANES), F32),
                        pltpu.VMEM((SUBLANES + T, gw), F32), pltpu.VMEM((SUBLANES + T, gw), F32),
                        pltpu.VMEM((SUBLANES + T, gw), F32), pltpu.VMEM((SUBLANES + T, TAIL_MIX), F32)],
        compiler_params=_cparams(("arbitrary", "arbitrary", "arbitrary"), 32),
        name="rwkv_chunk",
    )(proj, proj, proj, proj, proj, e_main, e_main, e_main, e_tail, mu_main, mu_main, mu_main, mu_tail,
      w0, w2p, a0, a2p, g2, k_k, k_a, rk, lnw, lnb, s0)


def _merge_out_body(ha_ref, hb_ref, x_ref, w_ref, g_ref, y_ref):
    mixed = ha_ref[...].astype(F32) + hb_ref[...].astype(F32)
    y = jnp.dot(mixed.astype(BF16), w_ref[...], preferred_element_type=F32)
    ms = jnp.mean(y * y, axis=-1, keepdims=True)
    y_ref[...] = x_ref[...] + y * lax.rsqrt(ms + EPS) * g_ref[...]


def _merge_out(h_a, h_b, x, w_out, g_post):
    n = x.shape[0]
    wide = pl.BlockSpec((MERGE_TM, D_MODEL), lambda i: (i, 0))
    return pl.pallas_call(
        _merge_out_body,
        grid=(n // MERGE_TM,),
        in_specs=[wide, wide, wide,
                  pl.BlockSpec((D_MODEL, D_MODEL), lambda i: (0, 0)),
                  pl.BlockSpec((1, D_MODEL), lambda i: (0, 0))],
        out_specs=wide,
        out_shape=jax.ShapeDtypeStruct((n, D_MODEL), F32),
        compiler_params=_cparams(("parallel",), 48),
        name="merge_out_proj",
    )(h_a, h_b, x, w_out, g_post)


def _mlp_body(x_ref, g1_ref, wu_ref, wd_ref, g2_ref, y_ref, hn_ref, acc_ref, *, row_chunk):
    j = pl.program_id(1)

    @pl.when(j == 0)
    def _():
        _norm_rows(x_ref, g1_ref, hn_ref, row_chunk)
        acc_ref[...] = jnp.zeros_like(acc_ref)

    u = jnp.dot(hn_ref[...], wu_ref[...], preferred_element_type=F32)
    a = jnp.square(jnp.maximum(u, 0.0))
    acc_ref[...] += jnp.dot(a.astype(BF16), wd_ref[...], preferred_element_type=F32)

    @pl.when(j == pl.num_programs(1) - 1)
    def _():
        def body(c, carry):
            rows = pl.ds(pl.multiple_of(c * row_chunk, row_chunk), row_chunk)
            ff = acc_ref[rows, :]
            ms = jnp.mean(ff * ff, axis=-1, keepdims=True)
            y_ref[rows, :] = x_ref[rows, :] + ff * lax.rsqrt(ms + EPS) * g2_ref[...]
            return carry
        lax.fori_loop(0, x_ref.shape[0] // row_chunk, body, 0)


def _mlp(x, g_pre, w_up, w_down, g_post):
    n, d = x.shape
    f = w_up.shape[1]
    return pl.pallas_call(
        functools.partial(_mlp_body, row_chunk=LANES),
        grid=(n // MLP_TM, f // MLP_TF),
        in_specs=[pl.BlockSpec((MLP_TM, d), lambda i, j: (i, 0)),
                  pl.BlockSpec((1, d), lambda i, j: (0, 0)),
                  pl.BlockSpec((d, MLP_TF), lambda i, j: (0, j)),
                  pl.BlockSpec((MLP_TF, d), lambda i, j: (j, 0)),
                  pl.BlockSpec((1, d), lambda i, j: (0, 0))],
        out_specs=pl.BlockSpec((MLP_TM, d), lambda i, j: (i, 0)),
        out_shape=jax.ShapeDtypeStruct((n, d), F32),
        scratch_shapes=[pltpu.VMEM((MLP_TM, d), BF16), pltpu.VMEM((MLP_TM, d), F32)],
        compiler_params=_cparams(("parallel", "arbitrary"), 56),
        name="mlp",
    )(x, g_pre, w_up, w_down, g_post)


def _pad_cols(a, width):
    return jnp.pad(a, [(0, 0)] * (a.ndim - 1) + [(0, width - a.shape[-1])])


def _pad_rows(a, before, total):
    return jnp.pad(a, ((before, total - before - a.shape[0]), (0, 0)))


def _edge_blocks(prev_rows):
    return jnp.pad(prev_rows, ((0, 0), (SUBLANES - prev_rows.shape[1], 0), (0, 0)))


def _pack_state(s):
    n = s.shape[0]
    s = s.reshape(n, N_PAIRS, 2, B_HEAD, B_HEAD)
    z = jnp.zeros((n, N_PAIRS, B_HEAD, B_HEAD), s.dtype)
    top = jnp.concatenate([s[:, :, 0], z], axis=-1)
    bot = jnp.concatenate([z, s[:, :, 1]], axis=-1)
    return jnp.concatenate([top, bot], axis=-2)


def _unpack_state(sp):
    n = sp.shape[0]
    a = sp[:, :, :B_HEAD, :B_HEAD]
    b = sp[:, :, B_HEAD:, B_HEAD:]
    return jnp.stack([a, b], axis=2).reshape(n, B_HEADS, B_HEAD, B_HEAD)


def _gates_row_form(proj, T):
    g = proj[:, COL_TAIL + TAIL_IF:COL_TAIL + TAIL_IF + 2 * A_HEADS]
    return jnp.swapaxes(g.reshape(-1, T, 2 * A_HEADS), 1, 2)


def kernel(x_prompt, x_sample, state_mlstm_C, state_mlstm_n, state_mlstm_m, state_mlstm_conv, state_rwkv_S, state_rwkv_shift, meta_tokens, norm_mix_pre, norm_mix_post, norm_mlp_pre, norm_mlp_post, w_in, mlstm_conv_w, mlstm_conv_b, mlstm_b_i, mlstm_b_f, mlstm_norm_w, rwkv_mu, rwkv_w0, rwkv_w2, rwkv_a0, rwkv_a2, rwkv_g2, rwkv_k_k, rwkv_k_a, rwkv_r_k, rwkv_ln_w, rwkv_ln_b, w_out, w_up, w_down):
    n_s_seq, s_len, _ = x_sample.shape
    _, p_len, _ = x_prompt.shape
    k3 = A_CONV - 1

    w_all = _relayout_weight(jnp.swapaxes(w_in[0], 0, 1))
    mu = rwkv_mu[0][None, :]
    mu_main = mu[:, :3 * B_W]
    mu_tail = _pad_cols(mu[:, 3 * B_W:], TAIL_MIX)
    w2p = _pad_rows(rwkv_w2[0], 0, LANES).astype(BF16)
    a2p = _pad_rows(rwkv_a2[0], LORA_DECAY, 2 * LANES).astype(BF16)
    g2 = _pad_rows(rwkv_g2[0], LORA_DECAY + LORA_ICLR - TAIL_GD_WIN, TAIL_MIX - TAIL_GD_WIN).astype(BF16)
    g_pre = norm_mix_pre[0][None, :]
    bias = jnp.concatenate([mlstm_b_i[0], mlstm_b_f[0]])
    bias_row = _pad_cols(bias[None, :], LANES)
    bias_col = bias[:, None]
    norm_w = mlstm_norm_w[0][None, :]
    conv_w = mlstm_conv_w[0]
    conv_b = mlstm_conv_b[0][None, :]
    rwkv_vecs = (rwkv_w0[0][None, :], w2p, rwkv_a0[0][None, :], a2p, g2, rwkv_k_k[0][None, :],
                 rwkv_k_a[0][None, :], rwkv_r_k[0].reshape(1, B_W), rwkv_ln_w[0][None, :], rwkv_ln_b[0][None, :])
    w_out_b = w_out[0].astype(BF16)
    w_up_b = w_up[0].astype(BF16)
    w_down_b = w_down[0].astype(BF16)

    def branches(x_rows, tm, T, n_seq, n_chunk, e_conv, e_main, e_tail, c0, n0, m0, s0):
        proj = _norm_matmul(x_rows, g_pre, w_all, tm)
        h_a, c_n, n_n, m_n = _mlstm(proj, _gates_row_form(proj, T), e_conv, conv_w, conv_b, bias_row, bias_col,
                                    norm_w, c0, n0, m0, T=T, n_seq=n_seq, n_chunk=n_chunk)
        h_b, s_n = _rwkv(proj, e_main, e_tail, mu_main, mu_tail, *rwkv_vecs, s0,
                         T=T, n_seq=n_seq, n_chunk=n_chunk)
        return proj, h_a, h_b, (c_n, n_n, m_n, s_n)

    def finish(h_a, h_b, x_rows):
        x1 = _merge_out(h_a, h_b, x_rows, w_out_b, norm_mix_post[0][None, :])
        return _mlp(x1, norm_mlp_pre[0][None, :], w_up_b, w_down_b, norm_mlp_post[0][None, :])

    zeros = lambda *shape: jnp.zeros(shape, F32)
    proj_m, _, _, (c_m, n_m, m_m, s_m) = branches(
        meta_tokens.astype(F32), N_META, N_META, 1, 1,
        zeros(1, SUBLANES, CONV_COLS), zeros(1, SUBLANES, 3 * B_W), zeros(1, SUBLANES, TAIL_MIX),
        zeros(1, A_HEADS, A_DK, A_DV), zeros(1, A_HEADS, A_DK), zeros(1, 1, LANES),
        zeros(1, N_PAIRS, LANES, LANES))

    xp = x_prompt[0].astype(F32)
    proj_p, ha_p, hb_p, (c_p, n_p, m_p, s_p) = branches(
        xp, PROJ_TM, MLSTM_CHUNK, 1, p_len // MLSTM_CHUNK,
        _edge_blocks(proj_m[None, N_META - k3:, COL_CONV:COL_CONV + CONV_COLS]),
        _edge_blocks(proj_m[None, N_META - 1:, COL_R:COL_R + 3 * B_W]),
        _edge_blocks(proj_m[None, N_META - 1:, COL_TAIL:COL_TAIL + TAIL_MIX]),
        c_m, n_m, m_m, s_m)
    y_prompt = finish(ha_p, hb_p, xp)[None].astype(x_prompt.dtype)

    xs = x_sample.reshape(n_s_seq * s_len, D_MODEL).astype(F32)
    sh = state_rwkv_shift[0].astype(F32)
    m0_s = jnp.pad(state_mlstm_m[0].astype(F32), ((0, 0), (A_HEADS, LANES - 2 * A_HEADS)))[:, None, :]
    proj_s, ha_s, hb_s, (c_s, n_s, m_s, s_s) = branches(
        xs, PROJ_TM, s_len, n_s_seq, 1,
        _edge_blocks(state_mlstm_conv[0].astype(F32)),
        _edge_blocks(sh[:, :, :3 * B_W]), _edge_blocks(_pad_cols(sh[:, :, 3 * B_W:], TAIL_MIX)),
        state_mlstm_C[0].astype(F32), state_mlstm_n[0].astype(F32), m0_s,
        _pack_state(state_rwkv_S[0].astype(F32)))
    y_sample = finish(ha_s, hb_s, xs).reshape(x_sample.shape).astype(x_sample.dtype)

    def shift_state(rows):
        return jnp.concatenate([rows[..., COL_R:COL_R + 3 * B_W],
                                rows[..., COL_TAIL:COL_TAIL + LORA_COLS]], axis=-1)

    last_s = proj_s.reshape(n_s_seq, s_len, PROJ_COLS)[:, s_len - k3:, :]
    conv_s = last_s[..., COL_CONV:COL_CONV + CONV_COLS]
    shift_s = shift_state(last_s[:, k3 - 1:, :])
    last_p = proj_p[p_len - k3:, :]
    conv_p = last_p[None, :, COL_CONV:COL_CONV + CONV_COLS]
    shift_p = shift_state(last_p[None, k3 - 1:, :])
    dt_c, dt_n, dt_m = state_mlstm_C.dtype, state_mlstm_n.dtype, state_mlstm_m.dtype
    dt_cv, dt_s, dt_sh = state_mlstm_conv.dtype, state_rwkv_S.dtype, state_rwkv_shift.dtype
    lanes_m = slice(A_HEADS, 2 * A_HEADS)
    return (y_prompt, y_sample,
            c_p[None].astype(dt_c), n_p[None].astype(dt_n), m_p[:, 0, lanes_m][None].astype(dt_m),
            conv_p[None].astype(dt_cv), _unpack_state(s_p)[None].astype(dt_s), shift_p[None].astype(dt_sh),
            c_s[None].astype(dt_c), n_s[None].astype(dt_n), m_s[:, 0, lanes_m][None].astype(dt_m),
            conv_s[None].astype(dt_cv), _unpack_state(s_s)[None].astype(dt_s), shift_s[None].astype(dt_sh))
```

```python
import functools

import jax
import jax.numpy as jnp
from jax import lax
from jax.experimental import pallas as pl
from jax.experimental.pallas import tpu as pltpu

F32 = jnp.float32
BF16 = jnp.bfloat16

D_MODEL = 2048
N_META = 16
EPS = 1e-6
D_FF = 4 * D_MODEL
A_HEADS = 8
A_DK = 128
A_DV = D_MODEL // A_HEADS
A_QK = A_HEADS * A_DK
A_VW = A_HEADS * A_DV
A_CONV = 4
MLSTM_CHUNK = 64
B_HEAD = 64
B_HEADS = D_MODEL // B_HEAD
B_W = B_HEADS * B_HEAD
LORA_DECAY = max(32, int(round(1.8 * D_MODEL ** 0.5 / 32)) * 32)
LORA_ICLR = max(32, int(round(1.8 * D_MODEL ** 0.5 / 32)) * 32)
LORA_GATE = max(32, int(round(0.6 * D_MODEL ** 0.8 / 32)) * 32)
GN_EPS = 64e-5
CONV_COLS = 2 * A_QK
SHIFT_COLS = 3 * B_W + LORA_DECAY + LORA_ICLR + LORA_GATE

LANES = 128
SUBLANES = 8

COL_CONV, COL_V, COL_O, COL_R, COL_K, COL_VB, COL_GA, COL_GB = (i * D_MODEL for i in range(8))
COL_TAIL = 8 * D_MODEL
LORA_COLS = LORA_DECAY + LORA_ICLR + LORA_GATE
TAIL_MIX = -(-LORA_COLS // LANES) * LANES
TAIL_IF = TAIL_MIX
TAIL_GD_WIN = (LORA_DECAY + LORA_ICLR) // LANES * LANES
PROJ_TN = 1024
PROJ_COLS = -(-(COL_TAIL + TAIL_IF + LANES) // PROJ_TN) * PROJ_TN
PROJ_TM = 1024
MERGE_TM = 256
MLP_TM, MLP_TF = 512, 1024
DECAY_SCALE = -0.6065306597126334
RWKV_CHUNK = 64
N_PAIRS = B_HEADS // 2
GROUP_PAIRS = 16
RWKV_WAVE, RWKV_LAG = 8, 2
GROUP_W = GROUP_PAIRS * LANES


def _cparams(semantics, vmem_mib):
    return pltpu.CompilerParams(dimension_semantics=semantics, vmem_limit_bytes=vmem_mib << 20)


def _softplus(x):
    return jnp.maximum(x, 0.0) + jnp.log1p(jnp.exp(-jnp.abs(x)))


def _dot(a, b):
    return jnp.dot(a.astype(BF16), b.astype(BF16), preferred_element_type=F32)


def _dot_nt(a, b):
    return lax.dot_general(a.astype(BF16), b.astype(BF16), (((1,), (1,)), ((), ())),
                           preferred_element_type=F32)


def _dot_tn(a, b):
    return lax.dot_general(a.astype(BF16), b.astype(BF16), (((0,), (0,)), ((), ())),
                           preferred_element_type=F32)


def _each(f, *lists):
    return [f(*args) for args in zip(*lists)]


def _run_staggered(programs, wave, lag):
    live = list(enumerate(programs))
    slot = 0
    while live:
        still = []
        for p, prog in live:
            if slot >= (p // wave) * lag:
                try:
                    next(prog)
                except StopIteration:
                    continue
            still.append((p, prog))
        live = still
        slot += 1


def _rows(*parts):
    return jnp.concatenate(parts, axis=0)


def _split_bf16(x, pieces):
    out = []
    for _ in range(pieces - 1):
        p = x.astype(BF16)
        out.append(p)
        x = x - p.astype(F32)
    out.append(x.astype(BF16))
    return out


def _dot_exact_rhs(x, m, pieces=3):
    mb = m.astype(BF16)
    acc = None
    for p in _split_bf16(x, pieces):
        t = jnp.dot(p, mb, preferred_element_type=F32)
        acc = t if acc is None else acc + t
    return acc


def _sigmoid(x):
    return 0.5 * jnp.tanh(0.5 * x) + 0.5


def _cumsum_rows(x):
    n = x.shape[0]
    row = lax.broadcasted_iota(jnp.int32, x.shape, 0)
    step = 1
    while step < n:
        x = x + jnp.where(row >= step, pltpu.roll(x, step, 0), 0.0)
        step *= 2
    return x


def _head_block_mask(scale):
    a = lax.broadcasted_iota(jnp.int32, (LANES, LANES), 0) // B_HEAD
    b = lax.broadcasted_iota(jnp.int32, (LANES, LANES), 1) // B_HEAD
    return jnp.where(a == b, scale, 0.0).astype(F32)


SRC_IF = CONV_COLS + 2 * A_VW
SRC_SHIFT = SRC_IF + 2 * A_HEADS
SRC_GATE = SRC_SHIFT + SHIFT_COLS
SRC_COLS = SRC_GATE + 2 * D_MODEL
RELAYOUT_ROWS = TAIL_MIX


def _relayout_src_row(t):
    unit = 2 * SUBLANES
    dst = t * (RELAYOUT_ROWS // unit)
    src = jnp.where(dst < COL_R // unit, dst,
                    jnp.where(dst < COL_GA // unit, dst + (SRC_SHIFT - COL_R) // unit,
                              jnp.where(dst < COL_TAIL // unit, dst + (SRC_GATE - COL_GA) // unit,
                                        jnp.where(dst < (COL_TAIL + TAIL_MIX) // unit,
                                                  (SRC_SHIFT + 3 * B_W) // unit, SRC_IF // unit))))
    return src * unit


def _relayout_body(w_ref, o_ref):
    o_ref[...] = w_ref[...].astype(BF16)


def _relayout_weight(wt):
    d = wt.shape[1]
    assert wt.shape[0] == SRC_COLS and PROJ_COLS == COL_TAIL + 2 * RELAYOUT_ROWS and COL_R == SRC_IF
    assert SRC_SHIFT + 3 * B_W + RELAYOUT_ROWS <= SRC_COLS and SRC_SHIFT % (2 * SUBLANES) == 0
    return pl.pallas_call(
        _relayout_body,
        grid=(PROJ_COLS // RELAYOUT_ROWS,),
        in_specs=[pl.BlockSpec((pl.Element(RELAYOUT_ROWS), pl.Element(d)), lambda t: (_relayout_src_row(t), 0))],
        out_specs=pl.BlockSpec((RELAYOUT_ROWS, d), lambda t: (t, 0)),
        out_shape=jax.ShapeDtypeStruct((PROJ_COLS, d), BF16),
        compiler_params=_cparams(("parallel",), 32),
        name="relayout_w_in",
    )(wt)


def _norm_rows(x_ref, g_ref, out_ref, row_chunk):
    def body(c, carry):
        rows = pl.ds(pl.multiple_of(c * row_chunk, row_chunk), row_chunk)
        x = x_ref[rows, :]
        ms = jnp.mean(x * x, axis=-1, keepdims=True)
        out_ref[rows, :] = (x * lax.rsqrt(ms + EPS) * g_ref[...]).astype(out_ref.dtype)
        return carry
    lax.fori_loop(0, x_ref.shape[0] // row_chunk, body, 0)


def _norm_matmul_body(x_ref, g_ref, w_ref, o_ref, hn_ref, *, row_chunk):
    @pl.when(pl.program_id(1) == 0)
    def _():
        _norm_rows(x_ref, g_ref, hn_ref, row_chunk)

    o_ref[...] = lax.dot_general(hn_ref[...], w_ref[...], (((1,), (1,)), ((), ())), preferred_element_type=F32)


def _norm_matmul(x, g, w, tm):
    n, d = x.shape
    p = w.shape[0]
    return pl.pallas_call(
        functools.partial(_norm_matmul_body, row_chunk=min(tm, LANES)),
        grid=(n // tm, p // PROJ_TN),
        in_specs=[pl.BlockSpec((tm, d), lambda i, j: (i, 0)),
                  pl.BlockSpec((1, d), lambda i, j: (0, 0)),
                  pl.BlockSpec((PROJ_TN, d), lambda i, j: (j, 0))],
        out_specs=pl.BlockSpec((tm, PROJ_TN), lambda i, j: (i, j)),
        out_shape=jax.ShapeDtypeStruct((n, p), F32),
        scratch_shapes=[pltpu.VMEM((tm, d), BF16)],
        compiler_params=_cparams(("parallel", "arbitrary"), 48),
        name="in_proj",
    )(x, g, w)


def _mlstm_body(u_ref, v_ref, o_ref, ga_ref, gc_ref, gr_ref, e_ref, cw_ref, cb_ref, brow_ref, bcol_ref,
                nw_ref, c0_ref, n0_ref, m0_ref,
                h_ref, cn_ref, nn_ref, mn_ref, c_s, n_s, m_s, e_s, *, T):
    c = pl.program_id(1)

    @pl.when(c == 0)
    def _():
        c_s[...] = c0_ref[0]
        n_s[...] = n0_ref[0]
        m_s[...] = m0_ref[0]
        e_s[:SUBLANES, :] = e_ref[0]

    u = u_ref[...]
    e_s[SUBLANES:, :] = u
    w = cw_ref[...]
    acc = cb_ref[...] + u * w[A_CONV - 1:A_CONV, :]
    for sh in range(1, A_CONV):
        acc = acc + e_s[pl.ds(SUBLANES - sh, T), :] * w[A_CONV - 1 - sh:A_CONV - sh, :]
    e_s[:SUBLANES, :] = u[T - SUBLANES:, :]
    qk_act = acc * _sigmoid(acc)

    gcol = gc_ref[...] + brow_ref[...]
    grow = gr_ref[0] + bcol_ref[...]
    lf_col = -_softplus(-gcol)
    ig_row = grow[:A_HEADS, :]
    lf_row = -_softplus(-grow[A_HEADS:, :])
    ri = lax.broadcasted_iota(jnp.int32, (T, T), 0)
    ci = lax.broadcasted_iota(jnp.int32, (T, T), 1)
    tri = ci <= ri
    b_col = _cumsum_rows(lf_col)
    b_row = _dot_exact_rhs(lf_row, jnp.where(ri <= ci, 1.0, 0.0))
    m_prev = m_s[...]
    lane = lax.broadcasted_iota(jnp.int32, (1, LANES), 1)
    heads = list(range(A_HEADS))
    bc = [b_col[:, A_HEADS + h:A_HEADS + h + 1] for h in heads]
    mh = [m_prev[:, A_HEADS + h:A_HEADS + h + 1] for h in heads]
    br = [b_row[h:h + 1, :] for h in heads]
    igr = [ig_row[h:h + 1, :] for h in heads]
    igc = [gcol[:, h:h + 1] for h in heads]
    qh = [qk_act[:, h * A_DK:(h + 1) * A_DK] for h in heads]
    kh = [qk_act[:, A_QK + h * A_DK:A_QK + (h + 1) * A_DK] * (A_DK ** -0.5) for h in heads]
    vh = [v_ref[:, h * A_DV:(h + 1) * A_DV] for h in heads]
    ch = [c_s[h] for h in heads]
    nh = [n_s[h:h + 1, :] for h in heads]
    qk = _each(_dot_nt, qh, kh)
    qc = _each(_dot, qh, ch)
    d = _each(lambda a, b, i: jnp.where(tri, a - b + i, -jnp.inf), bc, br, igr)
    inter = _each(jnp.add, bc, mh)
    m_t = _each(lambda i, x: jnp.maximum(i, jnp.max(x, axis=-1, keepdims=True)), inter, d)
    s = _each(lambda a, x, m: a * jnp.exp(x - m), qk, d, m_t)
    e_in = _each(lambda i, m: jnp.exp(i - m), inter, m_t)
    sv = _each(_dot, s, vh)
    num = _each(lambda a, x, b: a + x * b, sv, e_in, qc)
    den = _each(lambda a, x, qq, nn: jnp.sum(a, axis=-1, keepdims=True)
                + x * jnp.sum(qq * nn, axis=-1, keepdims=True), s, e_in, qh, nh)
    hh = _each(lambda a, b, m: a / jnp.maximum(jnp.abs(b), jnp.exp(-m)), num, den, m_t)
    hh = _each(lambda a: a * lax.rsqrt(jnp.mean(a * a, axis=-1, keepdims=True) + EPS), hh)
    b_end = _each(lambda a: a[T - 1:T, :], bc)
    m_new = _each(lambda be, m, b, i: jnp.maximum(be + m, jnp.max(be - b + i, axis=-1, keepdims=True)),
                  b_end, mh, br, igr)
    carry = _each(lambda be, m, mn: jnp.exp(be + m - mn), b_end, mh, m_new)
    wk = _each(lambda kk_, be, a, i, mn: kk_ * jnp.exp(be - a + i - mn), kh, b_end, bc, igc, m_new)
    kv = _each(_dot_tn, wk, vh)
    m_next = m_prev
    for h in heads:
        cols = slice(h * A_DV, (h + 1) * A_DV)
        gate = _sigmoid(o_ref[:, cols]) * _sigmoid(ga_ref[:, cols])
        h_ref[:, cols] = (hh[h] * nw_ref[:, cols] * gate).astype(h_ref.dtype)
        c_s[h] = carry[h] * ch[h] + kv[h]
        n_s[h:h + 1, :] = carry[h] * nh[h] + jnp.sum(wk[h], axis=0, keepdims=True)
        m_next = jnp.where(lane == A_HEADS + h, m_new[h], m_next)
    m_s[...] = m_next

    @pl.when(c == pl.num_programs(1) - 1)
    def _():
        cn_ref[0] = c_s[...]
        nn_ref[0] = n_s[...]
        mn_ref[0] = m_s[...]


def _mlstm(proj, gates_row, e_conv, conv_w, conv_b, bias_row, bias_col, norm_w, c0, n0, m0, *, T, n_seq, n_chunk):
    row = lambda s, c: (s * n_chunk + c, 0)
    fixed = lambda s, c: (0, 0)
    wide = lambda col: pl.BlockSpec((T, D_MODEL), lambda s, c: (s * n_chunk + c, col // D_MODEL))
    return pl.pallas_call(
        functools.partial(_mlstm_body, T=T),
        grid=(n_seq, n_chunk),
        in_specs=[wide(COL_CONV), wide(COL_V), wide(COL_O), wide(COL_GA),
                  pl.BlockSpec((T, LANES), lambda s, c: (s * n_chunk + c, (COL_TAIL + TAIL_IF) // LANES)),
                  pl.BlockSpec((1, 2 * A_HEADS, T), lambda s, c: (s * n_chunk + c, 0, 0)),
                  pl.BlockSpec((1, SUBLANES, CONV_COLS), lambda s, c: (s, 0, 0)),
                  pl.BlockSpec((A_CONV, CONV_COLS), fixed),
                  pl.BlockSpec((1, CONV_COLS), fixed),
                  pl.BlockSpec((1, LANES), fixed),
                  pl.BlockSpec((2 * A_HEADS, 1), fixed),
                  pl.BlockSpec((1, A_VW), fixed),
                  pl.BlockSpec((1, A_HEADS, A_DK, A_DV), lambda s, c: (s, 0, 0, 0)),
                  pl.BlockSpec((1, A_HEADS, A_DK), lambda s, c: (s, 0, 0)),
                  pl.BlockSpec((1, 1, LANES), lambda s, c: (s, 0, 0))],
        out_specs=[pl.BlockSpec((T, A_VW), row),
                   pl.BlockSpec((1, A_HEADS, A_DK, A_DV), lambda s, c: (s, 0, 0, 0)),
                   pl.BlockSpec((1, A_HEADS, A_DK), lambda s, c: (s, 0, 0)),
                   pl.BlockSpec((1, 1, LANES), lambda s, c: (s, 0, 0))],
        out_shape=[jax.ShapeDtypeStruct((n_seq * n_chunk * T, A_VW), BF16),
                   jax.ShapeDtypeStruct((n_seq, A_HEADS, A_DK, A_DV), F32),
                   jax.ShapeDtypeStruct((n_seq, A_HEADS, A_DK), F32),
                   jax.ShapeDtypeStruct((n_seq, 1, LANES), F32)],
        scratch_shapes=[pltpu.VMEM((A_HEADS, A_DK, A_DV), F32),
                        pltpu.VMEM((A_HEADS, A_DK), F32),
                        pltpu.VMEM((1, LANES), F32),
                        pltpu.VMEM((SUBLANES + T, CONV_COLS), F32)],
        compiler_params=_cparams(("arbitrary", "arbitrary"), 32),
        name="mlstm_chunk",
    )(proj, proj, proj, proj, proj, gates_row, e_conv, conv_w, conv_b, bias_row, bias_col, norm_w, c0, n0, m0)


def _kcat_dot(a1, b1, a2, b2):
    if a1.shape[1] % LANES == 0:
        return _dot(jnp.concatenate([a1, a2], axis=1), _rows(b1, b2))
    return _dot(a1, b1) + _dot(a2, b2)


def _head_sums(ones_blk, T, *xs):
    sums = jnp.dot(_rows(*xs).astype(BF16), ones_blk, preferred_element_type=F32)
    return [sums[i * T:(i + 1) * T] for i in range(len(xs))]


def _rwkv_body(ur_ref, uk_ref, uv_ref, ul_ref, gb_ref, er_ref, ek_ref, ev_ref, el_ref,
               mur_ref, muk_ref, muv_ref, mul_ref, w0_ref, w2_ref, a0_ref, a2_ref, g2_ref, kkw_ref, ka_ref,
               rk_ref, lnw_ref, lnb_ref, s0_ref,
               hb_ref, sn_ref, s_s, cr_s, ck_s, cv_s, cl_s, *, T, n_pairs):
    c = pl.program_id(2)

    @pl.when(c == 0)
    def _():
        s_s[...] = s0_ref[0]
        cr_s[:SUBLANES, :] = er_ref[0]
        ck_s[:SUBLANES, :] = ek_ref[0]
        cv_s[:SUBLANES, :] = ev_ref[0]
        cl_s[:SUBLANES, :] = el_ref[0]

    def shift_mix(u_ref, ext_ref, mu_ref, cs):
        u = u_ref[:, cs]
        ext_ref[SUBLANES:, cs] = u
        xs = u + mu_ref[:, cs] * (ext_ref[pl.ds(SUBLANES - 1, T), cs] - u)
        ext_ref[:SUBLANES, cs] = u[T - SUBLANES:, :]
        return xs

    xl = shift_mix(ul_ref, cl_s, mul_ref, slice(None))
    th_wd = jnp.tanh(xl[:, :LANES]).astype(BF16)
    x_ad = xl[:, :2 * LANES].astype(BF16)
    sg_gd = _sigmoid(xl[:, TAIL_GD_WIN:]).astype(BF16)

    lo = lax.broadcasted_iota(jnp.int32, (1, LANES), 1) < B_HEAD
    head1 = lambda x: jnp.where(lo, x, 0.0)
    head2 = lambda x: jnp.where(lo, 0.0, x)
    r2 = lax.broadcasted_iota(jnp.int32, (T, 2 * T), 0)
    c2 = lax.broadcasted_iota(jnp.int32, (T, 2 * T), 1)
    left = c2 < T
    cj = jnp.where(left, c2, c2 - T)
    strict2 = cj < r2
    incl2 = cj <= r2
    eye_pair = jnp.where(cj == r2, 1.0, 0.0)
    ones_blk = _head_block_mask(1.0).astype(BF16)
    same_head = _head_block_mask(1.0) > 0.5
    zero = jnp.zeros((T, LANES), F32)
    aligned = (2 * T) % LANES == 0

    def pair_program(p):
        cs = slice(p * LANES, (p + 1) * LANES)
        r = shift_mix(ur_ref, cr_s, mur_ref, cs)
        xk = shift_mix(uk_ref, ck_s, muk_ref, cs)
        v = shift_mix(uv_ref, cv_s, muv_ref, cs)
        lw = jnp.dot(th_wd, w2_ref[:, cs], preferred_element_type=F32)
        la = jnp.dot(x_ad, a2_ref[:, cs], preferred_element_type=F32)
        lg = jnp.dot(sg_gd, g2_ref[:, cs], preferred_element_type=F32)
        kkw = xk * kkw_ref[:, cs]
        ssq, = _head_sums(ones_blk, T, kkw * kkw)
        yield
        ld = DECAY_SCALE * _sigmoid(w0_ref[:, cs] + lw)
        a = _sigmoid(a0_ref[:, cs] + la)
        g = lg * _sigmoid(gb_ref[:, cs])
        k = xk * (1.0 + (a - 1.0) * ka_ref[:, cs])
        kk = kkw / jnp.maximum(jnp.sqrt(ssq), 1e-12)
        bi = kk * a
        cum = _cumsum_rows(ld)
        ewt = jnp.exp(cum[T - 1:T, :])
        Rt = r * jnp.exp(cum)
        At = -kk * jnp.exp(cum - ld)
        einv = jnp.exp(-cum)
        Bt = bi * einv
        Kt = k * einv
        Bh = Bt * ewt
        Kh = Kt * ewt
        V1 = head1(v)
        V2 = head2(v)
        if aligned:
            G = _dot_nt(_rows(head1(At), head1(Rt), head2(At), head2(Rt)), _rows(Bt, Kt))
            G1, G2 = G[:2 * T], G[2 * T:]
        else:
            G1 = _dot_nt(_rows(head1(At), head1(Rt)), _rows(Bt, Kt))
            G2 = _dot_nt(_rows(head2(At), head2(Rt)), _rows(Kt, Bt))
        yield
        LL1 = jnp.where(strict2, G1[:T], 0.0)
        LL2 = jnp.where(strict2, G2[:T], 0.0)
        P1 = jnp.where(incl2, G1[T:], 0.0)
        P2 = jnp.where(incl2, G2[T:], 0.0)
        lab2 = pltpu.roll(LL2, T, 1) if aligned else LL2
        pw = _rows(jnp.where(left, LL1, 0.0), jnp.where(left, 0.0, lab2))
        iv = eye_pair + jnp.where(left, LL1, lab2)
        X0 = _kcat_dot(LL1, _rows(zero, V1), LL2, _rows(zero, V2) if aligned else _rows(V2, zero))
        pw = _dot(pw, pw)
        yield
        span = 4
        while span < T:
            both = _dot(_rows(iv, pw), pw)
            iv, pw = iv + both[:T], both[T:]
            span *= 2
            yield
        iv = iv + _dot(iv, pw)
        S = s_s[p]
        AR = _dot_nt(_rows(At, Rt), S)
        yield
        X = AR[:T] + X0
        U = _dot(iv, _rows(head1(X), head2(X)))
        yield
        if aligned:
            rhs2 = _rows(head2(U), V2)
        else:
            rhs2 = _rows(V2, head2(U))
        o = AR[T:] + _kcat_dot(P1, _rows(head1(U), V1), P2, rhs2)
        s_s[p] = S * ewt + jnp.where(same_head, _dot_tn(_rows(U, v), _rows(Bh, Kh)), 0.0)
        yield
        o_sum, rk_sum = _head_sums(ones_blk, T, o, r * k * rk_ref[:, cs])
        yield
        oc = o - o_sum * (1.0 / B_HEAD)
        var, = _head_sums(ones_blk, T, oc * oc)
        yield
        on = oc * lax.rsqrt(var * (1.0 / B_HEAD) + GN_EPS) * lnw_ref[:, cs] + lnb_ref[:, cs]
        hb_ref[:, cs] = ((on + rk_sum * v) * g).astype(hb_ref.dtype)

    _run_staggered([pair_program(p) for p in range(n_pairs)], RWKV_WAVE, RWKV_LAG)

    @pl.when(c == pl.num_programs(2) - 1)
    def _():
        sn_ref[0] = s_s[...]


def _rwkv(proj, e_main, e_tail, mu_main, mu_tail, w0, w2p, a0, a2p, g2, k_k, k_a, rk, lnw, lnb, s0, *,
          T, n_seq, n_chunk):
    assert T <= RWKV_CHUNK
    gw = GROUP_W
    wide = lambda col: pl.BlockSpec((T, gw), lambda s, pg, c: (s * n_chunk + c, col // gw + pg))
    edge = lambda col: pl.BlockSpec((1, SUBLANES, gw), lambda s, pg, c: (s, 0, col // gw + pg))
    vec = lambda col: pl.BlockSpec((1, gw), lambda s, pg, c: (0, col // gw + pg))
    mat = lambda rows: pl.BlockSpec((rows, gw), lambda s, pg, c: (0, pg))
    st = pl.BlockSpec((1, GROUP_PAIRS, LANES, LANES), lambda s, pg, c: (s, pg, 0, 0))
    return pl.pallas_call(
        functools.partial(_rwkv_body, T=T, n_pairs=GROUP_PAIRS),
        grid=(n_seq, N_PAIRS // GROUP_PAIRS, n_chunk),
        in_specs=[wide(COL_R), wide(COL_K), wide(COL_VB),
                  pl.BlockSpec((T, TAIL_MIX), lambda s, pg, c: (s * n_chunk + c, COL_TAIL // TAIL_MIX)),
                  wide(COL_GB),
                  edge(0), edge(B_W), edge(2 * B_W),
                  pl.BlockSpec((1, SUBLANES, TAIL_MIX), lambda s, pg, c: (s, 0, 0)),
                  vec(0), vec(B_W), vec(2 * B_W),
                  pl.BlockSpec((1, TAIL_MIX), lambda s, pg, c: (0, 0)),
                  vec(0), mat(LANES), vec(0), mat(2 * LANES), mat(TAIL_MIX - TAIL_GD_WIN), vec(0), vec(0),
                  vec(0), vec(0), vec(0), st],
        out_specs=[pl.BlockSpec((T, gw), lambda s, pg, c: (s * n_chunk + c, pg)), st],
        out_shape=[jax.ShapeDtypeStruct((n_seq * n_chunk * T, B_W), BF16),
                   jax.ShapeDtypeStruct((n_seq, N_PAIRS, LANES, LANES), F32)],
        scratch_shapes=[pltpu.VMEM((GROUP_PAIRS, LANES, LANES), F32),
                        pltpu.VMEM((SUBLANES + T, gw), F32), pltpu.VMEM((SUBLANES + T, gw), F32),
                        pltpu.VMEM((SUBLANES + T, gw), F32), pltpu.VMEM((SUBLANES + T, TAIL_MIX), F32)],
        compiler_params=_cparams(("arbitrary", "arbitrary", "arbitrary"), 32),
        name="rwkv_chunk",
    )(proj, proj, proj, proj, proj, e_main, e_main, e_main, e_tail, mu_main, mu_main, mu_main, mu_tail,
      w0, w2p, a0, a2p, g2, k_k, k_a, rk, lnw, lnb, s0)


def _merge_out_body(ha_ref, hb_ref, x_ref, w_ref, g_ref, y_ref):
    mixed = ha_ref[...].astype(F32) + hb_ref[...].astype(F32)
    y = jnp.dot(mixed.astype(BF16), w_ref[...], preferred_element_type=F32)
    ms = jnp.mean(y * y, axis=-1, keepdims=True)
    y_ref[...] = x_ref[...] + y * lax.rsqrt(ms + EPS) * g_ref[...]


def _merge_out(h_a, h_b, x, w_out, g_post):
    n = x.shape[0]
    wide = pl.BlockSpec((MERGE_TM, D_MODEL), lambda i: (i, 0))
    return pl.pallas_call(
        _merge_out_body,
        grid=(n // MERGE_TM,),
        in_specs=[wide, wide, wide,
                  pl.BlockSpec((D_MODEL, D_MODEL), lambda i: (0, 0)),
                  pl.BlockSpec((1, D_MODEL), lambda i: (0, 0))],
        out_specs=wide,
        out_shape=jax.ShapeDtypeStruct((n, D_MODEL), F32),
        compiler_params=_cparams(("parallel",), 48),
        name="merge_out_proj",
    )(h_a, h_b, x, w_out, g_post)


def _mlp_body(x_ref, g1_ref, wu_ref, wd_ref, g2_ref, y_ref, hn_ref, acc_ref, *, row_chunk):
    j = pl.program_id(1)

    @pl.when(j == 0)
    def _():
        _norm_rows(x_ref, g1_ref, hn_ref, row_chunk)
        acc_ref[...] = jnp.zeros_like(acc_ref)

    u = jnp.dot(hn_ref[...], wu_ref[...], preferred_element_type=F32)
    a = jnp.square(jnp.maximum(u, 0.0))
    acc_ref[...] += jnp.dot(a.astype(BF16), wd_ref[...], preferred_element_type=F32)

    @pl.when(j == pl.num_programs(1) - 1)
    def _():
        def body(c, carry):
            rows = pl.ds(pl.multiple_of(c * row_chunk, row_chunk), row_chunk)
            ff = acc_ref[rows, :]
            ms = jnp.mean(ff * ff, axis=-1, keepdims=True)
            y_ref[rows, :] = x_ref[rows, :] + ff * lax.rsqrt(ms + EPS) * g2_ref[...]
            return carry
        lax.fori_loop(0, x_ref.shape[0] // row_chunk, body, 0)


def _mlp(x, g_pre, w_up, w_down, g_post):
    n, d = x.shape
    f = w_up.shape[1]
    return pl.pallas_call(
        functools.partial(_mlp_body, row_chunk=LANES),
        grid=(n // MLP_TM, f // MLP_TF),
        in_specs=[pl.BlockSpec((MLP_TM, d), lambda i, j: (i, 0)),
                  pl.BlockSpec((1, d), lambda i, j: (0, 0)),
                  pl.BlockSpec((d, MLP_TF), lambda i, j: (0, j)),
                  pl.BlockSpec((MLP_TF, d), lambda i, j: (j, 0)),
                  pl.BlockSpec((1, d), lambda i, j: (0, 0))],
        out_specs=pl.BlockSpec((MLP_TM, d), lambda i, j: (i, 0)),
        out_shape=jax.ShapeDtypeStruct((n, d), F32),
        scratch_shapes=[pltpu.VMEM((MLP_TM, d), BF16), pltpu.VMEM((MLP_TM, d), F32)],
        compiler_params=_cparams(("parallel", "arbitrary"), 56),
        name="mlp",
    )(x, g_pre, w_up, w_down, g_post)


def _pad_cols(a, width):
    return jnp.pad(a, [(0, 0)] * (a.ndim - 1) + [(0, width - a.shape[-1])])


def _pad_rows(a, before, total):
    return jnp.pad(a, ((before, total - before - a.shape[0]), (0, 0)))


def _edge_blocks(prev_rows):
    return jnp.pad(prev_rows, ((0, 0), (SUBLANES - prev_rows.shape[1], 0), (0, 0)))


def _pack_state(s):
    n = s.shape[0]
    s = s.reshape(n, N_PAIRS, 2, B_HEAD, B_HEAD)
    z = jnp.zeros((n, N_PAIRS, B_HEAD, B_HEAD), s.dtype)
    top = jnp.concatenate([s[:, :, 0], z], axis=-1)
    bot = jnp.concatenate([z, s[:, :, 1]], axis=-1)
    return jnp.concatenate([top, bot], axis=-2)


def _unpack_state(sp):
    n = sp.shape[0]
    a = sp[:, :, :B_HEAD, :B_HEAD]
    b = sp[:, :, B_HEAD:, B_HEAD:]
    return jnp.stack([a, b], axis=2).reshape(n, B_HEADS, B_HEAD, B_HEAD)


def _gates_row_form(proj, T):
    g = proj[:, COL_TAIL + TAIL_IF:COL_TAIL + TAIL_IF + 2 * A_HEADS]
    return jnp.swapaxes(g.reshape(-1, T, 2 * A_HEADS), 1, 2)


def kernel(x_prompt, x_sample, state_mlstm_C, state_mlstm_n, state_mlstm_m, state_mlstm_conv, state_rwkv_S, state_rwkv_shift, meta_tokens, norm_mix_pre, norm_mix_post, norm_mlp_pre, norm_mlp_post, w_in, mlstm_conv_w, mlstm_conv_b, mlstm_b_i, mlstm_b_f, mlstm_norm_w, rwkv_mu, rwkv_w0, rwkv_w2, rwkv_a0, rwkv_a2, rwkv_g2, rwkv_k_k, rwkv_k_a, rwkv_r_k, rwkv_ln_w, rwkv_ln_b, w_out, w_up, w_down):
    n_s_seq, s_len, _ = x_sample.shape
    _, p_len, _ = x_prompt.shape
    k3 = A_CONV - 1

    w_all = _relayout_weight(jnp.swapaxes(w_in[0], 0, 1))
    mu = rwkv_mu[0][None, :]
    mu_main = mu[:, :3 * B_W]
    mu_tail = _pad_cols(mu[:, 3 * B_W:], TAIL_MIX)
    w2p = _pad_rows(rwkv_w2[0], 0, LANES).astype(BF16)
    a2p = _pad_rows(rwkv_a2[0], LORA_DECAY, 2 * LANES).astype(BF16)
    g2 = _pad_rows(rwkv_g2[0], LORA_DECAY + LORA_ICLR - TAIL_GD_WIN, TAIL_MIX - TAIL_GD_WIN).astype(BF16)
    g_pre = norm_mix_pre[0][None, :]
    bias = jnp.concatenate([mlstm_b_i[0], mlstm_b_f[0]])
    bias_row = _pad_cols(bias[None, :], LANES)
    bias_col = bias[:, None]
    norm_w = mlstm_norm_w[0][None, :]
    conv_w = mlstm_conv_w[0]
    conv_b = mlstm_conv_b[0][None, :]
    rwkv_vecs = (rwkv_w0[0][None, :], w2p, rwkv_a0[0][None, :], a2p, g2, rwkv_k_k[0][None, :],
                 rwkv_k_a[0][None, :], rwkv_r_k[0].reshape(1, B_W), rwkv_ln_w[0][None, :], rwkv_ln_b[0][None, :])
    w_out_b = w_out[0].astype(BF16)
    w_up_b = w_up[0].astype(BF16)
    w_down_b = w_down[0].astype(BF16)

    def branches(x_rows, tm, T, n_seq, n_chunk, e_conv, e_main, e_tail, c0, n0, m0, s0):
        proj = _norm_matmul(x_rows, g_pre, w_all, tm)
        h_a, c_n, n_n, m_n = _mlstm(proj, _gates_row_form(proj, T), e_conv, conv_w, conv_b, bias_row, bias_col,
                                    norm_w, c0, n0, m0, T=T, n_seq=n_seq, n_chunk=n_chunk)
        h_b, s_n = _rwkv(proj, e_main, e_tail, mu_main, mu_tail, *rwkv_vecs, s0,
                         T=T, n_seq=n_seq, n_chunk=n_chunk)
        return proj, h_a, h_b, (c_n, n_n, m_n, s_n)

    def finish(h_a, h_b, x_rows):
        x1 = _merge_out(h_a, h_b, x_rows, w_out_b, norm_mix_post[0][None, :])
        return _mlp(x1, norm_mlp_pre[0][None, :], w_up_b, w_down_b, norm_mlp_post[0][None, :])

    zeros = lambda *shape: jnp.zeros(shape, F32)
    proj_m, _, _, (c_m, n_m, m_m, s_m) = branches(
        meta_tokens.astype(F32), N_META, N_META, 1, 1,
        zeros(1, SUBLANES, CONV_COLS), zeros(1, SUBLANES, 3 * B_W), zeros(1, SUBLANES, TAIL_MIX),
        zeros(1, A_HEADS, A_DK, A_DV), zeros(1, A_HEADS, A_DK), zeros(1, 1, LANES),
        zeros(1, N_PAIRS, LANES, LANES))

    xp = x_prompt[0].astype(F32)
    proj_p, ha_p, hb_p, (c_p, n_p, m_p, s_p) = branches(
        xp, PROJ_TM, MLSTM_CHUNK, 1, p_len // MLSTM_CHUNK,
        _edge_blocks(proj_m[None, N_META - k3:, COL_CONV:COL_CONV + CONV_COLS]),
        _edge_blocks(proj_m[None, N_META - 1:, COL_R:COL_R + 3 * B_W]),
        _edge_blocks(proj_m[None, N_META - 1:, COL_TAIL:COL_TAIL + TAIL_MIX]),
        c_m, n_m, m_m, s_m)
    y_prompt = finish(ha_p, hb_p, xp)[None].astype(x_prompt.dtype)

    xs = x_sample.reshape(n_s_seq * s_len, D_MODEL).astype(F32)
    sh = state_rwkv_shift[0].astype(F32)
    m0_s = jnp.pad(state_mlstm_m[0].astype(F32), ((0, 0), (A_HEADS, LANES - 2 * A_HEADS)))[:, None, :]
    proj_s, ha_s, hb_s, (c_s, n_s, m_s, s_s) = branches(
        xs, PROJ_TM, s_len, n_s_seq, 1,
        _edge_blocks(state_mlstm_conv[0].astype(F32)),
        _edge_blocks(sh[:, :, :3 * B_W]), _edge_blocks(_pad_cols(sh[:, :, 3 * B_W:], TAIL_MIX)),
        state_mlstm_C[0].astype(F32), state_mlstm_n[0].astype(F32), m0_s,
        _pack_state(state_rwkv_S[0].astype(F32)))
    y_sample = finish(ha_s, hb_s, xs).reshape(x_sample.shape).astype(x_sample.dtype)

    def shift_state(rows):
        return jnp.concatenate([rows[..., COL_R:COL_R + 3 * B_W],
                                rows[..., COL_TAIL:COL_TAIL + LORA_COLS]], axis=-1)

    last_s = proj_s.reshape(n_s_seq, s_len, PROJ_COLS)[:, s_len - k3:, :]
    conv_s = last_s[..., COL_CONV:COL_CONV + CONV_COLS]
    shift_s = shift_state(last_s[:, k3 - 1:, :])
    last_p = proj_p[p_len - k3:, :]
    conv_p = last_p[None, :, COL_CONV:COL_CONV + CONV_COLS]
    shift_p = shift_state(last_p[None, k3 - 1:, :])
    dt_c, dt_n, dt_m = state_mlstm_C.dtype, state_mlstm_n.dtype, state_mlstm_m.dtype
    dt_cv, dt_s, dt_sh = state_mlstm_conv.dtype, state_rwkv_S.dtype, state_rwkv_shift.dtype
    lanes_m = slice(A_HEADS, 2 * A_HEADS)
    return (y_prompt, y_sample,
            c_p[None].astype(dt_c), n_p[None].astype(dt_n), m_p[:, 0, lanes_m][None].astype(dt_m),
            conv_p[None].astype(dt_cv), _unpack_state(s_p)[None].astype(dt_s), shift_p[None].astype(dt_sh),
            c_s[None].astype(dt_c), n_s[None].astype(dt_n), m_s[:, 0, lanes_m][None].astype(dt_m),
            conv_s[None].astype(dt_cv), _unpack_state(s_s)[None].astype(dt_s), shift_s[None].astype(dt_sh))
```

```python
import functools

import jax
import jax.numpy as jnp
from jax import lax
from jax.experimental import pallas as pl
from jax.experimental.pallas import tpu as pltpu

F32 = jnp.float32
BF16 = jnp.bfloat16

D_MODEL = 2048
N_META = 16
EPS = 1e-6
D_FF = 4 * D_MODEL
A_HEADS = 8
A_DK = 128
A_DV = D_MODEL // A_HEADS
A_QK = A_HEADS * A_DK
A_VW = A_HEADS * A_DV
A_CONV = 4
MLSTM_CHUNK = 64
B_HEAD = 64
B_HEADS = D_MODEL // B_HEAD
B_W = B_HEADS * B_HEAD
LORA_DECAY = max(32, int(round(1.8 * D_MODEL ** 0.5 / 32)) * 32)
LORA_ICLR = max(32, int(round(1.8 * D_MODEL ** 0.5 / 32)) * 32)
LORA_GATE = max(32, int(round(0.6 * D_MODEL ** 0.8 / 32)) * 32)
GN_EPS = 64e-5
CONV_COLS = 2 * A_QK
SHIFT_COLS = 3 * B_W + LORA_DECAY + LORA_ICLR + LORA_GATE

LANES = 128
SUBLANES = 8

COL_CONV, COL_V, COL_O, COL_R, COL_K, COL_VB, COL_GA, COL_GB = (i * D_MODEL for i in range(8))
COL_TAIL = 8 * D_MODEL
LORA_COLS = LORA_DECAY + LORA_ICLR + LORA_GATE
TAIL_MIX = -(-LORA_COLS // LANES) * LANES
TAIL_IF = TAIL_MIX
TAIL_GD_WIN = (LORA_DECAY + LORA_ICLR) // LANES * LANES
PROJ_TN = 1024
PROJ_COLS = -(-(COL_TAIL + TAIL_IF + LANES) // PROJ_TN) * PROJ_TN
PROJ_TM = 1024
MERGE_TM = 256
MLP_TM, MLP_TF = 512, 1024
DECAY_SCALE = -0.6065306597126334
RWKV_CHUNK = 64
N_PAIRS = B_HEADS // 2
GROUP_PAIRS = 16
RWKV_WAVE, RWKV_LAG = 8, 2
GROUP_W = GROUP_PAIRS * LANES


def _cparams(semantics, vmem_mib):
    return pltpu.CompilerParams(dimension_semantics=semantics, vmem_limit_bytes=vmem_mib << 20)


def _softplus(x):
    return jnp.maximum(x, 0.0) + jnp.log1p(jnp.exp(-jnp.abs(x)))


def _dot(a, b):
    return jnp.dot(a.astype(BF16), b.astype(BF16), preferred_element_type=F32)


def _dot_nt(a, b):
    return lax.dot_general(a.astype(BF16), b.astype(BF16), (((1,), (1,)), ((), ())),
                           preferred_element_type=F32)


def _dot_tn(a, b):
    return lax.dot_general(a.astype(BF16), b.astype(BF16), (((0,), (0,)), ((), ())),
                           preferred_element_type=F32)


def _each(f, *lists):
    return [f(*args) for args in zip(*lists)]


def _run_staggered(programs, wave, lag):
    live = list(enumerate(programs))
    slot = 0
    while live:
        still = []
        for p, prog in live:
            if slot >= (p // wave) * lag:
                try:
                    next(prog)
                except StopIteration:
                    continue
            still.append((p, prog))
        live = still
        slot += 1


def _rows(*parts):
    return jnp.concatenate(parts, axis=0)


def _split_bf16(x, pieces):
    out = []
    for _ in range(pieces - 1):
        p = x.astype(BF16)
        out.append(p)
        x = x - p.astype(F32)
    out.append(x.astype(BF16))
    return out


def _dot_exact_rhs(x, m, pieces=3):
    mb = m.astype(BF16)
    acc = None
    for p in _split_bf16(x, pieces):
        t = jnp.dot(p, mb, preferred_element_type=F32)
        acc = t if acc is None else acc + t
    return acc


def _sigmoid(x):
    return 0.5 * jnp.tanh(0.5 * x) + 0.5


def _cumsum_rows(x):
    n = x.shape[0]
    row = lax.broadcasted_iota(jnp.int32, x.shape, 0)
    step = 1
    while step < n:
        x = x + jnp.where(row >= step, pltpu.roll(x, step, 0), 0.0)
        step *= 2
    return x


def _head_block_mask():
    a = lax.broadcasted_iota(jnp.int32, (LANES, LANES), 0) // B_HEAD
    b = lax.broadcasted_iota(jnp.int32, (LANES, LANES), 1) // B_HEAD
    return jnp.where(a == b, 1.0, 0.0).astype(F32)


SRC_IF = CONV_COLS + 2 * A_VW
SRC_SHIFT = SRC_IF + 2 * A_HEADS
SRC_GATE = SRC_SHIFT + SHIFT_COLS
SRC_COLS = SRC_GATE + 2 * D_MODEL
RELAYOUT_ROWS = TAIL_MIX


def _relayout_src_row(t):
    unit = 2 * SUBLANES
    dst = t * (RELAYOUT_ROWS // unit)
    src = jnp.where(dst < COL_R // unit, dst,
                    jnp.where(dst < COL_GA // unit, dst + (SRC_SHIFT - COL_R) // unit,
                              jnp.where(dst < COL_TAIL // unit, dst + (SRC_GATE - COL_GA) // unit,
                                        jnp.where(dst < (COL_TAIL + TAIL_MIX) // unit,
                                                  (SRC_SHIFT + 3 * B_W) // unit, SRC_IF // unit))))
    return src * unit


def _relayout_body(w_ref, o_ref):
    o_ref[...] = w_ref[...].astype(BF16)


def _relayout_weight(wt):
    d = wt.shape[1]
    assert wt.shape[0] == SRC_COLS and PROJ_COLS == COL_TAIL + 2 * RELAYOUT_ROWS and COL_R == SRC_IF
    assert SRC_SHIFT + 3 * B_W + RELAYOUT_ROWS <= SRC_COLS and SRC_SHIFT % (2 * SUBLANES) == 0
    return pl.pallas_call(
        _relayout_body,
        grid=(PROJ_COLS // RELAYOUT_ROWS,),
        in_specs=[pl.BlockSpec((pl.Element(RELAYOUT_ROWS), pl.Element(d)), lambda t: (_relayout_src_row(t), 0))],
        out_specs=pl.BlockSpec((RELAYOUT_ROWS, d), lambda t: (t, 0)),
        out_shape=jax.ShapeDtypeStruct((PROJ_COLS, d), BF16),
        compiler_params=_cparams(("parallel",), 32),
        name="relayout_w_in",
    )(wt)


def _norm_rows(x_ref, g_ref, out_ref, row_chunk):
    def body(c, carry):
        rows = pl.ds(pl.multiple_of(c * row_chunk, row_chunk), row_chunk)
        x = x_ref[rows, :]
        ms = jnp.mean(x * x, axis=-1, keepdims=True)
        out_ref[rows, :] = (x * lax.rsqrt(ms + EPS) * g_ref[...]).astype(out_ref.dtype)
        return carry
    lax.fori_loop(0, x_ref.shape[0] // row_chunk, body, 0)


def _norm_matmul_body(x_ref, g_ref, w_ref, o_ref, hn_ref, *, row_chunk):
    @pl.when(pl.program_id(1) == 0)
    def _():
        _norm_rows(x_ref, g_ref, hn_ref, row_chunk)

    o_ref[...] = lax.dot_general(hn_ref[...], w_ref[...], (((1,), (1,)), ((), ())), preferred_element_type=F32)


def _norm_matmul(x, g, w, tm):
    n, d = x.shape
    p = w.shape[0]
    return pl.pallas_call(
        functools.partial(_norm_matmul_body, row_chunk=min(tm, LANES)),
        grid=(n // tm, p // PROJ_TN),
        in_specs=[pl.BlockSpec((tm, d), lambda i, j: (i, 0)),
                  pl.BlockSpec((1, d), lambda i, j: (0, 0)),
                  pl.BlockSpec((PROJ_TN, d), lambda i, j: (j, 0))],
        out_specs=pl.BlockSpec((tm, PROJ_TN), lambda i, j: (i, j)),
        out_shape=jax.ShapeDtypeStruct((n, p), F32),
        scratch_shapes=[pltpu.VMEM((tm, d), BF16)],
        compiler_params=_cparams(("parallel", "arbitrary"), 48),
        name="in_proj",
    )(x, g, w)


def _mlstm_body(u_ref, v_ref, o_ref, ga_ref, gc_ref, gr_ref, e_ref, cw_ref, cb_ref, brow_ref, bcol_ref,
                nw_ref, c0_ref, n0_ref, m0_ref,
                h_ref, cn_ref, nn_ref, mn_ref, c_s, n_s, m_s, e_s, *, T):
    c = pl.program_id(1)

    @pl.when(c == 0)
    def _():
        c_s[...] = c0_ref[0]
        n_s[...] = n0_ref[0]
        m_s[...] = m0_ref[0]
        e_s[...] = e_ref[0]

    u = u_ref[...]
    ext = _rows(e_s[...], u)
    w = cw_ref[...]
    acc = cb_ref[...] + u * w[A_CONV - 1:A_CONV, :]
    for sh in range(1, A_CONV):
        acc = acc + pltpu.roll(ext, sh, 0)[SUBLANES:, :] * w[A_CONV - 1 - sh:A_CONV - sh, :]
    e_s[...] = u[T - SUBLANES:, :]
    qk_act = acc * _sigmoid(acc)

    gcol = gc_ref[...] + brow_ref[...]
    grow = gr_ref[0] + bcol_ref[...]
    lf_col = -_softplus(-gcol)
    ig_row = grow[:A_HEADS, :]
    lf_row = -_softplus(-grow[A_HEADS:, :])
    ri = lax.broadcasted_iota(jnp.int32, (T, T), 0)
    ci = lax.broadcasted_iota(jnp.int32, (T, T), 1)
    tri = ci <= ri
    b_col = _cumsum_rows(lf_col)
    b_row = _dot_exact_rhs(lf_row, jnp.where(ri <= ci, 1.0, 0.0))
    m_prev = m_s[...]
    lane = lax.broadcasted_iota(jnp.int32, (1, LANES), 1)
    heads = list(range(A_HEADS))
    bc = [b_col[:, A_HEADS + h:A_HEADS + h + 1] for h in heads]
    mh = [m_prev[:, A_HEADS + h:A_HEADS + h + 1] for h in heads]
    br = [b_row[h:h + 1, :] for h in heads]
    igr = [ig_row[h:h + 1, :] for h in heads]
    igc = [gcol[:, h:h + 1] for h in heads]
    qh = [qk_act[:, h * A_DK:(h + 1) * A_DK] for h in heads]
    kh = [qk_act[:, A_QK + h * A_DK:A_QK + (h + 1) * A_DK] * (A_DK ** -0.5) for h in heads]
    vh = [v_ref[:, h * A_DV:(h + 1) * A_DV] for h in heads]
    ch = [c_s[h] for h in heads]
    nh = [n_s[h:h + 1, :] for h in heads]
    qk = _each(_dot_nt, qh, kh)
    qc = _each(_dot, qh, ch)
    d = _each(lambda a, b, i: jnp.where(tri, a - b + i, -jnp.inf), bc, br, igr)
    inter = _each(jnp.add, bc, mh)
    m_t = _each(lambda i, x: jnp.maximum(i, jnp.max(x, axis=-1, keepdims=True)), inter, d)
    s = _each(lambda a, x, m: a * jnp.exp(x - m), qk, d, m_t)
    e_in = _each(lambda i, m: jnp.exp(i - m), inter, m_t)
    sv = _each(_dot, s, vh)
    num = _each(lambda a, x, b: a + x * b, sv, e_in, qc)
    den = _each(lambda a, x, qq, nn: jnp.sum(a, axis=-1, keepdims=True)
                + x * jnp.sum(qq * nn, axis=-1, keepdims=True), s, e_in, qh, nh)
    hh = _each(lambda a, b, m: a / jnp.maximum(jnp.abs(b), jnp.exp(-m)), num, den, m_t)
    hh = _each(lambda a: a * lax.rsqrt(jnp.mean(a * a, axis=-1, keepdims=True) + EPS), hh)
    b_end = _each(lambda a: a[T - 1:T, :], bc)
    m_new = _each(lambda be, m, b, i: jnp.maximum(be + m, jnp.max(be - b + i, axis=-1, keepdims=True)),
                  b_end, mh, br, igr)
    carry = _each(lambda be, m, mn: jnp.exp(be + m - mn), b_end, mh, m_new)
    wk = _each(lambda kk_, be, a, i, mn: kk_ * jnp.exp(be - a + i - mn), kh, b_end, bc, igc, m_new)
    kv = _each(_dot_tn, wk, vh)
    m_next = m_prev
    for h in heads:
        cols = slice(h * A_DV, (h + 1) * A_DV)
        gate = _sigmoid(o_ref[:, cols]) * _sigmoid(ga_ref[:, cols])
        h_ref[:, cols] = (hh[h] * nw_ref[:, cols] * gate).astype(h_ref.dtype)
        c_s[h] = carry[h] * ch[h] + kv[h]
        n_s[h:h + 1, :] = carry[h] * nh[h] + jnp.sum(wk[h], axis=0, keepdims=True)
        m_next = jnp.where(lane == A_HEADS + h, m_new[h], m_next)
    m_s[...] = m_next

    @pl.when(c == pl.num_programs(1) - 1)
    def _():
        cn_ref[0] = c_s[...]
        nn_ref[0] = n_s[...]
        mn_ref[0] = m_s[...]


def _mlstm(proj, gates_row, e_conv, conv_w, conv_b, bias_row, bias_col, norm_w, c0, n0, m0, *, T, n_seq, n_chunk):
    row = lambda s, c: (s * n_chunk + c, 0)
    fixed = lambda s, c: (0, 0)
    wide = lambda col: pl.BlockSpec((T, D_MODEL), lambda s, c: (s * n_chunk + c, col // D_MODEL))
    return pl.pallas_call(
        functools.partial(_mlstm_body, T=T),
        grid=(n_seq, n_chunk),
        in_specs=[wide(COL_CONV), wide(COL_V), wide(COL_O), wide(COL_GA),
                  pl.BlockSpec((T, LANES), lambda s, c: (s * n_chunk + c, (COL_TAIL + TAIL_IF) // LANES)),
                  pl.BlockSpec((1, 2 * A_HEADS, T), lambda s, c: (s * n_chunk + c, 0, 0)),
                  pl.BlockSpec((1, SUBLANES, CONV_COLS), lambda s, c: (s, 0, 0)),
                  pl.BlockSpec((A_CONV, CONV_COLS), fixed),
                  pl.BlockSpec((1, CONV_COLS), fixed),
                  pl.BlockSpec((1, LANES), fixed),
                  pl.BlockSpec((2 * A_HEADS, 1), fixed),
                  pl.BlockSpec((1, A_VW), fixed),
                  pl.BlockSpec((1, A_HEADS, A_DK, A_DV), lambda s, c: (s, 0, 0, 0)),
                  pl.BlockSpec((1, A_HEADS, A_DK), lambda s, c: (s, 0, 0)),
                  pl.BlockSpec((1, 1, LANES), lambda s, c: (s, 0, 0))],
        out_specs=[pl.BlockSpec((T, A_VW), row),
                   pl.BlockSpec((1, A_HEADS, A_DK, A_DV), lambda s, c: (s, 0, 0, 0)),
                   pl.BlockSpec((1, A_HEADS, A_DK), lambda s, c: (s, 0, 0)),
                   pl.BlockSpec((1, 1, LANES), lambda s, c: (s, 0, 0))],
        out_shape=[jax.ShapeDtypeStruct((n_seq * n_chunk * T, A_VW), BF16),
                   jax.ShapeDtypeStruct((n_seq, A_HEADS, A_DK, A_DV), F32),
                   jax.ShapeDtypeStruct((n_seq, A_HEADS, A_DK), F32),
                   jax.ShapeDtypeStruct((n_seq, 1, LANES), F32)],
        scratch_shapes=[pltpu.VMEM((A_HEADS, A_DK, A_DV), F32),
                        pltpu.VMEM((A_HEADS, A_DK), F32),
                        pltpu.VMEM((1, LANES), F32),
                        pltpu.VMEM((SUBLANES, CONV_COLS), F32)],
        compiler_params=_cparams(("arbitrary", "arbitrary"), 32),
        name="mlstm_chunk",
    )(proj, proj, proj, proj, proj, gates_row, e_conv, conv_w, conv_b, bias_row, bias_col, norm_w, c0, n0, m0)


def _kcat_dot(a1, b1, a2, b2):
    if a1.shape[1] % LANES == 0:
        return _dot(jnp.concatenate([a1, a2], axis=1), _rows(b1, b2))
    return _dot(a1, b1) + _dot(a2, b2)


def _head_sums(ones_blk, T, *xs):
    sums = jnp.dot(_rows(*xs).astype(BF16), ones_blk, preferred_element_type=F32)
    return [sums[i * T:(i + 1) * T] for i in range(len(xs))]


def _rwkv_body(ur_ref, uk_ref, uv_ref, ul_ref, gb_ref, er_ref, ek_ref, ev_ref, el_ref,
               mur_ref, muk_ref, muv_ref, mul_ref, w0_ref, w2_ref, a0_ref, a2_ref, g2_ref, kkw_ref, ka_ref,
               rk_ref, lnw_ref, lnb_ref, blk_ref, blkb_ref, s0_ref,
               hb_ref, sn_ref, s_s, cr_s, ck_s, cv_s, cl_s, *, T, n_pairs):
    c = pl.program_id(2)

    @pl.when(c == 0)
    def _():
        s_s[...] = s0_ref[0]
        cr_s[...] = er_ref[0]
        ck_s[...] = ek_ref[0]
        cv_s[...] = ev_ref[0]
        cl_s[...] = el_ref[0]

    def shift_mix(u_ref, carry_ref, mu_ref, cs):
        u = u_ref[:, cs]
        prev = pltpu.roll(_rows(carry_ref[:, cs], u), 1, 0)[SUBLANES:, :]
        carry_ref[:, cs] = u[T - SUBLANES:, :]
        return u + mu_ref[:, cs] * (prev - u)

    xl = shift_mix(ul_ref, cl_s, mul_ref, slice(None))
    th_wd = jnp.tanh(xl[:, :LANES]).astype(BF16)
    x_ad = pltpu.roll(xl[:, :2 * LANES], 2 * LANES - LORA_DECAY, 1)[:, :LANES].astype(BF16)
    gd_off = LORA_DECAY + LORA_ICLR - TAIL_GD_WIN
    sg_gd = _sigmoid(pltpu.roll(xl[:, TAIL_GD_WIN:], TAIL_MIX - TAIL_GD_WIN - gd_off, 1)[:, :LORA_GATE]).astype(BF16)

    lo = lax.broadcasted_iota(jnp.int32, (1, LANES), 1) < B_HEAD
    head1 = lambda x: jnp.where(lo, x, jnp.zeros_like(x))
    head2 = lambda x: jnp.where(lo, jnp.zeros_like(x), x)
    r2 = lax.broadcasted_iota(jnp.int32, (T, 2 * T), 0)
    c2 = lax.broadcasted_iota(jnp.int32, (T, 2 * T), 1)
    left = c2 < T
    cj = jnp.where(left, c2, c2 - T)
    strict2 = cj < r2
    incl2 = cj <= r2
    eye_pair = jnp.where(cj == r2, 1.0, 0.0)
    ones_blk = blkb_ref[...]
    zero = jnp.zeros((T, LANES), BF16)
    aligned = (2 * T) % LANES == 0

    def pair_program(p):
        cs = slice(p * LANES, (p + 1) * LANES)
        r = shift_mix(ur_ref, cr_s, mur_ref, cs)
        xk = shift_mix(uk_ref, ck_s, muk_ref, cs)
        v = shift_mix(uv_ref, cv_s, muv_ref, cs)
        lw = jnp.dot(th_wd, w2_ref[:, cs], preferred_element_type=F32)
        la = jnp.dot(x_ad, a2_ref[:, cs], preferred_element_type=F32)
        lg = jnp.dot(sg_gd, g2_ref[:, cs], preferred_element_type=F32)
        kkw = xk * kkw_ref[:, cs]
        ssq, = _head_sums(ones_blk, T, kkw * kkw)
        yield
        ld = DECAY_SCALE * _sigmoid(w0_ref[:, cs] + lw)
        a = _sigmoid(a0_ref[:, cs] + la)
        g = lg * _sigmoid(gb_ref[:, cs])
        k = xk * (1.0 + (a - 1.0) * ka_ref[:, cs])
        kk = kkw * lax.rsqrt(jnp.maximum(ssq, 1e-24))
        bi = kk * a
        cum = _cumsum_rows(ld)
        ewt = jnp.exp(cum[T - 1:T, :])
        Rt = (r * jnp.exp(cum)).astype(BF16)
        At = (-kk * jnp.exp(cum - ld)).astype(BF16)
        einv = jnp.exp(-cum)
        Bt = bi * einv
        Kt = k * einv
        BhKh = _rows(Bt * ewt, Kt * ewt).astype(BF16)
        Bt = Bt.astype(BF16)
        Kt = Kt.astype(BF16)
        vb = v.astype(BF16)
        V1 = head1(vb)
        V2 = head2(vb)
        if aligned:
            G = _dot_nt(_rows(head1(At), head1(Rt), head2(At), head2(Rt)), _rows(Bt, Kt))
            G1, G2 = G[:2 * T], G[2 * T:]
        else:
            G1 = _dot_nt(_rows(head1(At), head1(Rt)), _rows(Bt, Kt))
            G2 = _dot_nt(_rows(head2(At), head2(Rt)), _rows(Kt, Bt))
        yield
        LL1 = jnp.where(strict2, G1[:T], 0.0)
        LL2 = jnp.where(strict2, G2[:T], 0.0)
        P1 = jnp.where(incl2, G1[T:], 0.0)
        P2 = jnp.where(incl2, G2[T:], 0.0)
        lab2 = pltpu.roll(LL2, T, 1) if aligned else LL2
        pw = _rows(jnp.where(left, LL1, 0.0), jnp.where(left, 0.0, lab2)).astype(BF16)
        iv = eye_pair + jnp.where(left, LL1, lab2)
        X0 = _kcat_dot(LL1, _rows(zero, V1), LL2, _rows(zero, V2) if aligned else _rows(V2, zero))
        pw = _dot(pw, pw).astype(BF16)
        yield
        span = 4
        while span < T:
            both = _dot(_rows(iv.astype(BF16), pw), pw)
            iv, pw = iv + both[:T], both[T:].astype(BF16)
            span *= 2
            yield
        iv = (iv + _dot(iv, pw)).astype(BF16)
        S = s_s[p]
        AR = _dot_nt(_rows(At, Rt), S)
        yield
        X = (AR[:T] + X0).astype(BF16)
        U = _dot(iv, _rows(head1(X), head2(X))).astype(BF16)
        yield
        if aligned:
            rhs2 = _rows(head2(U), V2)
        else:
            rhs2 = _rows(V2, head2(U))
        o = AR[T:] + _kcat_dot(P1, _rows(head1(U), V1), P2, rhs2)
        s_s[p] = S * ewt + blk_ref[...] * _dot_tn(_rows(U, vb), BhKh)
        yield
        o_sum, rk_sum = _head_sums(ones_blk, T, o, r * k * rk_ref[:, cs])
        yield
        oc = o - o_sum * (1.0 / B_HEAD)
        var, = _head_sums(ones_blk, T, oc * oc)
        yield
        on = oc * lax.rsqrt(var * (1.0 / B_HEAD) + GN_EPS) * lnw_ref[:, cs] + lnb_ref[:, cs]
        hb_ref[:, cs] = ((on + rk_sum * v) * g).astype(hb_ref.dtype)

    _run_staggered([pair_program(p) for p in range(n_pairs)], RWKV_WAVE, RWKV_LAG)

    @pl.when(c == pl.num_programs(2) - 1)
    def _():
        sn_ref[0] = s_s[...]


def _rwkv(proj, e_main, e_tail, mu_main, mu_tail, w0, w2p, a0, a2p, g2, k_k, k_a, rk, lnw, lnb, blk, s0, *,
          T, n_seq, n_chunk):
    assert T <= RWKV_CHUNK
    gw = GROUP_W
    wide = lambda col: pl.BlockSpec((T, gw), lambda s, pg, c: (s * n_chunk + c, col // gw + pg))
    edge = lambda col: pl.BlockSpec((1, SUBLANES, gw), lambda s, pg, c: (s, 0, col // gw + pg))
    vec = lambda col: pl.BlockSpec((1, gw), lambda s, pg, c: (0, col // gw + pg))
    mat = lambda rows: pl.BlockSpec((rows, gw), lambda s, pg, c: (0, pg))
    st = pl.BlockSpec((1, GROUP_PAIRS, LANES, LANES), lambda s, pg, c: (s, pg, 0, 0))
    return pl.pallas_call(
        functools.partial(_rwkv_body, T=T, n_pairs=GROUP_PAIRS),
        grid=(n_seq, N_PAIRS // GROUP_PAIRS, n_chunk),
        in_specs=[wide(COL_R), wide(COL_K), wide(COL_VB),
                  pl.BlockSpec((T, TAIL_MIX), lambda s, pg, c: (s * n_chunk + c, COL_TAIL // TAIL_MIX)),
                  wide(COL_GB),
                  edge(0), edge(B_W), edge(2 * B_W),
                  pl.BlockSpec((1, SUBLANES, TAIL_MIX), lambda s, pg, c: (s, 0, 0)),
                  vec(0), vec(B_W), vec(2 * B_W),
                  pl.BlockSpec((1, TAIL_MIX), lambda s, pg, c: (0, 0)),
                  vec(0), mat(LANES), vec(0), mat(LANES), mat(LORA_GATE), vec(0), vec(0),
                  vec(0), vec(0), vec(0),
                  pl.BlockSpec((LANES, LANES), lambda s, pg, c: (0, 0)),
                  pl.BlockSpec((LANES, LANES), lambda s, pg, c: (0, 0)), st],
        out_specs=[pl.BlockSpec((T, gw), lambda s, pg, c: (s * n_chunk + c, pg)), st],
        out_shape=[jax.ShapeDtypeStruct((n_seq * n_chunk * T, B_W), BF16),
                   jax.ShapeDtypeStruct((n_seq, N_PAIRS, LANES, LANES), F32)],
        scratch_shapes=[pltpu.VMEM((GROUP_PAIRS, LANES, LANES), F32),
                        pltpu.VMEM((SUBLANES, gw), F32), pltpu.VMEM((SUBLANES, gw), F32),
                        pltpu.VMEM((SUBLANES, gw), F32), pltpu.VMEM((SUBLANES, TAIL_MIX), F32)],
        compiler_params=_cparams(("arbitrary", "arbitrary", "arbitrary"), 32),
        name="rwkv_chunk",
    )(proj, proj, proj, proj, proj, e_main, e_main, e_main, e_tail, mu_main, mu_main, mu_main, mu_tail,
      w0, w2p, a0, a2p, g2, k_k, k_a, rk, lnw, lnb, blk, blk.astype(BF16), s0)


def _merge_out_body(ha_ref, hb_ref, x_ref, w_ref, g_ref, y_ref):
    mixed = ha_ref[...].astype(F32) + hb_ref[...].astype(F32)
    y = jnp.dot(mixed.astype(BF16), w_ref[...], preferred_element_type=F32)
    ms = jnp.mean(y * y, axis=-1, keepdims=True)
    y_ref[...] = x_ref[...] + y * lax.rsqrt(ms + EPS) * g_ref[...]


def _merge_out(h_a, h_b, x, w_out, g_post):
    n = x.shape[0]
    wide = pl.BlockSpec((MERGE_TM, D_MODEL), lambda i: (i, 0))
    return pl.pallas_call(
        _merge_out_body,
        grid=(n // MERGE_TM,),
        in_specs=[wide, wide, wide,
                  pl.BlockSpec((D_MODEL, D_MODEL), lambda i: (0, 0)),
                  pl.BlockSpec((1, D_MODEL), lambda i: (0, 0))],
        out_specs=wide,
        out_shape=jax.ShapeDtypeStruct((n, D_MODEL), F32),
        compiler_params=_cparams(("parallel",), 48),
        name="merge_out_proj",
    )(h_a, h_b, x, w_out, g_post)


def _mlp_body(x_ref, g1_ref, wu_ref, wd_ref, g2_ref, y_ref, hn_ref, acc_ref, *, row_chunk):
    j = pl.program_id(1)

    @pl.when(j == 0)
    def _():
        _norm_rows(x_ref, g1_ref, hn_ref, row_chunk)
        acc_ref[...] = jnp.zeros_like(acc_ref)

    u = jnp.dot(hn_ref[...], wu_ref[...], preferred_element_type=F32)
    a = jnp.square(jnp.maximum(u, 0.0))
    acc_ref[...] += jnp.dot(a.astype(BF16), wd_ref[...], preferred_element_type=F32)

    @pl.when(j == pl.num_programs(1) - 1)
    def _():
        def body(c, carry):
            rows = pl.ds(pl.multiple_of(c * row_chunk, row_chunk), row_chunk)
            ff = acc_ref[rows, :]
            ms = jnp.mean(ff * ff, axis=-1, keepdims=True)
            y_ref[rows, :] = x_ref[rows, :] + ff * lax.rsqrt(ms + EPS) * g2_ref[...]
            return carry
        lax.fori_loop(0, x_ref.shape[0] // row_chunk, body, 0)


def _mlp(x, g_pre, w_up, w_down, g_post):
    n, d = x.shape
    f = w_up.shape[1]
    return pl.pallas_call(
        functools.partial(_mlp_body, row_chunk=LANES),
        grid=(n // MLP_TM, f // MLP_TF),
        in_specs=[pl.BlockSpec((MLP_TM, d), lambda i, j: (i, 0)),
                  pl.BlockSpec((1, d), lambda i, j: (0, 0)),
                  pl.BlockSpec((d, MLP_TF), lambda i, j: (0, j)),
                  pl.BlockSpec((MLP_TF, d), lambda i, j: (j, 0)),
                  pl.BlockSpec((1, d), lambda i, j: (0, 0))],
        out_specs=pl.BlockSpec((MLP_TM, d), lambda i, j: (i, 0)),
        out_shape=jax.ShapeDtypeStruct((n, d), F32),
        scratch_shapes=[pltpu.VMEM((MLP_TM, d), BF16), pltpu.VMEM((MLP_TM, d), F32)],
        compiler_params=_cparams(("parallel", "arbitrary"), 56),
        name="mlp",
    )(x, g_pre, w_up, w_down, g_post)


def _pad_cols(a, width):
    return jnp.pad(a, [(0, 0)] * (a.ndim - 1) + [(0, width - a.shape[-1])])


def _pad_rows(a, before, total):
    return jnp.pad(a, ((before, total - before - a.shape[0]), (0, 0)))


def _edge_blocks(prev_rows):
    return jnp.pad(prev_rows, ((0, 0), (SUBLANES - prev_rows.shape[1], 0), (0, 0)))


def _pack_state(s):
    n = s.shape[0]
    s = s.reshape(n, N_PAIRS, 2, B_HEAD, B_HEAD)
    z = jnp.zeros((n, N_PAIRS, B_HEAD, B_HEAD), s.dtype)
    top = jnp.concatenate([s[:, :, 0], z], axis=-1)
    bot = jnp.concatenate([z, s[:, :, 1]], axis=-1)
    return jnp.concatenate([top, bot], axis=-2)


def _unpack_state(sp):
    n = sp.shape[0]
    a = sp[:, :, :B_HEAD, :B_HEAD]
    b = sp[:, :, B_HEAD:, B_HEAD:]
    return jnp.stack([a, b], axis=2).reshape(n, B_HEADS, B_HEAD, B_HEAD)


def _gates_row_form(proj, T):
    g = proj[:, COL_TAIL + TAIL_IF:COL_TAIL + TAIL_IF + 2 * A_HEADS]
    return jnp.swapaxes(g.reshape(-1, T, 2 * A_HEADS), 1, 2)


def kernel(x_prompt, x_sample, state_mlstm_C, state_mlstm_n, state_mlstm_m, state_mlstm_conv, state_rwkv_S, state_rwkv_shift, meta_tokens, norm_mix_pre, norm_mix_post, norm_mlp_pre, norm_mlp_post, w_in, mlstm_conv_w, mlstm_conv_b, mlstm_b_i, mlstm_b_f, mlstm_norm_w, rwkv_mu, rwkv_w0, rwkv_w2, rwkv_a0, rwkv_a2, rwkv_g2, rwkv_k_k, rwkv_k_a, rwkv_r_k, rwkv_ln_w, rwkv_ln_b, w_out, w_up, w_down):
    n_s_seq, s_len, _ = x_sample.shape
    _, p_len, _ = x_prompt.shape
    k3 = A_CONV - 1

    w_all = _relayout_weight(jnp.swapaxes(w_in[0], 0, 1))
    mu = rwkv_mu[0][None, :]
    mu_main = mu[:, :3 * B_W]
    mu_tail = _pad_cols(mu[:, 3 * B_W:], TAIL_MIX)
    w2p = _pad_rows(rwkv_w2[0], 0, LANES).astype(BF16)
    a2p = _pad_rows(rwkv_a2[0], 0, LANES).astype(BF16)
    g2 = rwkv_g2[0].astype(BF16)
    g_pre = norm_mix_pre[0][None, :]
    bias = jnp.concatenate([mlstm_b_i[0], mlstm_b_f[0]])
    bias_row = _pad_cols(bias[None, :], LANES)
    bias_col = bias[:, None]
    norm_w = mlstm_norm_w[0][None, :]
    conv_w = mlstm_conv_w[0]
    conv_b = mlstm_conv_b[0][None, :]
    rwkv_vecs = (rwkv_w0[0][None, :], w2p, rwkv_a0[0][None, :], a2p, g2, rwkv_k_k[0][None, :],
                 rwkv_k_a[0][None, :], rwkv_r_k[0].reshape(1, B_W), rwkv_ln_w[0][None, :], rwkv_ln_b[0][None, :],
                 _head_block_mask())
    w_out_b = w_out[0].astype(BF16)
    w_up_b = w_up[0].astype(BF16)
    w_down_b = w_down[0].astype(BF16)

    def branches(x_rows, tm, T, n_seq, n_chunk, e_conv, e_main, e_tail, c0, n0, m0, s0):
        proj = _norm_matmul(x_rows, g_pre, w_all, tm)
        h_a, c_n, n_n, m_n = _mlstm(proj, _gates_row_form(proj, T), e_conv, conv_w, conv_b, bias_row, bias_col,
                                    norm_w, c0, n0, m0, T=T, n_seq=n_seq, n_chunk=n_chunk)
        h_b, s_n = _rwkv(proj, e_main, e_tail, mu_main, mu_tail, *rwkv_vecs, s0,
                         T=T, n_seq=n_seq, n_chunk=n_chunk)
        return proj, h_a, h_b, (c_n, n_n, m_n, s_n)

    def finish(h_a, h_b, x_rows):
        x1 = _merge_out(h_a, h_b, x_rows, w_out_b, norm_mix_post[0][None, :])
        return _mlp(x1, norm_mlp_pre[0][None, :], w_up_b, w_down_b, norm_mlp_post[0][None, :])

    zeros = lambda *shape: jnp.zeros(shape, F32)
    proj_m, _, _, (c_m, n_m, m_m, s_m) = branches(
        meta_tokens.astype(F32), N_META, N_META, 1, 1,
        zeros(1, SUBLANES, CONV_COLS), zeros(1, SUBLANES, 3 * B_W), zeros(1, SUBLANES, TAIL_MIX),
        zeros(1, A_HEADS, A_DK, A_DV), zeros(1, A_HEADS, A_DK), zeros(1, 1, LANES),
        zeros(1, N_PAIRS, LANES, LANES))

    xp = x_prompt[0].astype(F32)
    proj_p, ha_p, hb_p, (c_p, n_p, m_p, s_p) = branches(
        xp, PROJ_TM, MLSTM_CHUNK, 1, p_len // MLSTM_CHUNK,
        _edge_blocks(proj_m[None, N_META - k3:, COL_CONV:COL_CONV + CONV_COLS]),
        _edge_blocks(proj_m[None, N_META - 1:, COL_R:COL_R + 3 * B_W]),
        _edge_blocks(proj_m[None, N_META - 1:, COL_TAIL:COL_TAIL + TAIL_MIX]),
        c_m, n_m, m_m, s_m)
    y_prompt = finish(ha_p, hb_p, xp)[None].astype(x_prompt.dtype)

    xs = x_sample.reshape(n_s_seq * s_len, D_MODEL).astype(F32)
    sh = state_rwkv_shift[0].astype(F32)
    m0_s = jnp.pad(state_mlstm_m[0].astype(F32), ((0, 0), (A_HEADS, LANES - 2 * A_HEADS)))[:, None, :]
    proj_s, ha_s, hb_s, (c_s, n_s, m_s, s_s) = branches(
        xs, PROJ_TM, s_len, n_s_seq, 1,
        _edge_blocks(state_mlstm_conv[0].astype(F32)),
        _edge_blocks(sh[:, :, :3 * B_W]), _edge_blocks(_pad_cols(sh[:, :, 3 * B_W:], TAIL_MIX)),
        state_mlstm_C[0].astype(F32), state_mlstm_n[0].astype(F32), m0_s,
        _pack_state(state_rwkv_S[0].astype(F32)))
    y_sample = finish(ha_s, hb_s, xs).reshape(x_sample.shape).astype(x_sample.dtype)

    def shift_state(rows):
        return jnp.concatenate([rows[..., COL_R:COL_R + 3 * B_W],
                                rows[..., COL_TAIL:COL_TAIL + LORA_COLS]], axis=-1)

    last_s = proj_s.reshape(n_s_seq, s_len, PROJ_COLS)[:, s_len - k3:, :]
    conv_s = last_s[..., COL_CONV:COL_CONV + CONV_COLS]
    shift_s = shift_state(last_s[:, k3 - 1:, :])
    last_p = proj_p[p_len - k3:, :]
    conv_p = last_p[None, :, COL_CONV:COL_CONV + CONV_COLS]
    shift_p = shift_state(last_p[None, k3 - 1:, :])
    dt_c, dt_n, dt_m = state_mlstm_C.dtype, state_mlstm_n.dtype, state_mlstm_m.dtype
    dt_cv, dt_s, dt_sh = state_mlstm_conv.dtype, state_rwkv_S.dtype, state_rwkv_shift.dtype
    lanes_m = slice(A_HEADS, 2 * A_HEADS)
    return (y_prompt, y_sample,
            c_p[None].astype(dt_c), n_p[None].astype(dt_n), m_p[:, 0, lanes_m][None].astype(dt_m),
            conv_p[None].astype(dt_cv), _unpack_state(s_p)[None].astype(dt_s), shift_p[None].astype(dt_sh),
            c_s[None].astype(dt_c), n_s[None].astype(dt_n), m_s[:, 0, lanes_m][None].astype(dt_m),
            conv_s[None].astype(dt_cv), _unpack_state(s_s)[None].astype(dt_s), shift_s[None].astype(dt_sh))
```

```python
import functools

import jax
import jax.numpy as jnp
from jax import lax
from jax.experimental import pallas as pl
from jax.experimental.pallas import tpu as pltpu

F32 = jnp.float32
BF16 = jnp.bfloat16

D_MODEL = 2048
N_META = 16
EPS = 1e-6
D_FF = 4 * D_MODEL
A_HEADS = 8
A_DK = 128
A_DV = D_MODEL // A_HEADS
A_QK = A_HEADS * A_DK
A_VW = A_HEADS * A_DV
A_CONV = 4
MLSTM_CHUNK = 64
B_HEAD = 64
B_HEADS = D_MODEL // B_HEAD
B_W = B_HEADS * B_HEAD
LORA_DECAY = max(32, int(round(1.8 * D_MODEL ** 0.5 / 32)) * 32)
LORA_ICLR = max(32, int(round(1.8 * D_MODEL ** 0.5 / 32)) * 32)
LORA_GATE = max(32, int(round(0.6 * D_MODEL ** 0.8 / 32)) * 32)
GN_EPS = 64e-5
CONV_COLS = 2 * A_QK
SHIFT_COLS = 3 * B_W + LORA_DECAY + LORA_ICLR + LORA_GATE

LANES = 128
SUBLANES = 8

COL_CONV, COL_V, COL_O, COL_R, COL_K, COL_VB, COL_GA, COL_GB = (i * D_MODEL for i in range(8))
COL_TAIL = 8 * D_MODEL
LORA_COLS = LORA_DECAY + LORA_ICLR + LORA_GATE
TAIL_MIX = -(-LORA_COLS // LANES) * LANES
TAIL_IF = TAIL_MIX
TAIL_GD_WIN = (LORA_DECAY + LORA_ICLR) // LANES * LANES
PROJ_TN = 1024
PROJ_COLS = -(-(COL_TAIL + TAIL_IF + LANES) // PROJ_TN) * PROJ_TN
PROJ_TM = 1024
MERGE_TM = 256
MLP_TM, MLP_TF = 512, 1024
DECAY_SCALE = -0.6065306597126334
RWKV_CHUNK = 64
N_PAIRS = B_HEADS // 2
GROUP_PAIRS = 16
RWKV_WAVE, RWKV_LAG = 8, 2
GROUP_W = GROUP_PAIRS * LANES


def _cparams(semantics, vmem_mib):
    return pltpu.CompilerParams(dimension_semantics=semantics, vmem_limit_bytes=vmem_mib << 20)


def _softplus(x):
    return jnp.maximum(x, 0.0) + jnp.log1p(jnp.exp(-jnp.abs(x)))


def _dot(a, b):
    return jnp.dot(a.astype(BF16), b.astype(BF16), preferred_element_type=F32)


def _dot_nt(a, b):
    return lax.dot_general(a.astype(BF16), b.astype(BF16), (((1,), (1,)), ((), ())),
                           preferred_element_type=F32)


def _dot_tn(a, b):
    return lax.dot_general(a.astype(BF16), b.astype(BF16), (((0,), (0,)), ((), ())),
                           preferred_element_type=F32)


def _each(f, *lists):
    return [f(*args) for args in zip(*lists)]


def _run_staggered(programs, wave, lag):
    live = list(enumerate(programs))
    slot = 0
    while live:
        still = []
        for p, prog in live:
            if slot >= (p // wave) * lag:
                try:
                    next(prog)
                except StopIteration:
                    continue
            still.append((p, prog))
        live = still
        slot += 1


def _rows(*parts):
    return jnp.concatenate(parts, axis=0)


def _split_bf16(x, pieces):
    out = []
    for _ in range(pieces - 1):
        p = x.astype(BF16)
        out.append(p)
        x = x - p.astype(F32)
    out.append(x.astype(BF16))
    return out


def _dot_exact_rhs(x, m, pieces=3):
    mb = m.astype(BF16)
    acc = None
    for p in _split_bf16(x, pieces):
        t = jnp.dot(p, mb, preferred_element_type=F32)
        acc = t if acc is None else acc + t
    return acc


def _sigmoid(x):
    return 0.5 * jnp.tanh(0.5 * x) + 0.5


def _cumsum_rows(x):
    n = x.shape[0]
    row = lax.broadcasted_iota(jnp.int32, x.shape, 0)
    step = 1
    while step < n:
        x = x + jnp.where(row >= step, pltpu.roll(x, step, 0), 0.0)
        step *= 2
    return x


def _head_block_mask():
    a = lax.broadcasted_iota(jnp.int32, (LANES, LANES), 0) // B_HEAD
    b = lax.broadcasted_iota(jnp.int32, (LANES, LANES), 1) // B_HEAD
    return jnp.where(a == b, 1.0, 0.0).astype(F32)


SRC_IF = CONV_COLS + 2 * A_VW
SRC_SHIFT = SRC_IF + 2 * A_HEADS
SRC_GATE = SRC_SHIFT + SHIFT_COLS
SRC_COLS = SRC_GATE + 2 * D_MODEL
RELAYOUT_ROWS = TAIL_MIX


def _relayout_src_row(t):
    unit = 2 * SUBLANES
    dst = t * (RELAYOUT_ROWS // unit)
    src = jnp.where(dst < COL_R // unit, dst,
                    jnp.where(dst < COL_GA // unit, dst + (SRC_SHIFT - COL_R) // unit,
                              jnp.where(dst < COL_TAIL // unit, dst + (SRC_GATE - COL_GA) // unit,
                                        jnp.where(dst < (COL_TAIL + TAIL_MIX) // unit,
                                                  (SRC_SHIFT + 3 * B_W) // unit, SRC_IF // unit))))
    return src * unit


def _relayout_body(w_ref, o_ref):
    o_ref[...] = w_ref[...].astype(BF16)


def _relayout_weight(wt):
    d = wt.shape[1]
    assert wt.shape[0] == SRC_COLS and PROJ_COLS == COL_TAIL + 2 * RELAYOUT_ROWS and COL_R == SRC_IF
    assert SRC_SHIFT + 3 * B_W + RELAYOUT_ROWS <= SRC_COLS and SRC_SHIFT % (2 * SUBLANES) == 0
    return pl.pallas_call(
        _relayout_body,
        grid=(PROJ_COLS // RELAYOUT_ROWS,),
        in_specs=[pl.BlockSpec((pl.Element(RELAYOUT_ROWS), pl.Element(d)), lambda t: (_relayout_src_row(t), 0))],
        out_specs=pl.BlockSpec((RELAYOUT_ROWS, d), lambda t: (t, 0)),
        out_shape=jax.ShapeDtypeStruct((PROJ_COLS, d), BF16),
        compiler_params=_cparams(("parallel",), 32),
        name="relayout_w_in",
    )(wt)


def _norm_rows(x_ref, g_ref, out_ref, row_chunk):
    def body(c, carry):
        rows = pl.ds(pl.multiple_of(c * row_chunk, row_chunk), row_chunk)
        x = x_ref[rows, :]
        ms = jnp.mean(x * x, axis=-1, keepdims=True)
        out_ref[rows, :] = (x * lax.rsqrt(ms + EPS) * g_ref[...]).astype(out_ref.dtype)
        return carry
    lax.fori_loop(0, x_ref.shape[0] // row_chunk, body, 0)


def _norm_matmul_body(x_ref, g_ref, w_ref, o_ref, hn_ref, *, row_chunk):
    @pl.when(pl.program_id(1) == 0)
    def _():
        _norm_rows(x_ref, g_ref, hn_ref, row_chunk)

    o_ref[...] = lax.dot_general(hn_ref[...], w_ref[...], (((1,), (1,)), ((), ())), preferred_element_type=F32)


def _norm_matmul(x, g, w, tm):
    n, d = x.shape
    p = w.shape[0]
    return pl.pallas_call(
        functools.partial(_norm_matmul_body, row_chunk=min(tm, LANES)),
        grid=(n // tm, p // PROJ_TN),
        in_specs=[pl.BlockSpec((tm, d), lambda i, j: (i, 0)),
                  pl.BlockSpec((1, d), lambda i, j: (0, 0)),
                  pl.BlockSpec((PROJ_TN, d), lambda i, j: (j, 0))],
        out_specs=pl.BlockSpec((tm, PROJ_TN), lambda i, j: (i, j)),
        out_shape=jax.ShapeDtypeStruct((n, p), F32),
        scratch_shapes=[pltpu.VMEM((tm, d), BF16)],
        compiler_params=_cparams(("parallel", "arbitrary"), 48),
        name="in_proj",
    )(x, g, w)


def _mlstm_body(u_ref, v_ref, o_ref, ga_ref, gc_ref, gr_ref, e_ref, cw_ref, cb_ref, brow_ref, bcol_ref,
                nw_ref, c0_ref, n0_ref, m0_ref,
                h_ref, cn_ref, nn_ref, mn_ref, c_s, n_s, m_s, e_s, *, T):
    c = pl.program_id(1)

    @pl.when(c == 0)
    def _():
        c_s[...] = c0_ref[0]
        n_s[...] = n0_ref[0]
        m_s[...] = m0_ref[0]
        e_s[...] = e_ref[0]

    u = u_ref[...]
    ext = _rows(e_s[...], u)
    w = cw_ref[...]
    acc = cb_ref[...] + u * w[A_CONV - 1:A_CONV, :]
    for sh in range(1, A_CONV):
        acc = acc + pltpu.roll(ext, sh, 0)[SUBLANES:, :] * w[A_CONV - 1 - sh:A_CONV - sh, :]
    e_s[...] = u[T - SUBLANES:, :]
    qk_act = acc * _sigmoid(acc)

    gcol = gc_ref[...] + brow_ref[...]
    grow = gr_ref[0] + bcol_ref[...]
    lf_col = -_softplus(-gcol)
    ig_row = grow[:A_HEADS, :]
    lf_row = -_softplus(-grow[A_HEADS:, :])
    ri = lax.broadcasted_iota(jnp.int32, (T, T), 0)
    ci = lax.broadcasted_iota(jnp.int32, (T, T), 1)
    tri = ci <= ri
    b_col = _cumsum_rows(lf_col)
    b_row = _dot_exact_rhs(lf_row, jnp.where(ri <= ci, 1.0, 0.0))
    m_prev = m_s[...]
    lane = lax.broadcasted_iota(jnp.int32, (1, LANES), 1)
    heads = list(range(A_HEADS))
    bc = [b_col[:, A_HEADS + h:A_HEADS + h + 1] for h in heads]
    mh = [m_prev[:, A_HEADS + h:A_HEADS + h + 1] for h in heads]
    br = [b_row[h:h + 1, :] for h in heads]
    igr = [ig_row[h:h + 1, :] for h in heads]
    igc = [gcol[:, h:h + 1] for h in heads]
    qh = [qk_act[:, h * A_DK:(h + 1) * A_DK] for h in heads]
    kh = [qk_act[:, A_QK + h * A_DK:A_QK + (h + 1) * A_DK] * (A_DK ** -0.5) for h in heads]
    vh = [v_ref[:, h * A_DV:(h + 1) * A_DV] for h in heads]
    ch = [c_s[h] for h in heads]
    nh = [n_s[h:h + 1, :] for h in heads]
    qk = _each(_dot_nt, qh, kh)
    qc = _each(_dot, qh, ch)
    d = _each(lambda a, b, i: jnp.where(tri, a - b + i, -jnp.inf), bc, br, igr)
    inter = _each(jnp.add, bc, mh)
    m_t = _each(lambda i, x: jnp.maximum(i, jnp.max(x, axis=-1, keepdims=True)), inter, d)
    s = _each(lambda a, x, m: a * jnp.exp(x - m), qk, d, m_t)
    e_in = _each(lambda i, m: jnp.exp(i - m), inter, m_t)
    sv = _each(_dot, s, vh)
    num = _each(lambda a, x, b: a + x * b, sv, e_in, qc)
    den = _each(lambda a, x, qq, nn: jnp.sum(a, axis=-1, keepdims=True)
                + x * jnp.sum(qq * nn, axis=-1, keepdims=True), s, e_in, qh, nh)
    hh = _each(lambda a, b, m: a / jnp.maximum(jnp.abs(b), jnp.exp(-m)), num, den, m_t)
    hh = _each(lambda a: a * lax.rsqrt(jnp.mean(a * a, axis=-1, keepdims=True) + EPS), hh)
    b_end = _each(lambda a: a[T - 1:T, :], bc)
    m_new = _each(lambda be, m, b, i: jnp.maximum(be + m, jnp.max(be - b + i, axis=-1, keepdims=True)),
                  b_end, mh, br, igr)
    carry = _each(lambda be, m, mn: jnp.exp(be + m - mn), b_end, mh, m_new)
    wk = _each(lambda kk_, be, a, i, mn: kk_ * jnp.exp(be - a + i - mn), kh, b_end, bc, igc, m_new)
    kv = _each(_dot_tn, wk, vh)
    m_next = m_prev
    for h in heads:
        cols = slice(h * A_DV, (h + 1) * A_DV)
        gate = _sigmoid(o_ref[:, cols]) * _sigmoid(ga_ref[:, cols])
        h_ref[:, cols] = (hh[h] * nw_ref[:, cols] * gate).astype(h_ref.dtype)
        c_s[h] = carry[h] * ch[h] + kv[h]
        n_s[h:h + 1, :] = carry[h] * nh[h] + jnp.sum(wk[h], axis=0, keepdims=True)
        m_next = jnp.where(lane == A_HEADS + h, m_new[h], m_next)
    m_s[...] = m_next

    @pl.when(c == pl.num_programs(1) - 1)
    def _():
        cn_ref[0] = c_s[...]
        nn_ref[0] = n_s[...]
        mn_ref[0] = m_s[...]


def _mlstm(proj, gates_row, e_conv, conv_w, conv_b, bias_row, bias_col, norm_w, c0, n0, m0, *, T, n_seq, n_chunk):
    row = lambda s, c: (s * n_chunk + c, 0)
    fixed = lambda s, c: (0, 0)
    wide = lambda col: pl.BlockSpec((T, D_MODEL), lambda s, c: (s * n_chunk + c, col // D_MODEL))
    return pl.pallas_call(
        functools.partial(_mlstm_body, T=T),
        grid=(n_seq, n_chunk),
        in_specs=[wide(COL_CONV), wide(COL_V), wide(COL_O), wide(COL_GA),
                  pl.BlockSpec((T, LANES), lambda s, c: (s * n_chunk + c, (COL_TAIL + TAIL_IF) // LANES)),
                  pl.BlockSpec((1, 2 * A_HEADS, T), lambda s, c: (s * n_chunk + c, 0, 0)),
                  pl.BlockSpec((1, SUBLANES, CONV_COLS), lambda s, c: (s, 0, 0)),
                  pl.BlockSpec((A_CONV, CONV_COLS), fixed),
                  pl.BlockSpec((1, CONV_COLS), fixed),
                  pl.BlockSpec((1, LANES), fixed),
                  pl.BlockSpec((2 * A_HEADS, 1), fixed),
                  pl.BlockSpec((1, A_VW), fixed),
                  pl.BlockSpec((1, A_HEADS, A_DK, A_DV), lambda s, c: (s, 0, 0, 0)),
                  pl.BlockSpec((1, A_HEADS, A_DK), lambda s, c: (s, 0, 0)),
                  pl.BlockSpec((1, 1, LANES), lambda s, c: (s, 0, 0))],
        out_specs=[pl.BlockSpec((T, A_VW), row),
                   pl.BlockSpec((1, A_HEADS, A_DK, A_DV), lambda s, c: (s, 0, 0, 0)),
                   pl.BlockSpec((1, A_HEADS, A_DK), lambda s, c: (s, 0, 0)),
                   pl.BlockSpec((1, 1, LANES), lambda s, c: (s, 0, 0))],
        out_shape=[jax.ShapeDtypeStruct((n_seq * n_chunk * T, A_VW), BF16),
                   jax.ShapeDtypeStruct((n_seq, A_HEADS, A_DK, A_DV), F32),
                   jax.ShapeDtypeStruct((n_seq, A_HEADS, A_DK), F32),
                   jax.ShapeDtypeStruct((n_seq, 1, LANES), F32)],
        scratch_shapes=[pltpu.VMEM((A_HEADS, A_DK, A_DV), F32),
                        pltpu.VMEM((A_HEADS, A_DK), F32),
                        pltpu.VMEM((1, LANES), F32),
                        pltpu.VMEM((SUBLANES, CONV_COLS), F32)],
        compiler_params=_cparams(("arbitrary", "arbitrary"), 32),
        name="mlstm_chunk",
    )(proj, proj, proj, proj, proj, gates_row, e_conv, conv_w, conv_b, bias_row, bias_col, norm_w, c0, n0, m0)


def _kcat_dot(a1, b1, a2, b2):
    if a1.shape[1] % LANES == 0:
        return _dot(jnp.concatenate([a1, a2], axis=1), _rows(b1, b2))
    return _dot(a1, b1) + _dot(a2, b2)


def _head_sums(ones_blk, T, *xs):
    sums = jnp.dot(_rows(*xs).astype(BF16), ones_blk, preferred_element_type=F32)
    return [sums[i * T:(i + 1) * T] for i in range(len(xs))]


def _rwkv_body(ur_ref, uk_ref, uv_ref, ul_ref, gb_ref, er_ref, ek_ref, ev_ref, el_ref,
               mur_ref, muk_ref, muv_ref, mul_ref, w0_ref, w2_ref, a0_ref, a2_ref, g2_ref, kkw_ref, ka_ref,
               rk_ref, lnw_ref, lnb_ref, blk_ref, blkb_ref, s0_ref,
               hb_ref, sn_ref, s_s, cr_s, ck_s, cv_s, cl_s, *, T, n_pairs):
    c = pl.program_id(2)

    @pl.when(c == 0)
    def _():
        z = jnp.zeros((B_HEAD, B_HEAD), F32)
        for p in range(n_pairs):
            s_s[p] = _rows(jnp.concatenate([s0_ref[0, 2 * p], z], axis=1),
                           jnp.concatenate([z, s0_ref[0, 2 * p + 1]], axis=1))
        cr_s[...] = er_ref[0]
        ck_s[...] = ek_ref[0]
        cv_s[...] = ev_ref[0]
        cl_s[...] = el_ref[0]

    def shift_mix(u_ref, carry_ref, mu_ref, cs):
        u = u_ref[:, cs]
        prev = pltpu.roll(_rows(carry_ref[:, cs], u), 1, 0)[SUBLANES:, :]
        carry_ref[:, cs] = u[T - SUBLANES:, :]
        return u + mu_ref[:, cs] * (prev - u)

    xl = shift_mix(ul_ref, cl_s, mul_ref, slice(None))
    th_wd = jnp.tanh(xl[:, :LANES]).astype(BF16)
    x_ad = pltpu.roll(xl[:, :2 * LANES], 2 * LANES - LORA_DECAY, 1)[:, :LANES].astype(BF16)
    gd_off = LORA_DECAY + LORA_ICLR - TAIL_GD_WIN
    sg_gd = _sigmoid(pltpu.roll(xl[:, TAIL_GD_WIN:], TAIL_MIX - TAIL_GD_WIN - gd_off, 1)[:, :LORA_GATE]).astype(BF16)

    lo = lax.broadcasted_iota(jnp.int32, (1, LANES), 1) < B_HEAD
    head1 = lambda x: jnp.where(lo, x, jnp.zeros_like(x))
    head2 = lambda x: jnp.where(lo, jnp.zeros_like(x), x)
    r2 = lax.broadcasted_iota(jnp.int32, (T, 2 * T), 0)
    c2 = lax.broadcasted_iota(jnp.int32, (T, 2 * T), 1)
    left = c2 < T
    cj = jnp.where(left, c2, c2 - T)
    strict2 = cj < r2
    incl2 = cj <= r2
    eye_pair = jnp.where(cj == r2, 1.0, 0.0)
    ones_blk = blkb_ref[...]
    zero = jnp.zeros((T, LANES), BF16)
    aligned = (2 * T) % LANES == 0

    def pair_program(p):
        cs = slice(p * LANES, (p + 1) * LANES)
        r = shift_mix(ur_ref, cr_s, mur_ref, cs)
        xk = shift_mix(uk_ref, ck_s, muk_ref, cs)
        v = shift_mix(uv_ref, cv_s, muv_ref, cs)
        lw = jnp.dot(th_wd, w2_ref[:, cs], preferred_element_type=F32)
        la = jnp.dot(x_ad, a2_ref[:, cs], preferred_element_type=F32)
        lg = jnp.dot(sg_gd, g2_ref[:, cs], preferred_element_type=F32)
        kkw = xk * kkw_ref[:, cs]
        ssq, = _head_sums(ones_blk, T, kkw * kkw)
        yield
        ld = DECAY_SCALE * _sigmoid(w0_ref[:, cs] + lw)
        a = _sigmoid(a0_ref[:, cs] + la)
        g = lg * _sigmoid(gb_ref[:, cs])
        k = xk * (1.0 + (a - 1.0) * ka_ref[:, cs])
        kk = kkw * lax.rsqrt(jnp.maximum(ssq, 1e-24))
        bi = kk * a
        cum = _cumsum_rows(ld)
        ewt = jnp.exp(cum[T - 1:T, :])
        Rt = (r * jnp.exp(cum)).astype(BF16)
        At = (-kk * jnp.exp(cum - ld)).astype(BF16)
        einv = jnp.exp(-cum)
        Bt = bi * einv
        Kt = k * einv
        BhKh = _rows(Bt * ewt, Kt * ewt).astype(BF16)
        Bt = Bt.astype(BF16)
        Kt = Kt.astype(BF16)
        vb = v.astype(BF16)
        V1 = head1(vb)
        V2 = head2(vb)
        if aligned:
            G = _dot_nt(_rows(head1(At), head1(Rt), head2(At), head2(Rt)), _rows(Bt, Kt))
            G1, G2 = G[:2 * T], G[2 * T:]
        else:
            G1 = _dot_nt(_rows(head1(At), head1(Rt)), _rows(Bt, Kt))
            G2 = _dot_nt(_rows(head2(At), head2(Rt)), _rows(Kt, Bt))
        yield
        LL1 = jnp.where(strict2, G1[:T], 0.0)
        LL2 = jnp.where(strict2, G2[:T], 0.0)
        P1 = jnp.where(incl2, G1[T:], 0.0)
        P2 = jnp.where(incl2, G2[T:], 0.0)
        lab2 = pltpu.roll(LL2, T, 1) if aligned else LL2
        pw = _rows(jnp.where(left, LL1, 0.0), jnp.where(left, 0.0, lab2)).astype(BF16)
        iv = eye_pair + jnp.where(left, LL1, lab2)
        X0 = _kcat_dot(LL1, _rows(zero, V1), LL2, _rows(zero, V2) if aligned else _rows(V2, zero))
        pw = _dot(pw, pw).astype(BF16)
        yield
        span = 4
        while span < T:
            both = _dot(_rows(iv.astype(BF16), pw), pw)
            iv, pw = iv + both[:T], both[T:].astype(BF16)
            span *= 2
            yield
        iv = (iv + _dot(iv, pw)).astype(BF16)
        S = s_s[p]
        AR = _dot_nt(_rows(At, Rt), S)
        yield
        X = (AR[:T] + X0).astype(BF16)
        U = _dot(iv, _rows(head1(X), head2(X))).astype(BF16)
        yield
        if aligned:
            rhs2 = _rows(head2(U), V2)
        else:
            rhs2 = _rows(V2, head2(U))
        o = AR[T:] + _kcat_dot(P1, _rows(head1(U), V1), P2, rhs2)
        s_s[p] = S * ewt + blk_ref[...] * _dot_tn(_rows(U, vb), BhKh)
        yield
        o_sum, rk_sum = _head_sums(ones_blk, T, o, r * k * rk_ref[:, cs])
        yield
        oc = o - o_sum * (1.0 / B_HEAD)
        var, = _head_sums(ones_blk, T, oc * oc)
        yield
        on = oc * lax.rsqrt(var * (1.0 / B_HEAD) + GN_EPS) * lnw_ref[:, cs] + lnb_ref[:, cs]
        hb_ref[:, cs] = ((on + rk_sum * v) * g).astype(hb_ref.dtype)

    _run_staggered([pair_program(p) for p in range(n_pairs)], RWKV_WAVE, RWKV_LAG)

    @pl.when(c == pl.num_programs(2) - 1)
    def _():
        for p in range(n_pairs):
            sn_ref[0, 2 * p] = s_s[p, :B_HEAD, :B_HEAD]
            sn_ref[0, 2 * p + 1] = s_s[p, B_HEAD:, B_HEAD:]


def _rwkv(proj, e_main, e_tail, mu_main, mu_tail, w0, w2p, a0, a2p, g2, k_k, k_a, rk, lnw, lnb, blk, s0, *,
          T, n_seq, n_chunk):
    assert T <= RWKV_CHUNK
    gw = GROUP_W
    wide = lambda col: pl.BlockSpec((T, gw), lambda s, pg, c: (s * n_chunk + c, col // gw + pg))
    edge = lambda col: pl.BlockSpec((1, SUBLANES, gw), lambda s, pg, c: (s, 0, col // gw + pg))
    vec = lambda col: pl.BlockSpec((1, gw), lambda s, pg, c: (0, col // gw + pg))
    mat = lambda rows: pl.BlockSpec((rows, gw), lambda s, pg, c: (0, pg))
    st = pl.BlockSpec((1, 2 * GROUP_PAIRS, B_HEAD, B_HEAD), lambda s, pg, c: (s, pg, 0, 0))
    return pl.pallas_call(
        functools.partial(_rwkv_body, T=T, n_pairs=GROUP_PAIRS),
        grid=(n_seq, N_PAIRS // GROUP_PAIRS, n_chunk),
        in_specs=[wide(COL_R), wide(COL_K), wide(COL_VB),
                  pl.BlockSpec((T, TAIL_MIX), lambda s, pg, c: (s * n_chunk + c, COL_TAIL // TAIL_MIX)),
                  wide(COL_GB),
                  edge(0), edge(B_W), edge(2 * B_W),
                  pl.BlockSpec((1, SUBLANES, TAIL_MIX), lambda s, pg, c: (s, 0, 0)),
                  vec(0), vec(B_W), vec(2 * B_W),
                  pl.BlockSpec((1, TAIL_MIX), lambda s, pg, c: (0, 0)),
                  vec(0), mat(LANES), vec(0), mat(LANES), mat(LORA_GATE), vec(0), vec(0),
                  vec(0), vec(0), vec(0),
                  pl.BlockSpec((LANES, LANES), lambda s, pg, c: (0, 0)),
                  pl.BlockSpec((LANES, LANES), lambda s, pg, c: (0, 0)), st],
        out_specs=[pl.BlockSpec((T, gw), lambda s, pg, c: (s * n_chunk + c, pg)), st],
        out_shape=[jax.ShapeDtypeStruct((n_seq * n_chunk * T, B_W), BF16),
                   jax.ShapeDtypeStruct((n_seq, B_HEADS, B_HEAD, B_HEAD), F32)],
        scratch_shapes=[pltpu.VMEM((GROUP_PAIRS, LANES, LANES), F32),
                        pltpu.VMEM((SUBLANES, gw), F32), pltpu.VMEM((SUBLANES, gw), F32),
                        pltpu.VMEM((SUBLANES, gw), F32), pltpu.VMEM((SUBLANES, TAIL_MIX), F32)],
        compiler_params=_cparams(("arbitrary", "arbitrary", "arbitrary"), 32),
        name="rwkv_chunk",
    )(proj, proj, proj, proj, proj, e_main, e_main, e_main, e_tail, mu_main, mu_main, mu_main, mu_tail,
      w0, w2p, a0, a2p, g2, k_k, k_a, rk, lnw, lnb, blk, blk.astype(BF16), s0)


def _merge_out_body(ha_ref, hb_ref, x_ref, w_ref, g_ref, y_ref):
    mixed = ha_ref[...].astype(F32) + hb_ref[...].astype(F32)
    y = jnp.dot(mixed.astype(BF16), w_ref[...], preferred_element_type=F32)
    ms = jnp.mean(y * y, axis=-1, keepdims=True)
    y_ref[...] = x_ref[...] + y * lax.rsqrt(ms + EPS) * g_ref[...]


def _merge_out(h_a, h_b, x, w_out, g_post):
    n = x.shape[0]
    wide = pl.BlockSpec((MERGE_TM, D_MODEL), lambda i: (i, 0))
    return pl.pallas_call(
        _merge_out_body,
        grid=(n // MERGE_TM,),
        in_specs=[wide, wide, wide,
                  pl.BlockSpec((D_MODEL, D_MODEL), lambda i: (0, 0)),
                  pl.BlockSpec((1, D_MODEL), lambda i: (0, 0))],
        out_specs=wide,
        out_shape=jax.ShapeDtypeStruct((n, D_MODEL), F32),
        compiler_params=_cparams(("parallel",), 48),
        name="merge_out_proj",
    )(h_a, h_b, x, w_out, g_post)


def _mlp_body(x_ref, g1_ref, wu_ref, wd_ref, g2_ref, y_ref, hn_ref, acc_ref, *, row_chunk):
    j = pl.program_id(1)

    @pl.when(j == 0)
    def _():
        _norm_rows(x_ref, g1_ref, hn_ref, row_chunk)
        acc_ref[...] = jnp.zeros_like(acc_ref)

    u = jnp.dot(hn_ref[...], wu_ref[...], preferred_element_type=F32)
    a = jnp.square(jnp.maximum(u, 0.0))
    acc_ref[...] += jnp.dot(a.astype(BF16), wd_ref[...], preferred_element_type=F32)

    @pl.when(j == pl.num_programs(1) - 1)
    def _():
        def body(c, carry):
            rows = pl.ds(pl.multiple_of(c * row_chunk, row_chunk), row_chunk)
            ff = acc_ref[rows, :]
            ms = jnp.mean(ff * ff, axis=-1, keepdims=True)
            y_ref[rows, :] = x_ref[rows, :] + ff * lax.rsqrt(ms + EPS) * g2_ref[...]
            return carry
        lax.fori_loop(0, x_ref.shape[0] // row_chunk, body, 0)


def _mlp(x, g_pre, w_up, w_down, g_post):
    n, d = x.shape
    f = w_up.shape[1]
    return pl.pallas_call(
        functools.partial(_mlp_body, row_chunk=LANES),
        grid=(n // MLP_TM, f // MLP_TF),
        in_specs=[pl.BlockSpec((MLP_TM, d), lambda i, j: (i, 0)),
                  pl.BlockSpec((1, d), lambda i, j: (0, 0)),
                  pl.BlockSpec((d, MLP_TF), lambda i, j: (0, j)),
                  pl.BlockSpec((MLP_TF, d), lambda i, j: (j, 0)),
                  pl.BlockSpec((1, d), lambda i, j: (0, 0))],
        out_specs=pl.BlockSpec((MLP_TM, d), lambda i, j: (i, 0)),
        out_shape=jax.ShapeDtypeStruct((n, d), F32),
        scratch_shapes=[pltpu.VMEM((MLP_TM, d), BF16), pltpu.VMEM((MLP_TM, d), F32)],
        compiler_params=_cparams(("parallel", "arbitrary"), 56),
        name="mlp",
    )(x, g_pre, w_up, w_down, g_post)


def _pad_cols(a, width):
    return jnp.pad(a, [(0, 0)] * (a.ndim - 1) + [(0, width - a.shape[-1])])


def _pad_rows(a, before, total):
    return jnp.pad(a, ((before, total - before - a.shape[0]), (0, 0)))


def _edge_blocks(prev_rows):
    return jnp.pad(prev_rows, ((0, 0), (SUBLANES - prev_rows.shape[1], 0), (0, 0)))


def _gates_row_form(proj, T):
    g = proj[:, COL_TAIL + TAIL_IF:COL_TAIL + TAIL_IF + 2 * A_HEADS]
    return jnp.swapaxes(g.reshape(-1, T, 2 * A_HEADS), 1, 2)


def kernel(x_prompt, x_sample, state_mlstm_C, state_mlstm_n, state_mlstm_m, state_mlstm_conv, state_rwkv_S, state_rwkv_shift, meta_tokens, norm_mix_pre, norm_mix_post, norm_mlp_pre, norm_mlp_post, w_in, mlstm_conv_w, mlstm_conv_b, mlstm_b_i, mlstm_b_f, mlstm_norm_w, rwkv_mu, rwkv_w0, rwkv_w2, rwkv_a0, rwkv_a2, rwkv_g2, rwkv_k_k, rwkv_k_a, rwkv_r_k, rwkv_ln_w, rwkv_ln_b, w_out, w_up, w_down):
    n_s_seq, s_len, _ = x_sample.shape
    _, p_len, _ = x_prompt.shape
    k3 = A_CONV - 1

    w_all = _relayout_weight(jnp.swapaxes(w_in[0], 0, 1))
    mu = rwkv_mu[0][None, :]
    mu_main = mu[:, :3 * B_W]
    mu_tail = _pad_cols(mu[:, 3 * B_W:], TAIL_MIX)
    w2p = _pad_rows(rwkv_w2[0], 0, LANES).astype(BF16)
    a2p = _pad_rows(rwkv_a2[0], 0, LANES).astype(BF16)
    g2 = rwkv_g2[0].astype(BF16)
    g_pre = norm_mix_pre[0][None, :]
    bias = jnp.concatenate([mlstm_b_i[0], mlstm_b_f[0]])
    bias_row = _pad_cols(bias[None, :], LANES)
    bias_col = bias[:, None]
    norm_w = mlstm_norm_w[0][None, :]
    conv_w = mlstm_conv_w[0]
    conv_b = mlstm_conv_b[0][None, :]
    rwkv_vecs = (rwkv_w0[0][None, :], w2p, rwkv_a0[0][None, :], a2p, g2, rwkv_k_k[0][None, :],
                 rwkv_k_a[0][None, :], rwkv_r_k[0].reshape(1, B_W), rwkv_ln_w[0][None, :], rwkv_ln_b[0][None, :],
                 _head_block_mask())
    w_out_b = w_out[0].astype(BF16)
    w_up_b = w_up[0].astype(BF16)
    w_down_b = w_down[0].astype(BF16)

    def branches(x_rows, tm, T, n_seq, n_chunk, e_conv, e_main, e_tail, c0, n0, m0, s0):
        proj = _norm_matmul(x_rows, g_pre, w_all, tm)
        h_a, c_n, n_n, m_n = _mlstm(proj, _gates_row_form(proj, T), e_conv, conv_w, conv_b, bias_row, bias_col,
                                    norm_w, c0, n0, m0, T=T, n_seq=n_seq, n_chunk=n_chunk)
        h_b, s_n = _rwkv(proj, e_main, e_tail, mu_main, mu_tail, *rwkv_vecs, s0,
                         T=T, n_seq=n_seq, n_chunk=n_chunk)
        return proj, h_a, h_b, (c_n, n_n, m_n, s_n)

    def finish(h_a, h_b, x_rows):
        x1 = _merge_out(h_a, h_b, x_rows, w_out_b, norm_mix_post[0][None, :])
        return _mlp(x1, norm_mlp_pre[0][None, :], w_up_b, w_down_b, norm_mlp_post[0][None, :])

    zeros = lambda *shape: jnp.zeros(shape, F32)
    proj_m, _, _, (c_m, n_m, m_m, s_m) = branches(
        meta_tokens.astype(F32), N_META, N_META, 1, 1,
        zeros(1, SUBLANES, CONV_COLS), zeros(1, SUBLANES, 3 * B_W), zeros(1, SUBLANES, TAIL_MIX),
        zeros(1, A_HEADS, A_DK, A_DV), zeros(1, A_HEADS, A_DK), zeros(1, 1, LANES),
        zeros(1, B_HEADS, B_HEAD, B_HEAD))

    xp = x_prompt[0].astype(F32)
    proj_p, ha_p, hb_p, (c_p, n_p, m_p, s_p) = branches(
        xp, PROJ_TM, MLSTM_CHUNK, 1, p_len // MLSTM_CHUNK,
        _edge_blocks(proj_m[None, N_META - k3:, COL_CONV:COL_CONV + CONV_COLS]),
        _edge_blocks(proj_m[None, N_META - 1:, COL_R:COL_R + 3 * B_W]),
        _edge_blocks(proj_m[None, N_META - 1:, COL_TAIL:COL_TAIL + TAIL_MIX]),
        c_m, n_m, m_m, s_m)
    y_prompt = finish(ha_p, hb_p, xp)[None].astype(x_prompt.dtype)

    xs = x_sample.reshape(n_s_seq * s_len, D_MODEL).astype(F32)
    sh = state_rwkv_shift[0].astype(F32)
    m0_s = jnp.pad(state_mlstm_m[0].astype(F32), ((0, 0), (A_HEADS, LANES - 2 * A_HEADS)))[:, None, :]
    proj_s, ha_s, hb_s, (c_s, n_s, m_s, s_s) = branches(
        xs, PROJ_TM, s_len, n_s_seq, 1,
        _edge_blocks(state_mlstm_conv[0].astype(F32)),
        _edge_blocks(sh[:, :, :3 * B_W]), _edge_blocks(_pad_cols(sh[:, :, 3 * B_W:], TAIL_MIX)),
        state_mlstm_C[0].astype(F32), state_mlstm_n[0].astype(F32), m0_s,
        state_rwkv_S[0].astype(F32))
    y_sample = finish(ha_s, hb_s, xs).reshape(x_sample.shape).astype(x_sample.dtype)

    def shift_state(rows):
        return jnp.concatenate([rows[..., COL_R:COL_R + 3 * B_W],
                                rows[..., COL_TAIL:COL_TAIL + LORA_COLS]], axis=-1)

    last_s = proj_s.reshape(n_s_seq, s_len, PROJ_COLS)[:, s_len - k3:, :]
    conv_s = last_s[..., COL_CONV:COL_CONV + CONV_COLS]
    shift_s = shift_state(last_s[:, k3 - 1:, :])
    last_p = proj_p[p_len - k3:, :]
    conv_p = last_p[None, :, COL_CONV:COL_CONV + CONV_COLS]
    shift_p = shift_state(last_p[None, k3 - 1:, :])
    dt_c, dt_n, dt_m = state_mlstm_C.dtype, state_mlstm_n.dtype, state_mlstm_m.dtype
    dt_cv, dt_s, dt_sh = state_mlstm_conv.dtype, state_rwkv_S.dtype, state_rwkv_shift.dtype
    lanes_m = slice(A_HEADS, 2 * A_HEADS)
    return (y_prompt, y_sample,
            c_p[None].astype(dt_c), n_p[None].astype(dt_n), m_p[:, 0, lanes_m][None].astype(dt_m),
            conv_p[None].astype(dt_cv), s_p[None].astype(dt_s), shift_p[None].astype(dt_sh),
            c_s[None].astype(dt_c), n_s[None].astype(dt_n), m_s[:, 0, lanes_m][None].astype(dt_m),
            conv_s[None].astype(dt_cv), s_s[None].astype(dt_s), shift_s[None].astype(dt_sh))
```

```python
import functools

import jax
import jax.numpy as jnp
from jax import lax
from jax.experimental import pallas as pl
from jax.experimental.pallas import tpu as pltpu

F32 = jnp.float32
BF16 = jnp.bfloat16

D_MODEL = 2048
N_META = 16
EPS = 1e-6
D_FF = 4 * D_MODEL
A_HEADS = 8
A_DK = 128
A_DV = D_MODEL // A_HEADS
A_QK = A_HEADS * A_DK
A_VW = A_HEADS * A_DV
A_CONV = 4
MLSTM_CHUNK = 256
B_HEAD = 64
B_HEADS = D_MODEL // B_HEAD
B_W = B_HEADS * B_HEAD
LORA_DECAY = max(32, int(round(1.8 * D_MODEL ** 0.5 / 32)) * 32)
LORA_ICLR = max(32, int(round(1.8 * D_MODEL ** 0.5 / 32)) * 32)
LORA_GATE = max(32, int(round(0.6 * D_MODEL ** 0.8 / 32)) * 32)
GN_EPS = 64e-5
CONV_COLS = 2 * A_QK
SHIFT_COLS = 3 * B_W + LORA_DECAY + LORA_ICLR + LORA_GATE

LANES = 128
SUBLANES = 8

COL_CONV, COL_V, COL_O, COL_R, COL_K, COL_VB, COL_GA, COL_GB = (i * D_MODEL for i in range(8))
COL_TAIL = 8 * D_MODEL
LORA_COLS = LORA_DECAY + LORA_ICLR + LORA_GATE
TAIL_MIX = -(-LORA_COLS // LANES) * LANES
TAIL_IF = TAIL_MIX
TAIL_GD_WIN = (LORA_DECAY + LORA_ICLR) // LANES * LANES
PROJ_TN = 1024
PROJ_COLS = -(-(COL_TAIL + TAIL_IF + LANES) // PROJ_TN) * PROJ_TN
PROJ_TM = 1024
MERGE_TM = 512
MLP_TM, MLP_TF = 512, 1024
DECAY_SCALE = -0.6065306597126334
RWKV_CHUNK = 64
N_PAIRS = B_HEADS // 2
GROUP_PAIRS = 16
RWKV_WAVE, RWKV_LAG = 8, 2
GROUP_W = GROUP_PAIRS * LANES


def _cparams(semantics, vmem_mib):
    return pltpu.CompilerParams(dimension_semantics=semantics, vmem_limit_bytes=vmem_mib << 20)


def _softplus(x):
    return jnp.maximum(x, 0.0) + jnp.log1p(jnp.exp(-jnp.abs(x)))


def _dot(a, b):
    return jnp.dot(a.astype(BF16), b.astype(BF16), preferred_element_type=F32)


def _dot_nt(a, b):
    return lax.dot_general(a.astype(BF16), b.astype(BF16), (((1,), (1,)), ((), ())),
                           preferred_element_type=F32)


def _dot_tn(a, b):
    return lax.dot_general(a.astype(BF16), b.astype(BF16), (((0,), (0,)), ((), ())),
                           preferred_element_type=F32)


def _each(f, *lists):
    return [f(*args) for args in zip(*lists)]


def _run_staggered(programs, wave, lag):
    live = list(enumerate(programs))
    slot = 0
    while live:
        still = []
        for p, prog in live:
            if slot >= (p // wave) * lag:
                try:
                    next(prog)
                except StopIteration:
                    continue
            still.append((p, prog))
        live = still
        slot += 1


def _rows(*parts):
    return jnp.concatenate(parts, axis=0)


def _split_bf16(x, pieces):
    out = []
    for _ in range(pieces - 1):
        p = x.astype(BF16)
        out.append(p)
        x = x - p.astype(F32)
    out.append(x.astype(BF16))
    return out


def _dot_exact_rhs(x, m, pieces=3):
    mb = m.astype(BF16)
    acc = None
    for p in _split_bf16(x, pieces):
        t = jnp.dot(p, mb, preferred_element_type=F32)
        acc = t if acc is None else acc + t
    return acc


def _sigmoid(x):
    return 0.5 * jnp.tanh(0.5 * x) + 0.5


def _cumsum_rows(x):
    n = x.shape[0]
    row = lax.broadcasted_iota(jnp.int32, x.shape, 0)
    step = 1
    while step < n:
        x = x + jnp.where(row >= step, pltpu.roll(x, step, 0), 0.0)
        step *= 2
    return x


def _head_block_mask():
    a = lax.broadcasted_iota(jnp.int32, (LANES, LANES), 0) // B_HEAD
    b = lax.broadcasted_iota(jnp.int32, (LANES, LANES), 1) // B_HEAD
    return jnp.where(a == b, 1.0, 0.0).astype(F32)


SRC_IF = CONV_COLS + 2 * A_VW
SRC_SHIFT = SRC_IF + 2 * A_HEADS
SRC_GATE = SRC_SHIFT + SHIFT_COLS
SRC_COLS = SRC_GATE + 2 * D_MODEL
RELAYOUT_ROWS = TAIL_MIX


def _relayout_src_row(t):
    unit = 2 * SUBLANES
    dst = t * (RELAYOUT_ROWS // unit)
    src = jnp.where(dst < COL_R // unit, dst,
                    jnp.where(dst < COL_GA // unit, dst + (SRC_SHIFT - COL_R) // unit,
                              jnp.where(dst < COL_TAIL // unit, dst + (SRC_GATE - COL_GA) // unit,
                                        jnp.where(dst < (COL_TAIL + TAIL_MIX) // unit,
                                                  (SRC_SHIFT + 3 * B_W) // unit, SRC_IF // unit))))
    return src * unit


def _relayout_body(w_ref, o_ref):
    o_ref[...] = w_ref[...].astype(BF16)


def _relayout_weight(wt):
    d = wt.shape[1]
    assert wt.shape[0] == SRC_COLS and PROJ_COLS == COL_TAIL + 2 * RELAYOUT_ROWS and COL_R == SRC_IF
    assert SRC_SHIFT + 3 * B_W + RELAYOUT_ROWS <= SRC_COLS and SRC_SHIFT % (2 * SUBLANES) == 0
    return pl.pallas_call(
        _relayout_body,
        grid=(PROJ_COLS // RELAYOUT_ROWS,),
        in_specs=[pl.BlockSpec((pl.Element(RELAYOUT_ROWS), pl.Element(d)), lambda t: (_relayout_src_row(t), 0))],
        out_specs=pl.BlockSpec((RELAYOUT_ROWS, d), lambda t: (t, 0)),
        out_shape=jax.ShapeDtypeStruct((PROJ_COLS, d), BF16),
        compiler_params=_cparams(("parallel",), 32),
        name="relayout_w_in",
    )(wt)


def _norm_rows(x_ref, g_ref, out_ref, row_chunk):
    def body(c, carry):
        rows = pl.ds(pl.multiple_of(c * row_chunk, row_chunk), row_chunk)
        x = x_ref[rows, :]
        ms = jnp.mean(x * x, axis=-1, keepdims=True)
        out_ref[rows, :] = (x * lax.rsqrt(ms + EPS) * g_ref[...]).astype(out_ref.dtype)
        return carry
    lax.fori_loop(0, x_ref.shape[0] // row_chunk, body, 0)


def _norm_matmul_body(x_ref, g_ref, w_ref, o_ref, hn_ref, *, row_chunk):
    @pl.when(pl.program_id(1) == 0)
    def _():
        _norm_rows(x_ref, g_ref, hn_ref, row_chunk)

    o_ref[...] = lax.dot_general(hn_ref[...], w_ref[...], (((1,), (1,)), ((), ())), preferred_element_type=F32)


def _norm_matmul(x, g, w, tm):
    n, d = x.shape
    p = w.shape[0]
    return pl.pallas_call(
        functools.partial(_norm_matmul_body, row_chunk=min(tm, LANES)),
        grid=(n // tm, p // PROJ_TN),
        in_specs=[pl.BlockSpec((tm, d), lambda i, j: (i, 0)),
                  pl.BlockSpec((1, d), lambda i, j: (0, 0)),
                  pl.BlockSpec((PROJ_TN, d), lambda i, j: (j, 0))],
        out_specs=pl.BlockSpec((tm, PROJ_TN), lambda i, j: (i, j)),
        out_shape=jax.ShapeDtypeStruct((n, p), F32),
        scratch_shapes=[pltpu.VMEM((tm, d), BF16)],
        compiler_params=_cparams(("parallel", "arbitrary"), 48),
        name="in_proj",
    )(x, g, w)


def _mlstm_body(u_ref, v_ref, o_ref, ga_ref, gc_ref, gr_ref, e_ref, cw_ref, cb_ref, brow_ref, bcol_ref,
                nw_ref, c0_ref, n0_ref, m0_ref,
                h_ref, cn_ref, nn_ref, mn_ref, c_s, n_s, m_s, e_s, *, T):
    c = pl.program_id(1)

    @pl.when(c == 0)
    def _():
        c_s[...] = c0_ref[0]
        n_s[...] = n0_ref[0]
        m_s[...] = m0_ref[0]
        e_s[...] = e_ref[0]

    u = u_ref[...]
    ext = _rows(e_s[...], u)
    w = cw_ref[...]
    acc = cb_ref[...] + u * w[A_CONV - 1:A_CONV, :]
    for sh in range(1, A_CONV):
        acc = acc + pltpu.roll(ext, sh, 0)[SUBLANES:, :] * w[A_CONV - 1 - sh:A_CONV - sh, :]
    e_s[...] = u[T - SUBLANES:, :]
    qk_act = acc * _sigmoid(acc)

    gcol = gc_ref[...] + brow_ref[...]
    grow = gr_ref[0] + bcol_ref[...]
    lf_col = -_softplus(-gcol)
    ig_row = grow[:A_HEADS, :]
    lf_row = -_softplus(-grow[A_HEADS:, :])
    ri = lax.broadcasted_iota(jnp.int32, (T, T), 0)
    ci = lax.broadcasted_iota(jnp.int32, (T, T), 1)
    tri = ci <= ri
    b_col = _cumsum_rows(lf_col)
    b_row = _dot_exact_rhs(lf_row, jnp.where(ri <= ci, 1.0, 0.0))
    m_prev = m_s[...]
    lane = lax.broadcasted_iota(jnp.int32, (1, LANES), 1)
    heads = list(range(A_HEADS))
    bc = [b_col[:, A_HEADS + h:A_HEADS + h + 1] for h in heads]
    mh = [m_prev[:, A_HEADS + h:A_HEADS + h + 1] for h in heads]
    br = [b_row[h:h + 1, :] for h in heads]
    igr = [ig_row[h:h + 1, :] for h in heads]
    igc = [gcol[:, h:h + 1] for h in heads]
    qh = [qk_act[:, h * A_DK:(h + 1) * A_DK] for h in heads]
    kh = [qk_act[:, A_QK + h * A_DK:A_QK + (h + 1) * A_DK] * (A_DK ** -0.5) for h in heads]
    vh = [v_ref[:, h * A_DV:(h + 1) * A_DV] for h in heads]
    ch = [c_s[h] for h in heads]
    nh = [n_s[h:h + 1, :] for h in heads]
    qk = _each(_dot_nt, qh, kh)
    qc = _each(_dot, qh, ch)
    d = _each(lambda a, b, i: jnp.where(tri, a - b + i, -jnp.inf), bc, br, igr)
    inter = _each(jnp.add, bc, mh)
    m_t = _each(lambda i, x: jnp.maximum(i, jnp.max(x, axis=-1, keepdims=True)), inter, d)
    s = _each(lambda a, x, m: a * jnp.exp(x - m), qk, d, m_t)
    e_in = _each(lambda i, m: jnp.exp(i - m), inter, m_t)
    sv = _each(_dot, s, vh)
    num = _each(lambda a, x, b: a + x * b, sv, e_in, qc)
    den = _each(lambda a, x, qq, nn: jnp.sum(a, axis=-1, keepdims=True)
                + x * jnp.sum(qq * nn, axis=-1, keepdims=True), s, e_in, qh, nh)
    hh = _each(lambda a, b, m: a / jnp.maximum(jnp.abs(b), jnp.exp(-m)), num, den, m_t)
    hh = _each(lambda a: a * lax.rsqrt(jnp.mean(a * a, axis=-1, keepdims=True) + EPS), hh)
    b_end = _each(lambda a: a[T - 1:T, :], bc)
    m_new = _each(lambda be, m, b, i: jnp.maximum(be + m, jnp.max(be - b + i, axis=-1, keepdims=True)),
                  b_end, mh, br, igr)
    carry = _each(lambda be, m, mn: jnp.exp(be + m - mn), b_end, mh, m_new)
    wk = _each(lambda kk_, be, a, i, mn: kk_ * jnp.exp(be - a + i - mn), kh, b_end, bc, igc, m_new)
    kv = _each(_dot_tn, wk, vh)
    m_next = m_prev
    for h in heads:
        cols = slice(h * A_DV, (h + 1) * A_DV)
        gate = _sigmoid(o_ref[:, cols]) * _sigmoid(ga_ref[:, cols])
        h_ref[:, cols] = (hh[h] * nw_ref[:, cols] * gate).astype(h_ref.dtype)
        c_s[h] = carry[h] * ch[h] + kv[h]
        n_s[h:h + 1, :] = carry[h] * nh[h] + jnp.sum(wk[h], axis=0, keepdims=True)
        m_next = jnp.where(lane == A_HEADS + h, m_new[h], m_next)
    m_s[...] = m_next

    @pl.when(c == pl.num_programs(1) - 1)
    def _():
        cn_ref[0] = c_s[...]
        nn_ref[0] = n_s[...]
        mn_ref[0] = m_s[...]


def _mlstm(proj, gates_row, e_conv, conv_w, conv_b, bias_row, bias_col, norm_w, c0, n0, m0, *, T, n_seq, n_chunk):
    row = lambda s, c: (s * n_chunk + c, 0)
    fixed = lambda s, c: (0, 0)
    wide = lambda col: pl.BlockSpec((T, D_MODEL), lambda s, c: (s * n_chunk + c, col // D_MODEL))
    return pl.pallas_call(
        functools.partial(_mlstm_body, T=T),
        grid=(n_seq, n_chunk),
        in_specs=[wide(COL_CONV), wide(COL_V), wide(COL_O), wide(COL_GA),
                  pl.BlockSpec((T, LANES), lambda s, c: (s * n_chunk + c, (COL_TAIL + TAIL_IF) // LANES)),
                  pl.BlockSpec((1, 2 * A_HEADS, T), lambda s, c: (s * n_chunk + c, 0, 0)),
                  pl.BlockSpec((1, SUBLANES, CONV_COLS), lambda s, c: (s, 0, 0)),
                  pl.BlockSpec((A_CONV, CONV_COLS), fixed),
                  pl.BlockSpec((1, CONV_COLS), fixed),
                  pl.BlockSpec((1, LANES), fixed),
                  pl.BlockSpec((2 * A_HEADS, 1), fixed),
                  pl.BlockSpec((1, A_VW), fixed),
                  pl.BlockSpec((1, A_HEADS, A_DK, A_DV), lambda s, c: (s, 0, 0, 0)),
                  pl.BlockSpec((1, A_HEADS, A_DK), lambda s, c: (s, 0, 0)),
                  pl.BlockSpec((1, 1, LANES), lambda s, c: (s, 0, 0))],
        out_specs=[pl.BlockSpec((T, A_VW), row),
                   pl.BlockSpec((1, A_HEADS, A_DK, A_DV), lambda s, c: (s, 0, 0, 0)),
                   pl.BlockSpec((1, A_HEADS, A_DK), lambda s, c: (s, 0, 0)),
                   pl.BlockSpec((1, 1, LANES), lambda s, c: (s, 0, 0))],
        out_shape=[jax.ShapeDtypeStruct((n_seq * n_chunk * T, A_VW), BF16),
                   jax.ShapeDtypeStruct((n_seq, A_HEADS, A_DK, A_DV), F32),
                   jax.ShapeDtypeStruct((n_seq, A_HEADS, A_DK), F32),
                   jax.ShapeDtypeStruct((n_seq, 1, LANES), F32)],
        scratch_shapes=[pltpu.VMEM((A_HEADS, A_DK, A_DV), F32),
                        pltpu.VMEM((A_HEADS, A_DK), F32),
                        pltpu.VMEM((1, LANES), F32),
                        pltpu.VMEM((SUBLANES, CONV_COLS), F32)],
        compiler_params=_cparams(("arbitrary", "arbitrary"), 32),
        name="mlstm_chunk",
    )(proj, proj, proj, proj, proj, gates_row, e_conv, conv_w, conv_b, bias_row, bias_col, norm_w, c0, n0, m0)


def _kcat_dot(a1, b1, a2, b2):
    if a1.shape[1] % LANES == 0:
        return _dot(jnp.concatenate([a1, a2], axis=1), _rows(b1, b2))
    return _dot(a1, b1) + _dot(a2, b2)


def _head_sums(ones_blk, T, *xs):
    sums = jnp.dot(_rows(*xs).astype(BF16), ones_blk, preferred_element_type=F32)
    return [sums[i * T:(i + 1) * T] for i in range(len(xs))]


def _rwkv_body(ur_ref, uk_ref, uv_ref, ul_ref, gb_ref, er_ref, ek_ref, ev_ref, el_ref,
               mur_ref, muk_ref, muv_ref, mul_ref, w0_ref, w2_ref, a0_ref, a2_ref, g2_ref, kkw_ref, ka_ref,
               rk_ref, lnw_ref, lnb_ref, blk_ref, blkb_ref, s0_ref,
               hb_ref, sn_ref, s_s, cr_s, ck_s, cv_s, cl_s, *, T, n_pairs):
    c = pl.program_id(2)

    @pl.when(c == 0)
    def _():
        z = jnp.zeros((B_HEAD, B_HEAD), F32)
        for p in range(n_pairs):
            s_s[p] = _rows(jnp.concatenate([s0_ref[0, 2 * p], z], axis=1),
                           jnp.concatenate([z, s0_ref[0, 2 * p + 1]], axis=1))
        cr_s[...] = er_ref[0]
        ck_s[...] = ek_ref[0]
        cv_s[...] = ev_ref[0]
        cl_s[...] = el_ref[0]

    def shift_mix(u_ref, carry_ref, mu_ref, cs):
        u = u_ref[:, cs]
        prev = pltpu.roll(_rows(carry_ref[:, cs], u), 1, 0)[SUBLANES:, :]
        carry_ref[:, cs] = u[T - SUBLANES:, :]
        return u + mu_ref[:, cs] * (prev - u)

    xl = shift_mix(ul_ref, cl_s, mul_ref, slice(None))
    th_wd = jnp.tanh(xl[:, :LANES]).astype(BF16)
    x_ad = pltpu.roll(xl[:, :2 * LANES], 2 * LANES - LORA_DECAY, 1)[:, :LANES].astype(BF16)
    gd_off = LORA_DECAY + LORA_ICLR - TAIL_GD_WIN
    sg_gd = _sigmoid(pltpu.roll(xl[:, TAIL_GD_WIN:], TAIL_MIX - TAIL_GD_WIN - gd_off, 1)[:, :LORA_GATE]).astype(BF16)

    lo = lax.broadcasted_iota(jnp.int32, (1, LANES), 1) < B_HEAD
    head1 = lambda x: jnp.where(lo, x, jnp.zeros_like(x))
    head2 = lambda x: jnp.where(lo, jnp.zeros_like(x), x)
    r2 = lax.broadcasted_iota(jnp.int32, (T, 2 * T), 0)
    c2 = lax.broadcasted_iota(jnp.int32, (T, 2 * T), 1)
    left = c2 < T
    cj = jnp.where(left, c2, c2 - T)
    strict2 = cj < r2
    incl2 = cj <= r2
    eye_pair = jnp.where(cj == r2, 1.0, 0.0)
    ones_blk = blkb_ref[...]
    zero = jnp.zeros((T, LANES), BF16)
    aligned = (2 * T) % LANES == 0

    def pair_program(p):
        cs = slice(p * LANES, (p + 1) * LANES)
        r = shift_mix(ur_ref, cr_s, mur_ref, cs)
        xk = shift_mix(uk_ref, ck_s, muk_ref, cs)
        v = shift_mix(uv_ref, cv_s, muv_ref, cs)
        lw = jnp.dot(th_wd, w2_ref[:, cs], preferred_element_type=F32)
        la = jnp.dot(x_ad, a2_ref[:, cs], preferred_element_type=F32)
        lg = jnp.dot(sg_gd, g2_ref[:, cs], preferred_element_type=F32)
        kkw = xk * kkw_ref[:, cs]
        ssq, = _head_sums(ones_blk, T, kkw * kkw)
        yield
        ld = DECAY_SCALE * _sigmoid(w0_ref[:, cs] + lw)
        a = _sigmoid(a0_ref[:, cs] + la)
        g = lg * _sigmoid(gb_ref[:, cs])
        k = xk * (1.0 + (a - 1.0) * ka_ref[:, cs])
        kk = kkw * lax.rsqrt(jnp.maximum(ssq, 1e-24))
        bi = kk * a
        cum = _cumsum_rows(ld)
        ewt = jnp.exp(cum[T - 1:T, :])
        Rt = (r * jnp.exp(cum)).astype(BF16)
        At = (-kk * jnp.exp(cum - ld)).astype(BF16)
        einv = jnp.exp(-cum)
        Bt = bi * einv
        Kt = k * einv
        BhKh = _rows(Bt * ewt, Kt * ewt).astype(BF16)
        Bt = Bt.astype(BF16)
        Kt = Kt.astype(BF16)
        vb = v.astype(BF16)
        V1 = head1(vb)
        V2 = head2(vb)
        if aligned:
            G = _dot_nt(_rows(head1(At), head1(Rt), head2(At), head2(Rt)), _rows(Bt, Kt))
            G1, G2 = G[:2 * T], G[2 * T:]
        else:
            G1 = _dot_nt(_rows(head1(At), head1(Rt)), _rows(Bt, Kt))
            G2 = _dot_nt(_rows(head2(At), head2(Rt)), _rows(Kt, Bt))
        yield
        LL1 = jnp.where(strict2, G1[:T], 0.0)
        LL2 = jnp.where(strict2, G2[:T], 0.0)
        P1 = jnp.where(incl2, G1[T:], 0.0)
        P2 = jnp.where(incl2, G2[T:], 0.0)
        lab2 = pltpu.roll(LL2, T, 1) if aligned else LL2
        pw = _rows(jnp.where(left, LL1, 0.0), jnp.where(left, 0.0, lab2)).astype(BF16)
        iv = eye_pair + jnp.where(left, LL1, lab2)
        X0 = _kcat_dot(LL1, _rows(zero, V1), LL2, _rows(zero, V2) if aligned else _rows(V2, zero))
        pw = _dot(pw, pw).astype(BF16)
        yield
        span = 4
        while span < T:
            both = _dot(_rows(iv.astype(BF16), pw), pw)
            iv, pw = iv + both[:T], both[T:].astype(BF16)
            span *= 2
            yield
        iv = (iv + _dot(iv, pw)).astype(BF16)
        S = s_s[p]
        AR = _dot_nt(_rows(At, Rt), S)
        yield
        X = (AR[:T] + X0).astype(BF16)
        U = _dot(iv, _rows(head1(X), head2(X))).astype(BF16)
        yield
        if aligned:
            rhs2 = _rows(head2(U), V2)
        else:
            rhs2 = _rows(V2, head2(U))
        o = AR[T:] + _kcat_dot(P1, _rows(head1(U), V1), P2, rhs2)
        s_s[p] = S * ewt + blk_ref[...] * _dot_tn(_rows(U, vb), BhKh)
        yield
        o_sum, rk_sum = _head_sums(ones_blk, T, o, r * k * rk_ref[:, cs])
        yield
        oc = o - o_sum * (1.0 / B_HEAD)
        var, = _head_sums(ones_blk, T, oc * oc)
        yield
        on = oc * lax.rsqrt(var * (1.0 / B_HEAD) + GN_EPS) * lnw_ref[:, cs] + lnb_ref[:, cs]
        hb_ref[:, cs] = ((on + rk_sum * v) * g).astype(hb_ref.dtype)

    _run_staggered([pair_program(p) for p in range(n_pairs)], RWKV_WAVE, RWKV_LAG)

    @pl.when(c == pl.num_programs(2) - 1)
    def _():
        for p in range(n_pairs):
            sn_ref[0, 2 * p] = s_s[p, :B_HEAD, :B_HEAD]
            sn_ref[0, 2 * p + 1] = s_s[p, B_HEAD:, B_HEAD:]


def _rwkv(proj, e_main, e_tail, mu_main, mu_tail, w0, w2p, a0, a2p, g2, k_k, k_a, rk, lnw, lnb, blk, s0, *,
          T, n_seq, n_chunk):
    assert T <= RWKV_CHUNK
    gw = GROUP_W
    wide = lambda col: pl.BlockSpec((T, gw), lambda s, pg, c: (s * n_chunk + c, col // gw + pg))
    edge = lambda col: pl.BlockSpec((1, SUBLANES, gw), lambda s, pg, c: (s, 0, col // gw + pg))
    vec = lambda col: pl.BlockSpec((1, gw), lambda s, pg, c: (0, col // gw + pg))
    mat = lambda rows: pl.BlockSpec((rows, gw), lambda s, pg, c: (0, pg))
    st = pl.BlockSpec((1, 2 * GROUP_PAIRS, B_HEAD, B_HEAD), lambda s, pg, c: (s, pg, 0, 0))
    return pl.pallas_call(
        functools.partial(_rwkv_body, T=T, n_pairs=GROUP_PAIRS),
        grid=(n_seq, N_PAIRS // GROUP_PAIRS, n_chunk),
        in_specs=[wide(COL_R), wide(COL_K), wide(COL_VB),
                  pl.BlockSpec((T, TAIL_MIX), lambda s, pg, c: (s * n_chunk + c, COL_TAIL // TAIL_MIX)),
                  wide(COL_GB),
                  edge(0), edge(B_W), edge(2 * B_W),
                  pl.BlockSpec((1, SUBLANES, TAIL_MIX), lambda s, pg, c: (s, 0, 0)),
                  vec(0), vec(B_W), vec(2 * B_W),
                  pl.BlockSpec((1, TAIL_MIX), lambda s, pg, c: (0, 0)),
                  vec(0), mat(LANES), vec(0), mat(LANES), mat(LORA_GATE), vec(0), vec(0),
                  vec(0), vec(0), vec(0),
                  pl.BlockSpec((LANES, LANES), lambda s, pg, c: (0, 0)),
                  pl.BlockSpec((LANES, LANES), lambda s, pg, c: (0, 0)), st],
        out_specs=[pl.BlockSpec((T, gw), lambda s, pg, c: (s * n_chunk + c, pg)), st],
        out_shape=[jax.ShapeDtypeStruct((n_seq * n_chunk * T, B_W), BF16),
                   jax.ShapeDtypeStruct((n_seq, B_HEADS, B_HEAD, B_HEAD), F32)],
        scratch_shapes=[pltpu.VMEM((GROUP_PAIRS, LANES, LANES), F32),
                        pltpu.VMEM((SUBLANES, gw), F32), pltpu.VMEM((SUBLANES, gw), F32),
                        pltpu.VMEM((SUBLANES, gw), F32), pltpu.VMEM((SUBLANES, TAIL_MIX), F32)],
        compiler_params=_cparams(("arbitrary", "arbitrary", "arbitrary"), 32),
        name="rwkv_chunk",
    )(proj, proj, proj, proj, proj, e_main, e_main, e_main, e_tail, mu_main, mu_main, mu_main, mu_tail,
      w0, w2p, a0, a2p, g2, k_k, k_a, rk, lnw, lnb, blk, blk.astype(BF16), s0)


def _merge_out_body(ha_ref, hb_ref, x_ref, w_ref, g_ref, y_ref):
    mixed = ha_ref[...].astype(F32) + hb_ref[...].astype(F32)
    y = jnp.dot(mixed.astype(BF16), w_ref[...], preferred_element_type=F32)
    ms = jnp.mean(y * y, axis=-1, keepdims=True)
    y_ref[...] = x_ref[...] + y * lax.rsqrt(ms + EPS) * g_ref[...]


def _merge_out(h_a, h_b, x, w_out, g_post):
    n = x.shape[0]
    wide = pl.BlockSpec((MERGE_TM, D_MODEL), lambda i: (i, 0))
    return pl.pallas_call(
        _merge_out_body,
        grid=(n // MERGE_TM,),
        in_specs=[wide, wide, wide,
                  pl.BlockSpec((D_MODEL, D_MODEL), lambda i: (0, 0)),
                  pl.BlockSpec((1, D_MODEL), lambda i: (0, 0))],
        out_specs=wide,
        out_shape=jax.ShapeDtypeStruct((n, D_MODEL), F32),
        compiler_params=_cparams(("parallel",), 48),
        name="merge_out_proj",
    )(h_a, h_b, x, w_out, g_post)


def _mlp_body(x_ref, g1_ref, wu_ref, wd_ref, g2_ref, y_ref, hn_ref, acc_ref, *, row_chunk):
    j = pl.program_id(1)

    @pl.when(j == 0)
    def _():
        _norm_rows(x_ref, g1_ref, hn_ref, row_chunk)
        acc_ref[...] = jnp.zeros_like(acc_ref)

    u = jnp.dot(hn_ref[...], wu_ref[...], preferred_element_type=F32)
    a = jnp.square(jnp.maximum(u, 0.0))
    acc_ref[...] += jnp.dot(a.astype(BF16), wd_ref[...], preferred_element_type=F32)

    @pl.when(j == pl.num_programs(1) - 1)
    def _():
        def body(c, carry):
            rows = pl.ds(pl.multiple_of(c * row_chunk, row_chunk), row_chunk)
            ff = acc_ref[rows, :]
            ms = jnp.mean(ff * ff, axis=-1, keepdims=True)
            y_ref[rows, :] = x_ref[rows, :] + ff * lax.rsqrt(ms + EPS) * g2_ref[...]
            return carry
        lax.fori_loop(0, x_ref.shape[0] // row_chunk, body, 0)


def _mlp(x, g_pre, w_up, w_down, g_post):
    n, d = x.shape
    f = w_up.shape[1]
    return pl.pallas_call(
        functools.partial(_mlp_body, row_chunk=LANES),
        grid=(n // MLP_TM, f // MLP_TF),
        in_specs=[pl.BlockSpec((MLP_TM, d), lambda i, j: (i, 0)),
                  pl.BlockSpec((1, d), lambda i, j: (0, 0)),
                  pl.BlockSpec((d, MLP_TF), lambda i, j: (0, j)),
                  pl.BlockSpec((MLP_TF, d), lambda i, j: (j, 0)),
                  pl.BlockSpec((1, d), lambda i, j: (0, 0))],
        out_specs=pl.BlockSpec((MLP_TM, d), lambda i, j: (i, 0)),
        out_shape=jax.ShapeDtypeStruct((n, d), F32),
        scratch_shapes=[pltpu.VMEM((MLP_TM, d), BF16), pltpu.VMEM((MLP_TM, d), F32)],
        compiler_params=_cparams(("parallel", "arbitrary"), 56),
        name="mlp",
    )(x, g_pre, w_up, w_down, g_post)


def _pad_cols(a, width):
    return jnp.pad(a, [(0, 0)] * (a.ndim - 1) + [(0, width - a.shape[-1])])


def _pad_rows(a, before, total):
    return jnp.pad(a, ((before, total - before - a.shape[0]), (0, 0)))


def _edge_blocks(prev_rows):
    return jnp.pad(prev_rows, ((0, 0), (SUBLANES - prev_rows.shape[1], 0), (0, 0)))


def _gates_row_form(proj, T):
    g = proj[:, COL_TAIL + TAIL_IF:COL_TAIL + TAIL_IF + 2 * A_HEADS]
    return jnp.swapaxes(g.reshape(-1, T, 2 * A_HEADS), 1, 2)


def kernel(x_prompt, x_sample, state_mlstm_C, state_mlstm_n, state_mlstm_m, state_mlstm_conv, state_rwkv_S, state_rwkv_shift, meta_tokens, norm_mix_pre, norm_mix_post, norm_mlp_pre, norm_mlp_post, w_in, mlstm_conv_w, mlstm_conv_b, mlstm_b_i, mlstm_b_f, mlstm_norm_w, rwkv_mu, rwkv_w0, rwkv_w2, rwkv_a0, rwkv_a2, rwkv_g2, rwkv_k_k, rwkv_k_a, rwkv_r_k, rwkv_ln_w, rwkv_ln_b, w_out, w_up, w_down):
    n_s_seq, s_len, _ = x_sample.shape
    _, p_len, _ = x_prompt.shape
    k3 = A_CONV - 1

    w_all = _relayout_weight(jnp.swapaxes(w_in[0], 0, 1))
    mu = rwkv_mu[0][None, :]
    mu_main = mu[:, :3 * B_W]
    mu_tail = _pad_cols(mu[:, 3 * B_W:], TAIL_MIX)
    w2p = _pad_rows(rwkv_w2[0], 0, LANES).astype(BF16)
    a2p = _pad_rows(rwkv_a2[0], 0, LANES).astype(BF16)
    g2 = rwkv_g2[0].astype(BF16)
    g_pre = norm_mix_pre[0][None, :]
    bias = jnp.concatenate([mlstm_b_i[0], mlstm_b_f[0]])
    bias_row = _pad_cols(bias[None, :], LANES)
    bias_col = bias[:, None]
    norm_w = mlstm_norm_w[0][None, :]
    conv_w = mlstm_conv_w[0]
    conv_b = mlstm_conv_b[0][None, :]
    rwkv_vecs = (rwkv_w0[0][None, :], w2p, rwkv_a0[0][None, :], a2p, g2, rwkv_k_k[0][None, :],
                 rwkv_k_a[0][None, :], rwkv_r_k[0].reshape(1, B_W), rwkv_ln_w[0][None, :], rwkv_ln_b[0][None, :],
                 _head_block_mask())
    w_out_b = w_out[0].astype(BF16)
    w_up_b = w_up[0].astype(BF16)
    w_down_b = w_down[0].astype(BF16)

    def branches(x_rows, tm, T, n_seq, n_chunk, e_conv, e_main, e_tail, c0, n0, m0, s0):
        proj = _norm_matmul(x_rows, g_pre, w_all, tm)
        tm_ = MLSTM_CHUNK if n_chunk > 1 else T
        h_a, c_n, n_n, m_n = _mlstm(proj, _gates_row_form(proj, tm_), e_conv, conv_w, conv_b, bias_row, bias_col,
                                    norm_w, c0, n0, m0, T=tm_, n_seq=n_seq, n_chunk=n_chunk * T // tm_)
        h_b, s_n = _rwkv(proj, e_main, e_tail, mu_main, mu_tail, *rwkv_vecs, s0,
                         T=T, n_seq=n_seq, n_chunk=n_chunk)
        return proj, h_a, h_b, (c_n, n_n, m_n, s_n)

    def finish(h_a, h_b, x_rows):
        x1 = _merge_out(h_a, h_b, x_rows, w_out_b, norm_mix_post[0][None, :])
        return _mlp(x1, norm_mlp_pre[0][None, :], w_up_b, w_down_b, norm_mlp_post[0][None, :])

    zeros = lambda *shape: jnp.zeros(shape, F32)
    proj_m, _, _, (c_m, n_m, m_m, s_m) = branches(
        meta_tokens.astype(F32), N_META, N_META, 1, 1,
        zeros(1, SUBLANES, CONV_COLS), zeros(1, SUBLANES, 3 * B_W), zeros(1, SUBLANES, TAIL_MIX),
        zeros(1, A_HEADS, A_DK, A_DV), zeros(1, A_HEADS, A_DK), zeros(1, 1, LANES),
        zeros(1, B_HEADS, B_HEAD, B_HEAD))

    xp = x_prompt[0].astype(F32)
    proj_p, ha_p, hb_p, (c_p, n_p, m_p, s_p) = branches(
        xp, PROJ_TM, RWKV_CHUNK, 1, p_len // RWKV_CHUNK,
        _edge_blocks(proj_m[None, N_META - k3:, COL_CONV:COL_CONV + CONV_COLS]),
        _edge_blocks(proj_m[None, N_META - 1:, COL_R:COL_R + 3 * B_W]),
        _edge_blocks(proj_m[None, N_META - 1:, COL_TAIL:COL_TAIL + TAIL_MIX]),
        c_m, n_m, m_m, s_m)
    y_prompt = finish(ha_p, hb_p, xp)[None].astype(x_prompt.dtype)

    xs = x_sample.reshape(n_s_seq * s_len, D_MODEL).astype(F32)
    sh = state_rwkv_shift[0].astype(F32)
    m0_s = jnp.pad(state_mlstm_m[0].astype(F32), ((0, 0), (A_HEADS, LANES - 2 * A_HEADS)))[:, None, :]
    proj_s, ha_s, hb_s, (c_s, n_s, m_s, s_s) = branches(
        xs, PROJ_TM, s_len, n_s_seq, 1,
        _edge_blocks(state_mlstm_conv[0].astype(F32)),
        _edge_blocks(sh[:, :, :3 * B_W]), _edge_blocks(_pad_cols(sh[:, :, 3 * B_W:], TAIL_MIX)),
        state_mlstm_C[0].astype(F32), state_mlstm_n[0].astype(F32), m0_s,
        state_rwkv_S[0].astype(F32))
    y_sample = finish(ha_s, hb_s, xs).reshape(x_sample.shape).astype(x_sample.dtype)

    def shift_state(rows):
        return jnp.concatenate([rows[..., COL_R:COL_R + 3 * B_W],
                                rows[..., COL_TAIL:COL_TAIL + LORA_COLS]], axis=-1)

    last_s = proj_s.reshape(n_s_seq, s_len, PROJ_COLS)[:, s_len - k3:, :]
    conv_s = last_s[..., COL_CONV:COL_CONV + CONV_COLS]
    shift_s = shift_state(last_s[:, k3 - 1:, :])
    last_p = proj_p[p_len - k3:, :]
    conv_p = last_p[None, :, COL_CONV:COL_CONV + CONV_COLS]
    shift_p = shift_state(last_p[None, k3 - 1:, :])
    dt_c, dt_n, dt_m = state_mlstm_C.dtype, state_mlstm_n.dtype, state_mlstm_m.dtype
    dt_cv, dt_s, dt_sh = state_mlstm_conv.dtype, state_rwkv_S.dtype, state_rwkv_shift.dtype
    lanes_m = slice(A_HEADS, 2 * A_HEADS)
    return (y_prompt, y_sample,
            c_p[None].astype(dt_c), n_p[None].astype(dt_n), m_p[:, 0, lanes_m][None].astype(dt_m),
            conv_p[None].astype(dt_cv), s_p[None].astype(dt_s), shift_p[None].astype(dt_sh),
            c_s[None].astype(dt_c), n_s[None].astype(dt_n), m_s[:, 0, lanes_m][None].astype(dt_m),
            conv_s[None].astype(dt_cv), s_s[None].astype(dt_s), shift_s[None].astype(dt_sh))
```

```python
import functools

import jax
import jax.numpy as jnp
from jax import lax
from jax.experimental import pallas as pl
from jax.experimental.pallas import tpu as pltpu

F32 = jnp.float32
BF16 = jnp.bfloat16

D_MODEL = 2048
N_META = 16
EPS = 1e-6
D_FF = 4 * D_MODEL
A_HEADS = 8
A_DK = 128
A_DV = D_MODEL // A_HEADS
A_QK = A_HEADS * A_DK
A_VW = A_HEADS * A_DV
A_CONV = 4
MLSTM_CHUNK = 256
B_HEAD = 64
B_HEADS = D_MODEL // B_HEAD
B_W = B_HEADS * B_HEAD
LORA_DECAY = max(32, int(round(1.8 * D_MODEL ** 0.5 / 32)) * 32)
LORA_ICLR = max(32, int(round(1.8 * D_MODEL ** 0.5 / 32)) * 32)
LORA_GATE = max(32, int(round(0.6 * D_MODEL ** 0.8 / 32)) * 32)
GN_EPS = 64e-5
CONV_COLS = 2 * A_QK
SHIFT_COLS = 3 * B_W + LORA_DECAY + LORA_ICLR + LORA_GATE

LANES = 128
SUBLANES = 8

COL_CONV, COL_V, COL_O, COL_R, COL_K, COL_VB, COL_GA, COL_GB = (i * D_MODEL for i in range(8))
COL_TAIL = 8 * D_MODEL
LORA_COLS = LORA_DECAY + LORA_ICLR + LORA_GATE
TAIL_MIX = -(-LORA_COLS // LANES) * LANES
TAIL_IF = TAIL_MIX
TAIL_GD_WIN = (LORA_DECAY + LORA_ICLR) // LANES * LANES
PROJ_TN = 1024
PROJ_COLS = -(-(COL_TAIL + TAIL_IF + LANES) // PROJ_TN) * PROJ_TN
PROJ_TM = 1024
MERGE_TM = 512
MLP_TM, MLP_TF = 512, 1024
DECAY_SCALE = -0.6065306597126334
RWKV_CHUNK = 64
N_PAIRS = B_HEADS // 2
GROUP_PAIRS = 16
RWKV_WAVE, RWKV_LAG = 8, 1
GROUP_W = GROUP_PAIRS * LANES


def _cparams(semantics, vmem_mib):
    return pltpu.CompilerParams(dimension_semantics=semantics, vmem_limit_bytes=vmem_mib << 20)


def _softplus(x):
    return jnp.maximum(x, 0.0) + jnp.log1p(jnp.exp(-jnp.abs(x)))


def _dot(a, b):
    return jnp.dot(a.astype(BF16), b.astype(BF16), preferred_element_type=F32)


def _dot_nt(a, b):
    return lax.dot_general(a.astype(BF16), b.astype(BF16), (((1,), (1,)), ((), ())),
                           preferred_element_type=F32)


def _dot_tn(a, b):
    return lax.dot_general(a.astype(BF16), b.astype(BF16), (((0,), (0,)), ((), ())),
                           preferred_element_type=F32)


def _each(f, *lists):
    return [f(*args) for args in zip(*lists)]


def _run_staggered(programs, wave, lag):
    live = list(enumerate(programs))
    slot = 0
    while live:
        still = []
        for p, prog in live:
            if slot >= (p // wave) * lag:
                try:
                    next(prog)
                except StopIteration:
                    continue
            still.append((p, prog))
        live = still
        slot += 1


def _rows(*parts):
    return jnp.concatenate(parts, axis=0)


def _split_bf16(x, pieces):
    out = []
    for _ in range(pieces - 1):
        p = x.astype(BF16)
        out.append(p)
        x = x - p.astype(F32)
    out.append(x.astype(BF16))
    return out


def _dot_exact_rhs(x, m, pieces=3):
    mb = m.astype(BF16)
    acc = None
    for p in _split_bf16(x, pieces):
        t = jnp.dot(p, mb, preferred_element_type=F32)
        acc = t if acc is None else acc + t
    return acc


def _sigmoid(x):
    return 0.5 * jnp.tanh(0.5 * x) + 0.5


def _cumsum_rows(x):
    n = x.shape[0]
    row = lax.broadcasted_iota(jnp.int32, x.shape, 0)
    step = 1
    while step < n:
        x = x + jnp.where(row >= step, pltpu.roll(x, step, 0), 0.0)
        step *= 2
    return x


def _head_block_mask():
    a = lax.broadcasted_iota(jnp.int32, (LANES, LANES), 0) // B_HEAD
    b = lax.broadcasted_iota(jnp.int32, (LANES, LANES), 1) // B_HEAD
    return jnp.where(a == b, 1.0, 0.0).astype(F32)


SRC_IF = CONV_COLS + 2 * A_VW
SRC_SHIFT = SRC_IF + 2 * A_HEADS
SRC_GATE = SRC_SHIFT + SHIFT_COLS
SRC_COLS = SRC_GATE + 2 * D_MODEL
RELAYOUT_ROWS = TAIL_MIX


def _relayout_src_row(t):
    unit = 2 * SUBLANES
    dst = t * (RELAYOUT_ROWS // unit)
    src = jnp.where(dst < COL_R // unit, dst,
                    jnp.where(dst < COL_GA // unit, dst + (SRC_SHIFT - COL_R) // unit,
                              jnp.where(dst < COL_TAIL // unit, dst + (SRC_GATE - COL_GA) // unit,
                                        jnp.where(dst < (COL_TAIL + TAIL_MIX) // unit,
                                                  (SRC_SHIFT + 3 * B_W) // unit, SRC_IF // unit))))
    return src * unit


def _relayout_body(w_ref, o_ref):
    o_ref[...] = w_ref[...].astype(BF16)


def _relayout_weight(wt):
    d = wt.shape[1]
    assert wt.shape[0] == SRC_COLS and PROJ_COLS == COL_TAIL + 2 * RELAYOUT_ROWS and COL_R == SRC_IF
    assert SRC_SHIFT + 3 * B_W + RELAYOUT_ROWS <= SRC_COLS and SRC_SHIFT % (2 * SUBLANES) == 0
    return pl.pallas_call(
        _relayout_body,
        grid=(PROJ_COLS // RELAYOUT_ROWS,),
        in_specs=[pl.BlockSpec((pl.Element(RELAYOUT_ROWS), pl.Element(d)), lambda t: (_relayout_src_row(t), 0))],
        out_specs=pl.BlockSpec((RELAYOUT_ROWS, d), lambda t: (t, 0)),
        out_shape=jax.ShapeDtypeStruct((PROJ_COLS, d), BF16),
        compiler_params=_cparams(("parallel",), 32),
        name="relayout_w_in",
    )(wt)


def _norm_rows(x_ref, g_ref, out_ref, row_chunk):
    def body(c, carry):
        rows = pl.ds(pl.multiple_of(c * row_chunk, row_chunk), row_chunk)
        x = x_ref[rows, :]
        ms = jnp.mean(x * x, axis=-1, keepdims=True)
        out_ref[rows, :] = (x * lax.rsqrt(ms + EPS) * g_ref[...]).astype(out_ref.dtype)
        return carry
    lax.fori_loop(0, x_ref.shape[0] // row_chunk, body, 0)


def _norm_matmul_body(x_ref, g_ref, w_ref, o_ref, hn_ref, *, row_chunk):
    @pl.when(pl.program_id(1) == 0)
    def _():
        _norm_rows(x_ref, g_ref, hn_ref, row_chunk)

    o_ref[...] = lax.dot_general(hn_ref[...], w_ref[...], (((1,), (1,)), ((), ())), preferred_element_type=F32)


def _norm_matmul(x, g, w, tm):
    n, d = x.shape
    p = w.shape[0]
    return pl.pallas_call(
        functools.partial(_norm_matmul_body, row_chunk=min(tm, LANES)),
        grid=(n // tm, p // PROJ_TN),
        in_specs=[pl.BlockSpec((tm, d), lambda i, j: (i, 0)),
                  pl.BlockSpec((1, d), lambda i, j: (0, 0)),
                  pl.BlockSpec((PROJ_TN, d), lambda i, j: (j, 0))],
        out_specs=pl.BlockSpec((tm, PROJ_TN), lambda i, j: (i, j)),
        out_shape=jax.ShapeDtypeStruct((n, p), F32),
        scratch_shapes=[pltpu.VMEM((tm, d), BF16)],
        compiler_params=_cparams(("parallel", "arbitrary"), 48),
        name="in_proj",
    )(x, g, w)


def _mlstm_body(u_ref, v_ref, o_ref, ga_ref, gc_ref, gr_ref, e_ref, cw_ref, cb_ref, brow_ref, bcol_ref,
                nw_ref, c0_ref, n0_ref, m0_ref,
                h_ref, cn_ref, nn_ref, mn_ref, c_s, n_s, m_s, e_s, *, T):
    c = pl.program_id(1)

    @pl.when(c == 0)
    def _():
        c_s[...] = c0_ref[0]
        n_s[...] = n0_ref[0]
        m_s[...] = m0_ref[0]
        e_s[...] = e_ref[0]

    u = u_ref[...]
    ext = _rows(e_s[...], u)
    w = cw_ref[...]
    acc = cb_ref[...] + u * w[A_CONV - 1:A_CONV, :]
    for sh in range(1, A_CONV):
        acc = acc + pltpu.roll(ext, sh, 0)[SUBLANES:, :] * w[A_CONV - 1 - sh:A_CONV - sh, :]
    e_s[...] = u[T - SUBLANES:, :]
    qk_act = acc * _sigmoid(acc)

    gcol = gc_ref[...] + brow_ref[...]
    grow = gr_ref[0] + bcol_ref[...]
    lf_col = -_softplus(-gcol)
    ig_row = grow[:A_HEADS, :]
    lf_row = -_softplus(-grow[A_HEADS:, :])
    ri = lax.broadcasted_iota(jnp.int32, (T, T), 0)
    ci = lax.broadcasted_iota(jnp.int32, (T, T), 1)
    tri = ci <= ri
    b_col = _cumsum_rows(lf_col)
    b_row = _dot_exact_rhs(lf_row, jnp.where(ri <= ci, 1.0, 0.0))
    m_prev = m_s[...]
    lane = lax.broadcasted_iota(jnp.int32, (1, LANES), 1)
    heads = list(range(A_HEADS))
    bc = [b_col[:, A_HEADS + h:A_HEADS + h + 1] for h in heads]
    mh = [m_prev[:, A_HEADS + h:A_HEADS + h + 1] for h in heads]
    br = [b_row[h:h + 1, :] for h in heads]
    igr = [ig_row[h:h + 1, :] for h in heads]
    igc = [gcol[:, h:h + 1] for h in heads]
    qh = [qk_act[:, h * A_DK:(h + 1) * A_DK] for h in heads]
    kh = [qk_act[:, A_QK + h * A_DK:A_QK + (h + 1) * A_DK] * (A_DK ** -0.5) for h in heads]
    vh = [v_ref[:, h * A_DV:(h + 1) * A_DV] for h in heads]
    ch = [c_s[h] for h in heads]
    nh = [n_s[h:h + 1, :] for h in heads]
    qk = _each(_dot_nt, qh, kh)
    qc = _each(_dot, qh, ch)
    d = _each(lambda a, b, i: jnp.where(tri, a - b + i, -jnp.inf), bc, br, igr)
    inter = _each(jnp.add, bc, mh)
    m_t = _each(lambda i, x: jnp.maximum(i, jnp.max(x, axis=-1, keepdims=True)), inter, d)
    s = _each(lambda a, x, m: a * jnp.exp(x - m), qk, d, m_t)
    e_in = _each(lambda i, m: jnp.exp(i - m), inter, m_t)
    sv = _each(_dot, s, vh)
    num = _each(lambda a, x, b: a + x * b, sv, e_in, qc)
    den = _each(lambda a, x, qq, nn: jnp.sum(a, axis=-1, keepdims=True)
                + x * jnp.sum(qq * nn, axis=-1, keepdims=True), s, e_in, qh, nh)
    hh = _each(lambda a, b, m: a / jnp.maximum(jnp.abs(b), jnp.exp(-m)), num, den, m_t)
    hh = _each(lambda a: a * lax.rsqrt(jnp.mean(a * a, axis=-1, keepdims=True) + EPS), hh)
    b_end = _each(lambda a: a[T - 1:T, :], bc)
    m_new = _each(lambda be, m, b, i: jnp.maximum(be + m, jnp.max(be - b + i, axis=-1, keepdims=True)),
                  b_end, mh, br, igr)
    carry = _each(lambda be, m, mn: jnp.exp(be + m - mn), b_end, mh, m_new)
    wk = _each(lambda kk_, be, a, i, mn: kk_ * jnp.exp(be - a + i - mn), kh, b_end, bc, igc, m_new)
    kv = _each(_dot_tn, wk, vh)
    m_next = m_prev
    for h in heads:
        cols = slice(h * A_DV, (h + 1) * A_DV)
        gate = _sigmoid(o_ref[:, cols]) * _sigmoid(ga_ref[:, cols])
        h_ref[:, cols] = (hh[h] * nw_ref[:, cols] * gate).astype(h_ref.dtype)
        c_s[h] = carry[h] * ch[h] + kv[h]
        n_s[h:h + 1, :] = carry[h] * nh[h] + jnp.sum(wk[h], axis=0, keepdims=True)
        m_next = jnp.where(lane == A_HEADS + h, m_new[h], m_next)
    m_s[...] = m_next

    @pl.when(c == pl.num_programs(1) - 1)
    def _():
        cn_ref[0] = c_s[...]
        nn_ref[0] = n_s[...]
        mn_ref[0] = m_s[...]


def _mlstm(proj, gates_row, e_conv, conv_w, conv_b, bias_row, bias_col, norm_w, c0, n0, m0, *, T, n_seq, n_chunk):
    row = lambda s, c: (s * n_chunk + c, 0)
    fixed = lambda s, c: (0, 0)
    wide = lambda col: pl.BlockSpec((T, D_MODEL), lambda s, c: (s * n_chunk + c, col // D_MODEL))
    return pl.pallas_call(
        functools.partial(_mlstm_body, T=T),
        grid=(n_seq, n_chunk),
        in_specs=[wide(COL_CONV), wide(COL_V), wide(COL_O), wide(COL_GA),
                  pl.BlockSpec((T, LANES), lambda s, c: (s * n_chunk + c, (COL_TAIL + TAIL_IF) // LANES)),
                  pl.BlockSpec((1, 2 * A_HEADS, T), lambda s, c: (s * n_chunk + c, 0, 0)),
                  pl.BlockSpec((1, SUBLANES, CONV_COLS), lambda s, c: (s, 0, 0)),
                  pl.BlockSpec((A_CONV, CONV_COLS), fixed),
                  pl.BlockSpec((1, CONV_COLS), fixed),
                  pl.BlockSpec((1, LANES), fixed),
                  pl.BlockSpec((2 * A_HEADS, 1), fixed),
                  pl.BlockSpec((1, A_VW), fixed),
                  pl.BlockSpec((1, A_HEADS, A_DK, A_DV), lambda s, c: (s, 0, 0, 0)),
                  pl.BlockSpec((1, A_HEADS, A_DK), lambda s, c: (s, 0, 0)),
                  pl.BlockSpec((1, 1, LANES), lambda s, c: (s, 0, 0))],
        out_specs=[pl.BlockSpec((T, A_VW), row),
                   pl.BlockSpec((1, A_HEADS, A_DK, A_DV), lambda s, c: (s, 0, 0, 0)),
                   pl.BlockSpec((1, A_HEADS, A_DK), lambda s, c: (s, 0, 0)),
                   pl.BlockSpec((1, 1, LANES), lambda s, c: (s, 0, 0))],
        out_shape=[jax.ShapeDtypeStruct((n_seq * n_chunk * T, A_VW), BF16),
                   jax.ShapeDtypeStruct((n_seq, A_HEADS, A_DK, A_DV), F32),
                   jax.ShapeDtypeStruct((n_seq, A_HEADS, A_DK), F32),
                   jax.ShapeDtypeStruct((n_seq, 1, LANES), F32)],
        scratch_shapes=[pltpu.VMEM((A_HEADS, A_DK, A_DV), F32),
                        pltpu.VMEM((A_HEADS, A_DK), F32),
                        pltpu.VMEM((1, LANES), F32),
                        pltpu.VMEM((SUBLANES, CONV_COLS), F32)],
        compiler_params=_cparams(("arbitrary", "arbitrary"), 32),
        name="mlstm_chunk",
    )(proj, proj, proj, proj, proj, gates_row, e_conv, conv_w, conv_b, bias_row, bias_col, norm_w, c0, n0, m0)


def _kcat_dot(a1, b1, a2, b2):
    if a1.shape[1] % LANES == 0:
        return _dot(jnp.concatenate([a1, a2], axis=1), _rows(b1, b2))
    return _dot(a1, b1) + _dot(a2, b2)


def _head_sums(ones_blk, T, *xs):
    sums = jnp.dot(_rows(*xs).astype(BF16), ones_blk, preferred_element_type=F32)
    return [sums[i * T:(i + 1) * T] for i in range(len(xs))]


def _rwkv_body(ur_ref, uk_ref, uv_ref, ul_ref, gb_ref, er_ref, ek_ref, ev_ref, el_ref,
               mur_ref, muk_ref, muv_ref, mul_ref, w0_ref, w2_ref, a0_ref, a2_ref, g2_ref, kkw_ref, ka_ref,
               rk_ref, lnw_ref, lnb_ref, blk_ref, blkb_ref, s0_ref,
               hb_ref, sn_ref, s_s, cr_s, ck_s, cv_s, cl_s, *, T, n_pairs):
    c = pl.program_id(2)

    @pl.when(c == 0)
    def _():
        z = jnp.zeros((B_HEAD, B_HEAD), F32)
        for p in range(n_pairs):
            s_s[p] = _rows(jnp.concatenate([s0_ref[0, 2 * p], z], axis=1),
                           jnp.concatenate([z, s0_ref[0, 2 * p + 1]], axis=1))
        cr_s[...] = er_ref[0]
        ck_s[...] = ek_ref[0]
        cv_s[...] = ev_ref[0]
        cl_s[...] = el_ref[0]

    def shift_mix(u_ref, carry_ref, mu_ref, cs):
        u = u_ref[:, cs]
        prev = pltpu.roll(_rows(carry_ref[:, cs], u), 1, 0)[SUBLANES:, :]
        carry_ref[:, cs] = u[T - SUBLANES:, :]
        return u + mu_ref[:, cs] * (prev - u)

    xl = shift_mix(ul_ref, cl_s, mul_ref, slice(None))
    th_wd = jnp.tanh(xl[:, :LANES]).astype(BF16)
    x_ad = pltpu.roll(xl[:, :2 * LANES], 2 * LANES - LORA_DECAY, 1)[:, :LANES].astype(BF16)
    gd_off = LORA_DECAY + LORA_ICLR - TAIL_GD_WIN
    sg_gd = _sigmoid(pltpu.roll(xl[:, TAIL_GD_WIN:], TAIL_MIX - TAIL_GD_WIN - gd_off, 1)[:, :LORA_GATE]).astype(BF16)

    lo = lax.broadcasted_iota(jnp.int32, (1, LANES), 1) < B_HEAD
    head1 = lambda x: jnp.where(lo, x, jnp.zeros_like(x))
    head2 = lambda x: jnp.where(lo, jnp.zeros_like(x), x)
    r2 = lax.broadcasted_iota(jnp.int32, (T, 2 * T), 0)
    c2 = lax.broadcasted_iota(jnp.int32, (T, 2 * T), 1)
    left = c2 < T
    cj = jnp.where(left, c2, c2 - T)
    strict2 = cj < r2
    incl2 = cj <= r2
    eye_pair = jnp.where(cj == r2, 1.0, 0.0)
    ones_blk = blkb_ref[...]
    zero = jnp.zeros((T, LANES), BF16)
    aligned = (2 * T) % LANES == 0

    def block_diag(sb):
        z = jnp.zeros_like(sb)
        return _rows(jnp.where(left, sb, z), jnp.where(left, z, sb))

    def pair_program(p):
        cs = slice(p * LANES, (p + 1) * LANES)
        r = shift_mix(ur_ref, cr_s, mur_ref, cs)
        xk = shift_mix(uk_ref, ck_s, muk_ref, cs)
        v = shift_mix(uv_ref, cv_s, muv_ref, cs)
        lw = jnp.dot(th_wd, w2_ref[:, cs], preferred_element_type=F32)
        la = jnp.dot(x_ad, a2_ref[:, cs], preferred_element_type=F32)
        lg = jnp.dot(sg_gd, g2_ref[:, cs], preferred_element_type=F32)
        kkw = xk * kkw_ref[:, cs]
        ssq, = _head_sums(ones_blk, T, kkw * kkw)
        yield
        ld = DECAY_SCALE * _sigmoid(w0_ref[:, cs] + lw)
        a = _sigmoid(a0_ref[:, cs] + la)
        g = lg * _sigmoid(gb_ref[:, cs])
        k = xk * (1.0 + (a - 1.0) * ka_ref[:, cs])
        kk = kkw * lax.rsqrt(jnp.maximum(ssq, 1e-24))
        bi = kk * a
        cum = _cumsum_rows(ld)
        ewt = jnp.exp(cum[T - 1:T, :])
        Rt = (r * jnp.exp(cum)).astype(BF16)
        At = (-kk * jnp.exp(cum - ld)).astype(BF16)
        einv = jnp.exp(-cum)
        Bt = bi * einv
        Kt = k * einv
        BhKh = _rows(Bt * ewt, Kt * ewt).astype(BF16)
        Bt = Bt.astype(BF16)
        Kt = Kt.astype(BF16)
        vb = v.astype(BF16)
        V1 = head1(vb)
        V2 = head2(vb)
        if aligned:
            G = _dot_nt(_rows(head1(At), head1(Rt), head2(At), head2(Rt)), _rows(Bt, Kt))
            G1, G2 = G[:2 * T], G[2 * T:]
        else:
            G1 = _dot_nt(_rows(head1(At), head1(Rt)), _rows(Bt, Kt))
            G2 = _dot_nt(_rows(head2(At), head2(Rt)), _rows(Kt, Bt))
        yield
        LL1 = jnp.where(strict2, G1[:T], 0.0)
        LL2 = jnp.where(strict2, G2[:T], 0.0)
        P1 = jnp.where(incl2, G1[T:], 0.0)
        P2 = jnp.where(incl2, G2[T:], 0.0)
        lab2 = pltpu.roll(LL2, T, 1) if aligned else LL2
        pw = jnp.where(left, LL1, lab2).astype(BF16)
        iv = eye_pair + jnp.where(left, LL1, lab2)
        X0 = _kcat_dot(LL1, _rows(zero, V1), LL2, _rows(zero, V2) if aligned else _rows(V2, zero))
        pw = _dot(pw, block_diag(pw)).astype(BF16)
        yield
        span = 4
        while span < T:
            both = _dot(_rows(iv.astype(BF16), pw), block_diag(pw))
            iv, pw = iv + both[:T], both[T:].astype(BF16)
            span *= 2
            yield
        iv = (iv + _dot(iv, block_diag(pw))).astype(BF16)
        S = s_s[p]
        AR = _dot_nt(_rows(At, Rt), S)
        yield
        X = (AR[:T] + X0).astype(BF16)
        U = _dot(iv, _rows(head1(X), head2(X))).astype(BF16)
        yield
        if aligned:
            rhs2 = _rows(head2(U), V2)
        else:
            rhs2 = _rows(V2, head2(U))
        o = AR[T:] + _kcat_dot(P1, _rows(head1(U), V1), P2, rhs2)
        s_s[p] = S * ewt + blk_ref[...] * _dot_tn(_rows(U, vb), BhKh)
        yield
        o_sum, rk_sum = _head_sums(ones_blk, T, o, r * k * rk_ref[:, cs])
        yield
        oc = o - o_sum * (1.0 / B_HEAD)
        var, = _head_sums(ones_blk, T, oc * oc)
        yield
        on = oc * lax.rsqrt(var * (1.0 / B_HEAD) + GN_EPS) * lnw_ref[:, cs] + lnb_ref[:, cs]
        hb_ref[:, cs] = ((on + rk_sum * v) * g).astype(hb_ref.dtype)

    _run_staggered([pair_program(p) for p in range(n_pairs)], RWKV_WAVE, RWKV_LAG)

    @pl.when(c == pl.num_programs(2) - 1)
    def _():
        for p in range(n_pairs):
            sn_ref[0, 2 * p] = s_s[p, :B_HEAD, :B_HEAD]
            sn_ref[0, 2 * p + 1] = s_s[p, B_HEAD:, B_HEAD:]


def _rwkv(proj, e_main, e_tail, mu_main, mu_tail, w0, w2p, a0, a2p, g2, k_k, k_a, rk, lnw, lnb, blk, s0, *,
          T, n_seq, n_chunk):
    assert T <= RWKV_CHUNK
    gw = GROUP_W
    wide = lambda col: pl.BlockSpec((T, gw), lambda s, pg, c: (s * n_chunk + c, col // gw + pg))
    edge = lambda col: pl.BlockSpec((1, SUBLANES, gw), lambda s, pg, c: (s, 0, col // gw + pg))
    vec = lambda col: pl.BlockSpec((1, gw), lambda s, pg, c: (0, col // gw + pg))
    mat = lambda rows: pl.BlockSpec((rows, gw), lambda s, pg, c: (0, pg))
    st = pl.BlockSpec((1, 2 * GROUP_PAIRS, B_HEAD, B_HEAD), lambda s, pg, c: (s, pg, 0, 0))
    return pl.pallas_call(
        functools.partial(_rwkv_body, T=T, n_pairs=GROUP_PAIRS),
        grid=(n_seq, N_PAIRS // GROUP_PAIRS, n_chunk),
        in_specs=[wide(COL_R), wide(COL_K), wide(COL_VB),
                  pl.BlockSpec((T, TAIL_MIX), lambda s, pg, c: (s * n_chunk + c, COL_TAIL // TAIL_MIX)),
                  wide(COL_GB),
                  edge(0), edge(B_W), edge(2 * B_W),
                  pl.BlockSpec((1, SUBLANES, TAIL_MIX), lambda s, pg, c: (s, 0, 0)),
                  vec(0), vec(B_W), vec(2 * B_W),
                  pl.BlockSpec((1, TAIL_MIX), lambda s, pg, c: (0, 0)),
                  vec(0), mat(LANES), vec(0), mat(LANES), mat(LORA_GATE), vec(0), vec(0),
                  vec(0), vec(0), vec(0),
                  pl.BlockSpec((LANES, LANES), lambda s, pg, c: (0, 0)),
                  pl.BlockSpec((LANES, LANES), lambda s, pg, c: (0, 0)), st],
        out_specs=[pl.BlockSpec((T, gw), lambda s, pg, c: (s * n_chunk + c, pg)), st],
        out_shape=[jax.ShapeDtypeStruct((n_seq * n_chunk * T, B_W), BF16),
                   jax.ShapeDtypeStruct((n_seq, B_HEADS, B_HEAD, B_HEAD), F32)],
        scratch_shapes=[pltpu.VMEM((GROUP_PAIRS, LANES, LANES), F32),
                        pltpu.VMEM((SUBLANES, gw), F32), pltpu.VMEM((SUBLANES, gw), F32),
                        pltpu.VMEM((SUBLANES, gw), F32), pltpu.VMEM((SUBLANES, TAIL_MIX), F32)],
        compiler_params=_cparams(("arbitrary", "arbitrary", "arbitrary"), 32),
        name="rwkv_chunk",
    )(proj, proj, proj, proj, proj, e_main, e_main, e_main, e_tail, mu_main, mu_main, mu_main, mu_tail,
      w0, w2p, a0, a2p, g2, k_k, k_a, rk, lnw, lnb, blk, blk.astype(BF16), s0)


def _merge_out_body(ha_ref, hb_ref, x_ref, w_ref, g_ref, y_ref):
    mixed = ha_ref[...].astype(F32) + hb_ref[...].astype(F32)
    y = jnp.dot(mixed.astype(BF16), w_ref[...], preferred_element_type=F32)
    ms = jnp.mean(y * y, axis=-1, keepdims=True)
    y_ref[...] = x_ref[...] + y * lax.rsqrt(ms + EPS) * g_ref[...]


def _merge_out(h_a, h_b, x, w_out, g_post):
    n = x.shape[0]
    wide = pl.BlockSpec((MERGE_TM, D_MODEL), lambda i: (i, 0))
    return pl.pallas_call(
        _merge_out_body,
        grid=(n // MERGE_TM,),
        in_specs=[wide, wide, wide,
                  pl.BlockSpec((D_MODEL, D_MODEL), lambda i: (0, 0)),
                  pl.BlockSpec((1, D_MODEL), lambda i: (0, 0))],
        out_specs=wide,
        out_shape=jax.ShapeDtypeStruct((n, D_MODEL), F32),
        compiler_params=_cparams(("parallel",), 48),
        name="merge_out_proj",
    )(h_a, h_b, x, w_out, g_post)


def _mlp_body(x_ref, g1_ref, wu_ref, wd_ref, g2_ref, y_ref, hn_ref, acc_ref, *, row_chunk):
    j = pl.program_id(1)

    @pl.when(j == 0)
    def _():
        _norm_rows(x_ref, g1_ref, hn_ref, row_chunk)
        acc_ref[...] = jnp.zeros_like(acc_ref)

    u = jnp.dot(hn_ref[...], wu_ref[...], preferred_element_type=F32)
    a = jnp.square(jnp.maximum(u, 0.0))
    acc_ref[...] += jnp.dot(a.astype(BF16), wd_ref[...], preferred_element_type=F32)

    @pl.when(j == pl.num_programs(1) - 1)
    def _():
        def body(c, carry):
            rows = pl.ds(pl.multiple_of(c * row_chunk, row_chunk), row_chunk)
            ff = acc_ref[rows, :]
            ms = jnp.mean(ff * ff, axis=-1, keepdims=True)
            y_ref[rows, :] = x_ref[rows, :] + ff * lax.rsqrt(ms + EPS) * g2_ref[...]
            return carry
        lax.fori_loop(0, x_ref.shape[0] // row_chunk, body, 0)


def _mlp(x, g_pre, w_up, w_down, g_post):
    n, d = x.shape
    f = w_up.shape[1]
    return pl.pallas_call(
        functools.partial(_mlp_body, row_chunk=LANES),
        grid=(n // MLP_TM, f // MLP_TF),
        in_specs=[pl.BlockSpec((MLP_TM, d), lambda i, j: (i, 0)),
                  pl.BlockSpec((1, d), lambda i, j: (0, 0)),
                  pl.BlockSpec((d, MLP_TF), lambda i, j: (0, j)),
                  pl.BlockSpec((MLP_TF, d), lambda i, j: (j, 0)),
                  pl.BlockSpec((1, d), lambda i, j: (0, 0))],
        out_specs=pl.BlockSpec((MLP_TM, d), lambda i, j: (i, 0)),
        out_shape=jax.ShapeDtypeStruct((n, d), F32),
        scratch_shapes=[pltpu.VMEM((MLP_TM, d), BF16), pltpu.VMEM((MLP_TM, d), F32)],
        compiler_params=_cparams(("parallel", "arbitrary"), 56),
        name="mlp",
    )(x, g_pre, w_up, w_down, g_post)


def _pad_cols(a, width):
    return jnp.pad(a, [(0, 0)] * (a.ndim - 1) + [(0, width - a.shape[-1])])


def _pad_rows(a, before, total):
    return jnp.pad(a, ((before, total - before - a.shape[0]), (0, 0)))


def _edge_blocks(prev_rows):
    return jnp.pad(prev_rows, ((0, 0), (SUBLANES - prev_rows.shape[1], 0), (0, 0)))


def _gates_row_form(proj, T):
    g = proj[:, COL_TAIL + TAIL_IF:COL_TAIL + TAIL_IF + 2 * A_HEADS]
    return jnp.swapaxes(g.reshape(-1, T, 2 * A_HEADS), 1, 2)


def kernel(x_prompt, x_sample, state_mlstm_C, state_mlstm_n, state_mlstm_m, state_mlstm_conv, state_rwkv_S, state_rwkv_shift, meta_tokens, norm_mix_pre, norm_mix_post, norm_mlp_pre, norm_mlp_post, w_in, mlstm_conv_w, mlstm_conv_b, mlstm_b_i, mlstm_b_f, mlstm_norm_w, rwkv_mu, rwkv_w0, rwkv_w2, rwkv_a0, rwkv_a2, rwkv_g2, rwkv_k_k, rwkv_k_a, rwkv_r_k, rwkv_ln_w, rwkv_ln_b, w_out, w_up, w_down):
    n_s_seq, s_len, _ = x_sample.shape
    _, p_len, _ = x_prompt.shape
    k3 = A_CONV - 1

    w_all = _relayout_weight(jnp.swapaxes(w_in[0], 0, 1))
    mu = rwkv_mu[0][None, :]
    mu_main = mu[:, :3 * B_W]
    mu_tail = _pad_cols(mu[:, 3 * B_W:], TAIL_MIX)
    w2p = _pad_rows(rwkv_w2[0], 0, LANES).astype(BF16)
    a2p = _pad_rows(rwkv_a2[0], 0, LANES).astype(BF16)
    g2 = rwkv_g2[0].astype(BF16)
    g_pre = norm_mix_pre[0][None, :]
    bias = jnp.concatenate([mlstm_b_i[0], mlstm_b_f[0]])
    bias_row = _pad_cols(bias[None, :], LANES)
    bias_col = bias[:, None]
    norm_w = mlstm_norm_w[0][None, :]
    conv_w = mlstm_conv_w[0]
    conv_b = mlstm_conv_b[0][None, :]
    rwkv_vecs = (rwkv_w0[0][None, :], w2p, rwkv_a0[0][None, :], a2p, g2, rwkv_k_k[0][None, :],
                 rwkv_k_a[0][None, :], rwkv_r_k[0].reshape(1, B_W), rwkv_ln_w[0][None, :], rwkv_ln_b[0][None, :],
                 _head_block_mask())
    w_out_b = w_out[0].astype(BF16)
    w_up_b = w_up[0].astype(BF16)
    w_down_b = w_down[0].astype(BF16)

    def branches(x_rows, tm, T, n_seq, n_chunk, e_conv, e_main, e_tail, c0, n0, m0, s0):
        proj = _norm_matmul(x_rows, g_pre, w_all, tm)
        tm_ = MLSTM_CHUNK if n_chunk > 1 else T
        h_a, c_n, n_n, m_n = _mlstm(proj, _gates_row_form(proj, tm_), e_conv, conv_w, conv_b, bias_row, bias_col,
                                    norm_w, c0, n0, m0, T=tm_, n_seq=n_seq, n_chunk=n_chunk * T // tm_)
        h_b, s_n = _rwkv(proj, e_main, e_tail, mu_main, mu_tail, *rwkv_vecs, s0,
                         T=T, n_seq=n_seq, n_chunk=n_chunk)
        return proj, h_a, h_b, (c_n, n_n, m_n, s_n)

    def finish(h_a, h_b, x_rows):
        x1 = _merge_out(h_a, h_b, x_rows, w_out_b, norm_mix_post[0][None, :])
        return _mlp(x1, norm_mlp_pre[0][None, :], w_up_b, w_down_b, norm_mlp_post[0][None, :])

    zeros = lambda *shape: jnp.zeros(shape, F32)
    proj_m, _, _, (c_m, n_m, m_m, s_m) = branches(
        meta_tokens.astype(F32), N_META, N_META, 1, 1,
        zeros(1, SUBLANES, CONV_COLS), zeros(1, SUBLANES, 3 * B_W), zeros(1, SUBLANES, TAIL_MIX),
        zeros(1, A_HEADS, A_DK, A_DV), zeros(1, A_HEADS, A_DK), zeros(1, 1, LANES),
        zeros(1, B_HEADS, B_HEAD, B_HEAD))

    xp = x_prompt[0].astype(F32)
    proj_p, ha_p, hb_p, (c_p, n_p, m_p, s_p) = branches(
        xp, PROJ_TM, RWKV_CHUNK, 1, p_len // RWKV_CHUNK,
        _edge_blocks(proj_m[None, N_META - k3:, COL_CONV:COL_CONV + CONV_COLS]),
        _edge_blocks(proj_m[None, N_META - 1:, COL_R:COL_R + 3 * B_W]),
        _edge_blocks(proj_m[None, N_META - 1:, COL_TAIL:COL_TAIL + TAIL_MIX]),
        c_m, n_m, m_m, s_m)
    y_prompt = finish(ha_p, hb_p, xp)[None].astype(x_prompt.dtype)

    xs = x_sample.reshape(n_s_seq * s_len, D_MODEL).astype(F32)
    sh = state_rwkv_shift[0].astype(F32)
    m0_s = jnp.pad(state_mlstm_m[0].astype(F32), ((0, 0), (A_HEADS, LANES - 2 * A_HEADS)))[:, None, :]
    proj_s, ha_s, hb_s, (c_s, n_s, m_s, s_s) = branches(
        xs, PROJ_TM, s_len, n_s_seq, 1,
        _edge_blocks(state_mlstm_conv[0].astype(F32)),
        _edge_blocks(sh[:, :, :3 * B_W]), _edge_blocks(_pad_cols(sh[:, :, 3 * B_W:], TAIL_MIX)),
        state_mlstm_C[0].astype(F32), state_mlstm_n[0].astype(F32), m0_s,
        state_rwkv_S[0].astype(F32))
    y_sample = finish(ha_s, hb_s, xs).reshape(x_sample.shape).astype(x_sample.dtype)

    def shift_state(rows):
        return jnp.concatenate([rows[..., COL_R:COL_R + 3 * B_W],
                                rows[..., COL_TAIL:COL_TAIL + LORA_COLS]], axis=-1)

    last_s = proj_s.reshape(n_s_seq, s_len, PROJ_COLS)[:, s_len - k3:, :]
    conv_s = last_s[..., COL_CONV:COL_CONV + CONV_COLS]
    shift_s = shift_state(last_s[:, k3 - 1:, :])
    last_p = proj_p[p_len - k3:, :]
    conv_p = last_p[None, :, COL_CONV:COL_CONV + CONV_COLS]
    shift_p = shift_state(last_p[None, k3 - 1:, :])
    dt_c, dt_n, dt_m = state_mlstm_C.dtype, state_mlstm_n.dtype, state_mlstm_m.dtype
    dt_cv, dt_s, dt_sh = state_mlstm_conv.dtype, state_rwkv_S.dtype, state_rwkv_shift.dtype
    lanes_m = slice(A_HEADS, 2 * A_HEADS)
    return (y_prompt, y_sample,
            c_p[None].astype(dt_c), n_p[None].astype(dt_n), m_p[:, 0, lanes_m][None].astype(dt_m),
            conv_p[None].astype(dt_cv), s_p[None].astype(dt_s), shift_p[None].astype(dt_sh),
            c_s[None].astype(dt_c), n_s[None].astype(dt_n), m_s[:, 0, lanes_m][None].astype(dt_m),
            conv_s[None].astype(dt_cv), s_s[None].astype(dt_s), shift_s[None].astype(dt_sh))
```

```python
import functools

import jax
import jax.numpy as jnp
from jax import lax
from jax.experimental import pallas as pl
from jax.experimental.pallas import tpu as pltpu

F32 = jnp.float32
BF16 = jnp.bfloat16

D_MODEL = 2048
N_META = 16
EPS = 1e-6
D_FF = 4 * D_MODEL
A_HEADS = 8
A_DK = 128
A_DV = D_MODEL // A_HEADS
A_QK = A_HEADS * A_DK
A_VW = A_HEADS * A_DV
A_CONV = 4
MLSTM_CHUNK = 256
B_HEAD = 64
B_HEADS = D_MODEL // B_HEAD
B_W = B_HEADS * B_HEAD
LORA_DECAY = max(32, int(round(1.8 * D_MODEL ** 0.5 / 32)) * 32)
LORA_ICLR = max(32, int(round(1.8 * D_MODEL ** 0.5 / 32)) * 32)
LORA_GATE = max(32, int(round(0.6 * D_MODEL ** 0.8 / 32)) * 32)
GN_EPS = 64e-5
CONV_COLS = 2 * A_QK
SHIFT_COLS = 3 * B_W + LORA_DECAY + LORA_ICLR + LORA_GATE

LANES = 128
SUBLANES = 8

COL_CONV, COL_V, COL_O, COL_R, COL_K, COL_VB, COL_GA, COL_GB = (i * D_MODEL for i in range(8))
COL_TAIL = 8 * D_MODEL
LORA_COLS = LORA_DECAY + LORA_ICLR + LORA_GATE
TAIL_MIX = -(-LORA_COLS // LANES) * LANES
TAIL_IF = TAIL_MIX
TAIL_GD_WIN = (LORA_DECAY + LORA_ICLR) // LANES * LANES
PROJ_TN = 1024
PROJ_COLS = -(-(COL_TAIL + TAIL_IF + LANES) // PROJ_TN) * PROJ_TN
PROJ_TM = 1024
MERGE_TM = 512
MLP_TM, MLP_TF = 512, 1024
DECAY_SCALE = -0.6065306597126334
RWKV_CHUNK = 64
N_PAIRS = B_HEADS // 2
GROUP_PAIRS = 16
RWKV_WAVE, RWKV_LAG = 8, 1
GROUP_W = GROUP_PAIRS * LANES


def _cparams(semantics, vmem_mib):
    return pltpu.CompilerParams(dimension_semantics=semantics, vmem_limit_bytes=vmem_mib << 20)


def _softplus(x):
    return jnp.maximum(x, 0.0) + jnp.log1p(jnp.exp(-jnp.abs(x)))


def _dot(a, b):
    return jnp.dot(a.astype(BF16), b.astype(BF16), preferred_element_type=F32)


def _dot_nt(a, b):
    return lax.dot_general(a.astype(BF16), b.astype(BF16), (((1,), (1,)), ((), ())),
                           preferred_element_type=F32)


def _dot_tn(a, b):
    return lax.dot_general(a.astype(BF16), b.astype(BF16), (((0,), (0,)), ((), ())),
                           preferred_element_type=F32)


def _each(f, *lists):
    return [f(*args) for args in zip(*lists)]


def _run_staggered(programs, wave, lag):
    live = list(enumerate(programs))
    slot = 0
    while live:
        still = []
        for p, prog in live:
            if slot >= (p // wave) * lag:
                try:
                    next(prog)
                except StopIteration:
                    continue
            still.append((p, prog))
        live = still
        slot += 1


def _rows(*parts):
    return jnp.concatenate(parts, axis=0)


def _split_bf16(x, pieces):
    out = []
    for _ in range(pieces - 1):
        p = x.astype(BF16)
        out.append(p)
        x = x - p.astype(F32)
    out.append(x.astype(BF16))
    return out


def _dot_exact_rhs(x, m, pieces=3):
    mb = m.astype(BF16)
    acc = None
    for p in _split_bf16(x, pieces):
        t = jnp.dot(p, mb, preferred_element_type=F32)
        acc = t if acc is None else acc + t
    return acc


def _sigmoid(x):
    return 0.5 * jnp.tanh(0.5 * x) + 0.5


def _cumsum_rows(x):
    n = x.shape[0]
    row = lax.broadcasted_iota(jnp.int32, x.shape, 0)
    step = 1
    while step < n:
        x = x + jnp.where(row >= step, pltpu.roll(x, step, 0), 0.0)
        step *= 2
    return x


def _head_block_mask():
    a = lax.broadcasted_iota(jnp.int32, (LANES, LANES), 0) // B_HEAD
    b = lax.broadcasted_iota(jnp.int32, (LANES, LANES), 1) // B_HEAD
    return jnp.where(a == b, 1.0, 0.0).astype(F32)


SRC_IF = CONV_COLS + 2 * A_VW
SRC_SHIFT = SRC_IF + 2 * A_HEADS
SRC_GATE = SRC_SHIFT + SHIFT_COLS
SRC_COLS = SRC_GATE + 2 * D_MODEL
RELAYOUT_ROWS = TAIL_MIX


def _relayout_src_row(t):
    unit = 2 * SUBLANES
    dst = t * (RELAYOUT_ROWS // unit)
    src = jnp.where(dst < COL_R // unit, dst,
                    jnp.where(dst < COL_GA // unit, dst + (SRC_SHIFT - COL_R) // unit,
                              jnp.where(dst < COL_TAIL // unit, dst + (SRC_GATE - COL_GA) // unit,
                                        jnp.where(dst < (COL_TAIL + TAIL_MIX) // unit,
                                                  (SRC_SHIFT + 3 * B_W) // unit, SRC_IF // unit))))
    return src * unit


def _relayout_body(w_ref, o_ref):
    o_ref[...] = w_ref[...].astype(BF16)


def _relayout_weight(wt):
    d = wt.shape[1]
    assert wt.shape[0] == SRC_COLS and PROJ_COLS == COL_TAIL + 2 * RELAYOUT_ROWS and COL_R == SRC_IF
    assert SRC_SHIFT + 3 * B_W + RELAYOUT_ROWS <= SRC_COLS and SRC_SHIFT % (2 * SUBLANES) == 0
    return pl.pallas_call(
        _relayout_body,
        grid=(PROJ_COLS // RELAYOUT_ROWS,),
        in_specs=[pl.BlockSpec((pl.Element(RELAYOUT_ROWS), pl.Element(d)), lambda t: (_relayout_src_row(t), 0))],
        out_specs=pl.BlockSpec((RELAYOUT_ROWS, d), lambda t: (t, 0)),
        out_shape=jax.ShapeDtypeStruct((PROJ_COLS, d), BF16),
        compiler_params=_cparams(("parallel",), 32),
        name="relayout_w_in",
    )(wt)


def _norm_rows(x_ref, g_ref, out_ref, row_chunk):
    def body(c, carry):
        rows = pl.ds(pl.multiple_of(c * row_chunk, row_chunk), row_chunk)
        x = x_ref[rows, :]
        ms = jnp.mean(x * x, axis=-1, keepdims=True)
        out_ref[rows, :] = (x * lax.rsqrt(ms + EPS) * g_ref[...]).astype(out_ref.dtype)
        return carry
    lax.fori_loop(0, x_ref.shape[0] // row_chunk, body, 0)


def _norm_matmul_body(x_ref, g_ref, w_ref, o_ref, hn_ref, *, row_chunk):
    @pl.when(pl.program_id(1) == 0)
    def _():
        _norm_rows(x_ref, g_ref, hn_ref, row_chunk)

    o_ref[...] = lax.dot_general(hn_ref[...], w_ref[...], (((1,), (1,)), ((), ())), preferred_element_type=F32)


def _norm_matmul(x, g, w, tm):
    n, d = x.shape
    p = w.shape[0]
    return pl.pallas_call(
        functools.partial(_norm_matmul_body, row_chunk=min(tm, LANES)),
        grid=(n // tm, p // PROJ_TN),
        in_specs=[pl.BlockSpec((tm, d), lambda i, j: (i, 0)),
                  pl.BlockSpec((1, d), lambda i, j: (0, 0)),
                  pl.BlockSpec((PROJ_TN, d), lambda i, j: (j, 0))],
        out_specs=pl.BlockSpec((tm, PROJ_TN), lambda i, j: (i, j)),
        out_shape=jax.ShapeDtypeStruct((n, p), F32),
        scratch_shapes=[pltpu.VMEM((tm, d), BF16)],
        compiler_params=_cparams(("parallel", "arbitrary"), 48),
        name="in_proj",
    )(x, g, w)


def _norm_matmul_pair_body(x_ref, xs_ref, g_ref, w_ref, o_ref, os_ref, hn_ref, hns_ref, *, row_chunk):
    @pl.when(pl.program_id(0) == 0)
    def _():
        _norm_rows(x_ref, g_ref, hn_ref, row_chunk)
        _norm_rows(xs_ref, g_ref, hns_ref, xs_ref.shape[0])

    dims = (((1,), (1,)), ((), ()))
    o_ref[...] = lax.dot_general(hn_ref[...], w_ref[...], dims, preferred_element_type=F32)
    os_ref[...] = lax.dot_general(hns_ref[...], w_ref[...], dims, preferred_element_type=F32)


def _norm_matmul_pair(x, x_small, g, w):
    n, d = x.shape
    ns = x_small.shape[0]
    p = w.shape[0]
    return pl.pallas_call(
        functools.partial(_norm_matmul_pair_body, row_chunk=LANES),
        grid=(p // PROJ_TN,),
        in_specs=[pl.BlockSpec((n, d), lambda j: (0, 0)),
                  pl.BlockSpec((ns, d), lambda j: (0, 0)),
                  pl.BlockSpec((1, d), lambda j: (0, 0)),
                  pl.BlockSpec((PROJ_TN, d), lambda j: (j, 0))],
        out_specs=[pl.BlockSpec((n, PROJ_TN), lambda j: (0, j)),
                   pl.BlockSpec((ns, PROJ_TN), lambda j: (0, j))],
        out_shape=[jax.ShapeDtypeStruct((n, p), F32), jax.ShapeDtypeStruct((ns, p), F32)],
        scratch_shapes=[pltpu.VMEM((n, d), BF16), pltpu.VMEM((ns, d), BF16)],
        compiler_params=_cparams(("arbitrary",), 48),
        name="in_proj_pair",
    )(x, x_small, g, w)


def _mlstm_body(u_ref, v_ref, o_ref, ga_ref, gc_ref, gr_ref, e_ref, cw_ref, cb_ref, brow_ref, bcol_ref,
                nw_ref, c0_ref, n0_ref, m0_ref,
                h_ref, cn_ref, nn_ref, mn_ref, c_s, n_s, m_s, e_s, *, T):
    c = pl.program_id(1)

    @pl.when(c == 0)
    def _():
        c_s[...] = c0_ref[0]
        n_s[...] = n0_ref[0]
        m_s[...] = m0_ref[0]
        e_s[...] = e_ref[0]

    u = u_ref[...]
    ext = _rows(e_s[...], u)
    w = cw_ref[...]
    acc = cb_ref[...] + u * w[A_CONV - 1:A_CONV, :]
    for sh in range(1, A_CONV):
        acc = acc + pltpu.roll(ext, sh, 0)[SUBLANES:, :] * w[A_CONV - 1 - sh:A_CONV - sh, :]
    e_s[...] = u[T - SUBLANES:, :]
    qk_act = acc * _sigmoid(acc)

    gcol = gc_ref[...] + brow_ref[...]
    grow = gr_ref[0] + bcol_ref[...]
    lf_col = -_softplus(-gcol)
    ig_row = grow[:A_HEADS, :]
    lf_row = -_softplus(-grow[A_HEADS:, :])
    ri = lax.broadcasted_iota(jnp.int32, (T, T), 0)
    ci = lax.broadcasted_iota(jnp.int32, (T, T), 1)
    tri = ci <= ri
    b_col = _cumsum_rows(lf_col)
    b_row = _dot_exact_rhs(lf_row, jnp.where(ri <= ci, 1.0, 0.0))
    m_prev = m_s[...]
    lane = lax.broadcasted_iota(jnp.int32, (1, LANES), 1)
    heads = list(range(A_HEADS))
    bc = [b_col[:, A_HEADS + h:A_HEADS + h + 1] for h in heads]
    mh = [m_prev[:, A_HEADS + h:A_HEADS + h + 1] for h in heads]
    br = [b_row[h:h + 1, :] for h in heads]
    igr = [ig_row[h:h + 1, :] for h in heads]
    igc = [gcol[:, h:h + 1] for h in heads]
    qh = [qk_act[:, h * A_DK:(h + 1) * A_DK] for h in heads]
    kh = [qk_act[:, A_QK + h * A_DK:A_QK + (h + 1) * A_DK] * (A_DK ** -0.5) for h in heads]
    vh = [v_ref[:, h * A_DV:(h + 1) * A_DV] for h in heads]
    ch = [c_s[h] for h in heads]
    nh = [n_s[h:h + 1, :] for h in heads]
    qk = _each(_dot_nt, qh, kh)
    qc = _each(_dot, qh, ch)
    d = _each(lambda a, b, i: jnp.where(tri, a - b + i, -jnp.inf), bc, br, igr)
    inter = _each(jnp.add, bc, mh)
    m_t = _each(lambda i, x: jnp.maximum(i, jnp.max(x, axis=-1, keepdims=True)), inter, d)
    s = _each(lambda a, x, m: a * jnp.exp(x - m), qk, d, m_t)
    e_in = _each(lambda i, m: jnp.exp(i - m), inter, m_t)
    sv = _each(_dot, s, vh)
    num = _each(lambda a, x, b: a + x * b, sv, e_in, qc)
    den = _each(lambda a, x, qq, nn: jnp.sum(a, axis=-1, keepdims=True)
                + x * jnp.sum(qq * nn, axis=-1, keepdims=True), s, e_in, qh, nh)
    hh = _each(lambda a, b, m: a / jnp.maximum(jnp.abs(b), jnp.exp(-m)), num, den, m_t)
    hh = _each(lambda a: a * lax.rsqrt(jnp.mean(a * a, axis=-1, keepdims=True) + EPS), hh)
    b_end = _each(lambda a: a[T - 1:T, :], bc)
    m_new = _each(lambda be, m, b, i: jnp.maximum(be + m, jnp.max(be - b + i, axis=-1, keepdims=True)),
                  b_end, mh, br, igr)
    carry = _each(lambda be, m, mn: jnp.exp(be + m - mn), b_end, mh, m_new)
    wk = _each(lambda kk_, be, a, i, mn: kk_ * jnp.exp(be - a + i - mn), kh, b_end, bc, igc, m_new)
    kv = _each(_dot_tn, wk, vh)
    m_next = m_prev
    for h in heads:
        cols = slice(h * A_DV, (h + 1) * A_DV)
        gate = _sigmoid(o_ref[:, cols]) * _sigmoid(ga_ref[:, cols])
        h_ref[:, cols] = (hh[h] * nw_ref[:, cols] * gate).astype(h_ref.dtype)
        c_s[h] = carry[h] * ch[h] + kv[h]
        n_s[h:h + 1, :] = carry[h] * nh[h] + jnp.sum(wk[h], axis=0, keepdims=True)
        m_next = jnp.where(lane == A_HEADS + h, m_new[h], m_next)
    m_s[...] = m_next

    @pl.when(c == pl.num_programs(1) - 1)
    def _():
        cn_ref[0] = c_s[...]
        nn_ref[0] = n_s[...]
        mn_ref[0] = m_s[...]


def _mlstm(proj, gates_row, e_conv, conv_w, conv_b, bias_row, bias_col, norm_w, c0, n0, m0, *, T, n_seq, n_chunk):
    row = lambda s, c: (s * n_chunk + c, 0)
    fixed = lambda s, c: (0, 0)
    wide = lambda col: pl.BlockSpec((T, D_MODEL), lambda s, c: (s * n_chunk + c, col // D_MODEL))
    return pl.pallas_call(
        functools.partial(_mlstm_body, T=T),
        grid=(n_seq, n_chunk),
        in_specs=[wide(COL_CONV), wide(COL_V), wide(COL_O), wide(COL_GA),
                  pl.BlockSpec((T, LANES), lambda s, c: (s * n_chunk + c, (COL_TAIL + TAIL_IF) // LANES)),
                  pl.BlockSpec((1, 2 * A_HEADS, T), lambda s, c: (s * n_chunk + c, 0, 0)),
                  pl.BlockSpec((1, SUBLANES, CONV_COLS), lambda s, c: (s, 0, 0)),
                  pl.BlockSpec((A_CONV, CONV_COLS), fixed),
                  pl.BlockSpec((1, CONV_COLS), fixed),
                  pl.BlockSpec((1, LANES), fixed),
                  pl.BlockSpec((2 * A_HEADS, 1), fixed),
                  pl.BlockSpec((1, A_VW), fixed),
                  pl.BlockSpec((1, A_HEADS, A_DK, A_DV), lambda s, c: (s, 0, 0, 0)),
                  pl.BlockSpec((1, A_HEADS, A_DK), lambda s, c: (s, 0, 0)),
                  pl.BlockSpec((1, 1, LANES), lambda s, c: (s, 0, 0))],
        out_specs=[pl.BlockSpec((T, A_VW), row),
                   pl.BlockSpec((1, A_HEADS, A_DK, A_DV), lambda s, c: (s, 0, 0, 0)),
                   pl.BlockSpec((1, A_HEADS, A_DK), lambda s, c: (s, 0, 0)),
                   pl.BlockSpec((1, 1, LANES), lambda s, c: (s, 0, 0))],
        out_shape=[jax.ShapeDtypeStruct((n_seq * n_chunk * T, A_VW), BF16),
                   jax.ShapeDtypeStruct((n_seq, A_HEADS, A_DK, A_DV), F32),
                   jax.ShapeDtypeStruct((n_seq, A_HEADS, A_DK), F32),
                   jax.ShapeDtypeStruct((n_seq, 1, LANES), F32)],
        scratch_shapes=[pltpu.VMEM((A_HEADS, A_DK, A_DV), F32),
                        pltpu.VMEM((A_HEADS, A_DK), F32),
                        pltpu.VMEM((1, LANES), F32),
                        pltpu.VMEM((SUBLANES, CONV_COLS), F32)],
        compiler_params=_cparams(("arbitrary", "arbitrary"), 32),
        name="mlstm_chunk",
    )(proj, proj, proj, proj, proj, gates_row, e_conv, conv_w, conv_b, bias_row, bias_col, norm_w, c0, n0, m0)


def _kcat_dot(a1, b1, a2, b2):
    if a1.shape[1] % LANES == 0:
        return _dot(jnp.concatenate([a1, a2], axis=1), _rows(b1, b2))
    return _dot(a1, b1) + _dot(a2, b2)


def _head_sums(ones_blk, T, *xs):
    sums = jnp.dot(_rows(*xs).astype(BF16), ones_blk, preferred_element_type=F32)
    return [sums[i * T:(i + 1) * T] for i in range(len(xs))]


def _rwkv_body(ur_ref, uk_ref, uv_ref, ul_ref, gb_ref, er_ref, ek_ref, ev_ref, el_ref,
               mur_ref, muk_ref, muv_ref, mul_ref, w0_ref, w2_ref, a0_ref, a2_ref, g2_ref, kkw_ref, ka_ref,
               rk_ref, lnw_ref, lnb_ref, blk_ref, blkb_ref, s0_ref,
               hb_ref, sn_ref, s_s, cr_s, ck_s, cv_s, cl_s, *, T, n_pairs):
    c = pl.program_id(2)

    @pl.when(c == 0)
    def _():
        z = jnp.zeros((B_HEAD, B_HEAD), F32)
        for p in range(n_pairs):
            s_s[p] = _rows(jnp.concatenate([s0_ref[0, 2 * p], z], axis=1),
                           jnp.concatenate([z, s0_ref[0, 2 * p + 1]], axis=1))
        cr_s[...] = er_ref[0]
        ck_s[...] = ek_ref[0]
        cv_s[...] = ev_ref[0]
        cl_s[...] = el_ref[0]

    def shift_mix(u_ref, carry_ref, mu_ref, cs):
        u = u_ref[:, cs]
        prev = pltpu.roll(_rows(carry_ref[:, cs], u), 1, 0)[SUBLANES:, :]
        carry_ref[:, cs] = u[T - SUBLANES:, :]
        return u + mu_ref[:, cs] * (prev - u)

    xl = shift_mix(ul_ref, cl_s, mul_ref, slice(None))
    th_wd = jnp.tanh(xl[:, :LANES]).astype(BF16)
    x_ad = pltpu.roll(xl[:, :2 * LANES], 2 * LANES - LORA_DECAY, 1)[:, :LANES].astype(BF16)
    gd_off = LORA_DECAY + LORA_ICLR - TAIL_GD_WIN
    sg_gd = _sigmoid(pltpu.roll(xl[:, TAIL_GD_WIN:], TAIL_MIX - TAIL_GD_WIN - gd_off, 1)[:, :LORA_GATE]).astype(BF16)

    lo = lax.broadcasted_iota(jnp.int32, (1, LANES), 1) < B_HEAD
    head1 = lambda x: jnp.where(lo, x, jnp.zeros_like(x))
    head2 = lambda x: jnp.where(lo, jnp.zeros_like(x), x)
    r2 = lax.broadcasted_iota(jnp.int32, (T, 2 * T), 0)
    c2 = lax.broadcasted_iota(jnp.int32, (T, 2 * T), 1)
    left = c2 < T
    cj = jnp.where(left, c2, c2 - T)
    strict2 = cj < r2
    incl2 = cj <= r2
    eye_pair = jnp.where(cj == r2, 1.0, 0.0)
    ones_blk = blkb_ref[...]
    zero = jnp.zeros((T, LANES), BF16)
    aligned = (2 * T) % LANES == 0

    def block_diag(sb):
        z = jnp.zeros_like(sb)
        return _rows(jnp.where(left, sb, z), jnp.where(left, z, sb))

    def pair_program(p):
        cs = slice(p * LANES, (p + 1) * LANES)
        r = shift_mix(ur_ref, cr_s, mur_ref, cs)
        xk = shift_mix(uk_ref, ck_s, muk_ref, cs)
        v = shift_mix(uv_ref, cv_s, muv_ref, cs)
        lw = jnp.dot(th_wd, w2_ref[:, cs], preferred_element_type=F32)
        la = jnp.dot(x_ad, a2_ref[:, cs], preferred_element_type=F32)
        lg = jnp.dot(sg_gd, g2_ref[:, cs], preferred_element_type=F32)
        kkw = xk * kkw_ref[:, cs]
        ssq, = _head_sums(ones_blk, T, kkw * kkw)
        yield
        ld = DECAY_SCALE * _sigmoid(w0_ref[:, cs] + lw)
        a = _sigmoid(a0_ref[:, cs] + la)
        g = lg * _sigmoid(gb_ref[:, cs])
        k = xk * (1.0 + (a - 1.0) * ka_ref[:, cs])
        kk = kkw * lax.rsqrt(jnp.maximum(ssq, 1e-24))
        bi = kk * a
        cum = _cumsum_rows(ld)
        ewt = jnp.exp(cum[T - 1:T, :])
        Rt = (r * jnp.exp(cum)).astype(BF16)
        At = (-kk * jnp.exp(cum - ld)).astype(BF16)
        einv = jnp.exp(-cum)
        Bt = bi * einv
        Kt = k * einv
        BhKh = _rows(Bt * ewt, Kt * ewt).astype(BF16)
        Bt = Bt.astype(BF16)
        Kt = Kt.astype(BF16)
        vb = v.astype(BF16)
        V1 = head1(vb)
        V2 = head2(vb)
        if aligned:
            G = _dot_nt(_rows(head1(At), head1(Rt), head2(At), head2(Rt)), _rows(Bt, Kt))
            G1, G2 = G[:2 * T], G[2 * T:]
        else:
            G1 = _dot_nt(_rows(head1(At), head1(Rt)), _rows(Bt, Kt))
            G2 = _dot_nt(_rows(head2(At), head2(Rt)), _rows(Kt, Bt))
        yield
        LL1 = jnp.where(strict2, G1[:T], 0.0)
        LL2 = jnp.where(strict2, G2[:T], 0.0)
        P1 = jnp.where(incl2, G1[T:], 0.0)
        P2 = jnp.where(incl2, G2[T:], 0.0)
        lab2 = pltpu.roll(LL2, T, 1) if aligned else LL2
        pw = jnp.where(left, LL1, lab2).astype(BF16)
        iv = eye_pair + jnp.where(left, LL1, lab2)
        X0 = _kcat_dot(LL1, _rows(zero, V1), LL2, _rows(zero, V2) if aligned else _rows(V2, zero))
        pw = _dot(pw, block_diag(pw)).astype(BF16)
        yield
        span = 4
        while span < T:
            both = _dot(_rows(iv.astype(BF16), pw), block_diag(pw))
            iv, pw = iv + both[:T], both[T:].astype(BF16)
            span *= 2
            yield
        iv = (iv + _dot(iv, block_diag(pw))).astype(BF16)
        S = s_s[p]
        AR = _dot_nt(_rows(At, Rt), S)
        yield
        X = (AR[:T] + X0).astype(BF16)
        U = _dot(iv, _rows(head1(X), head2(X))).astype(BF16)
        yield
        if aligned:
            rhs2 = _rows(head2(U), V2)
        else:
            rhs2 = _rows(V2, head2(U))
        o = AR[T:] + _kcat_dot(P1, _rows(head1(U), V1), P2, rhs2)
        s_s[p] = S * ewt + blk_ref[...] * _dot_tn(_rows(U, vb), BhKh)
        yield
        o_sum, rk_sum = _head_sums(ones_blk, T, o, r * k * rk_ref[:, cs])
        yield
        oc = o - o_sum * (1.0 / B_HEAD)
        var, = _head_sums(ones_blk, T, oc * oc)
        yield
        on = oc * lax.rsqrt(var * (1.0 / B_HEAD) + GN_EPS) * lnw_ref[:, cs] + lnb_ref[:, cs]
        hb_ref[:, cs] = ((on + rk_sum * v) * g).astype(hb_ref.dtype)

    _run_staggered([pair_program(p) for p in range(n_pairs)], RWKV_WAVE, RWKV_LAG)

    @pl.when(c == pl.num_programs(2) - 1)
    def _():
        for p in range(n_pairs):
            sn_ref[0, 2 * p] = s_s[p, :B_HEAD, :B_HEAD]
            sn_ref[0, 2 * p + 1] = s_s[p, B_HEAD:, B_HEAD:]


def _rwkv(proj, e_main, e_tail, mu_main, mu_tail, w0, w2p, a0, a2p, g2, k_k, k_a, rk, lnw, lnb, blk, s0, *,
          T, n_seq, n_chunk):
    assert T <= RWKV_CHUNK
    gw = GROUP_W
    wide = lambda col: pl.BlockSpec((T, gw), lambda s, pg, c: (s * n_chunk + c, col // gw + pg))
    edge = lambda col: pl.BlockSpec((1, SUBLANES, gw), lambda s, pg, c: (s, 0, col // gw + pg))
    vec = lambda col: pl.BlockSpec((1, gw), lambda s, pg, c: (0, col // gw + pg))
    mat = lambda rows: pl.BlockSpec((rows, gw), lambda s, pg, c: (0, pg))
    st = pl.BlockSpec((1, 2 * GROUP_PAIRS, B_HEAD, B_HEAD), lambda s, pg, c: (s, pg, 0, 0))
    return pl.pallas_call(
        functools.partial(_rwkv_body, T=T, n_pairs=GROUP_PAIRS),
        grid=(n_seq, N_PAIRS // GROUP_PAIRS, n_chunk),
        in_specs=[wide(COL_R), wide(COL_K), wide(COL_VB),
                  pl.BlockSpec((T, TAIL_MIX), lambda s, pg, c: (s * n_chunk + c, COL_TAIL // TAIL_MIX)),
                  wide(COL_GB),
                  edge(0), edge(B_W), edge(2 * B_W),
                  pl.BlockSpec((1, SUBLANES, TAIL_MIX), lambda s, pg, c: (s, 0, 0)),
                  vec(0), vec(B_W), vec(2 * B_W),
                  pl.BlockSpec((1, TAIL_MIX), lambda s, pg, c: (0, 0)),
                  vec(0), mat(LANES), vec(0), mat(LANES), mat(LORA_GATE), vec(0), vec(0),
                  vec(0), vec(0), vec(0),
                  pl.BlockSpec((LANES, LANES), lambda s, pg, c: (0, 0)),
                  pl.BlockSpec((LANES, LANES), lambda s, pg, c: (0, 0)), st],
        out_specs=[pl.BlockSpec((T, gw), lambda s, pg, c: (s * n_chunk + c, pg)), st],
        out_shape=[jax.ShapeDtypeStruct((n_seq * n_chunk * T, B_W), BF16),
                   jax.ShapeDtypeStruct((n_seq, B_HEADS, B_HEAD, B_HEAD), F32)],
        scratch_shapes=[pltpu.VMEM((GROUP_PAIRS, LANES, LANES), F32),
                        pltpu.VMEM((SUBLANES, gw), F32), pltpu.VMEM((SUBLANES, gw), F32),
                        pltpu.VMEM((SUBLANES, gw), F32), pltpu.VMEM((SUBLANES, TAIL_MIX), F32)],
        compiler_params=_cparams(("arbitrary", "arbitrary", "arbitrary"), 32),
        name="rwkv_chunk",
    )(proj, proj, proj, proj, proj, e_main, e_main, e_main, e_tail, mu_main, mu_main, mu_main, mu_tail,
      w0, w2p, a0, a2p, g2, k_k, k_a, rk, lnw, lnb, blk, blk.astype(BF16), s0)


def _merge_out_body(ha_ref, hb_ref, x_ref, w_ref, g_ref, y_ref):
    mixed = ha_ref[...].astype(F32) + hb_ref[...].astype(F32)
    y = jnp.dot(mixed.astype(BF16), w_ref[...], preferred_element_type=F32)
    ms = jnp.mean(y * y, axis=-1, keepdims=True)
    y_ref[...] = x_ref[...] + y * lax.rsqrt(ms + EPS) * g_ref[...]


def _merge_out(h_a, h_b, x, w_out, g_post):
    n = x.shape[0]
    wide = pl.BlockSpec((MERGE_TM, D_MODEL), lambda i: (i, 0))
    return pl.pallas_call(
        _merge_out_body,
        grid=(n // MERGE_TM,),
        in_specs=[wide, wide, wide,
                  pl.BlockSpec((D_MODEL, D_MODEL), lambda i: (0, 0)),
                  pl.BlockSpec((1, D_MODEL), lambda i: (0, 0))],
        out_specs=wide,
        out_shape=jax.ShapeDtypeStruct((n, D_MODEL), F32),
        compiler_params=_cparams(("parallel",), 48),
        name="merge_out_proj",
    )(h_a, h_b, x, w_out, g_post)


def _mlp_body(x_ref, g1_ref, wu_ref, wd_ref, g2_ref, y_ref, hn_ref, acc_ref, *, row_chunk):
    j = pl.program_id(1)

    @pl.when(j == 0)
    def _():
        _norm_rows(x_ref, g1_ref, hn_ref, row_chunk)
        acc_ref[...] = jnp.zeros_like(acc_ref)

    u = jnp.dot(hn_ref[...], wu_ref[...], preferred_element_type=F32)
    a = jnp.square(jnp.maximum(u, 0.0))
    acc_ref[...] += jnp.dot(a.astype(BF16), wd_ref[...], preferred_element_type=F32)

    @pl.when(j == pl.num_programs(1) - 1)
    def _():
        def body(c, carry):
            rows = pl.ds(pl.multiple_of(c * row_chunk, row_chunk), row_chunk)
            ff = acc_ref[rows, :]
            ms = jnp.mean(ff * ff, axis=-1, keepdims=True)
            y_ref[rows, :] = x_ref[rows, :] + ff * lax.rsqrt(ms + EPS) * g2_ref[...]
            return carry
        lax.fori_loop(0, x_ref.shape[0] // row_chunk, body, 0)


def _mlp(x, g_pre, w_up, w_down, g_post):
    n, d = x.shape
    f = w_up.shape[1]
    return pl.pallas_call(
        functools.partial(_mlp_body, row_chunk=LANES),
        grid=(n // MLP_TM, f // MLP_TF),
        in_specs=[pl.BlockSpec((MLP_TM, d), lambda i, j: (i, 0)),
                  pl.BlockSpec((1, d), lambda i, j: (0, 0)),
                  pl.BlockSpec((d, MLP_TF), lambda i, j: (0, j)),
                  pl.BlockSpec((MLP_TF, d), lambda i, j: (j, 0)),
                  pl.BlockSpec((1, d), lambda i, j: (0, 0))],
        out_specs=pl.BlockSpec((MLP_TM, d), lambda i, j: (i, 0)),
        out_shape=jax.ShapeDtypeStruct((n, d), F32),
        scratch_shapes=[pltpu.VMEM((MLP_TM, d), BF16), pltpu.VMEM((MLP_TM, d), F32)],
        compiler_params=_cparams(("parallel", "arbitrary"), 56),
        name="mlp",
    )(x, g_pre, w_up, w_down, g_post)


def _pad_cols(a, width):
    return jnp.pad(a, [(0, 0)] * (a.ndim - 1) + [(0, width - a.shape[-1])])


def _pad_rows(a, before, total):
    return jnp.pad(a, ((before, total - before - a.shape[0]), (0, 0)))


def _edge_blocks(prev_rows):
    return jnp.pad(prev_rows, ((0, 0), (SUBLANES - prev_rows.shape[1], 0), (0, 0)))


def _gates_row_form(proj, T):
    g = proj[:, COL_TAIL + TAIL_IF:COL_TAIL + TAIL_IF + 2 * A_HEADS]
    return jnp.swapaxes(g.reshape(-1, T, 2 * A_HEADS), 1, 2)


def kernel(x_prompt, x_sample, state_mlstm_C, state_mlstm_n, state_mlstm_m, state_mlstm_conv, state_rwkv_S, state_rwkv_shift, meta_tokens, norm_mix_pre, norm_mix_post, norm_mlp_pre, norm_mlp_post, w_in, mlstm_conv_w, mlstm_conv_b, mlstm_b_i, mlstm_b_f, mlstm_norm_w, rwkv_mu, rwkv_w0, rwkv_w2, rwkv_a0, rwkv_a2, rwkv_g2, rwkv_k_k, rwkv_k_a, rwkv_r_k, rwkv_ln_w, rwkv_ln_b, w_out, w_up, w_down):
    n_s_seq, s_len, _ = x_sample.shape
    _, p_len, _ = x_prompt.shape
    k3 = A_CONV - 1

    w_all = _relayout_weight(jnp.swapaxes(w_in[0], 0, 1))
    mu = rwkv_mu[0][None, :]
    mu_main = mu[:, :3 * B_W]
    mu_tail = _pad_cols(mu[:, 3 * B_W:], TAIL_MIX)
    w2p = _pad_rows(rwkv_w2[0], 0, LANES).astype(BF16)
    a2p = _pad_rows(rwkv_a2[0], 0, LANES).astype(BF16)
    g2 = rwkv_g2[0].astype(BF16)
    g_pre = norm_mix_pre[0][None, :]
    bias = jnp.concatenate([mlstm_b_i[0], mlstm_b_f[0]])
    bias_row = _pad_cols(bias[None, :], LANES)
    bias_col = bias[:, None]
    norm_w = mlstm_norm_w[0][None, :]
    conv_w = mlstm_conv_w[0]
    conv_b = mlstm_conv_b[0][None, :]
    rwkv_vecs = (rwkv_w0[0][None, :], w2p, rwkv_a0[0][None, :], a2p, g2, rwkv_k_k[0][None, :],
                 rwkv_k_a[0][None, :], rwkv_r_k[0].reshape(1, B_W), rwkv_ln_w[0][None, :], rwkv_ln_b[0][None, :],
                 _head_block_mask())
    w_out_b = w_out[0].astype(BF16)
    w_up_b = w_up[0].astype(BF16)
    w_down_b = w_down[0].astype(BF16)

    def branches(proj, T, n_seq, n_chunk, e_conv, e_main, e_tail, c0, n0, m0, s0):
        tm_ = MLSTM_CHUNK if n_chunk > 1 else T
        h_a, c_n, n_n, m_n = _mlstm(proj, _gates_row_form(proj, tm_), e_conv, conv_w, conv_b, bias_row, bias_col,
                                    norm_w, c0, n0, m0, T=tm_, n_seq=n_seq, n_chunk=n_chunk * T // tm_)
        h_b, s_n = _rwkv(proj, e_main, e_tail, mu_main, mu_tail, *rwkv_vecs, s0,
                         T=T, n_seq=n_seq, n_chunk=n_chunk)
        return h_a, h_b, (c_n, n_n, m_n, s_n)

    def finish(h_a, h_b, x_rows):
        x1 = _merge_out(h_a, h_b, x_rows, w_out_b, norm_mix_post[0][None, :])
        return _mlp(x1, norm_mlp_pre[0][None, :], w_up_b, w_down_b, norm_mlp_post[0][None, :])

    zeros = lambda *shape: jnp.zeros(shape, F32)
    xs = x_sample.reshape(n_s_seq * s_len, D_MODEL).astype(F32)
    assert xs.shape[0] == PROJ_TM
    proj_s, proj_m = _norm_matmul_pair(xs, meta_tokens.astype(F32), g_pre, w_all)
    _, _, (c_m, n_m, m_m, s_m) = branches(
        proj_m, N_META, 1, 1,
        zeros(1, SUBLANES, CONV_COLS), zeros(1, SUBLANES, 3 * B_W), zeros(1, SUBLANES, TAIL_MIX),
        zeros(1, A_HEADS, A_DK, A_DV), zeros(1, A_HEADS, A_DK), zeros(1, 1, LANES),
        zeros(1, B_HEADS, B_HEAD, B_HEAD))

    xp = x_prompt[0].astype(F32)
    proj_p = _norm_matmul(xp, g_pre, w_all, PROJ_TM)
    ha_p, hb_p, (c_p, n_p, m_p, s_p) = branches(
        proj_p, RWKV_CHUNK, 1, p_len // RWKV_CHUNK,
        _edge_blocks(proj_m[None, N_META - k3:, COL_CONV:COL_CONV + CONV_COLS]),
        _edge_blocks(proj_m[None, N_META - 1:, COL_R:COL_R + 3 * B_W]),
        _edge_blocks(proj_m[None, N_META - 1:, COL_TAIL:COL_TAIL + TAIL_MIX]),
        c_m, n_m, m_m, s_m)
    y_prompt = finish(ha_p, hb_p, xp)[None].astype(x_prompt.dtype)

    sh = state_rwkv_shift[0].astype(F32)
    m0_s = jnp.pad(state_mlstm_m[0].astype(F32), ((0, 0), (A_HEADS, LANES - 2 * A_HEADS)))[:, None, :]
    ha_s, hb_s, (c_s, n_s, m_s, s_s) = branches(
        proj_s, s_len, n_s_seq, 1,
        _edge_blocks(state_mlstm_conv[0].astype(F32)),
        _edge_blocks(sh[:, :, :3 * B_W]), _edge_blocks(_pad_cols(sh[:, :, 3 * B_W:], TAIL_MIX)),
        state_mlstm_C[0].astype(F32), state_mlstm_n[0].astype(F32), m0_s,
        state_rwkv_S[0].astype(F32))
    y_sample = finish(ha_s, hb_s, xs).reshape(x_sample.shape).astype(x_sample.dtype)

    def shift_state(rows):
        return jnp.concatenate([rows[..., COL_R:COL_R + 3 * B_W],
                                rows[..., COL_TAIL:COL_TAIL + LORA_COLS]], axis=-1)

    last_s = proj_s.reshape(n_s_seq, s_len, PROJ_COLS)[:, s_len - k3:, :]
    conv_s = last_s[..., COL_CONV:COL_CONV + CONV_COLS]
    shift_s = shift_state(last_s[:, k3 - 1:, :])
    last_p = proj_p[p_len - k3:, :]
    conv_p = last_p[None, :, COL_CONV:COL_CONV + CONV_COLS]
    shift_p = shift_state(last_p[None, k3 - 1:, :])
    dt_c, dt_n, dt_m = state_mlstm_C.dtype, state_mlstm_n.dtype, state_mlstm_m.dtype
    dt_cv, dt_s, dt_sh = state_mlstm_conv.dtype, state_rwkv_S.dtype, state_rwkv_shift.dtype
    lanes_m = slice(A_HEADS, 2 * A_HEADS)
    return (y_prompt, y_sample,
            c_p[None].astype(dt_c), n_p[None].astype(dt_n), m_p[:, 0, lanes_m][None].astype(dt_m),
            conv_p[None].astype(dt_cv), s_p[None].astype(dt_s), shift_p[None].astype(dt_sh),
            c_s[None].astype(dt_c), n_s[None].astype(dt_n), m_s[:, 0, lanes_m][None].astype(dt_m),
            conv_s[None].astype(dt_cv), s_s[None].astype(dt_s), shift_s[None].astype(dt_sh))
```

```python
import functools

import jax
import jax.numpy as jnp
from jax import lax
from jax.experimental import pallas as pl
from jax.experimental.pallas import tpu as pltpu

F32 = jnp.float32
BF16 = jnp.bfloat16

D_MODEL = 2048
N_META = 16
EPS = 1e-6
D_FF = 4 * D_MODEL
A_HEADS = 8
A_DK = 128
A_DV = D_MODEL // A_HEADS
A_QK = A_HEADS * A_DK
A_VW = A_HEADS * A_DV
A_CONV = 4
MLSTM_CHUNK = 256
B_HEAD = 64
B_HEADS = D_MODEL // B_HEAD
B_W = B_HEADS * B_HEAD
LORA_DECAY = max(32, int(round(1.8 * D_MODEL ** 0.5 / 32)) * 32)
LORA_ICLR = max(32, int(round(1.8 * D_MODEL ** 0.5 / 32)) * 32)
LORA_GATE = max(32, int(round(0.6 * D_MODEL ** 0.8 / 32)) * 32)
GN_EPS = 64e-5
CONV_COLS = 2 * A_QK
SHIFT_COLS = 3 * B_W + LORA_DECAY + LORA_ICLR + LORA_GATE

LANES = 128
SUBLANES = 8

COL_CONV, COL_V, COL_O, COL_R, COL_K, COL_VB, COL_GA, COL_GB = (i * D_MODEL for i in range(8))
COL_TAIL = 8 * D_MODEL
LORA_COLS = LORA_DECAY + LORA_ICLR + LORA_GATE
TAIL_MIX = -(-LORA_COLS // LANES) * LANES
TAIL_IF = TAIL_MIX
TAIL_GD_WIN = (LORA_DECAY + LORA_ICLR) // LANES * LANES
PROJ_TN = 1024
PROJ_COLS = -(-(COL_TAIL + TAIL_IF + LANES) // PROJ_TN) * PROJ_TN
PROJ_TM = 1024
MERGE_TM = 512
MLP_TM, MLP_TF = 512, 1024
DECAY_SCALE = -0.6065306597126334
RWKV_CHUNK = 64
N_PAIRS = B_HEADS // 2
GROUP_PAIRS = 16
RWKV_WAVE, RWKV_LAG = 8, 1
GROUP_W = GROUP_PAIRS * LANES


def _cparams(semantics, vmem_mib):
    return pltpu.CompilerParams(dimension_semantics=semantics, vmem_limit_bytes=vmem_mib << 20)


def _softplus(x):
    return jnp.maximum(x, 0.0) + jnp.log1p(jnp.exp(-jnp.abs(x)))


def _dot(a, b):
    return jnp.dot(a.astype(BF16), b.astype(BF16), preferred_element_type=F32)


def _dot_nt(a, b):
    return lax.dot_general(a.astype(BF16), b.astype(BF16), (((1,), (1,)), ((), ())),
                           preferred_element_type=F32)


def _dot_tn(a, b):
    return lax.dot_general(a.astype(BF16), b.astype(BF16), (((0,), (0,)), ((), ())),
                           preferred_element_type=F32)


def _each(f, *lists):
    return [f(*args) for args in zip(*lists)]


def _run_staggered(programs, wave, lag):
    live = list(enumerate(programs))
    slot = 0
    while live:
        still = []
        for p, prog in live:
            if slot >= (p // wave) * lag:
                try:
                    next(prog)
                except StopIteration:
                    continue
            still.append((p, prog))
        live = still
        slot += 1


def _rows(*parts):
    return jnp.concatenate(parts, axis=0)


def _split_bf16(x, pieces):
    out = []
    for _ in range(pieces - 1):
        p = x.astype(BF16)
        out.append(p)
        x = x - p.astype(F32)
    out.append(x.astype(BF16))
    return out


def _dot_exact_rhs(x, m, pieces=3):
    mb = m.astype(BF16)
    acc = None
    for p in _split_bf16(x, pieces):
        t = jnp.dot(p, mb, preferred_element_type=F32)
        acc = t if acc is None else acc + t
    return acc


def _sigmoid(x):
    return 0.5 * jnp.tanh(0.5 * x) + 0.5


def _cumsum_rows(x):
    n = x.shape[0]
    row = lax.broadcasted_iota(jnp.int32, x.shape, 0)
    step = 1
    while step < n:
        x = x + jnp.where(row >= step, pltpu.roll(x, step, 0), 0.0)
        step *= 2
    return x


def _head_block_mask():
    a = lax.broadcasted_iota(jnp.int32, (LANES, LANES), 0) // B_HEAD
    b = lax.broadcasted_iota(jnp.int32, (LANES, LANES), 1) // B_HEAD
    return jnp.where(a == b, 1.0, 0.0).astype(F32)


SRC_IF = CONV_COLS + 2 * A_VW
SRC_SHIFT = SRC_IF + 2 * A_HEADS
SRC_GATE = SRC_SHIFT + SHIFT_COLS
SRC_COLS = SRC_GATE + 2 * D_MODEL
RELAYOUT_ROWS = TAIL_MIX


def _relayout_src_row(t):
    unit = 2 * SUBLANES
    dst = t * (RELAYOUT_ROWS // unit)
    src = jnp.where(dst < COL_R // unit, dst,
                    jnp.where(dst < COL_GA // unit, dst + (SRC_SHIFT - COL_R) // unit,
                              jnp.where(dst < COL_TAIL // unit, dst + (SRC_GATE - COL_GA) // unit,
                                        jnp.where(dst < (COL_TAIL + TAIL_MIX) // unit,
                                                  (SRC_SHIFT + 3 * B_W) // unit, SRC_IF // unit))))
    return src * unit


def _relayout_body(w_ref, o_ref):
    o_ref[...] = w_ref[...].astype(BF16)


def _relayout_weight(wt):
    d = wt.shape[1]
    assert wt.shape[0] == SRC_COLS and PROJ_COLS == COL_TAIL + 2 * RELAYOUT_ROWS and COL_R == SRC_IF
    assert SRC_SHIFT + 3 * B_W + RELAYOUT_ROWS <= SRC_COLS and SRC_SHIFT % (2 * SUBLANES) == 0
    return pl.pallas_call(
        _relayout_body,
        grid=(PROJ_COLS // RELAYOUT_ROWS,),
        in_specs=[pl.BlockSpec((pl.Element(RELAYOUT_ROWS), pl.Element(d)), lambda t: (_relayout_src_row(t), 0))],
        out_specs=pl.BlockSpec((RELAYOUT_ROWS, d), lambda t: (t, 0)),
        out_shape=jax.ShapeDtypeStruct((PROJ_COLS, d), BF16),
        compiler_params=_cparams(("parallel",), 32),
        name="relayout_w_in",
    )(wt)


def _norm_rows(x_ref, g_ref, out_ref, row_chunk):
    def body(c, carry):
        rows = pl.ds(pl.multiple_of(c * row_chunk, row_chunk), row_chunk)
        x = x_ref[rows, :]
        ms = jnp.mean(x * x, axis=-1, keepdims=True)
        out_ref[rows, :] = (x * lax.rsqrt(ms + EPS) * g_ref[...]).astype(out_ref.dtype)
        return carry
    lax.fori_loop(0, x_ref.shape[0] // row_chunk, body, 0)


def _norm_bf16(x, g):
    ms = jnp.mean(x * x, axis=-1, keepdims=True)
    return (x * lax.rsqrt(ms + EPS) * g).astype(BF16)


def _norm_matmul_body(x0_ref, xn_ref, g_ref, w_ref, o_ref, hn_ref, *, row_chunk):
    i = pl.program_id(0)
    j = pl.program_id(1)
    slot = i % 2
    step_rows = xn_ref.shape[0]

    @pl.when((i == 0) & (j == 0))
    def _():
        _norm_rows(x0_ref, g_ref, hn_ref.at[0], row_chunk)

    o_ref[...] = lax.dot_general(hn_ref[slot], w_ref[...], (((1,), (1,)), ((), ())), preferred_element_type=F32)
    s = jnp.minimum(j, hn_ref.shape[1] // step_rows - 1)
    rows = pl.ds(pl.multiple_of(s * step_rows, step_rows), step_rows)
    hn_ref[1 - slot, rows, :] = _norm_bf16(xn_ref[...], g_ref[...])


def _norm_matmul(x, g, w, tm):
    n, d = x.shape
    p = w.shape[0]
    nj = p // PROJ_TN
    step_rows = tm // (nj - 1)
    assert step_rows * (nj - 1) == tm and step_rows % SUBLANES == 0
    n_slices = n // step_rows
    per_tile = nj - 1
    return pl.pallas_call(
        functools.partial(_norm_matmul_body, row_chunk=min(tm, LANES)),
        grid=(n // tm, nj),
        in_specs=[pl.BlockSpec((tm, d), lambda i, j: (0, 0)),
                  pl.BlockSpec((step_rows, d),
                               lambda i, j: (jnp.minimum((i + 1) * per_tile + jnp.minimum(j, per_tile - 1),
                                                         n_slices - 1), 0)),
                  pl.BlockSpec((1, d), lambda i, j: (0, 0)),
                  pl.BlockSpec((PROJ_TN, d), lambda i, j: (j, 0))],
        out_specs=pl.BlockSpec((tm, PROJ_TN), lambda i, j: (i, j)),
        out_shape=jax.ShapeDtypeStruct((n, p), F32),
        scratch_shapes=[pltpu.VMEM((2, tm, d), BF16)],
        compiler_params=_cparams(("arbitrary", "arbitrary"), 48),
        name="in_proj",
    )(x, x, g, w)


def _norm_matmul_pair_body(x_ref, xs_ref, g_ref, w_ref, o_ref, os_ref, hn_ref, hns_ref, *, row_chunk):
    @pl.when(pl.program_id(0) == 0)
    def _():
        _norm_rows(x_ref, g_ref, hn_ref, row_chunk)
        _norm_rows(xs_ref, g_ref, hns_ref, xs_ref.shape[0])

    dims = (((1,), (1,)), ((), ()))
    o_ref[...] = lax.dot_general(hn_ref[...], w_ref[...], dims, preferred_element_type=F32)
    os_ref[...] = lax.dot_general(hns_ref[...], w_ref[...], dims, preferred_element_type=F32)


def _norm_matmul_pair(x, x_small, g, w):
    n, d = x.shape
    ns = x_small.shape[0]
    p = w.shape[0]
    return pl.pallas_call(
        functools.partial(_norm_matmul_pair_body, row_chunk=LANES),
        grid=(p // PROJ_TN,),
        in_specs=[pl.BlockSpec((n, d), lambda j: (0, 0)),
                  pl.BlockSpec((ns, d), lambda j: (0, 0)),
                  pl.BlockSpec((1, d), lambda j: (0, 0)),
                  pl.BlockSpec((PROJ_TN, d), lambda j: (j, 0))],
        out_specs=[pl.BlockSpec((n, PROJ_TN), lambda j: (0, j)),
                   pl.BlockSpec((ns, PROJ_TN), lambda j: (0, j))],
        out_shape=[jax.ShapeDtypeStruct((n, p), F32), jax.ShapeDtypeStruct((ns, p), F32)],
        scratch_shapes=[pltpu.VMEM((n, d), BF16), pltpu.VMEM((ns, d), BF16)],
        compiler_params=_cparams(("arbitrary",), 48),
        name="in_proj_pair",
    )(x, x_small, g, w)


def _mlstm_body(u_ref, v_ref, o_ref, ga_ref, gc_ref, gr_ref, e_ref, cw_ref, cb_ref, brow_ref, bcol_ref,
                nw_ref, c0_ref, n0_ref, m0_ref,
                h_ref, cn_ref, nn_ref, mn_ref, c_s, n_s, m_s, e_s, *, T):
    c = pl.program_id(1)

    @pl.when(c == 0)
    def _():
        c_s[...] = c0_ref[0]
        n_s[...] = n0_ref[0]
        m_s[...] = m0_ref[0]
        e_s[...] = e_ref[0]

    u = u_ref[...]
    ext = _rows(e_s[...], u)
    w = cw_ref[...]
    acc = cb_ref[...] + u * w[A_CONV - 1:A_CONV, :]
    for sh in range(1, A_CONV):
        acc = acc + pltpu.roll(ext, sh, 0)[SUBLANES:, :] * w[A_CONV - 1 - sh:A_CONV - sh, :]
    e_s[...] = u[T - SUBLANES:, :]
    qk_act = acc * _sigmoid(acc)

    gcol = gc_ref[...] + brow_ref[...]
    grow = gr_ref[0] + bcol_ref[...]
    lf_col = -_softplus(-gcol)
    ig_row = grow[:A_HEADS, :]
    lf_row = -_softplus(-grow[A_HEADS:, :])
    ri = lax.broadcasted_iota(jnp.int32, (T, T), 0)
    ci = lax.broadcasted_iota(jnp.int32, (T, T), 1)
    tri = ci <= ri
    b_col = _cumsum_rows(lf_col)
    b_row = _dot_exact_rhs(lf_row, jnp.where(ri <= ci, 1.0, 0.0))
    m_prev = m_s[...]
    lane = lax.broadcasted_iota(jnp.int32, (1, LANES), 1)
    heads = list(range(A_HEADS))
    bc = [b_col[:, A_HEADS + h:A_HEADS + h + 1] for h in heads]
    mh = [m_prev[:, A_HEADS + h:A_HEADS + h + 1] for h in heads]
    br = [b_row[h:h + 1, :] for h in heads]
    igr = [ig_row[h:h + 1, :] for h in heads]
    igc = [gcol[:, h:h + 1] for h in heads]
    qh = [qk_act[:, h * A_DK:(h + 1) * A_DK] for h in heads]
    kh = [qk_act[:, A_QK + h * A_DK:A_QK + (h + 1) * A_DK] * (A_DK ** -0.5) for h in heads]
    vh = [v_ref[:, h * A_DV:(h + 1) * A_DV] for h in heads]
    ch = [c_s[h] for h in heads]
    nh = [n_s[h:h + 1, :] for h in heads]
    qk = _each(_dot_nt, qh, kh)
    qc = _each(_dot, qh, ch)
    d = _each(lambda a, b, i: jnp.where(tri, a - b + i, -jnp.inf), bc, br, igr)
    inter = _each(jnp.add, bc, mh)
    m_t = _each(lambda i, x: jnp.maximum(i, jnp.max(x, axis=-1, keepdims=True)), inter, d)
    s = _each(lambda a, x, m: a * jnp.exp(x - m), qk, d, m_t)
    e_in = _each(lambda i, m: jnp.exp(i - m), inter, m_t)
    sv = _each(_dot, s, vh)
    num = _each(lambda a, x, b: a + x * b, sv, e_in, qc)
    den = _each(lambda a, x, qq, nn: jnp.sum(a, axis=-1, keepdims=True)
                + x * jnp.sum(qq * nn, axis=-1, keepdims=True), s, e_in, qh, nh)
    hh = _each(lambda a, b, m: a / jnp.maximum(jnp.abs(b), jnp.exp(-m)), num, den, m_t)
    hh = _each(lambda a: a * lax.rsqrt(jnp.mean(a * a, axis=-1, keepdims=True) + EPS), hh)
    b_end = _each(lambda a: a[T - 1:T, :], bc)
    m_new = _each(lambda be, m, b, i: jnp.maximum(be + m, jnp.max(be - b + i, axis=-1, keepdims=True)),
                  b_end, mh, br, igr)
    carry = _each(lambda be, m, mn: jnp.exp(be + m - mn), b_end, mh, m_new)
    wk = _each(lambda kk_, be, a, i, mn: kk_ * jnp.exp(be - a + i - mn), kh, b_end, bc, igc, m_new)
    kv = _each(_dot_tn, wk, vh)
    m_next = m_prev
    for h in heads:
        cols = slice(h * A_DV, (h + 1) * A_DV)
        gate = _sigmoid(o_ref[:, cols]) * _sigmoid(ga_ref[:, cols])
        h_ref[:, cols] = (hh[h] * nw_ref[:, cols] * gate).astype(h_ref.dtype)
        c_s[h] = carry[h] * ch[h] + kv[h]
        n_s[h:h + 1, :] = carry[h] * nh[h] + jnp.sum(wk[h], axis=0, keepdims=True)
        m_next = jnp.where(lane == A_HEADS + h, m_new[h], m_next)
    m_s[...] = m_next

    @pl.when(c == pl.num_programs(1) - 1)
    def _():
        cn_ref[0] = c_s[...]
        nn_ref[0] = n_s[...]
        mn_ref[0] = m_s[...]


def _mlstm(proj, gates_row, e_conv, conv_w, conv_b, bias_row, bias_col, norm_w, c0, n0, m0, *, T, n_seq, n_chunk):
    row = lambda s, c: (s * n_chunk + c, 0)
    fixed = lambda s, c: (0, 0)
    wide = lambda col: pl.BlockSpec((T, D_MODEL), lambda s, c: (s * n_chunk + c, col // D_MODEL))
    return pl.pallas_call(
        functools.partial(_mlstm_body, T=T),
        grid=(n_seq, n_chunk),
        in_specs=[wide(COL_CONV), wide(COL_V), wide(COL_O), wide(COL_GA),
                  pl.BlockSpec((T, LANES), lambda s, c: (s * n_chunk + c, (COL_TAIL + TAIL_IF) // LANES)),
                  pl.BlockSpec((1, 2 * A_HEADS, T), lambda s, c: (s * n_chunk + c, 0, 0)),
                  pl.BlockSpec((1, SUBLANES, CONV_COLS), lambda s, c: (s, 0, 0)),
                  pl.BlockSpec((A_CONV, CONV_COLS), fixed),
                  pl.BlockSpec((1, CONV_COLS), fixed),
                  pl.BlockSpec((1, LANES), fixed),
                  pl.BlockSpec((2 * A_HEADS, 1), fixed),
                  pl.BlockSpec((1, A_VW), fixed),
                  pl.BlockSpec((1, A_HEADS, A_DK, A_DV), lambda s, c: (s, 0, 0, 0)),
                  pl.BlockSpec((1, A_HEADS, A_DK), lambda s, c: (s, 0, 0)),
                  pl.BlockSpec((1, 1, LANES), lambda s, c: (s, 0, 0))],
        out_specs=[pl.BlockSpec((T, A_VW), row),
                   pl.BlockSpec((1, A_HEADS, A_DK, A_DV), lambda s, c: (s, 0, 0, 0)),
                   pl.BlockSpec((1, A_HEADS, A_DK), lambda s, c: (s, 0, 0)),
                   pl.BlockSpec((1, 1, LANES), lambda s, c: (s, 0, 0))],
        out_shape=[jax.ShapeDtypeStruct((n_seq * n_chunk * T, A_VW), BF16),
                   jax.ShapeDtypeStruct((n_seq, A_HEADS, A_DK, A_DV), F32),
                   jax.ShapeDtypeStruct((n_seq, A_HEADS, A_DK), F32),
                   jax.ShapeDtypeStruct((n_seq, 1, LANES), F32)],
        scratch_shapes=[pltpu.VMEM((A_HEADS, A_DK, A_DV), F32),
                        pltpu.VMEM((A_HEADS, A_DK), F32),
                        pltpu.VMEM((1, LANES), F32),
                        pltpu.VMEM((SUBLANES, CONV_COLS), F32)],
        compiler_params=_cparams(("arbitrary", "arbitrary"), 32),
        name="mlstm_chunk",
    )(proj, proj, proj, proj, proj, gates_row, e_conv, conv_w, conv_b, bias_row, bias_col, norm_w, c0, n0, m0)


def _kcat_dot(a1, b1, a2, b2):
    if a1.shape[1] % LANES == 0:
        return _dot(jnp.concatenate([a1, a2], axis=1), _rows(b1, b2))
    return _dot(a1, b1) + _dot(a2, b2)


def _head_sums(ones_blk, T, *xs):
    sums = jnp.dot(_rows(*xs).astype(BF16), ones_blk, preferred_element_type=F32)
    return [sums[i * T:(i + 1) * T] for i in range(len(xs))]


def _rwkv_body(ur_ref, uk_ref, uv_ref, ul_ref, gb_ref, er_ref, ek_ref, ev_ref, el_ref,
               mur_ref, muk_ref, muv_ref, mul_ref, w0_ref, w2_ref, a0_ref, a2_ref, g2_ref, kkw_ref, ka_ref,
               rk_ref, lnw_ref, lnb_ref, blk_ref, blkb_ref, s0_ref,
               hb_ref, sn_ref, s_s, cr_s, ck_s, cv_s, cl_s, *, T, n_pairs):
    c = pl.program_id(2)

    @pl.when(c == 0)
    def _():
        z = jnp.zeros((B_HEAD, B_HEAD), F32)
        for p in range(n_pairs):
            s_s[p] = _rows(jnp.concatenate([s0_ref[0, 2 * p], z], axis=1),
                           jnp.concatenate([z, s0_ref[0, 2 * p + 1]], axis=1))
        cr_s[...] = er_ref[0]
        ck_s[...] = ek_ref[0]
        cv_s[...] = ev_ref[0]
        cl_s[...] = el_ref[0]

    def shift_mix(u_ref, carry_ref, mu_ref, cs):
        u = u_ref[:, cs]
        prev = pltpu.roll(_rows(carry_ref[:, cs], u), 1, 0)[SUBLANES:, :]
        carry_ref[:, cs] = u[T - SUBLANES:, :]
        return u + mu_ref[:, cs] * (prev - u)

    xl = shift_mix(ul_ref, cl_s, mul_ref, slice(None))
    th_wd = jnp.tanh(xl[:, :LANES]).astype(BF16)
    x_ad = pltpu.roll(xl[:, :2 * LANES], 2 * LANES - LORA_DECAY, 1)[:, :LANES].astype(BF16)
    gd_off = LORA_DECAY + LORA_ICLR - TAIL_GD_WIN
    sg_gd = _sigmoid(pltpu.roll(xl[:, TAIL_GD_WIN:], TAIL_MIX - TAIL_GD_WIN - gd_off, 1)[:, :LORA_GATE]).astype(BF16)

    lo = lax.broadcasted_iota(jnp.int32, (1, LANES), 1) < B_HEAD
    head1 = lambda x: jnp.where(lo, x, jnp.zeros_like(x))
    head2 = lambda x: jnp.where(lo, jnp.zeros_like(x), x)
    r2 = lax.broadcasted_iota(jnp.int32, (T, 2 * T), 0)
    c2 = lax.broadcasted_iota(jnp.int32, (T, 2 * T), 1)
    left = c2 < T
    cj = jnp.where(left, c2, c2 - T)
    strict2 = cj < r2
    incl2 = cj <= r2
    eye_pair = jnp.where(cj == r2, 1.0, 0.0)
    ones_blk = blkb_ref[...]
    zero = jnp.zeros((T, LANES), BF16)
    aligned = (2 * T) % LANES == 0

    def block_diag(sb):
        z = jnp.zeros_like(sb)
        return _rows(jnp.where(left, sb, z), jnp.where(left, z, sb))

    def pair_program(p):
        cs = slice(p * LANES, (p + 1) * LANES)
        r = shift_mix(ur_ref, cr_s, mur_ref, cs)
        xk = shift_mix(uk_ref, ck_s, muk_ref, cs)
        v = shift_mix(uv_ref, cv_s, muv_ref, cs)
        lw = jnp.dot(th_wd, w2_ref[:, cs], preferred_element_type=F32)
        la = jnp.dot(x_ad, a2_ref[:, cs], preferred_element_type=F32)
        lg = jnp.dot(sg_gd, g2_ref[:, cs], preferred_element_type=F32)
        kkw = xk * kkw_ref[:, cs]
        ssq, = _head_sums(ones_blk, T, kkw * kkw)
        yield
        ld = DECAY_SCALE * _sigmoid(w0_ref[:, cs] + lw)
        a = _sigmoid(a0_ref[:, cs] + la)
        g = lg * _sigmoid(gb_ref[:, cs])
        k = xk * (1.0 + (a - 1.0) * ka_ref[:, cs])
        kk = kkw * lax.rsqrt(jnp.maximum(ssq, 1e-24))
        bi = kk * a
        cum = _cumsum_rows(ld)
        ewt = jnp.exp(cum[T - 1:T, :])
        Rt = (r * jnp.exp(cum)).astype(BF16)
        At = (-kk * jnp.exp(cum - ld)).astype(BF16)
        einv = jnp.exp(-cum)
        Bt = bi * einv
        Kt = k * einv
        BhKh = _rows(Bt * ewt, Kt * ewt).astype(BF16)
        Bt = Bt.astype(BF16)
        Kt = Kt.astype(BF16)
        vb = v.astype(BF16)
        V1 = head1(vb)
        V2 = head2(vb)
        if aligned:
            G = _dot_nt(_rows(head1(At), head1(Rt), head2(At), head2(Rt)), _rows(Bt, Kt))
            G1, G2 = G[:2 * T], G[2 * T:]
        else:
            G1 = _dot_nt(_rows(head1(At), head1(Rt)), _rows(Bt, Kt))
            G2 = _dot_nt(_rows(head2(At), head2(Rt)), _rows(Kt, Bt))
        yield
        LL1 = jnp.where(strict2, G1[:T], 0.0)
        LL2 = jnp.where(strict2, G2[:T], 0.0)
        P1 = jnp.where(incl2, G1[T:], 0.0)
        P2 = jnp.where(incl2, G2[T:], 0.0)
        lab2 = pltpu.roll(LL2, T, 1) if aligned else LL2
        pw = jnp.where(left, LL1, lab2).astype(BF16)
        iv = eye_pair + jnp.where(left, LL1, lab2)
        X0 = _kcat_dot(LL1, _rows(zero, V1), LL2, _rows(zero, V2) if aligned else _rows(V2, zero))
        pw = _dot(pw, block_diag(pw)).astype(BF16)
        yield
        span = 4
        while span < T:
            both = _dot(_rows(iv.astype(BF16), pw), block_diag(pw))
            iv, pw = iv + both[:T], both[T:].astype(BF16)
            span *= 2
            yield
        iv = (iv + _dot(iv, block_diag(pw))).astype(BF16)
        S = s_s[p]
        AR = _dot_nt(_rows(At, Rt), S)
        yield
        X = (AR[:T] + X0).astype(BF16)
        U = _dot(iv, _rows(head1(X), head2(X))).astype(BF16)
        yield
        if aligned:
            rhs2 = _rows(head2(U), V2)
        else:
            rhs2 = _rows(V2, head2(U))
        o = AR[T:] + _kcat_dot(P1, _rows(head1(U), V1), P2, rhs2)
        s_s[p] = S * ewt + blk_ref[...] * _dot_tn(_rows(U, vb), BhKh)
        yield
        o_sum, rk_sum = _head_sums(ones_blk, T, o, r * k * rk_ref[:, cs])
        yield
        oc = o - o_sum * (1.0 / B_HEAD)
        var, = _head_sums(ones_blk, T, oc * oc)
        yield
        on = oc * lax.rsqrt(var * (1.0 / B_HEAD) + GN_EPS) * lnw_ref[:, cs] + lnb_ref[:, cs]
        hb_ref[:, cs] = ((on + rk_sum * v) * g).astype(hb_ref.dtype)

    _run_staggered([pair_program(p) for p in range(n_pairs)], RWKV_WAVE, RWKV_LAG)

    @pl.when(c == pl.num_programs(2) - 1)
    def _():
        for p in range(n_pairs):
            sn_ref[0, 2 * p] = s_s[p, :B_HEAD, :B_HEAD]
            sn_ref[0, 2 * p + 1] = s_s[p, B_HEAD:, B_HEAD:]


def _rwkv(proj, e_main, e_tail, mu_main, mu_tail, w0, w2p, a0, a2p, g2, k_k, k_a, rk, lnw, lnb, blk, s0, *,
          T, n_seq, n_chunk):
    assert T <= RWKV_CHUNK
    gw = GROUP_W
    wide = lambda col: pl.BlockSpec((T, gw), lambda s, pg, c: (s * n_chunk + c, col // gw + pg))
    edge = lambda col: pl.BlockSpec((1, SUBLANES, gw), lambda s, pg, c: (s, 0, col // gw + pg))
    vec = lambda col: pl.BlockSpec((1, gw), lambda s, pg, c: (0, col // gw + pg))
    mat = lambda rows: pl.BlockSpec((rows, gw), lambda s, pg, c: (0, pg))
    st = pl.BlockSpec((1, 2 * GROUP_PAIRS, B_HEAD, B_HEAD), lambda s, pg, c: (s, pg, 0, 0))
    return pl.pallas_call(
        functools.partial(_rwkv_body, T=T, n_pairs=GROUP_PAIRS),
        grid=(n_seq, N_PAIRS // GROUP_PAIRS, n_chunk),
        in_specs=[wide(COL_R), wide(COL_K), wide(COL_VB),
                  pl.BlockSpec((T, TAIL_MIX), lambda s, pg, c: (s * n_chunk + c, COL_TAIL // TAIL_MIX)),
                  wide(COL_GB),
                  edge(0), edge(B_W), edge(2 * B_W),
                  pl.BlockSpec((1, SUBLANES, TAIL_MIX), lambda s, pg, c: (s, 0, 0)),
                  vec(0), vec(B_W), vec(2 * B_W),
                  pl.BlockSpec((1, TAIL_MIX), lambda s, pg, c: (0, 0)),
                  vec(0), mat(LANES), vec(0), mat(LANES), mat(LORA_GATE), vec(0), vec(0),
                  vec(0), vec(0), vec(0),
                  pl.BlockSpec((LANES, LANES), lambda s, pg, c: (0, 0)),
                  pl.BlockSpec((LANES, LANES), lambda s, pg, c: (0, 0)), st],
        out_specs=[pl.BlockSpec((T, gw), lambda s, pg, c: (s * n_chunk + c, pg)), st],
        out_shape=[jax.ShapeDtypeStruct((n_seq * n_chunk * T, B_W), BF16),
                   jax.ShapeDtypeStruct((n_seq, B_HEADS, B_HEAD, B_HEAD), F32)],
        scratch_shapes=[pltpu.VMEM((GROUP_PAIRS, LANES, LANES), F32),
                        pltpu.VMEM((SUBLANES, gw), F32), pltpu.VMEM((SUBLANES, gw), F32),
                        pltpu.VMEM((SUBLANES, gw), F32), pltpu.VMEM((SUBLANES, TAIL_MIX), F32)],
        compiler_params=_cparams(("arbitrary", "arbitrary", "arbitrary"), 32),
        name="rwkv_chunk",
    )(proj, proj, proj, proj, proj, e_main, e_main, e_main, e_tail, mu_main, mu_main, mu_main, mu_tail,
      w0, w2p, a0, a2p, g2, k_k, k_a, rk, lnw, lnb, blk, blk.astype(BF16), s0)


def _merge_out_body(ha_ref, hb_ref, x_ref, w_ref, g_ref, y_ref):
    mixed = ha_ref[...].astype(F32) + hb_ref[...].astype(F32)
    y = jnp.dot(mixed.astype(BF16), w_ref[...], preferred_element_type=F32)
    ms = jnp.mean(y * y, axis=-1, keepdims=True)
    y_ref[...] = x_ref[...] + y * lax.rsqrt(ms + EPS) * g_ref[...]


def _merge_out(h_a, h_b, x, w_out, g_post):
    n = x.shape[0]
    wide = pl.BlockSpec((MERGE_TM, D_MODEL), lambda i: (i, 0))
    return pl.pallas_call(
        _merge_out_body,
        grid=(n // MERGE_TM,),
        in_specs=[wide, wide, wide,
                  pl.BlockSpec((D_MODEL, D_MODEL), lambda i: (0, 0)),
                  pl.BlockSpec((1, D_MODEL), lambda i: (0, 0))],
        out_specs=wide,
        out_shape=jax.ShapeDtypeStruct((n, D_MODEL), F32),
        compiler_params=_cparams(("parallel",), 48),
        name="merge_out_proj",
    )(h_a, h_b, x, w_out, g_post)


def _mlp_body(x_ref, xn_ref, g1_ref, wu_ref, wd_ref, g2_ref, y_ref, hn_ref, acc_ref, *, row_chunk):
    i = pl.program_id(0)
    j = pl.program_id(1)
    last = pl.num_programs(1) - 1
    slot = i % 2
    step_rows = xn_ref.shape[0]

    @pl.when((i == 0) & (j == 0))
    def _():
        _norm_rows(x_ref, g1_ref, hn_ref.at[0], row_chunk)
        acc_ref[...] = jnp.zeros_like(acc_ref)

    def partial_product():
        u = jnp.dot(hn_ref[slot], wu_ref[...], preferred_element_type=F32)
        a = jnp.square(jnp.maximum(u, 0.0))
        part = jnp.dot(a.astype(BF16), wd_ref[...], preferred_element_type=F32)
        rows = pl.ds(pl.multiple_of(j * step_rows, step_rows), step_rows)
        hn_ref[1 - slot, rows, :] = _norm_bf16(xn_ref[...], g1_ref[...])
        return part

    @pl.when(j < last)
    def _():
        part = partial_product()
        acc_ref[...] = jnp.where(j == 0, part, acc_ref[...] + part)

    @pl.when(j == last)
    def _():
        ff = acc_ref[...] + partial_product()
        ms = jnp.mean(ff * ff, axis=-1, keepdims=True)
        y_ref[...] = x_ref[...] + ff * lax.rsqrt(ms + EPS) * g2_ref[...]


def _mlp(x, g_pre, w_up, w_down, g_post):
    n, d = x.shape
    f = w_up.shape[1]
    nj = f // MLP_TF
    step_rows = MLP_TM // nj
    n_slices = n // step_rows
    return pl.pallas_call(
        functools.partial(_mlp_body, row_chunk=LANES),
        grid=(n // MLP_TM, nj),
        in_specs=[pl.BlockSpec((MLP_TM, d), lambda i, j: (i, 0)),
                  pl.BlockSpec((step_rows, d), lambda i, j: (jnp.minimum((i + 1) * nj + j, n_slices - 1), 0)),
                  pl.BlockSpec((1, d), lambda i, j: (0, 0)),
                  pl.BlockSpec((d, MLP_TF), lambda i, j: (0, j)),
                  pl.BlockSpec((MLP_TF, d), lambda i, j: (j, 0)),
                  pl.BlockSpec((1, d), lambda i, j: (0, 0))],
        out_specs=pl.BlockSpec((MLP_TM, d), lambda i, j: (i, 0)),
        out_shape=jax.ShapeDtypeStruct((n, d), F32),
        scratch_shapes=[pltpu.VMEM((2, MLP_TM, d), BF16), pltpu.VMEM((MLP_TM, d), F32)],
        compiler_params=_cparams(("arbitrary", "arbitrary"), 56),
        name="mlp",
    )(x, x, g_pre, w_up, w_down, g_post)


def _pad_cols(a, width):
    return jnp.pad(a, [(0, 0)] * (a.ndim - 1) + [(0, width - a.shape[-1])])


def _pad_rows(a, before, total):
    return jnp.pad(a, ((before, total - before - a.shape[0]), (0, 0)))


def _edge_blocks(prev_rows):
    return jnp.pad(prev_rows, ((0, 0), (SUBLANES - prev_rows.shape[1], 0), (0, 0)))


def _gates_row_form(proj, T):
    g = proj[:, COL_TAIL + TAIL_IF:COL_TAIL + TAIL_IF + 2 * A_HEADS]
    return jnp.swapaxes(g.reshape(-1, T, 2 * A_HEADS), 1, 2)


def kernel(x_prompt, x_sample, state_mlstm_C, state_mlstm_n, state_mlstm_m, state_mlstm_conv, state_rwkv_S, state_rwkv_shift, meta_tokens, norm_mix_pre, norm_mix_post, norm_mlp_pre, norm_mlp_post, w_in, mlstm_conv_w, mlstm_conv_b, mlstm_b_i, mlstm_b_f, mlstm_norm_w, rwkv_mu, rwkv_w0, rwkv_w2, rwkv_a0, rwkv_a2, rwkv_g2, rwkv_k_k, rwkv_k_a, rwkv_r_k, rwkv_ln_w, rwkv_ln_b, w_out, w_up, w_down):
    n_s_seq, s_len, _ = x_sample.shape
    _, p_len, _ = x_prompt.shape
    k3 = A_CONV - 1

    w_all = _relayout_weight(jnp.swapaxes(w_in[0], 0, 1))
    mu = rwkv_mu[0][None, :]
    mu_main = mu[:, :3 * B_W]
    mu_tail = _pad_cols(mu[:, 3 * B_W:], TAIL_MIX)
    w2p = _pad_rows(rwkv_w2[0], 0, LANES).astype(BF16)
    a2p = _pad_rows(rwkv_a2[0], 0, LANES).astype(BF16)
    g2 = rwkv_g2[0].astype(BF16)
    g_pre = norm_mix_pre[0][None, :]
    bias = jnp.concatenate([mlstm_b_i[0], mlstm_b_f[0]])
    bias_row = _pad_cols(bias[None, :], LANES)
    bias_col = bias[:, None]
    norm_w = mlstm_norm_w[0][None, :]
    conv_w = mlstm_conv_w[0]
    conv_b = mlstm_conv_b[0][None, :]
    rwkv_vecs = (rwkv_w0[0][None, :], w2p, rwkv_a0[0][None, :], a2p, g2, rwkv_k_k[0][None, :],
                 rwkv_k_a[0][None, :], rwkv_r_k[0].reshape(1, B_W), rwkv_ln_w[0][None, :], rwkv_ln_b[0][None, :],
                 _head_block_mask())
    w_out_b = w_out[0].astype(BF16)
    w_up_b = w_up[0].astype(BF16)
    w_down_b = w_down[0].astype(BF16)

    def branches(proj, T, n_seq, n_chunk, e_conv, e_main, e_tail, c0, n0, m0, s0):
        tm_ = MLSTM_CHUNK if n_chunk > 1 else T
        h_a, c_n, n_n, m_n = _mlstm(proj, _gates_row_form(proj, tm_), e_conv, conv_w, conv_b, bias_row, bias_col,
                                    norm_w, c0, n0, m0, T=tm_, n_seq=n_seq, n_chunk=n_chunk * T // tm_)
        h_b, s_n = _rwkv(proj, e_main, e_tail, mu_main, mu_tail, *rwkv_vecs, s0,
                         T=T, n_seq=n_seq, n_chunk=n_chunk)
        return h_a, h_b, (c_n, n_n, m_n, s_n)

    def finish(h_a, h_b, x_rows):
        x1 = _merge_out(h_a, h_b, x_rows, w_out_b, norm_mix_post[0][None, :])
        return _mlp(x1, norm_mlp_pre[0][None, :], w_up_b, w_down_b, norm_mlp_post[0][None, :])

    zeros = lambda *shape: jnp.zeros(shape, F32)
    xs = x_sample.reshape(n_s_seq * s_len, D_MODEL).astype(F32)
    assert xs.shape[0] == PROJ_TM
    proj_s, proj_m = _norm_matmul_pair(xs, meta_tokens.astype(F32), g_pre, w_all)
    _, _, (c_m, n_m, m_m, s_m) = branches(
        proj_m, N_META, 1, 1,
        zeros(1, SUBLANES, CONV_COLS), zeros(1, SUBLANES, 3 * B_W), zeros(1, SUBLANES, TAIL_MIX),
        zeros(1, A_HEADS, A_DK, A_DV), zeros(1, A_HEADS, A_DK), zeros(1, 1, LANES),
        zeros(1, B_HEADS, B_HEAD, B_HEAD))

    xp = x_prompt[0].astype(F32)
    proj_p = _norm_matmul(xp, g_pre, w_all, PROJ_TM)
    ha_p, hb_p, (c_p, n_p, m_p, s_p) = branches(
        proj_p, RWKV_CHUNK, 1, p_len // RWKV_CHUNK,
        _edge_blocks(proj_m[None, N_META - k3:, COL_CONV:COL_CONV + CONV_COLS]),
        _edge_blocks(proj_m[None, N_META - 1:, COL_R:COL_R + 3 * B_W]),
        _edge_blocks(proj_m[None, N_META - 1:, COL_TAIL:COL_TAIL + TAIL_MIX]),
        c_m, n_m, m_m, s_m)
    y_prompt = finish(ha_p, hb_p, xp)[None].astype(x_prompt.dtype)

    sh = state_rwkv_shift[0].astype(F32)
    m0_s = jnp.pad(state_mlstm_m[0].astype(F32), ((0, 0), (A_HEADS, LANES - 2 * A_HEADS)))[:, None, :]
    ha_s, hb_s, (c_s, n_s, m_s, s_s) = branches(
        proj_s, s_len, n_s_seq, 1,
        _edge_blocks(state_mlstm_conv[0].astype(F32)),
        _edge_blocks(sh[:, :, :3 * B_W]), _edge_blocks(_pad_cols(sh[:, :, 3 * B_W:], TAIL_MIX)),
        state_mlstm_C[0].astype(F32), state_mlstm_n[0].astype(F32), m0_s,
        state_rwkv_S[0].astype(F32))
    y_sample = finish(ha_s, hb_s, xs).reshape(x_sample.shape).astype(x_sample.dtype)

    def shift_state(rows):
        return jnp.concatenate([rows[..., COL_R:COL_R + 3 * B_W],
                                rows[..., COL_TAIL:COL_TAIL + LORA_COLS]], axis=-1)

    last_s = proj_s.reshape(n_s_seq, s_len, PROJ_COLS)[:, s_len - k3:, :]
    conv_s = last_s[..., COL_CONV:COL_CONV + CONV_COLS]
    shift_s = shift_state(last_s[:, k3 - 1:, :])
    last_p = proj_p[p_len - k3:, :]
    conv_p = last_p[None, :, COL_CONV:COL_CONV + CONV_COLS]
    shift_p = shift_state(last_p[None, k3 - 1:, :])
    dt_c, dt_n, dt_m = state_mlstm_C.dtype, state_mlstm_n.dtype, state_mlstm_m.dtype
    dt_cv, dt_s, dt_sh = state_mlstm_conv.dtype, state_rwkv_S.dtype, state_rwkv_shift.dtype
    lanes_m = slice(A_HEADS, 2 * A_HEADS)
    return (y_prompt, y_sample,
            c_p[None].astype(dt_c), n_p[None].astype(dt_n), m_p[:, 0, lanes_m][None].astype(dt_m),
            conv_p[None].astype(dt_cv), s_p[None].astype(dt_s), shift_p[None].astype(dt_sh),
            c_s[None].astype(dt_c), n_s[None].astype(dt_n), m_s[:, 0, lanes_m][None].astype(dt_m),
            conv_s[None].astype(dt_cv), s_s[None].astype(dt_s), shift_s[None].astype(dt_sh))
```

```python
import functools

import jax
import jax.numpy as jnp
from jax import lax
from jax.experimental import pallas as pl
from jax.experimental.pallas import tpu as pltpu

F32 = jnp.float32
BF16 = jnp.bfloat16

D_MODEL = 2048
N_META = 16
EPS = 1e-6
D_FF = 4 * D_MODEL
A_HEADS = 8
A_DK = 128
A_DV = D_MODEL // A_HEADS
A_QK = A_HEADS * A_DK
A_VW = A_HEADS * A_DV
A_CONV = 4
MLSTM_CHUNK = 256
B_HEAD = 64
B_HEADS = D_MODEL // B_HEAD
B_W = B_HEADS * B_HEAD
LORA_DECAY = max(32, int(round(1.8 * D_MODEL ** 0.5 / 32)) * 32)
LORA_ICLR = max(32, int(round(1.8 * D_MODEL ** 0.5 / 32)) * 32)
LORA_GATE = max(32, int(round(0.6 * D_MODEL ** 0.8 / 32)) * 32)
GN_EPS = 64e-5
CONV_COLS = 2 * A_QK
SHIFT_COLS = 3 * B_W + LORA_DECAY + LORA_ICLR + LORA_GATE

LANES = 128
SUBLANES = 8

COL_CONV, COL_V, COL_O, COL_R, COL_K, COL_VB, COL_GA, COL_GB = (i * D_MODEL for i in range(8))
COL_TAIL = 8 * D_MODEL
LORA_COLS = LORA_DECAY + LORA_ICLR + LORA_GATE
TAIL_MIX = -(-LORA_COLS // LANES) * LANES
TAIL_IF = TAIL_MIX
TAIL_GD_WIN = (LORA_DECAY + LORA_ICLR) // LANES * LANES
PROJ_TN = 1024
PROJ_COLS = -(-(COL_TAIL + TAIL_IF + LANES) // PROJ_TN) * PROJ_TN
PROJ_TM = 1024
MERGE_TM = 512
MLP_TM, MLP_TF = 512, 1024
DECAY_SCALE = -0.6065306597126334
RWKV_CHUNK = 64
N_PAIRS = B_HEADS // 2
GROUP_PAIRS = 16
RWKV_WAVE, RWKV_LAG = 8, 1
GROUP_W = GROUP_PAIRS * LANES


def _cparams(semantics, vmem_mib):
    return pltpu.CompilerParams(dimension_semantics=semantics, vmem_limit_bytes=vmem_mib << 20)


def _softplus(x):
    return jnp.maximum(x, 0.0) + jnp.log1p(jnp.exp(-jnp.abs(x)))


def _dot(a, b):
    return jnp.dot(a.astype(BF16), b.astype(BF16), preferred_element_type=F32)


def _dot_nt(a, b):
    return lax.dot_general(a.astype(BF16), b.astype(BF16), (((1,), (1,)), ((), ())),
                           preferred_element_type=F32)


def _dot_tn(a, b):
    return lax.dot_general(a.astype(BF16), b.astype(BF16), (((0,), (0,)), ((), ())),
                           preferred_element_type=F32)


def _each(f, *lists):
    return [f(*args) for args in zip(*lists)]


def _run_staggered(programs, wave, lag):
    live = list(enumerate(programs))
    slot = 0
    while live:
        still = []
        for p, prog in live:
            if slot >= (p // wave) * lag:
                try:
                    next(prog)
                except StopIteration:
                    continue
            still.append((p, prog))
        live = still
        slot += 1


def _rows(*parts):
    return jnp.concatenate(parts, axis=0)


def _split_bf16(x, pieces):
    out = []
    for _ in range(pieces - 1):
        p = x.astype(BF16)
        out.append(p)
        x = x - p.astype(F32)
    out.append(x.astype(BF16))
    return out


def _dot_exact_rhs(x, m, pieces=3):
    mb = m.astype(BF16)
    acc = None
    for p in _split_bf16(x, pieces):
        t = jnp.dot(p, mb, preferred_element_type=F32)
        acc = t if acc is None else acc + t
    return acc


def _sigmoid(x):
    return 0.5 * jnp.tanh(0.5 * x) + 0.5


def _cumsum_rows(x):
    n = x.shape[0]
    row = lax.broadcasted_iota(jnp.int32, x.shape, 0)
    step = 1
    while step < n:
        x = x + jnp.where(row >= step, pltpu.roll(x, step, 0), 0.0)
        step *= 2
    return x


def _head_block_mask():
    a = lax.broadcasted_iota(jnp.int32, (LANES, LANES), 0) // B_HEAD
    b = lax.broadcasted_iota(jnp.int32, (LANES, LANES), 1) // B_HEAD
    return jnp.where(a == b, 1.0, 0.0).astype(F32)


SRC_IF = CONV_COLS + 2 * A_VW
SRC_SHIFT = SRC_IF + 2 * A_HEADS
SRC_GATE = SRC_SHIFT + SHIFT_COLS
SRC_COLS = SRC_GATE + 2 * D_MODEL
RELAYOUT_ROWS = TAIL_MIX


def _relayout_src_row(t):
    unit = 2 * SUBLANES
    dst = t * (RELAYOUT_ROWS // unit)
    src = jnp.where(dst < COL_R // unit, dst,
                    jnp.where(dst < COL_GA // unit, dst + (SRC_SHIFT - COL_R) // unit,
                              jnp.where(dst < COL_TAIL // unit, dst + (SRC_GATE - COL_GA) // unit,
                                        jnp.where(dst < (COL_TAIL + TAIL_MIX) // unit,
                                                  (SRC_SHIFT + 3 * B_W) // unit, SRC_IF // unit))))
    return src * unit


def _relayout_body(w_ref, o_ref):
    o_ref[...] = w_ref[...].astype(BF16)


def _relayout_weight(wt):
    d = wt.shape[1]
    assert wt.shape[0] == SRC_COLS and PROJ_COLS == COL_TAIL + 2 * RELAYOUT_ROWS and COL_R == SRC_IF
    assert SRC_SHIFT + 3 * B_W + RELAYOUT_ROWS <= SRC_COLS and SRC_SHIFT % (2 * SUBLANES) == 0
    return pl.pallas_call(
        _relayout_body,
        grid=(PROJ_COLS // RELAYOUT_ROWS,),
        in_specs=[pl.BlockSpec((pl.Element(RELAYOUT_ROWS), pl.Element(d)), lambda t: (_relayout_src_row(t), 0))],
        out_specs=pl.BlockSpec((RELAYOUT_ROWS, d), lambda t: (t, 0)),
        out_shape=jax.ShapeDtypeStruct((PROJ_COLS, d), BF16),
        compiler_params=_cparams(("parallel",), 32),
        name="relayout_w_in",
    )(wt)


def _norm_rows(x_ref, g_ref, out_ref, row_chunk):
    def body(c, carry):
        rows = pl.ds(pl.multiple_of(c * row_chunk, row_chunk), row_chunk)
        x = x_ref[rows, :]
        ms = jnp.mean(x * x, axis=-1, keepdims=True)
        out_ref[rows, :] = (x * lax.rsqrt(ms + EPS) * g_ref[...]).astype(out_ref.dtype)
        return carry
    lax.fori_loop(0, x_ref.shape[0] // row_chunk, body, 0)


def _norm_bf16(x, g):
    ms = jnp.mean(x * x, axis=-1, keepdims=True)
    return (x * lax.rsqrt(ms + EPS) * g).astype(BF16)


def _norm_matmul_body(x0_ref, xn_ref, g_ref, w_ref, o_ref, hn_ref, *, row_chunk):
    i = pl.program_id(0)
    j = pl.program_id(1)
    slot = i % 2
    step_rows = xn_ref.shape[0]

    @pl.when((i == 0) & (j == 0))
    def _():
        _norm_rows(x0_ref, g_ref, hn_ref.at[0], row_chunk)

    o_ref[...] = lax.dot_general(hn_ref[slot], w_ref[...], (((1,), (1,)), ((), ())), preferred_element_type=F32)
    s = jnp.minimum(j, hn_ref.shape[1] // step_rows - 1)
    rows = pl.ds(pl.multiple_of(s * step_rows, step_rows), step_rows)
    hn_ref[1 - slot, rows, :] = _norm_bf16(xn_ref[...], g_ref[...])


def _norm_matmul(x, g, w, tm):
    n, d = x.shape
    p = w.shape[0]
    nj = p // PROJ_TN
    step_rows = tm // (nj - 1)
    assert step_rows * (nj - 1) == tm and step_rows % SUBLANES == 0
    n_slices = n // step_rows
    per_tile = nj - 1
    return pl.pallas_call(
        functools.partial(_norm_matmul_body, row_chunk=min(tm, LANES)),
        grid=(n // tm, nj),
        in_specs=[pl.BlockSpec((tm, d), lambda i, j: (0, 0)),
                  pl.BlockSpec((step_rows, d),
                               lambda i, j: (jnp.minimum((i + 1) * per_tile + jnp.minimum(j, per_tile - 1),
                                                         n_slices - 1), 0)),
                  pl.BlockSpec((1, d), lambda i, j: (0, 0)),
                  pl.BlockSpec((PROJ_TN, d), lambda i, j: (j, 0))],
        out_specs=pl.BlockSpec((tm, PROJ_TN), lambda i, j: (i, j)),
        out_shape=jax.ShapeDtypeStruct((n, p), F32),
        scratch_shapes=[pltpu.VMEM((2, tm, d), BF16)],
        compiler_params=_cparams(("arbitrary", "arbitrary"), 48),
        name="in_proj",
    )(x, x, g, w)


def _norm_matmul_pair_body(x_ref, xs_ref, g_ref, w_ref, o_ref, os_ref, hn_ref, *, row_chunk):
    n = x_ref.shape[0]

    @pl.when(pl.program_id(0) == 0)
    def _():
        _norm_rows(x_ref, g_ref, hn_ref, row_chunk)
        hn_ref[n:, :] = _norm_bf16(xs_ref[...], g_ref[...])

    res = lax.dot_general(hn_ref[...], w_ref[...], (((1,), (1,)), ((), ())), preferred_element_type=F32)
    o_ref[...] = res[:n]
    os_ref[...] = res[n:]


def _norm_matmul_pair(x, x_small, g, w):
    n, d = x.shape
    ns = x_small.shape[0]
    p = w.shape[0]
    return pl.pallas_call(
        functools.partial(_norm_matmul_pair_body, row_chunk=LANES),
        grid=(p // PROJ_TN,),
        in_specs=[pl.BlockSpec((n, d), lambda j: (0, 0)),
                  pl.BlockSpec((ns, d), lambda j: (0, 0)),
                  pl.BlockSpec((1, d), lambda j: (0, 0)),
                  pl.BlockSpec((PROJ_TN, d), lambda j: (j, 0))],
        out_specs=[pl.BlockSpec((n, PROJ_TN), lambda j: (0, j)),
                   pl.BlockSpec((ns, PROJ_TN), lambda j: (0, j))],
        out_shape=[jax.ShapeDtypeStruct((n, p), F32), jax.ShapeDtypeStruct((ns, p), F32)],
        scratch_shapes=[pltpu.VMEM((n + ns, d), BF16)],
        compiler_params=_cparams(("arbitrary",), 48),
        name="in_proj_pair",
    )(x, x_small, g, w)


def _mlstm_body(u_ref, v_ref, o_ref, ga_ref, gc_ref, gr_ref, e_ref, cw_ref, cb_ref, brow_ref, bcol_ref,
                nw_ref, c0_ref, n0_ref, m0_ref,
                h_ref, cn_ref, nn_ref, mn_ref, c_s, n_s, m_s, e_s, *, T, single_chunk):
    c = pl.program_id(1)
    if single_chunk:
        c_in, n_in, m_in, e_in = c0_ref.at[0], n0_ref.at[0], m0_ref.at[0], e_ref.at[0]
        c_out, n_out, m_out = cn_ref.at[0], nn_ref.at[0], mn_ref.at[0]
    else:
        c_in = c_out = c_s
        n_in = n_out = n_s
        m_in = m_out = m_s
        e_in = e_s

        @pl.when(c == 0)
        def _():
            c_s[...] = c0_ref[0]
            n_s[...] = n0_ref[0]
            m_s[...] = m0_ref[0]
            e_s[...] = e_ref[0]

    u = u_ref[...]
    ext = _rows(e_in[...], u)
    w = cw_ref[...]
    acc = cb_ref[...] + u * w[A_CONV - 1:A_CONV, :]
    for sh in range(1, A_CONV):
        acc = acc + pltpu.roll(ext, sh, 0)[SUBLANES:, :] * w[A_CONV - 1 - sh:A_CONV - sh, :]
    if not single_chunk:
        e_s[...] = u[T - SUBLANES:, :]
    qk_act = acc * _sigmoid(acc)

    gcol = gc_ref[...] + brow_ref[...]
    grow = gr_ref[0] + bcol_ref[...]
    lf_col = -_softplus(-gcol)
    ig_row = grow[:A_HEADS, :]
    lf_row = -_softplus(-grow[A_HEADS:, :])
    ri = lax.broadcasted_iota(jnp.int32, (T, T), 0)
    ci = lax.broadcasted_iota(jnp.int32, (T, T), 1)
    tri = ci <= ri
    b_col = _cumsum_rows(lf_col)
    b_row = _dot_exact_rhs(lf_row, jnp.where(ri <= ci, 1.0, 0.0))
    m_prev = m_in[...]
    lane = lax.broadcasted_iota(jnp.int32, (1, LANES), 1)
    heads = list(range(A_HEADS))
    bc = [b_col[:, A_HEADS + h:A_HEADS + h + 1] for h in heads]
    mh = [m_prev[:, A_HEADS + h:A_HEADS + h + 1] for h in heads]
    br = [b_row[h:h + 1, :] for h in heads]
    igr = [ig_row[h:h + 1, :] for h in heads]
    igc = [gcol[:, h:h + 1] for h in heads]
    qh = [qk_act[:, h * A_DK:(h + 1) * A_DK] for h in heads]
    kh = [qk_act[:, A_QK + h * A_DK:A_QK + (h + 1) * A_DK] * (A_DK ** -0.5) for h in heads]
    vh = [v_ref[:, h * A_DV:(h + 1) * A_DV] for h in heads]
    ch = [c_in[h] for h in heads]
    nh = [n_in[h:h + 1, :] for h in heads]
    qk = _each(_dot_nt, qh, kh)
    qc = _each(_dot, qh, ch)
    d = _each(lambda a, b, i: jnp.where(tri, a - b + i, -jnp.inf), bc, br, igr)
    inter = _each(jnp.add, bc, mh)
    m_t = _each(lambda i, x: jnp.maximum(i, jnp.max(x, axis=-1, keepdims=True)), inter, d)
    s = _each(lambda a, x, m: a * jnp.exp(x - m), qk, d, m_t)
    e_in = _each(lambda i, m: jnp.exp(i - m), inter, m_t)
    sv = _each(_dot, s, vh)
    num = _each(lambda a, x, b: a + x * b, sv, e_in, qc)
    den = _each(lambda a, x, qq, nn: jnp.sum(a, axis=-1, keepdims=True)
                + x * jnp.sum(qq * nn, axis=-1, keepdims=True), s, e_in, qh, nh)
    hh = _each(lambda a, b, m: a / jnp.maximum(jnp.abs(b), jnp.exp(-m)), num, den, m_t)
    hh = _each(lambda a: a * lax.rsqrt(jnp.mean(a * a, axis=-1, keepdims=True) + EPS), hh)
    b_end = _each(lambda a: a[T - 1:T, :], bc)
    m_new = _each(lambda be, m, b, i: jnp.maximum(be + m, jnp.max(be - b + i, axis=-1, keepdims=True)),
                  b_end, mh, br, igr)
    carry = _each(lambda be, m, mn: jnp.exp(be + m - mn), b_end, mh, m_new)
    wk = _each(lambda kk_, be, a, i, mn: kk_ * jnp.exp(be - a + i - mn), kh, b_end, bc, igc, m_new)
    kv = _each(_dot_tn, wk, vh)
    m_next = m_prev
    for h in heads:
        cols = slice(h * A_DV, (h + 1) * A_DV)
        gate = _sigmoid(o_ref[:, cols]) * _sigmoid(ga_ref[:, cols])
        h_ref[:, cols] = (hh[h] * nw_ref[:, cols] * gate).astype(h_ref.dtype)
        c_out[h] = carry[h] * ch[h] + kv[h]
        n_out[h:h + 1, :] = carry[h] * nh[h] + jnp.sum(wk[h], axis=0, keepdims=True)
        m_next = jnp.where(lane == A_HEADS + h, m_new[h], m_next)
    m_out[...] = m_next

    if not single_chunk:
        @pl.when(c == pl.num_programs(1) - 1)
        def _():
            cn_ref[0] = c_s[...]
            nn_ref[0] = n_s[...]
            mn_ref[0] = m_s[...]


def _mlstm(proj, gates_row, e_conv, conv_w, conv_b, bias_row, bias_col, norm_w, c0, n0, m0, *, T, n_seq, n_chunk):
    row = lambda s, c: (s * n_chunk + c, 0)
    fixed = lambda s, c: (0, 0)
    wide = lambda col: pl.BlockSpec((T, D_MODEL), lambda s, c: (s * n_chunk + c, col // D_MODEL))
    return pl.pallas_call(
        functools.partial(_mlstm_body, T=T, single_chunk=n_chunk == 1),
        grid=(n_seq, n_chunk),
        in_specs=[wide(COL_CONV), wide(COL_V), wide(COL_O), wide(COL_GA),
                  pl.BlockSpec((T, LANES), lambda s, c: (s * n_chunk + c, (COL_TAIL + TAIL_IF) // LANES)),
                  pl.BlockSpec((1, 2 * A_HEADS, T), lambda s, c: (s * n_chunk + c, 0, 0)),
                  pl.BlockSpec((1, SUBLANES, CONV_COLS), lambda s, c: (s, 0, 0)),
                  pl.BlockSpec((A_CONV, CONV_COLS), fixed),
                  pl.BlockSpec((1, CONV_COLS), fixed),
                  pl.BlockSpec((1, LANES), fixed),
                  pl.BlockSpec((2 * A_HEADS, 1), fixed),
                  pl.BlockSpec((1, A_VW), fixed),
                  pl.BlockSpec((1, A_HEADS, A_DK, A_DV), lambda s, c: (s, 0, 0, 0)),
                  pl.BlockSpec((1, A_HEADS, A_DK), lambda s, c: (s, 0, 0)),
                  pl.BlockSpec((1, 1, LANES), lambda s, c: (s, 0, 0))],
        out_specs=[pl.BlockSpec((T, A_VW), row),
                   pl.BlockSpec((1, A_HEADS, A_DK, A_DV), lambda s, c: (s, 0, 0, 0)),
                   pl.BlockSpec((1, A_HEADS, A_DK), lambda s, c: (s, 0, 0)),
                   pl.BlockSpec((1, 1, LANES), lambda s, c: (s, 0, 0))],
        out_shape=[jax.ShapeDtypeStruct((n_seq * n_chunk * T, A_VW), BF16),
                   jax.ShapeDtypeStruct((n_seq, A_HEADS, A_DK, A_DV), F32),
                   jax.ShapeDtypeStruct((n_seq, A_HEADS, A_DK), F32),
                   jax.ShapeDtypeStruct((n_seq, 1, LANES), F32)],
        scratch_shapes=[pltpu.VMEM((A_HEADS, A_DK, A_DV), F32),
                        pltpu.VMEM((A_HEADS, A_DK), F32),
                        pltpu.VMEM((1, LANES), F32),
                        pltpu.VMEM((SUBLANES, CONV_COLS), F32)],
        compiler_params=_cparams(("arbitrary", "arbitrary"), 32),
        name="mlstm_chunk",
    )(proj, proj, proj, proj, proj, gates_row, e_conv, conv_w, conv_b, bias_row, bias_col, norm_w, c0, n0, m0)


def _kcat_dot(a1, b1, a2, b2):
    if a1.shape[1] % LANES == 0:
        return _dot(jnp.concatenate([a1, a2], axis=1), _rows(b1, b2))
    return _dot(a1, b1) + _dot(a2, b2)


def _head_sums(ones_blk, T, *xs):
    sums = jnp.dot(_rows(*xs).astype(BF16), ones_blk, preferred_element_type=F32)
    return [sums[i * T:(i + 1) * T] for i in range(len(xs))]


def _rwkv_body(ur_ref, uk_ref, uv_ref, ul_ref, gb_ref, er_ref, ek_ref, ev_ref, el_ref,
               mur_ref, muk_ref, muv_ref, mul_ref, w0_ref, w2_ref, a0_ref, a2_ref, g2_ref, kkw_ref, ka_ref,
               rk_ref, lnw_ref, lnb_ref, blk_ref, blkb_ref, s0_ref,
               hb_ref, sn_ref, s_s, cr_s, ck_s, cv_s, cl_s, *, T, n_pairs):
    c = pl.program_id(2)

    @pl.when(c == 0)
    def _():
        z = jnp.zeros((B_HEAD, B_HEAD), F32)
        for p in range(n_pairs):
            s_s[p] = _rows(jnp.concatenate([s0_ref[0, 2 * p], z], axis=1),
                           jnp.concatenate([z, s0_ref[0, 2 * p + 1]], axis=1))
        cr_s[...] = er_ref[0]
        ck_s[...] = ek_ref[0]
        cv_s[...] = ev_ref[0]
        cl_s[...] = el_ref[0]

    def shift_mix(u_ref, carry_ref, mu_ref, cs):
        u = u_ref[:, cs]
        prev = pltpu.roll(_rows(carry_ref[:, cs], u), 1, 0)[SUBLANES:, :]
        carry_ref[:, cs] = u[T - SUBLANES:, :]
        return u + mu_ref[:, cs] * (prev - u)

    xl = shift_mix(ul_ref, cl_s, mul_ref, slice(None))
    th_wd = jnp.tanh(xl[:, :LANES]).astype(BF16)
    x_ad = pltpu.roll(xl[:, :2 * LANES], 2 * LANES - LORA_DECAY, 1)[:, :LANES].astype(BF16)
    gd_off = LORA_DECAY + LORA_ICLR - TAIL_GD_WIN
    sg_gd = _sigmoid(pltpu.roll(xl[:, TAIL_GD_WIN:], TAIL_MIX - TAIL_GD_WIN - gd_off, 1)[:, :LORA_GATE]).astype(BF16)

    lo = lax.broadcasted_iota(jnp.int32, (1, LANES), 1) < B_HEAD
    head1 = lambda x: jnp.where(lo, x, jnp.zeros_like(x))
    head2 = lambda x: jnp.where(lo, jnp.zeros_like(x), x)
    r2 = lax.broadcasted_iota(jnp.int32, (T, 2 * T), 0)
    c2 = lax.broadcasted_iota(jnp.int32, (T, 2 * T), 1)
    left = c2 < T
    cj = jnp.where(left, c2, c2 - T)
    strict2 = cj < r2
    incl2 = cj <= r2
    eye_pair = jnp.where(cj == r2, 1.0, 0.0)
    ones_blk = blkb_ref[...]
    zero = jnp.zeros((T, LANES), BF16)
    aligned = (2 * T) % LANES == 0

    def block_diag(sb):
        z = jnp.zeros_like(sb)
        return _rows(jnp.where(left, sb, z), jnp.where(left, z, sb))

    def pair_program(p):
        cs = slice(p * LANES, (p + 1) * LANES)
        r = shift_mix(ur_ref, cr_s, mur_ref, cs)
        xk = shift_mix(uk_ref, ck_s, muk_ref, cs)
        v = shift_mix(uv_ref, cv_s, muv_ref, cs)
        lw = jnp.dot(th_wd, w2_ref[:, cs], preferred_element_type=F32)
        la = jnp.dot(x_ad, a2_ref[:, cs], preferred_element_type=F32)
        lg = jnp.dot(sg_gd, g2_ref[:, cs], preferred_element_type=F32)
        kkw = xk * kkw_ref[:, cs]
        ssq, = _head_sums(ones_blk, T, kkw * kkw)
        yield
        ld = DECAY_SCALE * _sigmoid(w0_ref[:, cs] + lw)
        a = _sigmoid(a0_ref[:, cs] + la)
        g = lg * _sigmoid(gb_ref[:, cs])
        k = xk * (1.0 + (a - 1.0) * ka_ref[:, cs])
        kk = kkw * lax.rsqrt(jnp.maximum(ssq, 1e-24))
        bi = kk * a
        cum = _cumsum_rows(ld)
        ewt = jnp.exp(cum[T - 1:T, :])
        Rt = (r * jnp.exp(cum)).astype(BF16)
        At = (-kk * jnp.exp(cum - ld)).astype(BF16)
        einv = jnp.exp(-cum)
        Bt = bi * einv
        Kt = k * einv
        BhKh = _rows(Bt * ewt, Kt * ewt).astype(BF16)
        Bt = Bt.astype(BF16)
        Kt = Kt.astype(BF16)
        vb = v.astype(BF16)
        V1 = head1(vb)
        V2 = head2(vb)
        if aligned:
            G = _dot_nt(_rows(head1(At), head1(Rt), head2(At), head2(Rt)), _rows(Bt, Kt))
            G1, G2 = G[:2 * T], G[2 * T:]
        else:
            G1 = _dot_nt(_rows(head1(At), head1(Rt)), _rows(Bt, Kt))
            G2 = _dot_nt(_rows(head2(At), head2(Rt)), _rows(Kt, Bt))
        yield
        LL1 = jnp.where(strict2, G1[:T], 0.0)
        LL2 = jnp.where(strict2, G2[:T], 0.0)
        P1 = jnp.where(incl2, G1[T:], 0.0)
        P2 = jnp.where(incl2, G2[T:], 0.0)
        lab2 = pltpu.roll(LL2, T, 1) if aligned else LL2
        pw = jnp.where(left, LL1, lab2).astype(BF16)
        iv = eye_pair + jnp.where(left, LL1, lab2)
        X0 = _kcat_dot(LL1, _rows(zero, V1), LL2, _rows(zero, V2) if aligned else _rows(V2, zero))
        pw = _dot(pw, block_diag(pw)).astype(BF16)
        yield
        span = 4
        while span < T:
            both = _dot(_rows(iv.astype(BF16), pw), block_diag(pw))
            iv, pw = iv + both[:T], both[T:].astype(BF16)
            span *= 2
            yield
        iv = (iv + _dot(iv, block_diag(pw))).astype(BF16)
        S = s_s[p]
        AR = _dot_nt(_rows(At, Rt), S)
        yield
        X = (AR[:T] + X0).astype(BF16)
        U = _dot(iv, _rows(head1(X), head2(X))).astype(BF16)
        yield
        if aligned:
            rhs2 = _rows(head2(U), V2)
        else:
            rhs2 = _rows(V2, head2(U))
        o = AR[T:] + _kcat_dot(P1, _rows(head1(U), V1), P2, rhs2)
        s_s[p] = S * ewt + blk_ref[...] * _dot_tn(_rows(U, vb), BhKh)
        yield
        o_sum, rk_sum = _head_sums(ones_blk, T, o, r * k * rk_ref[:, cs])
        yield
        oc = o - o_sum * (1.0 / B_HEAD)
        var, = _head_sums(ones_blk, T, oc * oc)
        yield
        on = oc * lax.rsqrt(var * (1.0 / B_HEAD) + GN_EPS) * lnw_ref[:, cs] + lnb_ref[:, cs]
        hb_ref[:, cs] = ((on + rk_sum * v) * g).astype(hb_ref.dtype)

    _run_staggered([pair_program(p) for p in range(n_pairs)], RWKV_WAVE, RWKV_LAG)

    @pl.when(c == pl.num_programs(2) - 1)
    def _():
        for p in range(n_pairs):
            sn_ref[0, 2 * p] = s_s[p, :B_HEAD, :B_HEAD]
            sn_ref[0, 2 * p + 1] = s_s[p, B_HEAD:, B_HEAD:]


def _rwkv(proj, e_main, e_tail, mu_main, mu_tail, w0, w2p, a0, a2p, g2, k_k, k_a, rk, lnw, lnb, blk, s0, *,
          T, n_seq, n_chunk):
    assert T <= RWKV_CHUNK
    gw = GROUP_W
    wide = lambda col: pl.BlockSpec((T, gw), lambda s, pg, c: (s * n_chunk + c, col // gw + pg))
    edge = lambda col: pl.BlockSpec((1, SUBLANES, gw), lambda s, pg, c: (s, 0, col // gw + pg))
    vec = lambda col: pl.BlockSpec((1, gw), lambda s, pg, c: (0, col // gw + pg))
    mat = lambda rows: pl.BlockSpec((rows, gw), lambda s, pg, c: (0, pg))
    st = pl.BlockSpec((1, 2 * GROUP_PAIRS, B_HEAD, B_HEAD), lambda s, pg, c: (s, pg, 0, 0))
    return pl.pallas_call(
        functools.partial(_rwkv_body, T=T, n_pairs=GROUP_PAIRS),
        grid=(n_seq, N_PAIRS // GROUP_PAIRS, n_chunk),
        in_specs=[wide(COL_R), wide(COL_K), wide(COL_VB),
                  pl.BlockSpec((T, TAIL_MIX), lambda s, pg, c: (s * n_chunk + c, COL_TAIL // TAIL_MIX)),
                  wide(COL_GB),
                  edge(0), edge(B_W), edge(2 * B_W),
                  pl.BlockSpec((1, SUBLANES, TAIL_MIX), lambda s, pg, c: (s, 0, 0)),
                  vec(0), vec(B_W), vec(2 * B_W),
                  pl.BlockSpec((1, TAIL_MIX), lambda s, pg, c: (0, 0)),
                  vec(0), mat(LANES), vec(0), mat(LANES), mat(LORA_GATE), vec(0), vec(0),
                  vec(0), vec(0), vec(0),
                  pl.BlockSpec((LANES, LANES), lambda s, pg, c: (0, 0)),
                  pl.BlockSpec((LANES, LANES), lambda s, pg, c: (0, 0)), st],
        out_specs=[pl.BlockSpec((T, gw), lambda s, pg, c: (s * n_chunk + c, pg)), st],
        out_shape=[jax.ShapeDtypeStruct((n_seq * n_chunk * T, B_W), BF16),
                   jax.ShapeDtypeStruct((n_seq, B_HEADS, B_HEAD, B_HEAD), F32)],
        scratch_shapes=[pltpu.VMEM((GROUP_PAIRS, LANES, LANES), F32),
                        pltpu.VMEM((SUBLANES, gw), F32), pltpu.VMEM((SUBLANES, gw), F32),
                        pltpu.VMEM((SUBLANES, gw), F32), pltpu.VMEM((SUBLANES, TAIL_MIX), F32)],
        compiler_params=_cparams(("arbitrary", "arbitrary", "arbitrary"), 32),
        name="rwkv_chunk",
    )(proj, proj, proj, proj, proj, e_main, e_main, e_main, e_tail, mu_main, mu_main, mu_main, mu_tail,
      w0, w2p, a0, a2p, g2, k_k, k_a, rk, lnw, lnb, blk, blk.astype(BF16), s0)


def _merge_out_body(ha_ref, hb_ref, x_ref, w_ref, g_ref, y_ref):
    mixed = ha_ref[...].astype(F32) + hb_ref[...].astype(F32)
    y = jnp.dot(mixed.astype(BF16), w_ref[...], preferred_element_type=F32)
    ms = jnp.mean(y * y, axis=-1, keepdims=True)
    y_ref[...] = x_ref[...] + y * lax.rsqrt(ms + EPS) * g_ref[...]


def _merge_out(h_a, h_b, x, w_out, g_post):
    n = x.shape[0]
    wide = pl.BlockSpec((MERGE_TM, D_MODEL), lambda i: (i, 0))
    return pl.pallas_call(
        _merge_out_body,
        grid=(n // MERGE_TM,),
        in_specs=[wide, wide, wide,
                  pl.BlockSpec((D_MODEL, D_MODEL), lambda i: (0, 0)),
                  pl.BlockSpec((1, D_MODEL), lambda i: (0, 0))],
        out_specs=wide,
        out_shape=jax.ShapeDtypeStruct((n, D_MODEL), F32),
        compiler_params=_cparams(("parallel",), 48),
        name="merge_out_proj",
    )(h_a, h_b, x, w_out, g_post)


def _mlp_body(x_ref, xn_ref, g1_ref, wu_ref, wd_ref, g2_ref, y_ref, hn_ref, acc_ref, *, row_chunk):
    i = pl.program_id(0)
    j = pl.program_id(1)
    last = pl.num_programs(1) - 1
    slot = i % 2
    step_rows = xn_ref.shape[0]

    @pl.when((i == 0) & (j == 0))
    def _():
        _norm_rows(x_ref, g1_ref, hn_ref.at[0], row_chunk)
        acc_ref[...] = jnp.zeros_like(acc_ref)

    def partial_product():
        u = jnp.dot(hn_ref[slot], wu_ref[...], preferred_element_type=F32)
        a = jnp.square(jnp.maximum(u, 0.0))
        part = jnp.dot(a.astype(BF16), wd_ref[...], preferred_element_type=F32)
        rows = pl.ds(pl.multiple_of(j * step_rows, step_rows), step_rows)
        hn_ref[1 - slot, rows, :] = _norm_bf16(xn_ref[...], g1_ref[...])
        return part

    @pl.when(j < last)
    def _():
        part = partial_product()
        acc_ref[...] = jnp.where(j == 0, part, acc_ref[...] + part)

    @pl.when(j == last)
    def _():
        ff = acc_ref[...] + partial_product()
        ms = jnp.mean(ff * ff, axis=-1, keepdims=True)
        y_ref[...] = x_ref[...] + ff * lax.rsqrt(ms + EPS) * g2_ref[...]


def _mlp(x, g_pre, w_up, w_down, g_post):
    n, d = x.shape
    f = w_up.shape[1]
    nj = f // MLP_TF
    step_rows = MLP_TM // nj
    n_slices = n // step_rows
    return pl.pallas_call(
        functools.partial(_mlp_body, row_chunk=LANES),
        grid=(n // MLP_TM, nj),
        in_specs=[pl.BlockSpec((MLP_TM, d), lambda i, j: (i, 0)),
                  pl.BlockSpec((step_rows, d), lambda i, j: (jnp.minimum((i + 1) * nj + j, n_slices - 1), 0)),
                  pl.BlockSpec((1, d), lambda i, j: (0, 0)),
                  pl.BlockSpec((d, MLP_TF), lambda i, j: (0, j)),
                  pl.BlockSpec((MLP_TF, d), lambda i, j: (j, 0)),
                  pl.BlockSpec((1, d), lambda i, j: (0, 0))],
        out_specs=pl.BlockSpec((MLP_TM, d), lambda i, j: (i, 0)),
        out_shape=jax.ShapeDtypeStruct((n, d), F32),
        scratch_shapes=[pltpu.VMEM((2, MLP_TM, d), BF16), pltpu.VMEM((MLP_TM, d), F32)],
        compiler_params=_cparams(("arbitrary", "arbitrary"), 56),
        name="mlp",
    )(x, x, g_pre, w_up, w_down, g_post)


def _pad_cols(a, width):
    return jnp.pad(a, [(0, 0)] * (a.ndim - 1) + [(0, width - a.shape[-1])])


def _pad_rows(a, before, total):
    return jnp.pad(a, ((before, total - before - a.shape[0]), (0, 0)))


def _edge_blocks(prev_rows):
    return jnp.pad(prev_rows, ((0, 0), (SUBLANES - prev_rows.shape[1], 0), (0, 0)))


def _gates_row_form(proj, T):
    g = proj[:, COL_TAIL + TAIL_IF:COL_TAIL + TAIL_IF + 2 * A_HEADS]
    return jnp.swapaxes(g.reshape(-1, T, 2 * A_HEADS), 1, 2)


def kernel(x_prompt, x_sample, state_mlstm_C, state_mlstm_n, state_mlstm_m, state_mlstm_conv, state_rwkv_S, state_rwkv_shift, meta_tokens, norm_mix_pre, norm_mix_post, norm_mlp_pre, norm_mlp_post, w_in, mlstm_conv_w, mlstm_conv_b, mlstm_b_i, mlstm_b_f, mlstm_norm_w, rwkv_mu, rwkv_w0, rwkv_w2, rwkv_a0, rwkv_a2, rwkv_g2, rwkv_k_k, rwkv_k_a, rwkv_r_k, rwkv_ln_w, rwkv_ln_b, w_out, w_up, w_down):
    n_s_seq, s_len, _ = x_sample.shape
    _, p_len, _ = x_prompt.shape
    k3 = A_CONV - 1

    w_all = _relayout_weight(jnp.swapaxes(w_in[0], 0, 1))
    mu = rwkv_mu[0][None, :]
    mu_main = mu[:, :3 * B_W]
    mu_tail = _pad_cols(mu[:, 3 * B_W:], TAIL_MIX)
    w2p = _pad_rows(rwkv_w2[0], 0, LANES).astype(BF16)
    a2p = _pad_rows(rwkv_a2[0], 0, LANES).astype(BF16)
    g2 = rwkv_g2[0].astype(BF16)
    g_pre = norm_mix_pre[0][None, :]
    bias = jnp.concatenate([mlstm_b_i[0], mlstm_b_f[0]])
    bias_row = _pad_cols(bias[None, :], LANES)
    bias_col = bias[:, None]
    norm_w = mlstm_norm_w[0][None, :]
    conv_w = mlstm_conv_w[0]
    conv_b = mlstm_conv_b[0][None, :]
    rwkv_vecs = (rwkv_w0[0][None, :], w2p, rwkv_a0[0][None, :], a2p, g2, rwkv_k_k[0][None, :],
                 rwkv_k_a[0][None, :], rwkv_r_k[0].reshape(1, B_W), rwkv_ln_w[0][None, :], rwkv_ln_b[0][None, :],
                 _head_block_mask())
    w_out_b = w_out[0].astype(BF16)
    w_up_b = w_up[0].astype(BF16)
    w_down_b = w_down[0].astype(BF16)

    def branches(proj, T, n_seq, n_chunk, e_conv, e_main, e_tail, c0, n0, m0, s0):
        tm_ = MLSTM_CHUNK if n_chunk > 1 else T
        h_a, c_n, n_n, m_n = _mlstm(proj, _gates_row_form(proj, tm_), e_conv, conv_w, conv_b, bias_row, bias_col,
                                    norm_w, c0, n0, m0, T=tm_, n_seq=n_seq, n_chunk=n_chunk * T // tm_)
        h_b, s_n = _rwkv(proj, e_main, e_tail, mu_main, mu_tail, *rwkv_vecs, s0,
                         T=T, n_seq=n_seq, n_chunk=n_chunk)
        return h_a, h_b, (c_n, n_n, m_n, s_n)

    def finish(h_a, h_b, x_rows):
        x1 = _merge_out(h_a, h_b, x_rows, w_out_b, norm_mix_post[0][None, :])
        return _mlp(x1, norm_mlp_pre[0][None, :], w_up_b, w_down_b, norm_mlp_post[0][None, :])

    zeros = lambda *shape: jnp.zeros(shape, F32)
    xs = x_sample.reshape(n_s_seq * s_len, D_MODEL).astype(F32)
    assert xs.shape[0] == PROJ_TM
    proj_s, proj_m = _norm_matmul_pair(xs, meta_tokens.astype(F32), g_pre, w_all)
    _, _, (c_m, n_m, m_m, s_m) = branches(
        proj_m, N_META, 1, 1,
        zeros(1, SUBLANES, CONV_COLS), zeros(1, SUBLANES, 3 * B_W), zeros(1, SUBLANES, TAIL_MIX),
        zeros(1, A_HEADS, A_DK, A_DV), zeros(1, A_HEADS, A_DK), zeros(1, 1, LANES),
        zeros(1, B_HEADS, B_HEAD, B_HEAD))

    xp = x_prompt[0].astype(F32)
    proj_p = _norm_matmul(xp, g_pre, w_all, PROJ_TM)
    ha_p, hb_p, (c_p, n_p, m_p, s_p) = branches(
        proj_p, RWKV_CHUNK, 1, p_len // RWKV_CHUNK,
        _edge_blocks(proj_m[None, N_META - k3:, COL_CONV:COL_CONV + CONV_COLS]),
        _edge_blocks(proj_m[None, N_META - 1:, COL_R:COL_R + 3 * B_W]),
        _edge_blocks(proj_m[None, N_META - 1:, COL_TAIL:COL_TAIL + TAIL_MIX]),
        c_m, n_m, m_m, s_m)
    y_prompt = finish(ha_p, hb_p, xp)[None].astype(x_prompt.dtype)

    sh = state_rwkv_shift[0].astype(F32)
    m0_s = jnp.pad(state_mlstm_m[0].astype(F32), ((0, 0), (A_HEADS, LANES - 2 * A_HEADS)))[:, None, :]
    ha_s, hb_s, (c_s, n_s, m_s, s_s) = branches(
        proj_s, s_len, n_s_seq, 1,
        _edge_blocks(state_mlstm_conv[0].astype(F32)),
        _edge_blocks(sh[:, :, :3 * B_W]), _edge_blocks(_pad_cols(sh[:, :, 3 * B_W:], TAIL_MIX)),
        state_mlstm_C[0].astype(F32), state_mlstm_n[0].astype(F32), m0_s,
        state_rwkv_S[0].astype(F32))
    y_sample = finish(ha_s, hb_s, xs).reshape(x_sample.shape).astype(x_sample.dtype)

    def shift_state(rows):
        return jnp.concatenate([rows[..., COL_R:COL_R + 3 * B_W],
                                rows[..., COL_TAIL:COL_TAIL + LORA_COLS]], axis=-1)

    last_s = proj_s.reshape(n_s_seq, s_len, PROJ_COLS)[:, s_len - k3:, :]
    conv_s = last_s[..., COL_CONV:COL_CONV + CONV_COLS]
    shift_s = shift_state(last_s[:, k3 - 1:, :])
    last_p = proj_p[p_len - k3:, :]
    conv_p = last_p[None, :, COL_CONV:COL_CONV + CONV_COLS]
    shift_p = shift_state(last_p[None, k3 - 1:, :])
    dt_c, dt_n, dt_m = state_mlstm_C.dtype, state_mlstm_n.dtype, state_mlstm_m.dtype
    dt_cv, dt_s, dt_sh = state_mlstm_conv.dtype, state_rwkv_S.dtype, state_rwkv_shift.dtype
    lanes_m = slice(A_HEADS, 2 * A_HEADS)
    return (y_prompt, y_sample,
            c_p[None].astype(dt_c), n_p[None].astype(dt_n), m_p[:, 0, lanes_m][None].astype(dt_m),
            conv_p[None].astype(dt_cv), s_p[None].astype(dt_s), shift_p[None].astype(dt_sh),
            c_s[None].astype(dt_c), n_s[None].astype(dt_n), m_s[:, 0, lanes_m][None].astype(dt_m),
            conv_s[None].astype(dt_cv), s_s[None].astype(dt_s), shift_s[None].astype(dt_sh))
```

```python
import functools

import jax
import jax.numpy as jnp
from jax import lax
from jax.experimental import pallas as pl
from jax.experimental.pallas import tpu as pltpu

F32 = jnp.float32
BF16 = jnp.bfloat16

D_MODEL = 2048
N_META = 16
EPS = 1e-6
D_FF = 4 * D_MODEL
A_HEADS = 8
A_DK = 128
A_DV = D_MODEL // A_HEADS
A_QK = A_HEADS * A_DK
A_VW = A_HEADS * A_DV
A_CONV = 4
MLSTM_CHUNK = 256
B_HEAD = 64
B_HEADS = D_MODEL // B_HEAD
B_W = B_HEADS * B_HEAD
LORA_DECAY = max(32, int(round(1.8 * D_MODEL ** 0.5 / 32)) * 32)
LORA_ICLR = max(32, int(round(1.8 * D_MODEL ** 0.5 / 32)) * 32)
LORA_GATE = max(32, int(round(0.6 * D_MODEL ** 0.8 / 32)) * 32)
GN_EPS = 64e-5
CONV_COLS = 2 * A_QK
SHIFT_COLS = 3 * B_W + LORA_DECAY + LORA_ICLR + LORA_GATE

LANES = 128
SUBLANES = 8

COL_CONV, COL_V, COL_O, COL_R, COL_K, COL_VB, COL_GA, COL_GB = (i * D_MODEL for i in range(8))
COL_TAIL = 8 * D_MODEL
LORA_COLS = LORA_DECAY + LORA_ICLR + LORA_GATE
TAIL_MIX = -(-LORA_COLS // LANES) * LANES
TAIL_IF = TAIL_MIX
TAIL_GD_WIN = (LORA_DECAY + LORA_ICLR) // LANES * LANES
PROJ_TN = 1024
PROJ_COLS = -(-(COL_TAIL + TAIL_IF + LANES) // PROJ_TN) * PROJ_TN
PROJ_TM = 1024
MERGE_TM = 512
MLP_TM, MLP_TF = 512, 1024
DECAY_SCALE = -0.6065306597126334
RWKV_CHUNK = 64
N_PAIRS = B_HEADS // 2
GROUP_PAIRS = 16
RWKV_CHUNKS_PER_STEP = 2
RWKV_WAVE, RWKV_LAG = 8, 1
GROUP_W = GROUP_PAIRS * LANES


def _cparams(semantics, vmem_mib):
    return pltpu.CompilerParams(dimension_semantics=semantics, vmem_limit_bytes=vmem_mib << 20)


def _softplus(x):
    return jnp.maximum(x, 0.0) + jnp.log1p(jnp.exp(-jnp.abs(x)))


def _dot(a, b):
    return jnp.dot(a.astype(BF16), b.astype(BF16), preferred_element_type=F32)


def _dot_nt(a, b):
    return lax.dot_general(a.astype(BF16), b.astype(BF16), (((1,), (1,)), ((), ())),
                           preferred_element_type=F32)


def _dot_tn(a, b):
    return lax.dot_general(a.astype(BF16), b.astype(BF16), (((0,), (0,)), ((), ())),
                           preferred_element_type=F32)


def _each(f, *lists):
    return [f(*args) for args in zip(*lists)]


def _run_staggered(programs, wave, lag):
    live = list(enumerate(programs))
    slot = 0
    while live:
        still = []
        for p, prog in live:
            if slot >= (p // wave) * lag:
                try:
                    next(prog)
                except StopIteration:
                    continue
            still.append((p, prog))
        live = still
        slot += 1


def _rows(*parts):
    return jnp.concatenate(parts, axis=0)


def _split_bf16(x, pieces):
    out = []
    for _ in range(pieces - 1):
        p = x.astype(BF16)
        out.append(p)
        x = x - p.astype(F32)
    out.append(x.astype(BF16))
    return out


def _dot_exact_rhs(x, m, pieces=3):
    mb = m.astype(BF16)
    acc = None
    for p in _split_bf16(x, pieces):
        t = jnp.dot(p, mb, preferred_element_type=F32)
        acc = t if acc is None else acc + t
    return acc


def _sigmoid(x):
    return 0.5 * jnp.tanh(0.5 * x) + 0.5


def _cumsum_rows(x):
    n = x.shape[0]
    row = lax.broadcasted_iota(jnp.int32, x.shape, 0)
    step = 1
    while step < n:
        x = x + jnp.where(row >= step, pltpu.roll(x, step, 0), 0.0)
        step *= 2
    return x


def _head_block_mask():
    a = lax.broadcasted_iota(jnp.int32, (LANES, LANES), 0) // B_HEAD
    b = lax.broadcasted_iota(jnp.int32, (LANES, LANES), 1) // B_HEAD
    return jnp.where(a == b, 1.0, 0.0).astype(F32)


SRC_IF = CONV_COLS + 2 * A_VW
SRC_SHIFT = SRC_IF + 2 * A_HEADS
SRC_GATE = SRC_SHIFT + SHIFT_COLS
SRC_COLS = SRC_GATE + 2 * D_MODEL
RELAYOUT_ROWS = TAIL_MIX


def _relayout_src_row(t):
    unit = 2 * SUBLANES
    dst = t * (RELAYOUT_ROWS // unit)
    src = jnp.where(dst < COL_R // unit, dst,
                    jnp.where(dst < COL_GA // unit, dst + (SRC_SHIFT - COL_R) // unit,
                              jnp.where(dst < COL_TAIL // unit, dst + (SRC_GATE - COL_GA) // unit,
                                        jnp.where(dst < (COL_TAIL + TAIL_MIX) // unit,
                                                  (SRC_SHIFT + 3 * B_W) // unit, SRC_IF // unit))))
    return src * unit


def _relayout_body(w_ref, o_ref):
    o_ref[...] = w_ref[...].astype(BF16)


def _relayout_weight(wt):
    d = wt.shape[1]
    assert wt.shape[0] == SRC_COLS and PROJ_COLS == COL_TAIL + 2 * RELAYOUT_ROWS and COL_R == SRC_IF
    assert SRC_SHIFT + 3 * B_W + RELAYOUT_ROWS <= SRC_COLS and SRC_SHIFT % (2 * SUBLANES) == 0
    return pl.pallas_call(
        _relayout_body,
        grid=(PROJ_COLS // RELAYOUT_ROWS,),
        in_specs=[pl.BlockSpec((pl.Element(RELAYOUT_ROWS), pl.Element(d)), lambda t: (_relayout_src_row(t), 0))],
        out_specs=pl.BlockSpec((RELAYOUT_ROWS, d), lambda t: (t, 0)),
        out_shape=jax.ShapeDtypeStruct((PROJ_COLS, d), BF16),
        compiler_params=_cparams(("parallel",), 32),
        name="relayout_w_in",
    )(wt)


def _norm_rows(x_ref, g_ref, out_ref, row_chunk):
    def body(c, carry):
        rows = pl.ds(pl.multiple_of(c * row_chunk, row_chunk), row_chunk)
        x = x_ref[rows, :]
        ms = jnp.mean(x * x, axis=-1, keepdims=True)
        out_ref[rows, :] = (x * lax.rsqrt(ms + EPS) * g_ref[...]).astype(out_ref.dtype)
        return carry
    lax.fori_loop(0, x_ref.shape[0] // row_chunk, body, 0)


def _norm_bf16(x, g):
    ms = jnp.mean(x * x, axis=-1, keepdims=True)
    return (x * lax.rsqrt(ms + EPS) * g).astype(BF16)


def _norm_matmul_body(x0_ref, xn_ref, g_ref, w_ref, o_ref, hn_ref, *, row_chunk):
    i = pl.program_id(0)
    j = pl.program_id(1)
    slot = i % 2
    step_rows = xn_ref.shape[0]

    @pl.when((i == 0) & (j == 0))
    def _():
        _norm_rows(x0_ref, g_ref, hn_ref.at[0], row_chunk)

    o_ref[...] = lax.dot_general(hn_ref[slot], w_ref[...], (((1,), (1,)), ((), ())), preferred_element_type=F32)
    s = jnp.minimum(j, hn_ref.shape[1] // step_rows - 1)
    rows = pl.ds(pl.multiple_of(s * step_rows, step_rows), step_rows)
    hn_ref[1 - slot, rows, :] = _norm_bf16(xn_ref[...], g_ref[...])


def _norm_matmul(x, g, w, tm):
    n, d = x.shape
    p = w.shape[0]
    nj = p // PROJ_TN
    step_rows = tm // (nj - 1)
    assert step_rows * (nj - 1) == tm and step_rows % SUBLANES == 0
    n_slices = n // step_rows
    per_tile = nj - 1
    return pl.pallas_call(
        functools.partial(_norm_matmul_body, row_chunk=min(tm, LANES)),
        grid=(n // tm, nj),
        in_specs=[pl.BlockSpec((tm, d), lambda i, j: (0, 0)),
                  pl.BlockSpec((step_rows, d),
                               lambda i, j: (jnp.minimum((i + 1) * per_tile + jnp.minimum(j, per_tile - 1),
                                                         n_slices - 1), 0)),
                  pl.BlockSpec((1, d), lambda i, j: (0, 0)),
                  pl.BlockSpec((PROJ_TN, d), lambda i, j: (j, 0))],
        out_specs=pl.BlockSpec((tm, PROJ_TN), lambda i, j: (i, j)),
        out_shape=jax.ShapeDtypeStruct((n, p), F32),
        scratch_shapes=[pltpu.VMEM((2, tm, d), BF16)],
        compiler_params=_cparams(("arbitrary", "arbitrary"), 48),
        name="in_proj",
    )(x, x, g, w)


def _norm_matmul_pair_body(x_ref, xs_ref, g_ref, w_ref, o_ref, os_ref, hn_ref, *, row_chunk):
    n = x_ref.shape[0]

    @pl.when(pl.program_id(0) == 0)
    def _():
        _norm_rows(x_ref, g_ref, hn_ref, row_chunk)
        hn_ref[n:, :] = _norm_bf16(xs_ref[...], g_ref[...])

    res = lax.dot_general(hn_ref[...], w_ref[...], (((1,), (1,)), ((), ())), preferred_element_type=F32)
    o_ref[...] = res[:n]
    os_ref[...] = res[n:]


def _norm_matmul_pair(x, x_small, g, w):
    n, d = x.shape
    ns = x_small.shape[0]
    p = w.shape[0]
    return pl.pallas_call(
        functools.partial(_norm_matmul_pair_body, row_chunk=LANES),
        grid=(p // PROJ_TN,),
        in_specs=[pl.BlockSpec((n, d), lambda j: (0, 0)),
                  pl.BlockSpec((ns, d), lambda j: (0, 0)),
                  pl.BlockSpec((1, d), lambda j: (0, 0)),
                  pl.BlockSpec((PROJ_TN, d), lambda j: (j, 0))],
        out_specs=[pl.BlockSpec((n, PROJ_TN), lambda j: (0, j)),
                   pl.BlockSpec((ns, PROJ_TN), lambda j: (0, j))],
        out_shape=[jax.ShapeDtypeStruct((n, p), F32), jax.ShapeDtypeStruct((ns, p), F32)],
        scratch_shapes=[pltpu.VMEM((n + ns, d), BF16)],
        compiler_params=_cparams(("arbitrary",), 48),
        name="in_proj_pair",
    )(x, x_small, g, w)


def _mlstm_body(u_ref, v_ref, o_ref, ga_ref, gc_ref, gr_ref, e_ref, cw_ref, cb_ref, brow_ref, bcol_ref,
                nw_ref, c0_ref, n0_ref, m0_ref,
                h_ref, cn_ref, nn_ref, mn_ref, c_s, n_s, m_s, e_s, *, T, single_chunk):
    c = pl.program_id(1)
    if single_chunk:
        c_in, n_in, m_in, e_in = c0_ref.at[0], n0_ref.at[0], m0_ref.at[0], e_ref.at[0]
        c_out, n_out, m_out = cn_ref.at[0], nn_ref.at[0], mn_ref.at[0]
    else:
        c_in = c_out = c_s
        n_in = n_out = n_s
        m_in = m_out = m_s
        e_in = e_s

        @pl.when(c == 0)
        def _():
            c_s[...] = c0_ref[0]
            n_s[...] = n0_ref[0]
            m_s[...] = m0_ref[0]
            e_s[...] = e_ref[0]

    u = u_ref[...]
    ext = _rows(e_in[...], u)
    w = cw_ref[...]
    acc = cb_ref[...] + u * w[A_CONV - 1:A_CONV, :]
    for sh in range(1, A_CONV):
        acc = acc + pltpu.roll(ext, sh, 0)[SUBLANES:, :] * w[A_CONV - 1 - sh:A_CONV - sh, :]
    if not single_chunk:
        e_s[...] = u[T - SUBLANES:, :]
    qk_act = acc * _sigmoid(acc)

    gcol = gc_ref[...] + brow_ref[...]
    grow = gr_ref[0] + bcol_ref[...]
    lf_col = -_softplus(-gcol)
    ig_row = grow[:A_HEADS, :]
    lf_row = -_softplus(-grow[A_HEADS:, :])
    ri = lax.broadcasted_iota(jnp.int32, (T, T), 0)
    ci = lax.broadcasted_iota(jnp.int32, (T, T), 1)
    tri = ci <= ri
    b_col = _cumsum_rows(lf_col)
    b_row = _dot_exact_rhs(lf_row, jnp.where(ri <= ci, 1.0, 0.0))
    m_prev = m_in[...]
    lane = lax.broadcasted_iota(jnp.int32, (1, LANES), 1)
    heads = list(range(A_HEADS))
    bc = [b_col[:, A_HEADS + h:A_HEADS + h + 1] for h in heads]
    mh = [m_prev[:, A_HEADS + h:A_HEADS + h + 1] for h in heads]
    br = [b_row[h:h + 1, :] for h in heads]
    igr = [ig_row[h:h + 1, :] for h in heads]
    igc = [gcol[:, h:h + 1] for h in heads]
    qh = [qk_act[:, h * A_DK:(h + 1) * A_DK] for h in heads]
    kh = [qk_act[:, A_QK + h * A_DK:A_QK + (h + 1) * A_DK] * (A_DK ** -0.5) for h in heads]
    vh = [v_ref[:, h * A_DV:(h + 1) * A_DV] for h in heads]
    ch = [c_in[h] for h in heads]
    nh = [n_in[h:h + 1, :] for h in heads]
    qk = _each(_dot_nt, qh, kh)
    qc = _each(_dot, qh, ch)
    d = _each(lambda a, b, i: jnp.where(tri, a - b + i, -jnp.inf), bc, br, igr)
    inter = _each(jnp.add, bc, mh)
    m_t = _each(lambda i, x: jnp.maximum(i, jnp.max(x, axis=-1, keepdims=True)), inter, d)
    s = _each(lambda a, x, m: a * jnp.exp(x - m), qk, d, m_t)
    e_in = _each(lambda i, m: jnp.exp(i - m), inter, m_t)
    sv = _each(_dot, s, vh)
    num = _each(lambda a, x, b: a + x * b, sv, e_in, qc)
    den = _each(lambda a, x, qq, nn: jnp.sum(a, axis=-1, keepdims=True)
                + x * jnp.sum(qq * nn, axis=-1, keepdims=True), s, e_in, qh, nh)
    hh = _each(lambda a, b, m: a / jnp.maximum(jnp.abs(b), jnp.exp(-m)), num, den, m_t)
    hh = _each(lambda a: a * lax.rsqrt(jnp.mean(a * a, axis=-1, keepdims=True) + EPS), hh)
    b_end = _each(lambda a: a[T - 1:T, :], bc)
    m_new = _each(lambda be, m, b, i: jnp.maximum(be + m, jnp.max(be - b + i, axis=-1, keepdims=True)),
                  b_end, mh, br, igr)
    carry = _each(lambda be, m, mn: jnp.exp(be + m - mn), b_end, mh, m_new)
    wk = _each(lambda kk_, be, a, i, mn: kk_ * jnp.exp(be - a + i - mn), kh, b_end, bc, igc, m_new)
    kv = _each(_dot_tn, wk, vh)
    m_next = m_prev
    for h in heads:
        cols = slice(h * A_DV, (h + 1) * A_DV)
        gate = _sigmoid(o_ref[:, cols]) * _sigmoid(ga_ref[:, cols])
        h_ref[:, cols] = (hh[h] * nw_ref[:, cols] * gate).astype(h_ref.dtype)
        c_out[h] = carry[h] * ch[h] + kv[h]
        n_out[h:h + 1, :] = carry[h] * nh[h] + jnp.sum(wk[h], axis=0, keepdims=True)
        m_next = jnp.where(lane == A_HEADS + h, m_new[h], m_next)
    m_out[...] = m_next

    if not single_chunk:
        @pl.when(c == pl.num_programs(1) - 1)
        def _():
            cn_ref[0] = c_s[...]
            nn_ref[0] = n_s[...]
            mn_ref[0] = m_s[...]


def _mlstm(proj, gates_row, e_conv, conv_w, conv_b, bias_row, bias_col, norm_w, c0, n0, m0, *, T, n_seq, n_chunk):
    row = lambda s, c: (s * n_chunk + c, 0)
    fixed = lambda s, c: (0, 0)
    wide = lambda col: pl.BlockSpec((T, D_MODEL), lambda s, c: (s * n_chunk + c, col // D_MODEL))
    return pl.pallas_call(
        functools.partial(_mlstm_body, T=T, single_chunk=n_chunk == 1),
        grid=(n_seq, n_chunk),
        in_specs=[wide(COL_CONV), wide(COL_V), wide(COL_O), wide(COL_GA),
                  pl.BlockSpec((T, LANES), lambda s, c: (s * n_chunk + c, (COL_TAIL + TAIL_IF) // LANES)),
                  pl.BlockSpec((1, 2 * A_HEADS, T), lambda s, c: (s * n_chunk + c, 0, 0)),
                  pl.BlockSpec((1, SUBLANES, CONV_COLS), lambda s, c: (s, 0, 0)),
                  pl.BlockSpec((A_CONV, CONV_COLS), fixed),
                  pl.BlockSpec((1, CONV_COLS), fixed),
                  pl.BlockSpec((1, LANES), fixed),
                  pl.BlockSpec((2 * A_HEADS, 1), fixed),
                  pl.BlockSpec((1, A_VW), fixed),
                  pl.BlockSpec((1, A_HEADS, A_DK, A_DV), lambda s, c: (s, 0, 0, 0)),
                  pl.BlockSpec((1, A_HEADS, A_DK), lambda s, c: (s, 0, 0)),
                  pl.BlockSpec((1, 1, LANES), lambda s, c: (s, 0, 0))],
        out_specs=[pl.BlockSpec((T, A_VW), row),
                   pl.BlockSpec((1, A_HEADS, A_DK, A_DV), lambda s, c: (s, 0, 0, 0)),
                   pl.BlockSpec((1, A_HEADS, A_DK), lambda s, c: (s, 0, 0)),
                   pl.BlockSpec((1, 1, LANES), lambda s, c: (s, 0, 0))],
        out_shape=[jax.ShapeDtypeStruct((n_seq * n_chunk * T, A_VW), BF16),
                   jax.ShapeDtypeStruct((n_seq, A_HEADS, A_DK, A_DV), F32),
                   jax.ShapeDtypeStruct((n_seq, A_HEADS, A_DK), F32),
                   jax.ShapeDtypeStruct((n_seq, 1, LANES), F32)],
        scratch_shapes=[pltpu.VMEM((A_HEADS, A_DK, A_DV), F32),
                        pltpu.VMEM((A_HEADS, A_DK), F32),
                        pltpu.VMEM((1, LANES), F32),
                        pltpu.VMEM((SUBLANES, CONV_COLS), F32)],
        compiler_params=_cparams(("arbitrary", "arbitrary"), 32),
        name="mlstm_chunk",
    )(proj, proj, proj, proj, proj, gates_row, e_conv, conv_w, conv_b, bias_row, bias_col, norm_w, c0, n0, m0)


def _kcat_dot(a1, b1, a2, b2):
    if a1.shape[1] % LANES == 0:
        return _dot(jnp.concatenate([a1, a2], axis=1), _rows(b1, b2))
    return _dot(a1, b1) + _dot(a2, b2)


def _head_sums(ones_blk, T, *xs):
    sums = jnp.dot(_rows(*xs).astype(BF16), ones_blk, preferred_element_type=F32)
    return [sums[i * T:(i + 1) * T] for i in range(len(xs))]


def _rwkv_body(ur_ref, uk_ref, uv_ref, ul_ref, gb_ref, er_ref, ek_ref, ev_ref, el_ref,
               mur_ref, muk_ref, muv_ref, mul_ref, w0_ref, w2_ref, a0_ref, a2_ref, g2_ref, kkw_ref, ka_ref,
               rk_ref, lnw_ref, lnb_ref, blk_ref, blkb_ref, s0_ref,
               hb_ref, sn_ref, s_s, cr_s, ck_s, cv_s, cl_s, *, T, n_pairs):
    c = pl.program_id(2)

    @pl.when(c == 0)
    def _():
        z = jnp.zeros((B_HEAD, B_HEAD), F32)
        for p in range(n_pairs):
            s_s[p] = _rows(jnp.concatenate([s0_ref[0, 2 * p], z], axis=1),
                           jnp.concatenate([z, s0_ref[0, 2 * p + 1]], axis=1))
        cr_s[...] = er_ref[0]
        ck_s[...] = ek_ref[0]
        cv_s[...] = ev_ref[0]
        cl_s[...] = el_ref[0]

    n_sub = ur_ref.shape[0] // T

    def shift_mix(u_ref, carry_ref, mu_ref, cs, rows):
        u = u_ref[rows, cs]
        before = carry_ref[:, cs] if rows.start == 0 else u_ref[rows.start - SUBLANES:rows.start, cs]
        prev = pltpu.roll(_rows(before, u), 1, 0)[SUBLANES:, :]
        if rows.stop == u_ref.shape[0]:
            carry_ref[:, cs] = u[rows.stop - rows.start - SUBLANES:, :]
        return u + mu_ref[:, cs] * (prev - u)

    xl = shift_mix(ul_ref, cl_s, mul_ref, slice(None), slice(0, n_sub * T))
    th_wd = jnp.tanh(xl[:, :LANES]).astype(BF16)
    x_ad = pltpu.roll(xl[:, :2 * LANES], 2 * LANES - LORA_DECAY, 1)[:, :LANES].astype(BF16)
    gd_off = LORA_DECAY + LORA_ICLR - TAIL_GD_WIN
    sg_gd = _sigmoid(pltpu.roll(xl[:, TAIL_GD_WIN:], TAIL_MIX - TAIL_GD_WIN - gd_off, 1)[:, :LORA_GATE]).astype(BF16)

    lo = lax.broadcasted_iota(jnp.int32, (1, LANES), 1) < B_HEAD
    head1 = lambda x: jnp.where(lo, x, jnp.zeros_like(x))
    head2 = lambda x: jnp.where(lo, jnp.zeros_like(x), x)
    r2 = lax.broadcasted_iota(jnp.int32, (T, 2 * T), 0)
    c2 = lax.broadcasted_iota(jnp.int32, (T, 2 * T), 1)
    left = c2 < T
    cj = jnp.where(left, c2, c2 - T)
    strict2 = cj < r2
    incl2 = cj <= r2
    eye_pair = jnp.where(cj == r2, 1.0, 0.0)
    ones_blk = blkb_ref[...]
    zero = jnp.zeros((T, LANES), BF16)
    aligned = (2 * T) % LANES == 0

    def block_diag(sb):
        z = jnp.zeros_like(sb)
        return _rows(jnp.where(left, sb, z), jnp.where(left, z, sb))

    def pair_program(p):
        S = None
        for sub in range(n_sub):
            S = yield from chunk_program(p, slice(sub * T, (sub + 1) * T), S)
        s_s[p] = S

    def chunk_program(p, rows, S):
        cs = slice(p * LANES, (p + 1) * LANES)
        r = shift_mix(ur_ref, cr_s, mur_ref, cs, rows)
        xk = shift_mix(uk_ref, ck_s, muk_ref, cs, rows)
        v = shift_mix(uv_ref, cv_s, muv_ref, cs, rows)
        lw = jnp.dot(th_wd[rows], w2_ref[:, cs], preferred_element_type=F32)
        la = jnp.dot(x_ad[rows], a2_ref[:, cs], preferred_element_type=F32)
        lg = jnp.dot(sg_gd[rows], g2_ref[:, cs], preferred_element_type=F32)
        kkw = xk * kkw_ref[:, cs]
        ssq, = _head_sums(ones_blk, T, kkw * kkw)
        yield
        ld = DECAY_SCALE * _sigmoid(w0_ref[:, cs] + lw)
        a = _sigmoid(a0_ref[:, cs] + la)
        g = lg * _sigmoid(gb_ref[rows, cs])
        k = xk * (1.0 + (a - 1.0) * ka_ref[:, cs])
        kk = kkw * lax.rsqrt(jnp.maximum(ssq, 1e-24))
        bi = kk * a
        cum = _cumsum_rows(ld)
        ewt = jnp.exp(cum[T - 1:T, :])
        Rt = (r * jnp.exp(cum)).astype(BF16)
        At = (-kk * jnp.exp(cum - ld)).astype(BF16)
        einv = jnp.exp(-cum)
        Bt = bi * einv
        Kt = k * einv
        BhKh = _rows(Bt * ewt, Kt * ewt).astype(BF16)
        Bt = Bt.astype(BF16)
        Kt = Kt.astype(BF16)
        vb = v.astype(BF16)
        V1 = head1(vb)
        V2 = head2(vb)
        if aligned:
            G = _dot_nt(_rows(head1(At), head1(Rt), head2(At), head2(Rt)), _rows(Bt, Kt))
            G1, G2 = G[:2 * T], G[2 * T:]
        else:
            G1 = _dot_nt(_rows(head1(At), head1(Rt)), _rows(Bt, Kt))
            G2 = _dot_nt(_rows(head2(At), head2(Rt)), _rows(Kt, Bt))
        yield
        LL1 = jnp.where(strict2, G1[:T], 0.0)
        LL2 = jnp.where(strict2, G2[:T], 0.0)
        P1 = jnp.where(incl2, G1[T:], 0.0)
        P2 = jnp.where(incl2, G2[T:], 0.0)
        lab2 = pltpu.roll(LL2, T, 1) if aligned else LL2
        pw = jnp.where(left, LL1, lab2).astype(BF16)
        iv = eye_pair + jnp.where(left, LL1, lab2)
        X0 = _kcat_dot(LL1, _rows(zero, V1), LL2, _rows(zero, V2) if aligned else _rows(V2, zero))
        pw = _dot(pw, block_diag(pw)).astype(BF16)
        yield
        span = 4
        while span < T:
            both = _dot(_rows(iv.astype(BF16), pw), block_diag(pw))
            iv, pw = iv + both[:T], both[T:].astype(BF16)
            span *= 2
            yield
        iv = (iv + _dot(iv, block_diag(pw))).astype(BF16)
        if S is None:
            S = s_s[p]
        AR = _dot_nt(_rows(At, Rt), S)
        yield
        X = (AR[:T] + X0).astype(BF16)
        U = _dot(iv, _rows(head1(X), head2(X))).astype(BF16)
        yield
        if aligned:
            rhs2 = _rows(head2(U), V2)
        else:
            rhs2 = _rows(V2, head2(U))
        o = AR[T:] + _kcat_dot(P1, _rows(head1(U), V1), P2, rhs2)
        S_next = S * ewt + blk_ref[...] * _dot_tn(_rows(U, vb), BhKh)
        yield
        o_sum, rk_sum = _head_sums(ones_blk, T, o, r * k * rk_ref[:, cs])
        yield
        oc = o - o_sum * (1.0 / B_HEAD)
        var, = _head_sums(ones_blk, T, oc * oc)
        yield
        on = oc * lax.rsqrt(var * (1.0 / B_HEAD) + GN_EPS) * lnw_ref[:, cs] + lnb_ref[:, cs]
        hb_ref[rows, cs] = ((on + rk_sum * v) * g).astype(hb_ref.dtype)
        return S_next

    _run_staggered([pair_program(p) for p in range(n_pairs)], RWKV_WAVE, RWKV_LAG)

    @pl.when(c == pl.num_programs(2) - 1)
    def _():
        for p in range(n_pairs):
            sn_ref[0, 2 * p] = s_s[p, :B_HEAD, :B_HEAD]
            sn_ref[0, 2 * p + 1] = s_s[p, B_HEAD:, B_HEAD:]


def _rwkv(proj, e_main, e_tail, mu_main, mu_tail, w0, w2p, a0, a2p, g2, k_k, k_a, rk, lnw, lnb, blk, s0, *,
          T, n_seq, n_chunk):
    assert T <= RWKV_CHUNK
    gw = GROUP_W
    n_sub = RWKV_CHUNKS_PER_STEP if n_chunk % RWKV_CHUNKS_PER_STEP == 0 else 1
    n_step = n_chunk // n_sub
    tb = n_sub * T
    wide = lambda col: pl.BlockSpec((tb, gw), lambda s, pg, c: (s * n_step + c, col // gw + pg))
    edge = lambda col: pl.BlockSpec((1, SUBLANES, gw), lambda s, pg, c: (s, 0, col // gw + pg))
    vec = lambda col: pl.BlockSpec((1, gw), lambda s, pg, c: (0, col // gw + pg))
    mat = lambda rows: pl.BlockSpec((rows, gw), lambda s, pg, c: (0, pg))
    st = pl.BlockSpec((1, 2 * GROUP_PAIRS, B_HEAD, B_HEAD), lambda s, pg, c: (s, pg, 0, 0))
    return pl.pallas_call(
        functools.partial(_rwkv_body, T=T, n_pairs=GROUP_PAIRS),
        grid=(n_seq, N_PAIRS // GROUP_PAIRS, n_step),
        in_specs=[wide(COL_R), wide(COL_K), wide(COL_VB),
                  pl.BlockSpec((tb, TAIL_MIX), lambda s, pg, c: (s * n_step + c, COL_TAIL // TAIL_MIX)),
                  wide(COL_GB),
                  edge(0), edge(B_W), edge(2 * B_W),
                  pl.BlockSpec((1, SUBLANES, TAIL_MIX), lambda s, pg, c: (s, 0, 0)),
                  vec(0), vec(B_W), vec(2 * B_W),
                  pl.BlockSpec((1, TAIL_MIX), lambda s, pg, c: (0, 0)),
                  vec(0), mat(LANES), vec(0), mat(LANES), mat(LORA_GATE), vec(0), vec(0),
                  vec(0), vec(0), vec(0),
                  pl.BlockSpec((LANES, LANES), lambda s, pg, c: (0, 0)),
                  pl.BlockSpec((LANES, LANES), lambda s, pg, c: (0, 0)), st],
        out_specs=[pl.BlockSpec((tb, gw), lambda s, pg, c: (s * n_step + c, pg)), st],
        out_shape=[jax.ShapeDtypeStruct((n_seq * n_chunk * T, B_W), BF16),
                   jax.ShapeDtypeStruct((n_seq, B_HEADS, B_HEAD, B_HEAD), F32)],
        scratch_shapes=[pltpu.VMEM((GROUP_PAIRS, LANES, LANES), F32),
                        pltpu.VMEM((SUBLANES, gw), F32), pltpu.VMEM((SUBLANES, gw), F32),
                        pltpu.VMEM((SUBLANES, gw), F32), pltpu.VMEM((SUBLANES, TAIL_MIX), F32)],
        compiler_params=_cparams(("arbitrary", "arbitrary", "arbitrary"), 32),
        name="rwkv_chunk",
    )(proj, proj, proj, proj, proj, e_main, e_main, e_main, e_tail, mu_main, mu_main, mu_main, mu_tail,
      w0, w2p, a0, a2p, g2, k_k, k_a, rk, lnw, lnb, blk, blk.astype(BF16), s0)


def _merge_out_body(ha_ref, hb_ref, x_ref, w_ref, g_ref, y_ref):
    mixed = ha_ref[...].astype(F32) + hb_ref[...].astype(F32)
    y = jnp.dot(mixed.astype(BF16), w_ref[...], preferred_element_type=F32)
    ms = jnp.mean(y * y, axis=-1, keepdims=True)
    y_ref[...] = x_ref[...] + y * lax.rsqrt(ms + EPS) * g_ref[...]


def _merge_out(h_a, h_b, x, w_out, g_post):
    n = x.shape[0]
    wide = pl.BlockSpec((MERGE_TM, D_MODEL), lambda i: (i, 0))
    return pl.pallas_call(
        _merge_out_body,
        grid=(n // MERGE_TM,),
        in_specs=[wide, wide, wide,
                  pl.BlockSpec((D_MODEL, D_MODEL), lambda i: (0, 0)),
                  pl.BlockSpec((1, D_MODEL), lambda i: (0, 0))],
        out_specs=wide,
        out_shape=jax.ShapeDtypeStruct((n, D_MODEL), F32),
        compiler_params=_cparams(("parallel",), 48),
        name="merge_out_proj",
    )(h_a, h_b, x, w_out, g_post)


def _mlp_body(x_ref, xn_ref, g1_ref, wu_ref, wd_ref, g2_ref, y_ref, hn_ref, acc_ref, *, row_chunk):
    i = pl.program_id(0)
    j = pl.program_id(1)
    last = pl.num_programs(1) - 1
    slot = i % 2
    step_rows = xn_ref.shape[0]

    @pl.when((i == 0) & (j == 0))
    def _():
        _norm_rows(x_ref, g1_ref, hn_ref.at[0], row_chunk)
        acc_ref[...] = jnp.zeros_like(acc_ref)

    def partial_product():
        u = jnp.dot(hn_ref[slot], wu_ref[...], preferred_element_type=F32)
        a = jnp.square(jnp.maximum(u, 0.0))
        part = jnp.dot(a.astype(BF16), wd_ref[...], preferred_element_type=F32)
        rows = pl.ds(pl.multiple_of(j * step_rows, step_rows), step_rows)
        hn_ref[1 - slot, rows, :] = _norm_bf16(xn_ref[...], g1_ref[...])
        return part

    @pl.when(j < last)
    def _():
        part = partial_product()
        acc_ref[...] = jnp.where(j == 0, part, acc_ref[...] + part)

    @pl.when(j == last)
    def _():
        ff = acc_ref[...] + partial_product()
        ms = jnp.mean(ff * ff, axis=-1, keepdims=True)
        y_ref[...] = x_ref[...] + ff * lax.rsqrt(ms + EPS) * g2_ref[...]


def _mlp(x, g_pre, w_up, w_down, g_post):
    n, d = x.shape
    f = w_up.shape[1]
    nj = f // MLP_TF
    step_rows = MLP_TM // nj
    n_slices = n // step_rows
    return pl.pallas_call(
        functools.partial(_mlp_body, row_chunk=LANES),
        grid=(n // MLP_TM, nj),
        in_specs=[pl.BlockSpec((MLP_TM, d), lambda i, j: (i, 0)),
                  pl.BlockSpec((step_rows, d), lambda i, j: (jnp.minimum((i + 1) * nj + j, n_slices - 1), 0)),
                  pl.BlockSpec((1, d), lambda i, j: (0, 0)),
                  pl.BlockSpec((d, MLP_TF), lambda i, j: (0, j)),
                  pl.BlockSpec((MLP_TF, d), lambda i, j: (j, 0)),
                  pl.BlockSpec((1, d), lambda i, j: (0, 0))],
        out_specs=pl.BlockSpec((MLP_TM, d), lambda i, j: (i, 0)),
        out_shape=jax.ShapeDtypeStruct((n, d), F32),
        scratch_shapes=[pltpu.VMEM((2, MLP_TM, d), BF16), pltpu.VMEM((MLP_TM, d), F32)],
        compiler_params=_cparams(("arbitrary", "arbitrary"), 56),
        name="mlp",
    )(x, x, g_pre, w_up, w_down, g_post)


def _pad_cols(a, width):
    return jnp.pad(a, [(0, 0)] * (a.ndim - 1) + [(0, width - a.shape[-1])])


def _pad_rows(a, before, total):
    return jnp.pad(a, ((before, total - before - a.shape[0]), (0, 0)))


def _edge_blocks(prev_rows):
    return jnp.pad(prev_rows, ((0, 0), (SUBLANES - prev_rows.shape[1], 0), (0, 0)))


def _gates_row_form(proj, T):
    g = proj[:, COL_TAIL + TAIL_IF:COL_TAIL + TAIL_IF + 2 * A_HEADS]
    return jnp.swapaxes(g.reshape(-1, T, 2 * A_HEADS), 1, 2)


def kernel(x_prompt, x_sample, state_mlstm_C, state_mlstm_n, state_mlstm_m, state_mlstm_conv, state_rwkv_S, state_rwkv_shift, meta_tokens, norm_mix_pre, norm_mix_post, norm_mlp_pre, norm_mlp_post, w_in, mlstm_conv_w, mlstm_conv_b, mlstm_b_i, mlstm_b_f, mlstm_norm_w, rwkv_mu, rwkv_w0, rwkv_w2, rwkv_a0, rwkv_a2, rwkv_g2, rwkv_k_k, rwkv_k_a, rwkv_r_k, rwkv_ln_w, rwkv_ln_b, w_out, w_up, w_down):
    n_s_seq, s_len, _ = x_sample.shape
    _, p_len, _ = x_prompt.shape
    k3 = A_CONV - 1

    w_all = _relayout_weight(jnp.swapaxes(w_in[0], 0, 1))
    mu = rwkv_mu[0][None, :]
    mu_main = mu[:, :3 * B_W]
    mu_tail = _pad_cols(mu[:, 3 * B_W:], TAIL_MIX)
    w2p = _pad_rows(rwkv_w2[0], 0, LANES).astype(BF16)
    a2p = _pad_rows(rwkv_a2[0], 0, LANES).astype(BF16)
    g2 = rwkv_g2[0].astype(BF16)
    g_pre = norm_mix_pre[0][None, :]
    bias = jnp.concatenate([mlstm_b_i[0], mlstm_b_f[0]])
    bias_row = _pad_cols(bias[None, :], LANES)
    bias_col = bias[:, None]
    norm_w = mlstm_norm_w[0][None, :]
    conv_w = mlstm_conv_w[0]
    conv_b = mlstm_conv_b[0][None, :]
    rwkv_vecs = (rwkv_w0[0][None, :], w2p, rwkv_a0[0][None, :], a2p, g2, rwkv_k_k[0][None, :],
                 rwkv_k_a[0][None, :], rwkv_r_k[0].reshape(1, B_W), rwkv_ln_w[0][None, :], rwkv_ln_b[0][None, :],
                 _head_block_mask())
    w_out_b = w_out[0].astype(BF16)
    w_up_b = w_up[0].astype(BF16)
    w_down_b = w_down[0].astype(BF16)

    def branches(proj, T, n_seq, n_chunk, e_conv, e_main, e_tail, c0, n0, m0, s0):
        tm_ = MLSTM_CHUNK if n_chunk > 1 else T
        h_a, c_n, n_n, m_n = _mlstm(proj, _gates_row_form(proj, tm_), e_conv, conv_w, conv_b, bias_row, bias_col,
                                    norm_w, c0, n0, m0, T=tm_, n_seq=n_seq, n_chunk=n_chunk * T // tm_)
        h_b, s_n = _rwkv(proj, e_main, e_tail, mu_main, mu_tail, *rwkv_vecs, s0,
                         T=T, n_seq=n_seq, n_chunk=n_chunk)
        return h_a, h_b, (c_n, n_n, m_n, s_n)

    def finish(h_a, h_b, x_rows):
        x1 = _merge_out(h_a, h_b, x_rows, w_out_b, norm_mix_post[0][None, :])
        return _mlp(x1, norm_mlp_pre[0][None, :], w_up_b, w_down_b, norm_mlp_post[0][None, :])

    zeros = lambda *shape: jnp.zeros(shape, F32)
    xs = x_sample.reshape(n_s_seq * s_len, D_MODEL).astype(F32)
    assert xs.shape[0] == PROJ_TM
    proj_s, proj_m = _norm_matmul_pair(xs, meta_tokens.astype(F32), g_pre, w_all)
    _, _, (c_m, n_m, m_m, s_m) = branches(
        proj_m, N_META, 1, 1,
        zeros(1, SUBLANES, CONV_COLS), zeros(1, SUBLANES, 3 * B_W), zeros(1, SUBLANES, TAIL_MIX),
        zeros(1, A_HEADS, A_DK, A_DV), zeros(1, A_HEADS, A_DK), zeros(1, 1, LANES),
        zeros(1, B_HEADS, B_HEAD, B_HEAD))

    xp = x_prompt[0].astype(F32)
    proj_p = _norm_matmul(xp, g_pre, w_all, PROJ_TM)
    ha_p, hb_p, (c_p, n_p, m_p, s_p) = branches(
        proj_p, RWKV_CHUNK, 1, p_len // RWKV_CHUNK,
        _edge_blocks(proj_m[None, N_META - k3:, COL_CONV:COL_CONV + CONV_COLS]),
        _edge_blocks(proj_m[None, N_META - 1:, COL_R:COL_R + 3 * B_W]),
        _edge_blocks(proj_m[None, N_META - 1:, COL_TAIL:COL_TAIL + TAIL_MIX]),
        c_m, n_m, m_m, s_m)
    y_prompt = finish(ha_p, hb_p, xp)[None].astype(x_prompt.dtype)

    sh = state_rwkv_shift[0].astype(F32)
    m0_s = jnp.pad(state_mlstm_m[0].astype(F32), ((0, 0), (A_HEADS, LANES - 2 * A_HEADS)))[:, None, :]
    ha_s, hb_s, (c_s, n_s, m_s, s_s) = branches(
        proj_s, s_len, n_s_seq, 1,
        _edge_blocks(state_mlstm_conv[0].astype(F32)),
        _edge_blocks(sh[:, :, :3 * B_W]), _edge_blocks(_pad_cols(sh[:, :, 3 * B_W:], TAIL_MIX)),
        state_mlstm_C[0].astype(F32), state_mlstm_n[0].astype(F32), m0_s,
        state_rwkv_S[0].astype(F32))
    y_sample = finish(ha_s, hb_s, xs).reshape(x_sample.shape).astype(x_sample.dtype)

    def shift_state(rows):
        return jnp.concatenate([rows[..., COL_R:COL_R + 3 * B_W],
                                rows[..., COL_TAIL:COL_TAIL + LORA_COLS]], axis=-1)

    last_s = proj_s.reshape(n_s_seq, s_len, PROJ_COLS)[:, s_len - k3:, :]
    conv_s = last_s[..., COL_CONV:COL_CONV + CONV_COLS]
    shift_s = shift_state(last_s[:, k3 - 1:, :])
    last_p = proj_p[p_len - k3:, :]
    conv_p = last_p[None, :, COL_CONV:COL_CONV + CONV_COLS]
    shift_p = shift_state(last_p[None, k3 - 1:, :])
    dt_c, dt_n, dt_m = state_mlstm_C.dtype, state_mlstm_n.dtype, state_mlstm_m.dtype
    dt_cv, dt_s, dt_sh = state_mlstm_conv.dtype, state_rwkv_S.dtype, state_rwkv_shift.dtype
    lanes_m = slice(A_HEADS, 2 * A_HEADS)
    return (y_prompt, y_sample,
            c_p[None].astype(dt_c), n_p[None].astype(dt_n), m_p[:, 0, lanes_m][None].astype(dt_m),
            conv_p[None].astype(dt_cv), s_p[None].astype(dt_s), shift_p[None].astype(dt_sh),
            c_s[None].astype(dt_c), n_s[None].astype(dt_n), m_s[:, 0, lanes_m][None].astype(dt_m),
            conv_s[None].astype(dt_cv), s_s[None].astype(dt_s), shift_s[None].astype(dt_sh))
```

```python
import functools

import jax
import jax.numpy as jnp
from jax import lax
from jax.experimental import pallas as pl
from jax.experimental.pallas import tpu as pltpu

F32 = jnp.float32
BF16 = jnp.bfloat16

D_MODEL = 2048
N_META = 16
EPS = 1e-6
D_FF = 4 * D_MODEL
A_HEADS = 8
A_DK = 128
A_DV = D_MODEL // A_HEADS
A_QK = A_HEADS * A_DK
A_VW = A_HEADS * A_DV
A_CONV = 4
MLSTM_CHUNK = 256
B_HEAD = 64
B_HEADS = D_MODEL // B_HEAD
B_W = B_HEADS * B_HEAD
LORA_DECAY = max(32, int(round(1.8 * D_MODEL ** 0.5 / 32)) * 32)
LORA_ICLR = max(32, int(round(1.8 * D_MODEL ** 0.5 / 32)) * 32)
LORA_GATE = max(32, int(round(0.6 * D_MODEL ** 0.8 / 32)) * 32)
GN_EPS = 64e-5
CONV_COLS = 2 * A_QK
SHIFT_COLS = 3 * B_W + LORA_DECAY + LORA_ICLR + LORA_GATE

LANES = 128
SUBLANES = 8

COL_CONV, COL_V, COL_O, COL_R, COL_K, COL_VB, COL_GA, COL_GB = (i * D_MODEL for i in range(8))
COL_TAIL = 8 * D_MODEL
LORA_COLS = LORA_DECAY + LORA_ICLR + LORA_GATE
TAIL_MIX = -(-LORA_COLS // LANES) * LANES
TAIL_IF = TAIL_MIX
TAIL_GD_WIN = (LORA_DECAY + LORA_ICLR) // LANES * LANES
PROJ_TN = 1024
PROJ_COLS = -(-(COL_TAIL + TAIL_IF + LANES) // PROJ_TN) * PROJ_TN
PROJ_TM = 1024
MERGE_TM = 512
MLP_TM, MLP_TF = 512, 1024
DECAY_SCALE = -0.6065306597126334
RWKV_CHUNK = 64
N_PAIRS = B_HEADS // 2
GROUP_PAIRS = 16
RWKV_CHUNKS_PER_STEP = 4
RWKV_WAVE, RWKV_LAG = 8, 1
GROUP_W = GROUP_PAIRS * LANES


def _cparams(semantics, vmem_mib):
    return pltpu.CompilerParams(dimension_semantics=semantics, vmem_limit_bytes=vmem_mib << 20)


def _softplus(x):
    return jnp.maximum(x, 0.0) + jnp.log1p(jnp.exp(-jnp.abs(x)))


def _dot(a, b):
    return jnp.dot(a.astype(BF16), b.astype(BF16), preferred_element_type=F32)


def _dot_nt(a, b):
    return lax.dot_general(a.astype(BF16), b.astype(BF16), (((1,), (1,)), ((), ())),
                           preferred_element_type=F32)


def _dot_tn(a, b):
    return lax.dot_general(a.astype(BF16), b.astype(BF16), (((0,), (0,)), ((), ())),
                           preferred_element_type=F32)


def _each(f, *lists):
    return [f(*args) for args in zip(*lists)]


def _run_staggered(programs, wave, lag):
    live = list(enumerate(programs))
    slot = 0
    while live:
        still = []
        for p, prog in live:
            if slot >= (p // wave) * lag:
                try:
                    next(prog)
                except StopIteration:
                    continue
            still.append((p, prog))
        live = still
        slot += 1


def _rows(*parts):
    return jnp.concatenate(parts, axis=0)


def _split_bf16(x, pieces):
    out = []
    for _ in range(pieces - 1):
        p = x.astype(BF16)
        out.append(p)
        x = x - p.astype(F32)
    out.append(x.astype(BF16))
    return out


def _dot_exact_rhs(x, m, pieces=3):
    mb = m.astype(BF16)
    acc = None
    for p in _split_bf16(x, pieces):
        t = jnp.dot(p, mb, preferred_element_type=F32)
        acc = t if acc is None else acc + t
    return acc


def _sigmoid(x):
    return 0.5 * jnp.tanh(0.5 * x) + 0.5


def _cumsum_rows(x):
    n = x.shape[0]
    row = lax.broadcasted_iota(jnp.int32, x.shape, 0)
    step = 1
    while step < n:
        x = x + jnp.where(row >= step, pltpu.roll(x, step, 0), 0.0)
        step *= 2
    return x


def _head_block_mask():
    a = lax.broadcasted_iota(jnp.int32, (LANES, LANES), 0) // B_HEAD
    b = lax.broadcasted_iota(jnp.int32, (LANES, LANES), 1) // B_HEAD
    return jnp.where(a == b, 1.0, 0.0).astype(F32)


SRC_IF = CONV_COLS + 2 * A_VW
SRC_SHIFT = SRC_IF + 2 * A_HEADS
SRC_GATE = SRC_SHIFT + SHIFT_COLS
SRC_COLS = SRC_GATE + 2 * D_MODEL
RELAYOUT_ROWS = TAIL_MIX


def _relayout_src_row(t):
    unit = 2 * SUBLANES
    dst = t * (RELAYOUT_ROWS // unit)
    src = jnp.where(dst < COL_R // unit, dst,
                    jnp.where(dst < COL_GA // unit, dst + (SRC_SHIFT - COL_R) // unit,
                              jnp.where(dst < COL_TAIL // unit, dst + (SRC_GATE - COL_GA) // unit,
                                        jnp.where(dst < (COL_TAIL + TAIL_MIX) // unit,
                                                  (SRC_SHIFT + 3 * B_W) // unit, SRC_IF // unit))))
    return src * unit


def _relayout_body(w_ref, o_ref):
    o_ref[...] = w_ref[...].astype(BF16)


def _relayout_weight(wt):
    d = wt.shape[1]
    assert wt.shape[0] == SRC_COLS and PROJ_COLS == COL_TAIL + 2 * RELAYOUT_ROWS and COL_R == SRC_IF
    assert SRC_SHIFT + 3 * B_W + RELAYOUT_ROWS <= SRC_COLS and SRC_SHIFT % (2 * SUBLANES) == 0
    return pl.pallas_call(
        _relayout_body,
        grid=(PROJ_COLS // RELAYOUT_ROWS,),
        in_specs=[pl.BlockSpec((pl.Element(RELAYOUT_ROWS), pl.Element(d)), lambda t: (_relayout_src_row(t), 0))],
        out_specs=pl.BlockSpec((RELAYOUT_ROWS, d), lambda t: (t, 0)),
        out_shape=jax.ShapeDtypeStruct((PROJ_COLS, d), BF16),
        compiler_params=_cparams(("parallel",), 32),
        name="relayout_w_in",
    )(wt)


def _norm_rows(x_ref, g_ref, out_ref, row_chunk):
    def body(c, carry):
        rows = pl.ds(pl.multiple_of(c * row_chunk, row_chunk), row_chunk)
        x = x_ref[rows, :]
        ms = jnp.mean(x * x, axis=-1, keepdims=True)
        out_ref[rows, :] = (x * lax.rsqrt(ms + EPS) * g_ref[...]).astype(out_ref.dtype)
        return carry
    lax.fori_loop(0, x_ref.shape[0] // row_chunk, body, 0)


def _norm_bf16(x, g):
    ms = jnp.mean(x * x, axis=-1, keepdims=True)
    return (x * lax.rsqrt(ms + EPS) * g).astype(BF16)


def _norm_matmul_body(x0_ref, xn_ref, g_ref, w_ref, o_ref, hn_ref, *, row_chunk):
    i = pl.program_id(0)
    j = pl.program_id(1)
    slot = i % 2
    step_rows = xn_ref.shape[0]

    @pl.when((i == 0) & (j == 0))
    def _():
        _norm_rows(x0_ref, g_ref, hn_ref.at[0], row_chunk)

    o_ref[...] = lax.dot_general(hn_ref[slot], w_ref[...], (((1,), (1,)), ((), ())), preferred_element_type=F32)
    s = jnp.minimum(j, hn_ref.shape[1] // step_rows - 1)
    rows = pl.ds(pl.multiple_of(s * step_rows, step_rows), step_rows)
    hn_ref[1 - slot, rows, :] = _norm_bf16(xn_ref[...], g_ref[...])


def _norm_matmul(x, g, w, tm):
    n, d = x.shape
    p = w.shape[0]
    nj = p // PROJ_TN
    step_rows = tm // (nj - 1)
    assert step_rows * (nj - 1) == tm and step_rows % SUBLANES == 0
    n_slices = n // step_rows
    per_tile = nj - 1
    return pl.pallas_call(
        functools.partial(_norm_matmul_body, row_chunk=min(tm, LANES)),
        grid=(n // tm, nj),
        in_specs=[pl.BlockSpec((tm, d), lambda i, j: (0, 0)),
                  pl.BlockSpec((step_rows, d),
                               lambda i, j: (jnp.minimum((i + 1) * per_tile + jnp.minimum(j, per_tile - 1),
                                                         n_slices - 1), 0)),
                  pl.BlockSpec((1, d), lambda i, j: (0, 0)),
                  pl.BlockSpec((PROJ_TN, d), lambda i, j: (j, 0))],
        out_specs=pl.BlockSpec((tm, PROJ_TN), lambda i, j: (i, j)),
        out_shape=jax.ShapeDtypeStruct((n, p), F32),
        scratch_shapes=[pltpu.VMEM((2, tm, d), BF16)],
        compiler_params=_cparams(("arbitrary", "arbitrary"), 48),
        name="in_proj",
    )(x, x, g, w)


def _norm_matmul_pair_body(x_ref, xs_ref, g_ref, w_ref, o_ref, os_ref, hn_ref, *, row_chunk):
    n = x_ref.shape[0]

    @pl.when(pl.program_id(0) == 0)
    def _():
        _norm_rows(x_ref, g_ref, hn_ref, row_chunk)
        hn_ref[n:, :] = _norm_bf16(xs_ref[...], g_ref[...])

    res = lax.dot_general(hn_ref[...], w_ref[...], (((1,), (1,)), ((), ())), preferred_element_type=F32)
    o_ref[...] = res[:n]
    os_ref[...] = res[n:]


def _norm_matmul_pair(x, x_small, g, w):
    n, d = x.shape
    ns = x_small.shape[0]
    p = w.shape[0]
    return pl.pallas_call(
        functools.partial(_norm_matmul_pair_body, row_chunk=LANES),
        grid=(p // PROJ_TN,),
        in_specs=[pl.BlockSpec((n, d), lambda j: (0, 0)),
                  pl.BlockSpec((ns, d), lambda j: (0, 0)),
                  pl.BlockSpec((1, d), lambda j: (0, 0)),
                  pl.BlockSpec((PROJ_TN, d), lambda j: (j, 0))],
        out_specs=[pl.BlockSpec((n, PROJ_TN), lambda j: (0, j)),
                   pl.BlockSpec((ns, PROJ_TN), lambda j: (0, j))],
        out_shape=[jax.ShapeDtypeStruct((n, p), F32), jax.ShapeDtypeStruct((ns, p), F32)],
        scratch_shapes=[pltpu.VMEM((n + ns, d), BF16)],
        compiler_params=_cparams(("arbitrary",), 48),
        name="in_proj_pair",
    )(x, x_small, g, w)


def _mlstm_body(u_ref, v_ref, o_ref, ga_ref, gc_ref, gr_ref, e_ref, cw_ref, cb_ref, brow_ref, bcol_ref,
                nw_ref, c0_ref, n0_ref, m0_ref,
                h_ref, cn_ref, nn_ref, mn_ref, c_s, n_s, m_s, e_s, *, T, single_chunk):
    c = pl.program_id(1)
    if single_chunk:
        c_in, n_in, m_in, e_in = c0_ref.at[0], n0_ref.at[0], m0_ref.at[0], e_ref.at[0]
        c_out, n_out, m_out = cn_ref.at[0], nn_ref.at[0], mn_ref.at[0]
    else:
        c_in = c_out = c_s
        n_in = n_out = n_s
        m_in = m_out = m_s
        e_in = e_s

        @pl.when(c == 0)
        def _():
            c_s[...] = c0_ref[0]
            n_s[...] = n0_ref[0]
            m_s[...] = m0_ref[0]
            e_s[...] = e_ref[0]

    u = u_ref[...]
    ext = _rows(e_in[...], u)
    w = cw_ref[...]
    acc = cb_ref[...] + u * w[A_CONV - 1:A_CONV, :]
    for sh in range(1, A_CONV):
        acc = acc + pltpu.roll(ext, sh, 0)[SUBLANES:, :] * w[A_CONV - 1 - sh:A_CONV - sh, :]
    if not single_chunk:
        e_s[...] = u[T - SUBLANES:, :]
    qk_act = acc * _sigmoid(acc)

    gcol = gc_ref[...] + brow_ref[...]
    grow = gr_ref[0] + bcol_ref[...]
    lf_col = -_softplus(-gcol)
    ig_row = grow[:A_HEADS, :]
    lf_row = -_softplus(-grow[A_HEADS:, :])
    ri = lax.broadcasted_iota(jnp.int32, (T, T), 0)
    ci = lax.broadcasted_iota(jnp.int32, (T, T), 1)
    tri = ci <= ri
    b_col = _cumsum_rows(lf_col)
    b_row = _dot_exact_rhs(lf_row, jnp.where(ri <= ci, 1.0, 0.0))
    m_prev = m_in[...]
    lane = lax.broadcasted_iota(jnp.int32, (1, LANES), 1)
    heads = list(range(A_HEADS))
    bc = [b_col[:, A_HEADS + h:A_HEADS + h + 1] for h in heads]
    mh = [m_prev[:, A_HEADS + h:A_HEADS + h + 1] for h in heads]
    br = [b_row[h:h + 1, :] for h in heads]
    igr = [ig_row[h:h + 1, :] for h in heads]
    igc = [gcol[:, h:h + 1] for h in heads]
    qh = [qk_act[:, h * A_DK:(h + 1) * A_DK] for h in heads]
    kh = [qk_act[:, A_QK + h * A_DK:A_QK + (h + 1) * A_DK] * (A_DK ** -0.5) for h in heads]
    vh = [v_ref[:, h * A_DV:(h + 1) * A_DV] for h in heads]
    ch = [c_in[h] for h in heads]
    nh = [n_in[h:h + 1, :] for h in heads]
    qk = _each(_dot_nt, qh, kh)
    qc = _each(_dot, qh, ch)
    d = _each(lambda a, b, i: jnp.where(tri, a - b + i, -jnp.inf), bc, br, igr)
    inter = _each(jnp.add, bc, mh)
    m_t = _each(lambda i, x: jnp.maximum(i, jnp.max(x, axis=-1, keepdims=True)), inter, d)
    s = _each(lambda a, x, m: a * jnp.exp(x - m), qk, d, m_t)
    e_in = _each(lambda i, m: jnp.exp(i - m), inter, m_t)
    sv = _each(_dot, s, vh)
    num = _each(lambda a, x, b: a + x * b, sv, e_in, qc)
    den = _each(lambda a, x, qq, nn: jnp.sum(a, axis=-1, keepdims=True)
                + x * jnp.sum(qq * nn, axis=-1, keepdims=True), s, e_in, qh, nh)
    hh = _each(lambda a, b, m: a / jnp.maximum(jnp.abs(b), jnp.exp(-m)), num, den, m_t)
    hh = _each(lambda a: a * lax.rsqrt(jnp.mean(a * a, axis=-1, keepdims=True) + EPS), hh)
    b_end = _each(lambda a: a[T - 1:T, :], bc)
    m_new = _each(lambda be, m, b, i: jnp.maximum(be + m, jnp.max(be - b + i, axis=-1, keepdims=True)),
                  b_end, mh, br, igr)
    carry = _each(lambda be, m, mn: jnp.exp(be + m - mn), b_end, mh, m_new)
    wk = _each(lambda kk_, be, a, i, mn: kk_ * jnp.exp(be - a + i - mn), kh, b_end, bc, igc, m_new)
    kv = _each(_dot_tn, wk, vh)
    m_next = m_prev
    for h in heads:
        cols = slice(h * A_DV, (h + 1) * A_DV)
        gate = _sigmoid(o_ref[:, cols]) * _sigmoid(ga_ref[:, cols])
        h_ref[:, cols] = (hh[h] * nw_ref[:, cols] * gate).astype(h_ref.dtype)
        c_out[h] = carry[h] * ch[h] + kv[h]
        n_out[h:h + 1, :] = carry[h] * nh[h] + jnp.sum(wk[h], axis=0, keepdims=True)
        m_next = jnp.where(lane == A_HEADS + h, m_new[h], m_next)
    m_out[...] = m_next

    if not single_chunk:
        @pl.when(c == pl.num_programs(1) - 1)
        def _():
            cn_ref[0] = c_s[...]
            nn_ref[0] = n_s[...]
            mn_ref[0] = m_s[...]


def _mlstm(proj, gates_row, e_conv, conv_w, conv_b, bias_row, bias_col, norm_w, c0, n0, m0, *, T, n_seq, n_chunk):
    row = lambda s, c: (s * n_chunk + c, 0)
    fixed = lambda s, c: (0, 0)
    wide = lambda col: pl.BlockSpec((T, D_MODEL), lambda s, c: (s * n_chunk + c, col // D_MODEL))
    return pl.pallas_call(
        functools.partial(_mlstm_body, T=T, single_chunk=n_chunk == 1),
        grid=(n_seq, n_chunk),
        in_specs=[wide(COL_CONV), wide(COL_V), wide(COL_O), wide(COL_GA),
                  pl.BlockSpec((T, LANES), lambda s, c: (s * n_chunk + c, (COL_TAIL + TAIL_IF) // LANES)),
                  pl.BlockSpec((1, 2 * A_HEADS, T), lambda s, c: (s * n_chunk + c, 0, 0)),
                  pl.BlockSpec((1, SUBLANES, CONV_COLS), lambda s, c: (s, 0, 0)),
                  pl.BlockSpec((A_CONV, CONV_COLS), fixed),
                  pl.BlockSpec((1, CONV_COLS), fixed),
                  pl.BlockSpec((1, LANES), fixed),
                  pl.BlockSpec((2 * A_HEADS, 1), fixed),
                  pl.BlockSpec((1, A_VW), fixed),
                  pl.BlockSpec((1, A_HEADS, A_DK, A_DV), lambda s, c: (s, 0, 0, 0)),
                  pl.BlockSpec((1, A_HEADS, A_DK), lambda s, c: (s, 0, 0)),
                  pl.BlockSpec((1, 1, LANES), lambda s, c: (s, 0, 0))],
        out_specs=[pl.BlockSpec((T, A_VW), row),
                   pl.BlockSpec((1, A_HEADS, A_DK, A_DV), lambda s, c: (s, 0, 0, 0)),
                   pl.BlockSpec((1, A_HEADS, A_DK), lambda s, c: (s, 0, 0)),
                   pl.BlockSpec((1, 1, LANES), lambda s, c: (s, 0, 0))],
        out_shape=[jax.ShapeDtypeStruct((n_seq * n_chunk * T, A_VW), BF16),
                   jax.ShapeDtypeStruct((n_seq, A_HEADS, A_DK, A_DV), F32),
                   jax.ShapeDtypeStruct((n_seq, A_HEADS, A_DK), F32),
                   jax.ShapeDtypeStruct((n_seq, 1, LANES), F32)],
        scratch_shapes=[pltpu.VMEM((A_HEADS, A_DK, A_DV), F32),
                        pltpu.VMEM((A_HEADS, A_DK), F32),
                        pltpu.VMEM((1, LANES), F32),
                        pltpu.VMEM((SUBLANES, CONV_COLS), F32)],
        compiler_params=_cparams(("arbitrary", "arbitrary"), 32),
        name="mlstm_chunk",
    )(proj, proj, proj, proj, proj, gates_row, e_conv, conv_w, conv_b, bias_row, bias_col, norm_w, c0, n0, m0)


def _kcat_dot(a1, b1, a2, b2):
    if a1.shape[1] % LANES == 0:
        return _dot(jnp.concatenate([a1, a2], axis=1), _rows(b1, b2))
    return _dot(a1, b1) + _dot(a2, b2)


def _head_sums(ones_blk, T, *xs):
    sums = jnp.dot(_rows(*xs).astype(BF16), ones_blk, preferred_element_type=F32)
    return [sums[i * T:(i + 1) * T] for i in range(len(xs))]


def _rwkv_body(ur_ref, uk_ref, uv_ref, ul_ref, gb_ref, er_ref, ek_ref, ev_ref, el_ref,
               mur_ref, muk_ref, muv_ref, mul_ref, w0_ref, w2_ref, a0_ref, a2_ref, g2_ref, kkw_ref, ka_ref,
               rk_ref, lnw_ref, lnb_ref, blk_ref, blkb_ref, s0_ref,
               hb_ref, sn_ref, s_s, cr_s, ck_s, cv_s, cl_s, *, T, n_pairs):
    c = pl.program_id(2)

    @pl.when(c == 0)
    def _():
        z = jnp.zeros((B_HEAD, B_HEAD), F32)
        for p in range(n_pairs):
            s_s[p] = _rows(jnp.concatenate([s0_ref[0, 2 * p], z], axis=1),
                           jnp.concatenate([z, s0_ref[0, 2 * p + 1]], axis=1))
        cr_s[...] = er_ref[0]
        ck_s[...] = ek_ref[0]
        cv_s[...] = ev_ref[0]
        cl_s[...] = el_ref[0]

    n_sub = ur_ref.shape[0] // T

    def shift_mix(u_ref, carry_ref, mu_ref, cs, rows):
        u = u_ref[rows, cs]
        before = carry_ref[:, cs] if rows.start == 0 else u_ref[rows.start - SUBLANES:rows.start, cs]
        prev = pltpu.roll(_rows(before, u), 1, 0)[SUBLANES:, :]
        if rows.stop == u_ref.shape[0]:
            carry_ref[:, cs] = u[rows.stop - rows.start - SUBLANES:, :]
        return u + mu_ref[:, cs] * (prev - u)

    xl = shift_mix(ul_ref, cl_s, mul_ref, slice(None), slice(0, n_sub * T))
    th_wd = jnp.tanh(xl[:, :LANES]).astype(BF16)
    x_ad = pltpu.roll(xl[:, :2 * LANES], 2 * LANES - LORA_DECAY, 1)[:, :LANES].astype(BF16)
    gd_off = LORA_DECAY + LORA_ICLR - TAIL_GD_WIN
    sg_gd = _sigmoid(pltpu.roll(xl[:, TAIL_GD_WIN:], TAIL_MIX - TAIL_GD_WIN - gd_off, 1)[:, :LORA_GATE]).astype(BF16)

    lo = lax.broadcasted_iota(jnp.int32, (1, LANES), 1) < B_HEAD
    head1 = lambda x: jnp.where(lo, x, jnp.zeros_like(x))
    head2 = lambda x: jnp.where(lo, jnp.zeros_like(x), x)
    r2 = lax.broadcasted_iota(jnp.int32, (T, 2 * T), 0)
    c2 = lax.broadcasted_iota(jnp.int32, (T, 2 * T), 1)
    left = c2 < T
    cj = jnp.where(left, c2, c2 - T)
    strict2 = cj < r2
    incl2 = cj <= r2
    eye_pair = jnp.where(cj == r2, 1.0, 0.0)
    ones_blk = blkb_ref[...]
    zero = jnp.zeros((T, LANES), BF16)
    aligned = (2 * T) % LANES == 0

    def block_diag(sb):
        z = jnp.zeros_like(sb)
        return _rows(jnp.where(left, sb, z), jnp.where(left, z, sb))

    def pair_program(p):
        S = None
        for sub in range(n_sub):
            S = yield from chunk_program(p, slice(sub * T, (sub + 1) * T), S)
        s_s[p] = S

    def chunk_program(p, rows, S):
        cs = slice(p * LANES, (p + 1) * LANES)
        r = shift_mix(ur_ref, cr_s, mur_ref, cs, rows)
        xk = shift_mix(uk_ref, ck_s, muk_ref, cs, rows)
        v = shift_mix(uv_ref, cv_s, muv_ref, cs, rows)
        lw = jnp.dot(th_wd[rows], w2_ref[:, cs], preferred_element_type=F32)
        la = jnp.dot(x_ad[rows], a2_ref[:, cs], preferred_element_type=F32)
        lg = jnp.dot(sg_gd[rows], g2_ref[:, cs], preferred_element_type=F32)
        kkw = xk * kkw_ref[:, cs]
        ssq, = _head_sums(ones_blk, T, kkw * kkw)
        yield
        ld = DECAY_SCALE * _sigmoid(w0_ref[:, cs] + lw)
        a = _sigmoid(a0_ref[:, cs] + la)
        g = lg * _sigmoid(gb_ref[rows, cs])
        k = xk * (1.0 + (a - 1.0) * ka_ref[:, cs])
        kk = kkw * lax.rsqrt(jnp.maximum(ssq, 1e-24))
        bi = kk * a
        cum = _cumsum_rows(ld)
        ewt = jnp.exp(cum[T - 1:T, :])
        Rt = (r * jnp.exp(cum)).astype(BF16)
        At = (-kk * jnp.exp(cum - ld)).astype(BF16)
        einv = jnp.exp(-cum)
        Bt = bi * einv
        Kt = k * einv
        BhKh = _rows(Bt * ewt, Kt * ewt).astype(BF16)
        Bt = Bt.astype(BF16)
        Kt = Kt.astype(BF16)
        vb = v.astype(BF16)
        V1 = head1(vb)
        V2 = head2(vb)
        if aligned:
            G = _dot_nt(_rows(head1(At), head1(Rt), head2(At), head2(Rt)), _rows(Bt, Kt))
            G1, G2 = G[:2 * T], G[2 * T:]
        else:
            G1 = _dot_nt(_rows(head1(At), head1(Rt)), _rows(Bt, Kt))
            G2 = _dot_nt(_rows(head2(At), head2(Rt)), _rows(Kt, Bt))
        yield
        LL1 = jnp.where(strict2, G1[:T], 0.0)
        LL2 = jnp.where(strict2, G2[:T], 0.0)
        P1 = jnp.where(incl2, G1[T:], 0.0)
        P2 = jnp.where(incl2, G2[T:], 0.0)
        lab2 = pltpu.roll(LL2, T, 1) if aligned else LL2
        pw = jnp.where(left, LL1, lab2).astype(BF16)
        iv = eye_pair + jnp.where(left, LL1, lab2)
        X0 = _kcat_dot(LL1, _rows(zero, V1), LL2, _rows(zero, V2) if aligned else _rows(V2, zero))
        pw = _dot(pw, block_diag(pw)).astype(BF16)
        yield
        span = 4
        while span < T:
            both = _dot(_rows(iv.astype(BF16), pw), block_diag(pw))
            iv, pw = iv + both[:T], both[T:].astype(BF16)
            span *= 2
            yield
        iv = (iv + _dot(iv, block_diag(pw))).astype(BF16)
        if S is None:
            S = s_s[p]
        AR = _dot_nt(_rows(At, Rt), S)
        yield
        X = (AR[:T] + X0).astype(BF16)
        U = _dot(iv, _rows(head1(X), head2(X))).astype(BF16)
        yield
        if aligned:
            rhs2 = _rows(head2(U), V2)
        else:
            rhs2 = _rows(V2, head2(U))
        o = AR[T:] + _kcat_dot(P1, _rows(head1(U), V1), P2, rhs2)
        S_next = S * ewt + blk_ref[...] * _dot_tn(_rows(U, vb), BhKh)
        yield
        o_sum, rk_sum = _head_sums(ones_blk, T, o, r * k * rk_ref[:, cs])
        yield
        oc = o - o_sum * (1.0 / B_HEAD)
        var, = _head_sums(ones_blk, T, oc * oc)
        yield
        on = oc * lax.rsqrt(var * (1.0 / B_HEAD) + GN_EPS) * lnw_ref[:, cs] + lnb_ref[:, cs]
        hb_ref[rows, cs] = ((on + rk_sum * v) * g).astype(hb_ref.dtype)
        return S_next

    _run_staggered([pair_program(p) for p in range(n_pairs)], RWKV_WAVE, RWKV_LAG)

    @pl.when(c == pl.num_programs(2) - 1)
    def _():
        for p in range(n_pairs):
            sn_ref[0, 2 * p] = s_s[p, :B_HEAD, :B_HEAD]
            sn_ref[0, 2 * p + 1] = s_s[p, B_HEAD:, B_HEAD:]


def _rwkv(proj, e_main, e_tail, mu_main, mu_tail, w0, w2p, a0, a2p, g2, k_k, k_a, rk, lnw, lnb, blk, s0, *,
          T, n_seq, n_chunk):
    assert T <= RWKV_CHUNK
    gw = GROUP_W
    n_sub = RWKV_CHUNKS_PER_STEP if n_chunk % RWKV_CHUNKS_PER_STEP == 0 else 1
    n_step = n_chunk // n_sub
    tb = n_sub * T
    wide = lambda col: pl.BlockSpec((tb, gw), lambda s, pg, c: (s * n_step + c, col // gw + pg))
    edge = lambda col: pl.BlockSpec((1, SUBLANES, gw), lambda s, pg, c: (s, 0, col // gw + pg))
    vec = lambda col: pl.BlockSpec((1, gw), lambda s, pg, c: (0, col // gw + pg))
    mat = lambda rows: pl.BlockSpec((rows, gw), lambda s, pg, c: (0, pg))
    st = pl.BlockSpec((1, 2 * GROUP_PAIRS, B_HEAD, B_HEAD), lambda s, pg, c: (s, pg, 0, 0))
    return pl.pallas_call(
        functools.partial(_rwkv_body, T=T, n_pairs=GROUP_PAIRS),
        grid=(n_seq, N_PAIRS // GROUP_PAIRS, n_step),
        in_specs=[wide(COL_R), wide(COL_K), wide(COL_VB),
                  pl.BlockSpec((tb, TAIL_MIX), lambda s, pg, c: (s * n_step + c, COL_TAIL // TAIL_MIX)),
                  wide(COL_GB),
                  edge(0), edge(B_W), edge(2 * B_W),
                  pl.BlockSpec((1, SUBLANES, TAIL_MIX), lambda s, pg, c: (s, 0, 0)),
                  vec(0), vec(B_W), vec(2 * B_W),
                  pl.BlockSpec((1, TAIL_MIX), lambda s, pg, c: (0, 0)),
                  vec(0), mat(LANES), vec(0), mat(LANES), mat(LORA_GATE), vec(0), vec(0),
                  vec(0), vec(0), vec(0),
                  pl.BlockSpec((LANES, LANES), lambda s, pg, c: (0, 0)),
                  pl.BlockSpec((LANES, LANES), lambda s, pg, c: (0, 0)), st],
        out_specs=[pl.BlockSpec((tb, gw), lambda s, pg, c: (s * n_step + c, pg)), st],
        out_shape=[jax.ShapeDtypeStruct((n_seq * n_chunk * T, B_W), BF16),
                   jax.ShapeDtypeStruct((n_seq, B_HEADS, B_HEAD, B_HEAD), F32)],
        scratch_shapes=[pltpu.VMEM((GROUP_PAIRS, LANES, LANES), F32),
                        pltpu.VMEM((SUBLANES, gw), F32), pltpu.VMEM((SUBLANES, gw), F32),
                        pltpu.VMEM((SUBLANES, gw), F32), pltpu.VMEM((SUBLANES, TAIL_MIX), F32)],
        compiler_params=_cparams(("arbitrary", "arbitrary", "arbitrary"), 32),
        name="rwkv_chunk",
    )(proj, proj, proj, proj, proj, e_main, e_main, e_main, e_tail, mu_main, mu_main, mu_main, mu_tail,
      w0, w2p, a0, a2p, g2, k_k, k_a, rk, lnw, lnb, blk, blk.astype(BF16), s0)


def _merge_out_body(ha_ref, hb_ref, x_ref, w_ref, g_ref, y_ref):
    mixed = ha_ref[...].astype(F32) + hb_ref[...].astype(F32)
    y = jnp.dot(mixed.astype(BF16), w_ref[...], preferred_element_type=F32)
    ms = jnp.mean(y * y, axis=-1, keepdims=True)
    y_ref[...] = x_ref[...] + y * lax.rsqrt(ms + EPS) * g_ref[...]


def _merge_out(h_a, h_b, x, w_out, g_post):
    n = x.shape[0]
    wide = pl.BlockSpec((MERGE_TM, D_MODEL), lambda i: (i, 0))
    return pl.pallas_call(
        _merge_out_body,
        grid=(n // MERGE_TM,),
        in_specs=[wide, wide, wide,
                  pl.BlockSpec((D_MODEL, D_MODEL), lambda i: (0, 0)),
                  pl.BlockSpec((1, D_MODEL), lambda i: (0, 0))],
        out_specs=wide,
        out_shape=jax.ShapeDtypeStruct((n, D_MODEL), F32),
        compiler_params=_cparams(("parallel",), 48),
        name="merge_out_proj",
    )(h_a, h_b, x, w_out, g_post)


def _mlp_body(x_ref, xn_ref, g1_ref, wu_ref, wd_ref, g2_ref, y_ref, hn_ref, acc_ref, *, row_chunk):
    i = pl.program_id(0)
    j = pl.program_id(1)
    last = pl.num_programs(1) - 1
    slot = i % 2
    step_rows = xn_ref.shape[0]

    @pl.when((i == 0) & (j == 0))
    def _():
        _norm_rows(x_ref, g1_ref, hn_ref.at[0], row_chunk)
        acc_ref[...] = jnp.zeros_like(acc_ref)

    def partial_product():
        u = jnp.dot(hn_ref[slot], wu_ref[...], preferred_element_type=F32)
        a = jnp.square(jnp.maximum(u, 0.0))
        part = jnp.dot(a.astype(BF16), wd_ref[...], preferred_element_type=F32)
        rows = pl.ds(pl.multiple_of(j * step_rows, step_rows), step_rows)
        hn_ref[1 - slot, rows, :] = _norm_bf16(xn_ref[...], g1_ref[...])
        return part

    @pl.when(j < last)
    def _():
        part = partial_product()
        acc_ref[...] = jnp.where(j == 0, part, acc_ref[...] + part)

    @pl.when(j == last)
    def _():
        ff = acc_ref[...] + partial_product()
        ms = jnp.mean(ff * ff, axis=-1, keepdims=True)
        y_ref[...] = x_ref[...] + ff * lax.rsqrt(ms + EPS) * g2_ref[...]


def _mlp(x, g_pre, w_up, w_down, g_post):
    n, d = x.shape
    f = w_up.shape[1]
    nj = f // MLP_TF
    step_rows = MLP_TM // nj
    n_slices = n // step_rows
    return pl.pallas_call(
        functools.partial(_mlp_body, row_chunk=LANES),
        grid=(n // MLP_TM, nj),
        in_specs=[pl.BlockSpec((MLP_TM, d), lambda i, j: (i, 0)),
                  pl.BlockSpec((step_rows, d), lambda i, j: (jnp.minimum((i + 1) * nj + j, n_slices - 1), 0)),
                  pl.BlockSpec((1, d), lambda i, j: (0, 0)),
                  pl.BlockSpec((d, MLP_TF), lambda i, j: (0, j)),
                  pl.BlockSpec((MLP_TF, d), lambda i, j: (j, 0)),
                  pl.BlockSpec((1, d), lambda i, j: (0, 0))],
        out_specs=pl.BlockSpec((MLP_TM, d), lambda i, j: (i, 0)),
        out_shape=jax.ShapeDtypeStruct((n, d), F32),
        scratch_shapes=[pltpu.VMEM((2, MLP_TM, d), BF16), pltpu.VMEM((MLP_TM, d), F32)],
        compiler_params=_cparams(("arbitrary", "arbitrary"), 56),
        name="mlp",
    )(x, x, g_pre, w_up, w_down, g_post)


def _pad_cols(a, width):
    return jnp.pad(a, [(0, 0)] * (a.ndim - 1) + [(0, width - a.shape[-1])])


def _pad_rows(a, before, total):
    return jnp.pad(a, ((before, total - before - a.shape[0]), (0, 0)))


def _edge_blocks(prev_rows):
    return jnp.pad(prev_rows, ((0, 0), (SUBLANES - prev_rows.shape[1], 0), (0, 0)))


def _gates_row_form(proj, T):
    g = proj[:, COL_TAIL + TAIL_IF:COL_TAIL + TAIL_IF + 2 * A_HEADS]
    return jnp.swapaxes(g.reshape(-1, T, 2 * A_HEADS), 1, 2)


def kernel(x_prompt, x_sample, state_mlstm_C, state_mlstm_n, state_mlstm_m, state_mlstm_conv, state_rwkv_S, state_rwkv_shift, meta_tokens, norm_mix_pre, norm_mix_post, norm_mlp_pre, norm_mlp_post, w_in, mlstm_conv_w, mlstm_conv_b, mlstm_b_i, mlstm_b_f, mlstm_norm_w, rwkv_mu, rwkv_w0, rwkv_w2, rwkv_a0, rwkv_a2, rwkv_g2, rwkv_k_k, rwkv_k_a, rwkv_r_k, rwkv_ln_w, rwkv_ln_b, w_out, w_up, w_down):
    n_s_seq, s_len, _ = x_sample.shape
    _, p_len, _ = x_prompt.shape
    k3 = A_CONV - 1

    w_all = _relayout_weight(jnp.swapaxes(w_in[0], 0, 1))
    mu = rwkv_mu[0][None, :]
    mu_main = mu[:, :3 * B_W]
    mu_tail = _pad_cols(mu[:, 3 * B_W:], TAIL_MIX)
    w2p = _pad_rows(rwkv_w2[0], 0, LANES).astype(BF16)
    a2p = _pad_rows(rwkv_a2[0], 0, LANES).astype(BF16)
    g2 = rwkv_g2[0].astype(BF16)
    g_pre = norm_mix_pre[0][None, :]
    bias = jnp.concatenate([mlstm_b_i[0], mlstm_b_f[0]])
    bias_row = _pad_cols(bias[None, :], LANES)
    bias_col = bias[:, None]
    norm_w = mlstm_norm_w[0][None, :]
    conv_w = mlstm_conv_w[0]
    conv_b = mlstm_conv_b[0][None, :]
    rwkv_vecs = (rwkv_w0[0][None, :], w2p, rwkv_a0[0][None, :], a2p, g2, rwkv_k_k[0][None, :],
                 rwkv_k_a[0][None, :], rwkv_r_k[0].reshape(1, B_W), rwkv_ln_w[0][None, :], rwkv_ln_b[0][None, :],
                 _head_block_mask())
    w_out_b = w_out[0].astype(BF16)
    w_up_b = w_up[0].astype(BF16)
    w_down_b = w_down[0].astype(BF16)

    def branches(proj, T, n_seq, n_chunk, e_conv, e_main, e_tail, c0, n0, m0, s0):
        tm_ = MLSTM_CHUNK if n_chunk > 1 else T
        h_a, c_n, n_n, m_n = _mlstm(proj, _gates_row_form(proj, tm_), e_conv, conv_w, conv_b, bias_row, bias_col,
                                    norm_w, c0, n0, m0, T=tm_, n_seq=n_seq, n_chunk=n_chunk * T // tm_)
        h_b, s_n = _rwkv(proj, e_main, e_tail, mu_main, mu_tail, *rwkv_vecs, s0,
                         T=T, n_seq=n_seq, n_chunk=n_chunk)
        return h_a, h_b, (c_n, n_n, m_n, s_n)

    def finish(h_a, h_b, x_rows):
        x1 = _merge_out(h_a, h_b, x_rows, w_out_b, norm_mix_post[0][None, :])
        return _mlp(x1, norm_mlp_pre[0][None, :], w_up_b, w_down_b, norm_mlp_post[0][None, :])

    zeros = lambda *shape: jnp.zeros(shape, F32)
    xs = x_sample.reshape(n_s_seq * s_len, D_MODEL).astype(F32)
    assert xs.shape[0] == PROJ_TM
    proj_s, proj_m = _norm_matmul_pair(xs, meta_tokens.astype(F32), g_pre, w_all)
    _, _, (c_m, n_m, m_m, s_m) = branches(
        proj_m, N_META, 1, 1,
        zeros(1, SUBLANES, CONV_COLS), zeros(1, SUBLANES, 3 * B_W), zeros(1, SUBLANES, TAIL_MIX),
        zeros(1, A_HEADS, A_DK, A_DV), zeros(1, A_HEADS, A_DK), zeros(1, 1, LANES),
        zeros(1, B_HEADS, B_HEAD, B_HEAD))

    xp = x_prompt[0].astype(F32)
    proj_p = _norm_matmul(xp, g_pre, w_all, PROJ_TM)
    ha_p, hb_p, (c_p, n_p, m_p, s_p) = branches(
        proj_p, RWKV_CHUNK, 1, p_len // RWKV_CHUNK,
        _edge_blocks(proj_m[None, N_META - k3:, COL_CONV:COL_CONV + CONV_COLS]),
        _edge_blocks(proj_m[None, N_META - 1:, COL_R:COL_R + 3 * B_W]),
        _edge_blocks(proj_m[None, N_META - 1:, COL_TAIL:COL_TAIL + TAIL_MIX]),
        c_m, n_m, m_m, s_m)
    y_prompt = finish(ha_p, hb_p, xp)[None].astype(x_prompt.dtype)

    sh = state_rwkv_shift[0].astype(F32)
    m0_s = jnp.pad(state_mlstm_m[0].astype(F32), ((0, 0), (A_HEADS, LANES - 2 * A_HEADS)))[:, None, :]
    ha_s, hb_s, (c_s, n_s, m_s, s_s) = branches(
        proj_s, s_len, n_s_seq, 1,
        _edge_blocks(state_mlstm_conv[0].astype(F32)),
        _edge_blocks(sh[:, :, :3 * B_W]), _edge_blocks(_pad_cols(sh[:, :, 3 * B_W:], TAIL_MIX)),
        state_mlstm_C[0].astype(F32), state_mlstm_n[0].astype(F32), m0_s,
        state_rwkv_S[0].astype(F32))
    y_sample = finish(ha_s, hb_s, xs).reshape(x_sample.shape).astype(x_sample.dtype)

    def shift_state(rows):
        return jnp.concatenate([rows[..., COL_R:COL_R + 3 * B_W],
                                rows[..., COL_TAIL:COL_TAIL + LORA_COLS]], axis=-1)

    last_s = proj_s.reshape(n_s_seq, s_len, PROJ_COLS)[:, s_len - k3:, :]
    conv_s = last_s[..., COL_CONV:COL_CONV + CONV_COLS]
    shift_s = shift_state(last_s[:, k3 - 1:, :])
    last_p = proj_p[p_len - k3:, :]
    conv_p = last_p[None, :, COL_CONV:COL_CONV + CONV_COLS]
    shift_p = shift_state(last_p[None, k3 - 1:, :])
    dt_c, dt_n, dt_m = state_mlstm_C.dtype, state_mlstm_n.dtype, state_mlstm_m.dtype
    dt_cv, dt_s, dt_sh = state_mlstm_conv.dtype, state_rwkv_S.dtype, state_rwkv_shift.dtype
    lanes_m = slice(A_HEADS, 2 * A_HEADS)
    return (y_prompt, y_sample,
            c_p[None].astype(dt_c), n_p[None].astype(dt_n), m_p[:, 0, lanes_m][None].astype(dt_m),
            conv_p[None].astype(dt_cv), s_p[None].astype(dt_s), shift_p[None].astype(dt_sh),
            c_s[None].astype(dt_c), n_s[None].astype(dt_n), m_s[:, 0, lanes_m][None].astype(dt_m),
            conv_s[None].astype(dt_cv), s_s[None].astype(dt_s), shift_s[None].astype(dt_sh))
```

```python
import functools

import jax
import jax.numpy as jnp
from jax import lax
from jax.experimental import pallas as pl
from jax.experimental.pallas import tpu as pltpu

F32 = jnp.float32
BF16 = jnp.bfloat16

D_MODEL = 2048
N_META = 16
EPS = 1e-6
D_FF = 4 * D_MODEL
A_HEADS = 8
A_DK = 128
A_DV = D_MODEL // A_HEADS
A_QK = A_HEADS * A_DK
A_VW = A_HEADS * A_DV
A_CONV = 4
MLSTM_CHUNK = 256
B_HEAD = 64
B_HEADS = D_MODEL // B_HEAD
B_W = B_HEADS * B_HEAD
LORA_DECAY = max(32, int(round(1.8 * D_MODEL ** 0.5 / 32)) * 32)
LORA_ICLR = max(32, int(round(1.8 * D_MODEL ** 0.5 / 32)) * 32)
LORA_GATE = max(32, int(round(0.6 * D_MODEL ** 0.8 / 32)) * 32)
GN_EPS = 64e-5
CONV_COLS = 2 * A_QK
SHIFT_COLS = 3 * B_W + LORA_DECAY + LORA_ICLR + LORA_GATE

LANES = 128
SUBLANES = 8

COL_CONV, COL_V, COL_O, COL_R, COL_K, COL_VB, COL_GA, COL_GB = (i * D_MODEL for i in range(8))
COL_TAIL = 8 * D_MODEL
LORA_COLS = LORA_DECAY + LORA_ICLR + LORA_GATE
TAIL_MIX = -(-LORA_COLS // LANES) * LANES
TAIL_IF = TAIL_MIX
TAIL_GD_WIN = (LORA_DECAY + LORA_ICLR) // LANES * LANES
PROJ_TN = 1024
PROJ_COLS = -(-(COL_TAIL + TAIL_IF + LANES) // PROJ_TN) * PROJ_TN
PROJ_TM = 1024
MERGE_TM = 512
MLP_TM, MLP_TF = 512, 1024
DECAY_SCALE = -0.6065306597126334
RWKV_CHUNK = 64
N_PAIRS = B_HEADS // 2
GROUP_PAIRS = 16
RWKV_CHUNKS_PER_STEP = 4
RWKV_WAVE, RWKV_LAG = 8, 1
GROUP_W = GROUP_PAIRS * LANES


def _cparams(semantics, vmem_mib):
    return pltpu.CompilerParams(dimension_semantics=semantics, vmem_limit_bytes=vmem_mib << 20)


def _softplus(x):
    return jnp.maximum(x, 0.0) + jnp.log1p(jnp.exp(-jnp.abs(x)))


def _dot(a, b):
    return jnp.dot(a.astype(BF16), b.astype(BF16), preferred_element_type=F32)


def _dot_nt(a, b):
    return lax.dot_general(a.astype(BF16), b.astype(BF16), (((1,), (1,)), ((), ())),
                           preferred_element_type=F32)


def _dot_tn(a, b):
    return lax.dot_general(a.astype(BF16), b.astype(BF16), (((0,), (0,)), ((), ())),
                           preferred_element_type=F32)


def _each(f, *lists):
    return [f(*args) for args in zip(*lists)]


def _run_staggered(programs, wave, lag):
    live = list(enumerate(programs))
    slot = 0
    while live:
        still = []
        for p, prog in live:
            if slot >= (p // wave) * lag:
                try:
                    next(prog)
                except StopIteration:
                    continue
            still.append((p, prog))
        live = still
        slot += 1


def _rows(*parts):
    return jnp.concatenate(parts, axis=0)


def _split_bf16(x, pieces):
    out = []
    for _ in range(pieces - 1):
        p = x.astype(BF16)
        out.append(p)
        x = x - p.astype(F32)
    out.append(x.astype(BF16))
    return out


def _dot_exact_rhs(x, m, pieces=3):
    mb = m.astype(BF16)
    acc = None
    for p in _split_bf16(x, pieces):
        t = jnp.dot(p, mb, preferred_element_type=F32)
        acc = t if acc is None else acc + t
    return acc


def _sigmoid(x):
    return 0.5 * jnp.tanh(0.5 * x) + 0.5


def _cumsum_rows(x):
    n = x.shape[0]
    row = lax.broadcasted_iota(jnp.int32, x.shape, 0)
    step = 1
    while step < n:
        x = x + jnp.where(row >= step, pltpu.roll(x, step, 0), 0.0)
        step *= 2
    return x


def _head_block_mask():
    a = lax.broadcasted_iota(jnp.int32, (LANES, LANES), 0) // B_HEAD
    b = lax.broadcasted_iota(jnp.int32, (LANES, LANES), 1) // B_HEAD
    return jnp.where(a == b, 1.0, 0.0).astype(F32)


SRC_IF = CONV_COLS + 2 * A_VW
SRC_SHIFT = SRC_IF + 2 * A_HEADS
SRC_GATE = SRC_SHIFT + SHIFT_COLS
SRC_COLS = SRC_GATE + 2 * D_MODEL
RELAYOUT_ROWS = TAIL_MIX


def _relayout_src_row(t):
    unit = 2 * SUBLANES
    dst = t * (RELAYOUT_ROWS // unit)
    src = jnp.where(dst < COL_R // unit, dst,
                    jnp.where(dst < COL_GA // unit, dst + (SRC_SHIFT - COL_R) // unit,
                              jnp.where(dst < COL_TAIL // unit, dst + (SRC_GATE - COL_GA) // unit,
                                        jnp.where(dst < (COL_TAIL + TAIL_MIX) // unit,
                                                  (SRC_SHIFT + 3 * B_W) // unit, SRC_IF // unit))))
    return src * unit


def _relayout_body(w_ref, o_ref):
    o_ref[...] = w_ref[...].astype(BF16)


def _relayout_weight(wt):
    d = wt.shape[1]
    assert wt.shape[0] == SRC_COLS and PROJ_COLS == COL_TAIL + 2 * RELAYOUT_ROWS and COL_R == SRC_IF
    assert SRC_SHIFT + 3 * B_W + RELAYOUT_ROWS <= SRC_COLS and SRC_SHIFT % (2 * SUBLANES) == 0
    return pl.pallas_call(
        _relayout_body,
        grid=(PROJ_COLS // RELAYOUT_ROWS,),
        in_specs=[pl.BlockSpec((pl.Element(RELAYOUT_ROWS), pl.Element(d)), lambda t: (_relayout_src_row(t), 0))],
        out_specs=pl.BlockSpec((RELAYOUT_ROWS, d), lambda t: (t, 0)),
        out_shape=jax.ShapeDtypeStruct((PROJ_COLS, d), BF16),
        compiler_params=_cparams(("parallel",), 32),
        name="relayout_w_in",
    )(wt)


def _norm_rows(x_ref, g_ref, out_ref, row_chunk):
    def body(c, carry):
        rows = pl.ds(pl.multiple_of(c * row_chunk, row_chunk), row_chunk)
        x = x_ref[rows, :]
        ms = jnp.mean(x * x, axis=-1, keepdims=True)
        out_ref[rows, :] = (x * lax.rsqrt(ms + EPS) * g_ref[...]).astype(out_ref.dtype)
        return carry
    lax.fori_loop(0, x_ref.shape[0] // row_chunk, body, 0)


def _norm_bf16(x, g):
    ms = jnp.mean(x * x, axis=-1, keepdims=True)
    return (x * lax.rsqrt(ms + EPS) * g).astype(BF16)


def _norm_matmul_body(x0_ref, xn_ref, g_ref, w_ref, o_ref, hn_ref, *, row_chunk):
    i = pl.program_id(0)
    j = pl.program_id(1)
    slot = i % 2
    step_rows = xn_ref.shape[0]

    @pl.when((i == 0) & (j == 0))
    def _():
        _norm_rows(x0_ref, g_ref, hn_ref.at[0], row_chunk)

    o_ref[...] = lax.dot_general(hn_ref[slot], w_ref[...], (((1,), (1,)), ((), ())), preferred_element_type=F32)
    s = jnp.minimum(j, hn_ref.shape[1] // step_rows - 1)
    rows = pl.ds(pl.multiple_of(s * step_rows, step_rows), step_rows)
    hn_ref[1 - slot, rows, :] = _norm_bf16(xn_ref[...], g_ref[...])


def _norm_matmul(x, g, w, tm):
    n, d = x.shape
    p = w.shape[0]
    nj = p // PROJ_TN
    step_rows = tm // (nj - 1)
    assert step_rows * (nj - 1) == tm and step_rows % SUBLANES == 0
    n_slices = n // step_rows
    per_tile = nj - 1
    return pl.pallas_call(
        functools.partial(_norm_matmul_body, row_chunk=min(tm, LANES)),
        grid=(n // tm, nj),
        in_specs=[pl.BlockSpec((tm, d), lambda i, j: (0, 0)),
                  pl.BlockSpec((step_rows, d),
                               lambda i, j: (jnp.minimum((i + 1) * per_tile + jnp.minimum(j, per_tile - 1),
                                                         n_slices - 1), 0)),
                  pl.BlockSpec((1, d), lambda i, j: (0, 0)),
                  pl.BlockSpec((PROJ_TN, d), lambda i, j: (j, 0))],
        out_specs=pl.BlockSpec((tm, PROJ_TN), lambda i, j: (i, j)),
        out_shape=jax.ShapeDtypeStruct((n, p), F32),
        scratch_shapes=[pltpu.VMEM((2, tm, d), BF16)],
        compiler_params=_cparams(("arbitrary", "arbitrary"), 48),
        name="in_proj",
    )(x, x, g, w)


def _norm_matmul_pair_body(x_ref, xs_ref, g_ref, w_ref, o_ref, os_ref, hn_ref, *, row_chunk):
    n = x_ref.shape[0]

    @pl.when(pl.program_id(0) == 0)
    def _():
        _norm_rows(x_ref, g_ref, hn_ref, row_chunk)
        hn_ref[n:, :] = _norm_bf16(xs_ref[...], g_ref[...])

    res = lax.dot_general(hn_ref[...], w_ref[...], (((1,), (1,)), ((), ())), preferred_element_type=F32)
    o_ref[...] = res[:n]
    os_ref[...] = res[n:]


def _norm_matmul_pair(x, x_small, g, w):
    n, d = x.shape
    ns = x_small.shape[0]
    p = w.shape[0]
    return pl.pallas_call(
        functools.partial(_norm_matmul_pair_body, row_chunk=LANES),
        grid=(p // PROJ_TN,),
        in_specs=[pl.BlockSpec((n, d), lambda j: (0, 0)),
                  pl.BlockSpec((ns, d), lambda j: (0, 0)),
                  pl.BlockSpec((1, d), lambda j: (0, 0)),
                  pl.BlockSpec((PROJ_TN, d), lambda j: (j, 0))],
        out_specs=[pl.BlockSpec((n, PROJ_TN), lambda j: (0, j)),
                   pl.BlockSpec((ns, PROJ_TN), lambda j: (0, j))],
        out_shape=[jax.ShapeDtypeStruct((n, p), F32), jax.ShapeDtypeStruct((ns, p), F32)],
        scratch_shapes=[pltpu.VMEM((n + ns, d), BF16)],
        compiler_params=_cparams(("arbitrary",), 48),
        name="in_proj_pair",
    )(x, x_small, g, w)


def _mlstm_body(u_ref, v_ref, o_ref, ga_ref, gc_ref, gr_ref, e_ref, cw_ref, cb_ref, brow_ref, bcol_ref,
                nw_ref, c0_ref, n0_ref, m0_ref,
                h_ref, cn_ref, nn_ref, mn_ref, c_s, n_s, m_s, e_s, *, T, single_chunk):
    c = pl.program_id(1)
    if single_chunk:
        c_in, n_in, m_in, e_in = c0_ref.at[0], n0_ref.at[0], m0_ref.at[0], e_ref.at[0]
        c_out, n_out, m_out = cn_ref.at[0], nn_ref.at[0], mn_ref.at[0]
    else:
        c_in = c_out = c_s
        n_in = n_out = n_s
        m_in = m_out = m_s
        e_in = e_s

        @pl.when(c == 0)
        def _():
            c_s[...] = c0_ref[0]
            n_s[...] = n0_ref[0]
            m_s[...] = m0_ref[0]
            e_s[...] = e_ref[0]

    u = u_ref[...]
    ext = _rows(e_in[...], u)
    w = cw_ref[...]
    acc = cb_ref[...] + u * w[A_CONV - 1:A_CONV, :]
    for sh in range(1, A_CONV):
        acc = acc + pltpu.roll(ext, sh, 0)[SUBLANES:, :] * w[A_CONV - 1 - sh:A_CONV - sh, :]
    if not single_chunk:
        e_s[...] = u[T - SUBLANES:, :]
    qk_act = acc * _sigmoid(acc)

    gcol = gc_ref[...] + brow_ref[...]
    grow = gr_ref[0] + bcol_ref[...]
    lf_col = -_softplus(-gcol)
    ig_row = grow[:A_HEADS, :]
    lf_row = -_softplus(-grow[A_HEADS:, :])
    ri = lax.broadcasted_iota(jnp.int32, (T, T), 0)
    ci = lax.broadcasted_iota(jnp.int32, (T, T), 1)
    tri = ci <= ri
    b_col = _cumsum_rows(lf_col)
    b_row = _dot_exact_rhs(lf_row, jnp.where(ri <= ci, 1.0, 0.0))
    m_prev = m_in[...]
    lane = lax.broadcasted_iota(jnp.int32, (1, LANES), 1)
    heads = list(range(A_HEADS))
    bc = [b_col[:, A_HEADS + h:A_HEADS + h + 1] for h in heads]
    mh = [m_prev[:, A_HEADS + h:A_HEADS + h + 1] for h in heads]
    br = [b_row[h:h + 1, :] for h in heads]
    igr = [ig_row[h:h + 1, :] for h in heads]
    igc = [gcol[:, h:h + 1] for h in heads]
    qh = [qk_act[:, h * A_DK:(h + 1) * A_DK] for h in heads]
    kh = [qk_act[:, A_QK + h * A_DK:A_QK + (h + 1) * A_DK] * (A_DK ** -0.5) for h in heads]
    vh = [v_ref[:, h * A_DV:(h + 1) * A_DV] for h in heads]
    ch = [c_in[h] for h in heads]
    nh = [n_in[h:h + 1, :] for h in heads]
    qk = _each(_dot_nt, qh, kh)
    qc = _each(_dot, qh, ch)
    d = _each(lambda a, b, i: jnp.where(tri, a - b + i, -jnp.inf), bc, br, igr)
    inter = _each(jnp.add, bc, mh)
    m_t = _each(lambda i, x: jnp.maximum(i, jnp.max(x, axis=-1, keepdims=True)), inter, d)
    s = _each(lambda a, x, m: a * jnp.exp(x - m), qk, d, m_t)
    e_in = _each(lambda i, m: jnp.exp(i - m), inter, m_t)
    sv = _each(_dot, s, vh)
    num = _each(lambda a, x, b: a + x * b, sv, e_in, qc)
    den = _each(lambda a, x, qq, nn: jnp.sum(a, axis=-1, keepdims=True)
                + x * jnp.sum(qq * nn, axis=-1, keepdims=True), s, e_in, qh, nh)
    hh = _each(lambda a, b, m: a / jnp.maximum(jnp.abs(b), jnp.exp(-m)), num, den, m_t)
    hh = _each(lambda a: a * lax.rsqrt(jnp.mean(a * a, axis=-1, keepdims=True) + EPS), hh)
    b_end = _each(lambda a: a[T - 1:T, :], bc)
    m_new = _each(lambda be, m, b, i: jnp.maximum(be + m, jnp.max(be - b + i, axis=-1, keepdims=True)),
                  b_end, mh, br, igr)
    carry = _each(lambda be, m, mn: jnp.exp(be + m - mn), b_end, mh, m_new)
    wk = _each(lambda kk_, be, a, i, mn: kk_ * jnp.exp(be - a + i - mn), kh, b_end, bc, igc, m_new)
    kv = _each(_dot_tn, wk, vh)
    m_next = m_prev
    for h in heads:
        cols = slice(h * A_DV, (h + 1) * A_DV)
        gate = _sigmoid(o_ref[:, cols]) * _sigmoid(ga_ref[:, cols])
        h_ref[:, cols] = (hh[h] * nw_ref[:, cols] * gate).astype(h_ref.dtype)
        c_out[h] = carry[h] * ch[h] + kv[h]
        n_out[h:h + 1, :] = carry[h] * nh[h] + jnp.sum(wk[h], axis=0, keepdims=True)
        m_next = jnp.where(lane == A_HEADS + h, m_new[h], m_next)
    m_out[...] = m_next

    if not single_chunk:
        @pl.when(c == pl.num_programs(1) - 1)
        def _():
            cn_ref[0] = c_s[...]
            nn_ref[0] = n_s[...]
            mn_ref[0] = m_s[...]


def _mlstm(proj, gates_row, e_conv, conv_w, conv_b, bias_row, bias_col, norm_w, c0, n0, m0, *, T, n_seq, n_chunk):
    row = lambda s, c: (s * n_chunk + c, 0)
    fixed = lambda s, c: (0, 0)
    wide = lambda col: pl.BlockSpec((T, D_MODEL), lambda s, c: (s * n_chunk + c, col // D_MODEL))
    return pl.pallas_call(
        functools.partial(_mlstm_body, T=T, single_chunk=n_chunk == 1),
        grid=(n_seq, n_chunk),
        in_specs=[wide(COL_CONV), wide(COL_V), wide(COL_O), wide(COL_GA),
                  pl.BlockSpec((T, LANES), lambda s, c: (s * n_chunk + c, (COL_TAIL + TAIL_IF) // LANES)),
                  pl.BlockSpec((1, 2 * A_HEADS, T), lambda s, c: (s * n_chunk + c, 0, 0)),
                  pl.BlockSpec((1, SUBLANES, CONV_COLS), lambda s, c: (s, 0, 0)),
                  pl.BlockSpec((A_CONV, CONV_COLS), fixed),
                  pl.BlockSpec((1, CONV_COLS), fixed),
                  pl.BlockSpec((1, LANES), fixed),
                  pl.BlockSpec((2 * A_HEADS, 1), fixed),
                  pl.BlockSpec((1, A_VW), fixed),
                  pl.BlockSpec((1, A_HEADS, A_DK, A_DV), lambda s, c: (s, 0, 0, 0)),
                  pl.BlockSpec((1, A_HEADS, A_DK), lambda s, c: (s, 0, 0)),
                  pl.BlockSpec((1, 1, LANES), lambda s, c: (s, 0, 0))],
        out_specs=[pl.BlockSpec((T, A_VW), row),
                   pl.BlockSpec((1, A_HEADS, A_DK, A_DV), lambda s, c: (s, 0, 0, 0)),
                   pl.BlockSpec((1, A_HEADS, A_DK), lambda s, c: (s, 0, 0)),
                   pl.BlockSpec((1, 1, LANES), lambda s, c: (s, 0, 0))],
        out_shape=[jax.ShapeDtypeStruct((n_seq * n_chunk * T, A_VW), BF16),
                   jax.ShapeDtypeStruct((n_seq, A_HEADS, A_DK, A_DV), F32),
                   jax.ShapeDtypeStruct((n_seq, A_HEADS, A_DK), F32),
                   jax.ShapeDtypeStruct((n_seq, 1, LANES), F32)],
        scratch_shapes=[pltpu.VMEM((A_HEADS, A_DK, A_DV), F32),
                        pltpu.VMEM((A_HEADS, A_DK), F32),
                        pltpu.VMEM((1, LANES), F32),
                        pltpu.VMEM((SUBLANES, CONV_COLS), F32)],
        compiler_params=_cparams(("arbitrary", "arbitrary"), 32),
        name="mlstm_chunk",
    )(proj, proj, proj, proj, proj, gates_row, e_conv, conv_w, conv_b, bias_row, bias_col, norm_w, c0, n0, m0)


def _kcat_dot(a1, b1, a2, b2):
    if a1.shape[1] % LANES == 0:
        return _dot(jnp.concatenate([a1, a2], axis=1), _rows(b1, b2))
    return _dot(a1, b1) + _dot(a2, b2)


def _head_sums(ones_blk, T, *xs):
    sums = jnp.dot(_rows(*xs).astype(BF16), ones_blk, preferred_element_type=F32)
    return [sums[i * T:(i + 1) * T] for i in range(len(xs))]


def _rwkv_body(ur_ref, uk_ref, uv_ref, ul_ref, gb_ref, er_ref, ek_ref, ev_ref, el_ref,
               mur_ref, muk_ref, muv_ref, mul_ref, w0_ref, w2_ref, a0_ref, a2_ref, g2_ref, kkw_ref, ka_ref,
               rk_ref, lnw_ref, lnb_ref, blk_ref, blkb_ref, s0_ref,
               hb_ref, sn_ref, s_s, cr_s, ck_s, cv_s, cl_s, *, T, n_pairs):
    c = pl.program_id(2)

    @pl.when(c == 0)
    def _():
        z = jnp.zeros((B_HEAD, B_HEAD), F32)
        for p in range(n_pairs):
            s_s[p] = _rows(jnp.concatenate([s0_ref[0, 2 * p], z], axis=1),
                           jnp.concatenate([z, s0_ref[0, 2 * p + 1]], axis=1))
        cr_s[...] = er_ref[0]
        ck_s[...] = ek_ref[0]
        cv_s[...] = ev_ref[0]
        cl_s[...] = el_ref[0]

    n_sub = ur_ref.shape[0] // T

    def shift_mix(u_ref, carry_ref, mu_ref, cs, rows):
        u = u_ref[rows, cs]
        before = carry_ref[:, cs] if rows.start == 0 else u_ref[rows.start - SUBLANES:rows.start, cs]
        prev = pltpu.roll(_rows(before, u), 1, 0)[SUBLANES:, :]
        if rows.stop == u_ref.shape[0]:
            carry_ref[:, cs] = u[rows.stop - rows.start - SUBLANES:, :]
        return u + mu_ref[:, cs] * (prev - u)

    xl = shift_mix(ul_ref, cl_s, mul_ref, slice(None), slice(0, n_sub * T))
    th_wd = jnp.tanh(xl[:, :LANES]).astype(BF16)
    x_ad = pltpu.roll(xl[:, :2 * LANES], 2 * LANES - LORA_DECAY, 1)[:, :LANES].astype(BF16)
    gd_off = LORA_DECAY + LORA_ICLR - TAIL_GD_WIN
    sg_gd = _sigmoid(pltpu.roll(xl[:, TAIL_GD_WIN:], TAIL_MIX - TAIL_GD_WIN - gd_off, 1)[:, :LORA_GATE]).astype(BF16)

    lo = lax.broadcasted_iota(jnp.int32, (1, LANES), 1) < B_HEAD
    head1 = lambda x: jnp.where(lo, x, jnp.zeros_like(x))
    head2 = lambda x: jnp.where(lo, jnp.zeros_like(x), x)
    r2 = lax.broadcasted_iota(jnp.int32, (T, 2 * T), 0)
    c2 = lax.broadcasted_iota(jnp.int32, (T, 2 * T), 1)
    left = c2 < T
    cj = jnp.where(left, c2, c2 - T)
    strict2 = cj < r2
    incl2 = cj <= r2
    eye_pair = jnp.where(cj == r2, 1.0, 0.0)
    ones_blk = blkb_ref[...]
    zero = jnp.zeros((T, LANES), BF16)
    aligned = (2 * T) % LANES == 0

    def block_diag(sb):
        z = jnp.zeros_like(sb)
        return _rows(jnp.where(left, sb, z), jnp.where(left, z, sb))

    def pair_program(p):
        S = None
        for sub in range(n_sub):
            S = yield from chunk_program(p, slice(sub * T, (sub + 1) * T), S)
        s_s[p] = S

    def chunk_program(p, rows, S):
        cs = slice(p * LANES, (p + 1) * LANES)
        r = shift_mix(ur_ref, cr_s, mur_ref, cs, rows)
        xk = shift_mix(uk_ref, ck_s, muk_ref, cs, rows)
        v = shift_mix(uv_ref, cv_s, muv_ref, cs, rows)
        lw = jnp.dot(th_wd[rows], w2_ref[:, cs], preferred_element_type=F32)
        la = jnp.dot(x_ad[rows], a2_ref[:, cs], preferred_element_type=F32)
        lg = jnp.dot(sg_gd[rows], g2_ref[:, cs], preferred_element_type=F32)
        kkw = xk * kkw_ref[:, cs]
        ssq, = _head_sums(ones_blk, T, kkw * kkw)
        yield
        ld = DECAY_SCALE * _sigmoid(w0_ref[:, cs] + lw)
        a = _sigmoid(a0_ref[:, cs] + la)
        g = lg * _sigmoid(gb_ref[rows, cs])
        k = xk * (1.0 + (a - 1.0) * ka_ref[:, cs])
        kk = kkw * lax.rsqrt(jnp.maximum(ssq, 1e-24))
        bi = kk * a
        cum = _cumsum_rows(ld)
        ewt = jnp.exp(cum[T - 1:T, :])
        Rt = (r * jnp.exp(cum)).astype(BF16)
        At = (-kk * jnp.exp(cum - ld)).astype(BF16)
        einv = jnp.exp(-cum)
        Bt = bi * einv
        Kt = k * einv
        BhKh = _rows(Bt * ewt, Kt * ewt).astype(BF16)
        Bt = Bt.astype(BF16)
        Kt = Kt.astype(BF16)
        vb = v.astype(BF16)
        V1 = head1(vb)
        V2 = head2(vb)
        if aligned:
            G = _dot_nt(_rows(head1(At), head1(Rt), head2(At), head2(Rt)), _rows(Bt, Kt))
            G1, G2 = G[:2 * T], G[2 * T:]
        else:
            G1 = _dot_nt(_rows(head1(At), head1(Rt)), _rows(Bt, Kt))
            G2 = _dot_nt(_rows(head2(At), head2(Rt)), _rows(Kt, Bt))
        yield
        LL1 = jnp.where(strict2, G1[:T], 0.0)
        LL2 = jnp.where(strict2, G2[:T], 0.0)
        P1 = jnp.where(incl2, G1[T:], 0.0)
        P2 = jnp.where(incl2, G2[T:], 0.0)
        lab2 = pltpu.roll(LL2, T, 1) if aligned else LL2
        pw = jnp.where(left, LL1, lab2).astype(BF16)
        iv = eye_pair + jnp.where(left, LL1, lab2)
        X0 = _kcat_dot(LL1, _rows(zero, V1), LL2, _rows(zero, V2) if aligned else _rows(V2, zero))
        pw = _dot(pw, block_diag(pw)).astype(BF16)
        yield
        span = 4
        while span < T:
            both = _dot(_rows(iv.astype(BF16), pw), block_diag(pw))
            iv, pw = iv + both[:T], both[T:].astype(BF16)
            span *= 2
            yield
        iv = (iv + _dot(iv, block_diag(pw))).astype(BF16)
        if S is None:
            S = s_s[p]
        AR = _dot_nt(_rows(At, Rt), S)
        yield
        X = (AR[:T] + X0).astype(BF16)
        U = _dot(iv, _rows(head1(X), head2(X))).astype(BF16)
        yield
        if aligned:
            rhs2 = _rows(head2(U), V2)
        else:
            rhs2 = _rows(V2, head2(U))
        o = AR[T:] + _kcat_dot(P1, _rows(head1(U), V1), P2, rhs2)
        S_next = S * ewt + blk_ref[...] * _dot_tn(_rows(U, vb), BhKh)
        yield
        o_sum, rk_sum = _head_sums(ones_blk, T, o, r * k * rk_ref[:, cs])
        yield
        oc = o - o_sum * (1.0 / B_HEAD)
        var, = _head_sums(ones_blk, T, oc * oc)
        yield
        on = oc * lax.rsqrt(var * (1.0 / B_HEAD) + GN_EPS) * lnw_ref[:, cs] + lnb_ref[:, cs]
        hb_ref[rows, cs] = ((on + rk_sum * v) * g).astype(hb_ref.dtype)
        return S_next

    _run_staggered([pair_program(p) for p in range(n_pairs)], RWKV_WAVE, RWKV_LAG)

    @pl.when(c == pl.num_programs(2) - 1)
    def _():
        for p in range(n_pairs):
            sn_ref[0, 2 * p] = s_s[p, :B_HEAD, :B_HEAD]
            sn_ref[0, 2 * p + 1] = s_s[p, B_HEAD:, B_HEAD:]


def _rwkv(proj, e_main, e_tail, mu_main, mu_tail, w0, w2p, a0, a2p, g2, k_k, k_a, rk, lnw, lnb, blk, s0, *,
          T, n_seq, n_chunk):
    assert T <= RWKV_CHUNK
    gw = GROUP_W
    n_sub = RWKV_CHUNKS_PER_STEP if n_chunk % RWKV_CHUNKS_PER_STEP == 0 else 1
    n_step = n_chunk // n_sub
    tb = n_sub * T
    wide = lambda col: pl.BlockSpec((tb, gw), lambda s, pg, c: (s * n_step + c, col // gw + pg))
    edge = lambda col: pl.BlockSpec((1, SUBLANES, gw), lambda s, pg, c: (s, 0, col // gw + pg))
    vec = lambda col: pl.BlockSpec((1, gw), lambda s, pg, c: (0, col // gw + pg))
    mat = lambda rows: pl.BlockSpec((rows, gw), lambda s, pg, c: (0, pg))
    st = pl.BlockSpec((1, 2 * GROUP_PAIRS, B_HEAD, B_HEAD), lambda s, pg, c: (s, pg, 0, 0))
    return pl.pallas_call(
        functools.partial(_rwkv_body, T=T, n_pairs=GROUP_PAIRS),
        grid=(n_seq, N_PAIRS // GROUP_PAIRS, n_step),
        in_specs=[wide(COL_R), wide(COL_K), wide(COL_VB),
                  pl.BlockSpec((tb, TAIL_MIX), lambda s, pg, c: (s * n_step + c, COL_TAIL // TAIL_MIX)),
                  wide(COL_GB),
                  edge(0), edge(B_W), edge(2 * B_W),
                  pl.BlockSpec((1, SUBLANES, TAIL_MIX), lambda s, pg, c: (s, 0, 0)),
                  vec(0), vec(B_W), vec(2 * B_W),
                  pl.BlockSpec((1, TAIL_MIX), lambda s, pg, c: (0, 0)),
                  vec(0), mat(LANES), vec(0), mat(LANES), mat(LORA_GATE), vec(0), vec(0),
                  vec(0), vec(0), vec(0),
                  pl.BlockSpec((LANES, LANES), lambda s, pg, c: (0, 0)),
                  pl.BlockSpec((LANES, LANES), lambda s, pg, c: (0, 0)), st],
        out_specs=[pl.BlockSpec((tb, gw), lambda s, pg, c: (s * n_step + c, pg)), st],
        out_shape=[jax.ShapeDtypeStruct((n_seq * n_chunk * T, B_W), BF16),
                   jax.ShapeDtypeStruct((n_seq, B_HEADS, B_HEAD, B_HEAD), F32)],
        scratch_shapes=[pltpu.VMEM((GROUP_PAIRS, LANES, LANES), F32),
                        pltpu.VMEM((SUBLANES, gw), F32), pltpu.VMEM((SUBLANES, gw), F32),
                        pltpu.VMEM((SUBLANES, gw), F32), pltpu.VMEM((SUBLANES, TAIL_MIX), F32)],
        compiler_params=_cparams(("arbitrary", "arbitrary", "arbitrary"), 32),
        name="rwkv_chunk",
    )(proj, proj, proj, proj, proj, e_main, e_main, e_main, e_tail, mu_main, mu_main, mu_main, mu_tail,
      w0, w2p, a0, a2p, g2, k_k, k_a, rk, lnw, lnb, blk, blk.astype(BF16), s0)


def _merge_out_body(ha_ref, hb_ref, x_ref, w_ref, g_ref, y_ref):
    half = ha_ref.shape[0] // 2
    for rows in (slice(0, half), slice(half, 2 * half)):
        mixed = ha_ref[rows, :].astype(F32) + hb_ref[rows, :].astype(F32)
        y = jnp.dot(mixed.astype(BF16), w_ref[...], preferred_element_type=F32)
        ms = jnp.mean(y * y, axis=-1, keepdims=True)
        y_ref[rows, :] = x_ref[rows, :] + y * lax.rsqrt(ms + EPS) * g_ref[...]


def _merge_out(h_a, h_b, x, w_out, g_post):
    n = x.shape[0]
    wide = pl.BlockSpec((MERGE_TM, D_MODEL), lambda i: (i, 0))
    return pl.pallas_call(
        _merge_out_body,
        grid=(n // MERGE_TM,),
        in_specs=[wide, wide, wide,
                  pl.BlockSpec((D_MODEL, D_MODEL), lambda i: (0, 0)),
                  pl.BlockSpec((1, D_MODEL), lambda i: (0, 0))],
        out_specs=wide,
        out_shape=jax.ShapeDtypeStruct((n, D_MODEL), F32),
        compiler_params=_cparams(("parallel",), 48),
        name="merge_out_proj",
    )(h_a, h_b, x, w_out, g_post)


def _mlp_body(x_ref, xn_ref, g1_ref, wu_ref, wd_ref, g2_ref, y_ref, hn_ref, acc_ref, *, row_chunk):
    i = pl.program_id(0)
    j = pl.program_id(1)
    last = pl.num_programs(1) - 1
    slot = i % 2
    step_rows = xn_ref.shape[0]

    @pl.when((i == 0) & (j == 0))
    def _():
        _norm_rows(x_ref, g1_ref, hn_ref.at[0], row_chunk)
        acc_ref[...] = jnp.zeros_like(acc_ref)

    def partial_product():
        u = jnp.dot(hn_ref[slot], wu_ref[...], preferred_element_type=F32)
        a = jnp.square(jnp.maximum(u, 0.0))
        part = jnp.dot(a.astype(BF16), wd_ref[...], preferred_element_type=F32)
        rows = pl.ds(pl.multiple_of(j * step_rows, step_rows), step_rows)
        hn_ref[1 - slot, rows, :] = _norm_bf16(xn_ref[...], g1_ref[...])
        return part

    @pl.when(j < last)
    def _():
        part = partial_product()
        acc_ref[...] = jnp.where(j == 0, part, acc_ref[...] + part)

    @pl.when(j == last)
    def _():
        ff = acc_ref[...] + partial_product()
        ms = jnp.mean(ff * ff, axis=-1, keepdims=True)
        y_ref[...] = x_ref[...] + ff * lax.rsqrt(ms + EPS) * g2_ref[...]


def _mlp(x, g_pre, w_up, w_down, g_post):
    n, d = x.shape
    f = w_up.shape[1]
    nj = f // MLP_TF
    step_rows = MLP_TM // nj
    n_slices = n // step_rows
    return pl.pallas_call(
        functools.partial(_mlp_body, row_chunk=LANES),
        grid=(n // MLP_TM, nj),
        in_specs=[pl.BlockSpec((MLP_TM, d), lambda i, j: (i, 0)),
                  pl.BlockSpec((step_rows, d), lambda i, j: (jnp.minimum((i + 1) * nj + j, n_slices - 1), 0)),
                  pl.BlockSpec((1, d), lambda i, j: (0, 0)),
                  pl.BlockSpec((d, MLP_TF), lambda i, j: (0, j)),
                  pl.BlockSpec((MLP_TF, d), lambda i, j: (j, 0)),
                  pl.BlockSpec((1, d), lambda i, j: (0, 0))],
        out_specs=pl.BlockSpec((MLP_TM, d), lambda i, j: (i, 0)),
        out_shape=jax.ShapeDtypeStruct((n, d), F32),
        scratch_shapes=[pltpu.VMEM((2, MLP_TM, d), BF16), pltpu.VMEM((MLP_TM, d), F32)],
        compiler_params=_cparams(("arbitrary", "arbitrary"), 56),
        name="mlp",
    )(x, x, g_pre, w_up, w_down, g_post)


def _pad_cols(a, width):
    return jnp.pad(a, [(0, 0)] * (a.ndim - 1) + [(0, width - a.shape[-1])])


def _pad_rows(a, before, total):
    return jnp.pad(a, ((before, total - before - a.shape[0]), (0, 0)))


def _edge_blocks(prev_rows):
    return jnp.pad(prev_rows, ((0, 0), (SUBLANES - prev_rows.shape[1], 0), (0, 0)))


def _gates_row_form(proj, T):
    g = proj[:, COL_TAIL + TAIL_IF:COL_TAIL + TAIL_IF + 2 * A_HEADS]
    return jnp.swapaxes(g.reshape(-1, T, 2 * A_HEADS), 1, 2)


def kernel(x_prompt, x_sample, state_mlstm_C, state_mlstm_n, state_mlstm_m, state_mlstm_conv, state_rwkv_S, state_rwkv_shift, meta_tokens, norm_mix_pre, norm_mix_post, norm_mlp_pre, norm_mlp_post, w_in, mlstm_conv_w, mlstm_conv_b, mlstm_b_i, mlstm_b_f, mlstm_norm_w, rwkv_mu, rwkv_w0, rwkv_w2, rwkv_a0, rwkv_a2, rwkv_g2, rwkv_k_k, rwkv_k_a, rwkv_r_k, rwkv_ln_w, rwkv_ln_b, w_out, w_up, w_down):
    n_s_seq, s_len, _ = x_sample.shape
    _, p_len, _ = x_prompt.shape
    k3 = A_CONV - 1

    w_all = _relayout_weight(jnp.swapaxes(w_in[0], 0, 1))
    mu = rwkv_mu[0][None, :]
    mu_main = mu[:, :3 * B_W]
    mu_tail = _pad_cols(mu[:, 3 * B_W:], TAIL_MIX)
    w2p = _pad_rows(rwkv_w2[0], 0, LANES).astype(BF16)
    a2p = _pad_rows(rwkv_a2[0], 0, LANES).astype(BF16)
    g2 = rwkv_g2[0].astype(BF16)
    g_pre = norm_mix_pre[0][None, :]
    bias = jnp.concatenate([mlstm_b_i[0], mlstm_b_f[0]])
    bias_row = _pad_cols(bias[None, :], LANES)
    bias_col = bias[:, None]
    norm_w = mlstm_norm_w[0][None, :]
    conv_w = mlstm_conv_w[0]
    conv_b = mlstm_conv_b[0][None, :]
    rwkv_vecs = (rwkv_w0[0][None, :], w2p, rwkv_a0[0][None, :], a2p, g2, rwkv_k_k[0][None, :],
                 rwkv_k_a[0][None, :], rwkv_r_k[0].reshape(1, B_W), rwkv_ln_w[0][None, :], rwkv_ln_b[0][None, :],
                 _head_block_mask())
    w_out_b = w_out[0].astype(BF16)
    w_up_b = w_up[0].astype(BF16)
    w_down_b = w_down[0].astype(BF16)

    def branches(proj, T, n_seq, n_chunk, e_conv, e_main, e_tail, c0, n0, m0, s0):
        tm_ = MLSTM_CHUNK if n_chunk > 1 else T
        h_a, c_n, n_n, m_n = _mlstm(proj, _gates_row_form(proj, tm_), e_conv, conv_w, conv_b, bias_row, bias_col,
                                    norm_w, c0, n0, m0, T=tm_, n_seq=n_seq, n_chunk=n_chunk * T // tm_)
        h_b, s_n = _rwkv(proj, e_main, e_tail, mu_main, mu_tail, *rwkv_vecs, s0,
                         T=T, n_seq=n_seq, n_chunk=n_chunk)
        return h_a, h_b, (c_n, n_n, m_n, s_n)

    def finish(h_a, h_b, x_rows):
        x1 = _merge_out(h_a, h_b, x_rows, w_out_b, norm_mix_post[0][None, :])
        return _mlp(x1, norm_mlp_pre[0][None, :], w_up_b, w_down_b, norm_mlp_post[0][None, :])

    zeros = lambda *shape: jnp.zeros(shape, F32)
    xs = x_sample.reshape(n_s_seq * s_len, D_MODEL).astype(F32)
    assert xs.shape[0] == PROJ_TM
    proj_s, proj_m = _norm_matmul_pair(xs, meta_tokens.astype(F32), g_pre, w_all)
    _, _, (c_m, n_m, m_m, s_m) = branches(
        proj_m, N_META, 1, 1,
        zeros(1, SUBLANES, CONV_COLS), zeros(1, SUBLANES, 3 * B_W), zeros(1, SUBLANES, TAIL_MIX),
        zeros(1, A_HEADS, A_DK, A_DV), zeros(1, A_HEADS, A_DK), zeros(1, 1, LANES),
        zeros(1, B_HEADS, B_HEAD, B_HEAD))

    xp = x_prompt[0].astype(F32)
    proj_p = _norm_matmul(xp, g_pre, w_all, PROJ_TM)
    ha_p, hb_p, (c_p, n_p, m_p, s_p) = branches(
        proj_p, RWKV_CHUNK, 1, p_len // RWKV_CHUNK,
        _edge_blocks(proj_m[None, N_META - k3:, COL_CONV:COL_CONV + CONV_COLS]),
        _edge_blocks(proj_m[None, N_META - 1:, COL_R:COL_R + 3 * B_W]),
        _edge_blocks(proj_m[None, N_META - 1:, COL_TAIL:COL_TAIL + TAIL_MIX]),
        c_m, n_m, m_m, s_m)
    y_prompt = finish(ha_p, hb_p, xp)[None].astype(x_prompt.dtype)

    sh = state_rwkv_shift[0].astype(F32)
    m0_s = jnp.pad(state_mlstm_m[0].astype(F32), ((0, 0), (A_HEADS, LANES - 2 * A_HEADS)))[:, None, :]
    ha_s, hb_s, (c_s, n_s, m_s, s_s) = branches(
        proj_s, s_len, n_s_seq, 1,
        _edge_blocks(state_mlstm_conv[0].astype(F32)),
        _edge_blocks(sh[:, :, :3 * B_W]), _edge_blocks(_pad_cols(sh[:, :, 3 * B_W:], TAIL_MIX)),
        state_mlstm_C[0].astype(F32), state_mlstm_n[0].astype(F32), m0_s,
        state_rwkv_S[0].astype(F32))
    y_sample = finish(ha_s, hb_s, xs).reshape(x_sample.shape).astype(x_sample.dtype)

    def shift_state(rows):
        return jnp.concatenate([rows[..., COL_R:COL_R + 3 * B_W],
                                rows[..., COL_TAIL:COL_TAIL + LORA_COLS]], axis=-1)

    last_s = proj_s.reshape(n_s_seq, s_len, PROJ_COLS)[:, s_len - k3:, :]
    conv_s = last_s[..., COL_CONV:COL_CONV + CONV_COLS]
    shift_s = shift_state(last_s[:, k3 - 1:, :])
    last_p = proj_p[p_len - k3:, :]
    conv_p = last_p[None, :, COL_CONV:COL_CONV + CONV_COLS]
    shift_p = shift_state(last_p[None, k3 - 1:, :])
    dt_c, dt_n, dt_m = state_mlstm_C.dtype, state_mlstm_n.dtype, state_mlstm_m.dtype
    dt_cv, dt_s, dt_sh = state_mlstm_conv.dtype, state_rwkv_S.dtype, state_rwkv_shift.dtype
    lanes_m = slice(A_HEADS, 2 * A_HEADS)
    return (y_prompt, y_sample,
            c_p[None].astype(dt_c), n_p[None].astype(dt_n), m_p[:, 0, lanes_m][None].astype(dt_m),
            conv_p[None].astype(dt_cv), s_p[None].astype(dt_s), shift_p[None].astype(dt_sh),
            c_s[None].astype(dt_c), n_s[None].astype(dt_n), m_s[:, 0, lanes_m][None].astype(dt_m),
            conv_s[None].astype(dt_cv), s_s[None].astype(dt_s), shift_s[None].astype(dt_sh))
```
